```python
import jax, jax.numpy as jnp
from jax import lax
import numpy as np

D_MODEL = 1024
BATCH = 32
SEQ = 2048
DEPTH = 2

N_MEM = 256
EPS = 1e-6
FOX_HEAD_DIM = 64
FOX_WIDTH = D_MODEL // 2
FOX_HEADS = FOX_WIDTH // FOX_HEAD_DIM
GMLP_GROUP_DIM = 64
GMLP_WIDTH = D_MODEL // 2
GMLP_GROUPS = GMLP_WIDTH // GMLP_GROUP_DIM
CHUNK = 128
Q_BLOCK = 128
MIX_WIDTH = FOX_WIDTH + GMLP_WIDTH
IN_WIDTH = 3 * FOX_WIDTH + FOX_HEADS + 2 * GMLP_WIDTH
CONV_WIDTH = D_MODEL
CONV_KERNEL = 31
XA_HEADS = 4
XA_HEAD_DIM = D_MODEL // XA_HEADS
FFN_HIDDEN = -(-8 * D_MODEL // (3 * 256)) * 256
N_EVEN = (DEPTH + 1) // 2
N_ODD = DEPTH // 2

kernel_name = "hybrid_gmlp_fox_conformer_memxattn"


def rmsnorm(x, g):
    x32 = x.astype(jnp.float32)
    y = x32 * lax.rsqrt(jnp.mean(x32 * x32, axis=-1, keepdims=True) + EPS)
    return (y * g.astype(jnp.float32)).astype(x.dtype)


def layernorm(x, g, b):
    x32 = x.astype(jnp.float32)
    mu = jnp.mean(x32, axis=-1, keepdims=True)
    xc = x32 - mu
    y = xc * lax.rsqrt(jnp.mean(xc * xc, axis=-1, keepdims=True) + EPS)
    return (y * g.astype(jnp.float32) + b.astype(jnp.float32)).astype(x.dtype)


def fox_attention(q, k, v, f_logit, f_bias):
    B, T, _ = q.shape
    scale = FOX_HEAD_DIM ** -0.5
    q = q.reshape(B, T, FOX_HEADS, FOX_HEAD_DIM) * scale
    k = k.reshape(B, T, FOX_HEADS, FOX_HEAD_DIM)
    v = v.reshape(B, T, FOX_HEADS, FOX_HEAD_DIM)
    log_f = jax.nn.log_sigmoid((f_logit + f_bias).astype(jnp.float32))
    cum = jnp.cumsum(log_f, axis=1).transpose(0, 2, 1)
    outs = []
    for i in range(T // Q_BLOCK):
        q0 = i * Q_BLOCK
        q1 = q0 + Q_BLOCK
        s = jnp.einsum('bqhd,bkhd->bhqk', q[:, q0:q1], k[:, :q1]).astype(jnp.float32)
        s = s + cum[:, :, q0:q1, None] - cum[:, :, None, :q1]
        causal = (q0 + jnp.arange(Q_BLOCK))[:, None] >= jnp.arange(q1)[None, :]
        p = jax.nn.softmax(jnp.where(causal, s, -jnp.inf), axis=-1).astype(v.dtype)
        outs.append(jnp.einsum('bhqk,bkhd->bqhd', p, v[:, :q1]))
    return jnp.concatenate(outs, axis=1).reshape(B, T, FOX_WIDTH)


def gmlp_spatial_gate(z, ln_g, ln_b, w_s, b_s):
    B, T, _ = z.shape
    z = jax.nn.gelu(z)
    u, vg = jnp.split(z, 2, axis=-1)
    vg = layernorm(vg, ln_g, ln_b)
    vg = vg.reshape(B, T // CHUNK, CHUNK, GMLP_GROUPS, GMLP_GROUP_DIM)
    w = w_s * jnp.tril(jnp.ones((CHUNK, CHUNK), dtype=w_s.dtype))
    mixed = jnp.einsum('gts,bcsgd->bctgd', w, vg) + b_s.T[:, :, None]
    return u * mixed.reshape(B, T, GMLP_WIDTH)


def even_mixer(h, w_in, f_bias, ln_g, ln_b, w_s, b_s, w_out):
    proj = h @ w_in
    F = FOX_WIDTH
    q, k, v, f_logit, z = jnp.split(proj, [F, 2 * F, 3 * F, 3 * F + FOX_HEADS], axis=-1)
    a_out = gmlp_spatial_gate(z, ln_g, ln_b, w_s, b_s)
    b_out = fox_attention(q, k, v, f_logit, f_bias)
    return jnp.concatenate([b_out, a_out], axis=-1) @ w_out


def conformer_conv(h, w_in, b_in, dw_w, dw_b, ln_g, ln_b, w_out, b_out):
    a, g = jnp.split(h @ w_in + b_in, 2, axis=-1)
    y = a * jax.nn.sigmoid(g)
    y = lax.conv_general_dilated(
        y, dw_w[:, None, :].astype(y.dtype), window_strides=(1,),
        padding=[(CONV_KERNEL - 1, 0)], dimension_numbers=('NWC', 'WIO', 'NWC'),
        feature_group_count=CONV_WIDTH) + dw_b
    y = jax.nn.silu(layernorm(y, ln_g, ln_b))
    return y @ w_out + b_out


def memory_cross_attention(h, m, wq, wkv, wo):
    B, T, _ = h.shape
    q = (h @ wq).reshape(B, T, XA_HEADS, XA_HEAD_DIM) * (XA_HEAD_DIM ** -0.5)
    k, v = jnp.split(m @ wkv, 2, axis=-1)
    k = k.reshape(B, -1, XA_HEADS, XA_HEAD_DIM)
    v = v.reshape(B, -1, XA_HEADS, XA_HEAD_DIM)
    s = jnp.einsum('bthd,bmhd->bhtm', q, k).astype(jnp.float32)
    p = jax.nn.softmax(s, axis=-1).astype(v.dtype)
    o = jnp.einsum('bhtm,bmhd->bthd', p, v).reshape(B, T, D_MODEL)
    return o @ wo


def swiglu(h, w_gu, w_down):
    g, u = jnp.split(h @ w_gu, 2, axis=-1)
    return (jax.nn.silu(g) * u) @ w_down


def _fwd_setup_inputs(seed: int = 0) -> dict:
    key = jax.random.key(seed)
    ks = iter(jax.random.split(key, 40))

    def nrm(shape, scale):
        return jax.random.normal(next(ks), shape, jnp.float32) * scale

    def gain(shape):
        return 1.0 + nrm(shape, 0.02)

    D = D_MODEL
    return {
        "x": nrm((BATCH, SEQ, D), 1.0),
        "mem": nrm((BATCH, N_MEM, D), 1.0),
        "mix_norm_e": gain((N_EVEN, D)),
        "w_in_e": nrm((N_EVEN, D, IN_WIDTH), D ** -0.5),
        "fox_f_bias": 2.0 + nrm((N_EVEN, FOX_HEADS), 0.5),
        "gmlp_ln_g": gain((N_EVEN, GMLP_WIDTH)),
        "gmlp_ln_b": nrm((N_EVEN, GMLP_WIDTH), 0.02),
        "gmlp_w_s": nrm((N_EVEN, GMLP_GROUPS, CHUNK, CHUNK), CHUNK ** -0.5),
        "gmlp_b_s": gain((N_EVEN, GMLP_GROUPS, CHUNK)),
        "w_out_e": nrm((N_EVEN, MIX_WIDTH, D), MIX_WIDTH ** -0.5),
        "mix_norm_o": gain((N_ODD, D)),
        "conv_w_in": nrm((N_ODD, D, 2 * CONV_WIDTH), D ** -0.5),
        "conv_b_in": nrm((N_ODD, 2 * CONV_WIDTH), 0.02),
        "conv_dw_w": nrm((N_ODD, CONV_KERNEL, CONV_WIDTH), CONV_KERNEL ** -0.5),
        "conv_dw_b": nrm((N_ODD, CONV_WIDTH), 0.02),
        "conv_ln_g": gain((N_ODD, CONV_WIDTH)),
        "conv_ln_b": nrm((N_ODD, CONV_WIDTH), 0.02),
        "conv_w_out": nrm((N_ODD, CONV_WIDTH, D), CONV_WIDTH ** -0.5),
        "conv_b_out": nrm((N_ODD, D), 0.02),
        "xa_norm": gain((DEPTH, D)),
        "mem_norm": gain((DEPTH, D)),
        "xa_wq": nrm((DEPTH, D, D), D ** -0.5),
        "xa_wkv": nrm((DEPTH, D, 2 * D), D ** -0.5),
        "xa_wo": nrm((DEPTH, D, D), D ** -0.5),
        "ffn_norm": gain((DEPTH, D)),
        "ffn_w_gu": nrm((DEPTH, D, 2 * FFN_HIDDEN), D ** -0.5),
        "ffn_w_down": nrm((DEPTH, FFN_HIDDEN, D), FFN_HIDDEN ** -0.5),
        "final_norm": gain((D,)),
    }


def _fwd_reference(x, mem, mix_norm_e, w_in_e, fox_f_bias, gmlp_ln_g, gmlp_ln_b, gmlp_w_s,
              gmlp_b_s, w_out_e, mix_norm_o, conv_w_in, conv_b_in, conv_dw_w, conv_dw_b,
              conv_ln_g, conv_ln_b, conv_w_out, conv_b_out, xa_norm, mem_norm, xa_wq,
              xa_wkv, xa_wo, ffn_norm, ffn_w_gu, ffn_w_down, final_norm):
    for layer in range(DEPTH):
        li = layer // 2
        if layer % 2 == 0:
            h = rmsnorm(x, mix_norm_e[li])
            x = x + even_mixer(h, w_in_e[li], fox_f_bias[li], gmlp_ln_g[li], gmlp_ln_b[li],
                               gmlp_w_s[li], gmlp_b_s[li], w_out_e[li])
        else:
            h = rmsnorm(x, mix_norm_o[li])
            x = x + conformer_conv(h, conv_w_in[li], conv_b_in[li], conv_dw_w[li],
                                   conv_dw_b[li], conv_ln_g[li], conv_ln_b[li],
                                   conv_w_out[li], conv_b_out[li])
        h = rmsnorm(x, xa_norm[layer])
        m = rmsnorm(mem, mem_norm[layer])
        x = x + memory_cross_attention(h, m, xa_wq[layer], xa_wkv[layer], xa_wo[layer])
        h = rmsnorm(x, ffn_norm[layer])
        x = x + swiglu(h, ffn_w_gu[layer], ffn_w_down[layer])
    return rmsnorm(x, final_norm)


import jax as _jax
import jax.numpy as _jnp

TWIN_FORMAT = 'train_step'
FWD_PARAMS = ['x', 'mem', 'mix_norm_e', 'w_in_e', 'fox_f_bias', 'gmlp_ln_g', 'gmlp_ln_b', 'gmlp_w_s', 'gmlp_b_s', 'w_out_e', 'mix_norm_o', 'conv_w_in', 'conv_b_in', 'conv_dw_w', 'conv_dw_b', 'conv_ln_g', 'conv_ln_b', 'conv_w_out', 'conv_b_out', 'xa_norm', 'mem_norm', 'xa_wq', 'xa_wkv', 'xa_wo', 'ffn_norm', 'ffn_w_gu', 'ffn_w_down', 'final_norm']
TWIN_WEIGHTS = ['mix_norm_e', 'w_in_e', 'fox_f_bias', 'gmlp_ln_g', 'gmlp_ln_b', 'gmlp_w_s', 'gmlp_b_s', 'w_out_e', 'mix_norm_o', 'conv_w_in', 'conv_b_in', 'conv_dw_w', 'conv_dw_b', 'conv_ln_g', 'conv_ln_b', 'conv_w_out', 'conv_b_out', 'xa_norm', 'mem_norm', 'xa_wq', 'xa_wkv', 'xa_wo', 'ffn_norm', 'ffn_w_gu', 'ffn_w_down', 'final_norm']
TWIN_DIFF_INPUT = 'x'
TWIN_INPUTS = ['x', 'mem', 'mix_norm_e', 'w_in_e', 'fox_f_bias', 'gmlp_ln_g', 'gmlp_ln_b', 'gmlp_w_s', 'gmlp_b_s', 'w_out_e', 'mix_norm_o', 'conv_w_in', 'conv_b_in', 'conv_dw_w', 'conv_dw_b', 'conv_ln_g', 'conv_ln_b', 'conv_w_out', 'conv_b_out', 'xa_norm', 'mem_norm', 'xa_wq', 'xa_wkv', 'xa_wo', 'ffn_norm', 'ffn_w_gu', 'ffn_w_down', 'final_norm', 'loss_target', 'm_mix_norm_e', 'm_w_in_e', 'm_fox_f_bias', 'm_gmlp_ln_g', 'm_gmlp_ln_b', 'm_gmlp_w_s', 'm_gmlp_b_s', 'm_w_out_e', 'm_mix_norm_o', 'm_conv_w_in', 'm_conv_b_in', 'm_conv_dw_w', 'm_conv_dw_b', 'm_conv_ln_g', 'm_conv_ln_b', 'm_conv_w_out', 'm_conv_b_out', 'm_xa_norm', 'm_mem_norm', 'm_xa_wq', 'm_xa_wkv', 'm_xa_wo', 'm_ffn_norm', 'm_ffn_w_gu', 'm_ffn_w_down', 'm_final_norm', 'v_mix_norm_e', 'v_w_in_e', 'v_fox_f_bias', 'v_gmlp_ln_g', 'v_gmlp_ln_b', 'v_gmlp_w_s', 'v_gmlp_b_s', 'v_w_out_e', 'v_mix_norm_o', 'v_conv_w_in', 'v_conv_b_in', 'v_conv_dw_w', 'v_conv_dw_b', 'v_conv_ln_g', 'v_conv_ln_b', 'v_conv_w_out', 'v_conv_b_out', 'v_xa_norm', 'v_mem_norm', 'v_xa_wq', 'v_xa_wkv', 'v_xa_wo', 'v_ffn_norm', 'v_ffn_w_gu', 'v_ffn_w_down', 'v_final_norm']
TWIN_OUTPUTS = ['loss', 'grad_x', 'grad_mix_norm_e', 'grad_w_in_e', 'grad_fox_f_bias', 'grad_gmlp_ln_g', 'grad_gmlp_ln_b', 'grad_gmlp_w_s', 'grad_gmlp_b_s', 'grad_w_out_e', 'grad_mix_norm_o', 'grad_conv_w_in', 'grad_conv_b_in', 'grad_conv_dw_w', 'grad_conv_dw_b', 'grad_conv_ln_g', 'grad_conv_ln_b', 'grad_conv_w_out', 'grad_conv_b_out', 'grad_xa_norm', 'grad_mem_norm', 'grad_xa_wq', 'grad_xa_wkv', 'grad_xa_wo', 'grad_ffn_norm', 'grad_ffn_w_gu', 'grad_ffn_w_down', 'grad_final_norm', 'delta_mix_norm_e', 'delta_w_in_e', 'delta_fox_f_bias', 'delta_gmlp_ln_g', 'delta_gmlp_ln_b', 'delta_gmlp_w_s', 'delta_gmlp_b_s', 'delta_w_out_e', 'delta_mix_norm_o', 'delta_conv_w_in', 'delta_conv_b_in', 'delta_conv_dw_w', 'delta_conv_dw_b', 'delta_conv_ln_g', 'delta_conv_ln_b', 'delta_conv_w_out', 'delta_conv_b_out', 'delta_xa_norm', 'delta_mem_norm', 'delta_xa_wq', 'delta_xa_wkv', 'delta_xa_wo', 'delta_ffn_norm', 'delta_ffn_w_gu', 'delta_ffn_w_down', 'delta_final_norm', 'new_m_mix_norm_e', 'new_m_w_in_e', 'new_m_fox_f_bias', 'new_m_gmlp_ln_g', 'new_m_gmlp_ln_b', 'new_m_gmlp_w_s', 'new_m_gmlp_b_s', 'new_m_w_out_e', 'new_m_mix_norm_o', 'new_m_conv_w_in', 'new_m_conv_b_in', 'new_m_conv_dw_w', 'new_m_conv_dw_b', 'new_m_conv_ln_g', 'new_m_conv_ln_b', 'new_m_conv_w_out', 'new_m_conv_b_out', 'new_m_xa_norm', 'new_m_mem_norm', 'new_m_xa_wq', 'new_m_xa_wkv', 'new_m_xa_wo', 'new_m_ffn_norm', 'new_m_ffn_w_gu', 'new_m_ffn_w_down', 'new_m_final_norm', 'new_v_mix_norm_e', 'new_v_w_in_e', 'new_v_fox_f_bias', 'new_v_gmlp_ln_g', 'new_v_gmlp_ln_b', 'new_v_gmlp_w_s', 'new_v_gmlp_b_s', 'new_v_w_out_e', 'new_v_mix_norm_o', 'new_v_conv_w_in', 'new_v_conv_b_in', 'new_v_conv_dw_w', 'new_v_conv_dw_b', 'new_v_conv_ln_g', 'new_v_conv_ln_b', 'new_v_conv_w_out', 'new_v_conv_b_out', 'new_v_xa_norm', 'new_v_mem_norm', 'new_v_xa_wq', 'new_v_xa_wkv', 'new_v_xa_wo', 'new_v_ffn_norm', 'new_v_ffn_w_gu', 'new_v_ffn_w_down', 'new_v_final_norm']
TWIN_LEAF_KINDS = {'loss': 'loss', 'grad_x': 'grad_x', 'grad_mix_norm_e': 'grad_w', 'grad_w_in_e': 'grad_w', 'grad_fox_f_bias': 'grad_w', 'grad_gmlp_ln_g': 'grad_w', 'grad_gmlp_ln_b': 'grad_w', 'grad_gmlp_w_s': 'grad_w', 'grad_gmlp_b_s': 'grad_w', 'grad_w_out_e': 'grad_w', 'grad_mix_norm_o': 'grad_w', 'grad_conv_w_in': 'grad_w', 'grad_conv_b_in': 'grad_w', 'grad_conv_dw_w': 'grad_w', 'grad_conv_dw_b': 'grad_w', 'grad_conv_ln_g': 'grad_w', 'grad_conv_ln_b': 'grad_w', 'grad_conv_w_out': 'grad_w', 'grad_conv_b_out': 'grad_w', 'grad_xa_norm': 'grad_w', 'grad_mem_norm': 'grad_w', 'grad_xa_wq': 'grad_w', 'grad_xa_wkv': 'grad_w', 'grad_xa_wo': 'grad_w', 'grad_ffn_norm': 'grad_w', 'grad_ffn_w_gu': 'grad_w', 'grad_ffn_w_down': 'grad_w', 'grad_final_norm': 'grad_w', 'delta_mix_norm_e': 'delta_w', 'delta_w_in_e': 'delta_w', 'delta_fox_f_bias': 'delta_w', 'delta_gmlp_ln_g': 'delta_w', 'delta_gmlp_ln_b': 'delta_w', 'delta_gmlp_w_s': 'delta_w', 'delta_gmlp_b_s': 'delta_w', 'delta_w_out_e': 'delta_w', 'delta_mix_norm_o': 'delta_w', 'delta_conv_w_in': 'delta_w', 'delta_conv_b_in': 'delta_w', 'delta_conv_dw_w': 'delta_w', 'delta_conv_dw_b': 'delta_w', 'delta_conv_ln_g': 'delta_w', 'delta_conv_ln_b': 'delta_w', 'delta_conv_w_out': 'delta_w', 'delta_conv_b_out': 'delta_w', 'delta_xa_norm': 'delta_w', 'delta_mem_norm': 'delta_w', 'delta_xa_wq': 'delta_w', 'delta_xa_wkv': 'delta_w', 'delta_xa_wo': 'delta_w', 'delta_ffn_norm': 'delta_w', 'delta_ffn_w_gu': 'delta_w', 'delta_ffn_w_down': 'delta_w', 'delta_final_norm': 'delta_w', 'new_m_mix_norm_e': 'new_m', 'new_m_w_in_e': 'new_m', 'new_m_fox_f_bias': 'new_m', 'new_m_gmlp_ln_g': 'new_m', 'new_m_gmlp_ln_b': 'new_m', 'new_m_gmlp_w_s': 'new_m', 'new_m_gmlp_b_s': 'new_m', 'new_m_w_out_e': 'new_m', 'new_m_mix_norm_o': 'new_m', 'new_m_conv_w_in': 'new_m', 'new_m_conv_b_in': 'new_m', 'new_m_conv_dw_w': 'new_m', 'new_m_conv_dw_b': 'new_m', 'new_m_conv_ln_g': 'new_m', 'new_m_conv_ln_b': 'new_m', 'new_m_conv_w_out': 'new_m', 'new_m_conv_b_out': 'new_m', 'new_m_xa_norm': 'new_m', 'new_m_mem_norm': 'new_m', 'new_m_xa_wq': 'new_m', 'new_m_xa_wkv': 'new_m', 'new_m_xa_wo': 'new_m', 'new_m_ffn_norm': 'new_m', 'new_m_ffn_w_gu': 'new_m', 'new_m_ffn_w_down': 'new_m', 'new_m_final_norm': 'new_m', 'new_v_mix_norm_e': 'new_v', 'new_v_w_in_e': 'new_v', 'new_v_fox_f_bias': 'new_v', 'new_v_gmlp_ln_g': 'new_v', 'new_v_gmlp_ln_b': 'new_v', 'new_v_gmlp_w_s': 'new_v', 'new_v_gmlp_b_s': 'new_v', 'new_v_w_out_e': 'new_v', 'new_v_mix_norm_o': 'new_v', 'new_v_conv_w_in': 'new_v', 'new_v_conv_b_in': 'new_v', 'new_v_conv_dw_w': 'new_v', 'new_v_conv_dw_b': 'new_v', 'new_v_conv_ln_g': 'new_v', 'new_v_conv_ln_b': 'new_v', 'new_v_conv_w_out': 'new_v', 'new_v_conv_b_out': 'new_v', 'new_v_xa_norm': 'new_v', 'new_v_mem_norm': 'new_v', 'new_v_xa_wq': 'new_v', 'new_v_xa_wkv': 'new_v', 'new_v_xa_wo': 'new_v', 'new_v_ffn_norm': 'new_v', 'new_v_ffn_w_gu': 'new_v', 'new_v_ffn_w_down': 'new_v', 'new_v_final_norm': 'new_v'}


def _forward(args):
    return _fwd_reference(*[args[k] for k in FWD_PARAMS])


def _output_shape():
    out = _jax.eval_shape(lambda: _forward(_fwd_setup_inputs(0)))
    return out.shape, out.dtype

N_MICROBATCH = 1
ADAM_LR = 0.001
ADAM_B1 = 0.9
ADAM_B2 = 0.999
ADAM_EPS = 1e-08
ADAM_WD = 0.01
ADAM_STEP = 10
PER_EXAMPLE_BATCH_AXIS = {'x': 0, 'mem': 0, 'loss_target': 0}
SHARED_INPUTS = []
_WEIGHT_DTYPES = {'mix_norm_e': _jnp.float32, 'w_in_e': _jnp.float32, 'fox_f_bias': _jnp.float32, 'gmlp_ln_g': _jnp.float32, 'gmlp_ln_b': _jnp.float32, 'gmlp_w_s': _jnp.float32, 'gmlp_b_s': _jnp.float32, 'w_out_e': _jnp.float32, 'mix_norm_o': _jnp.float32, 'conv_w_in': _jnp.float32, 'conv_b_in': _jnp.float32, 'conv_dw_w': _jnp.float32, 'conv_dw_b': _jnp.float32, 'conv_ln_g': _jnp.float32, 'conv_ln_b': _jnp.float32, 'conv_w_out': _jnp.float32, 'conv_b_out': _jnp.float32, 'xa_norm': _jnp.float32, 'mem_norm': _jnp.float32, 'xa_wq': _jnp.float32, 'xa_wkv': _jnp.float32, 'xa_wo': _jnp.float32, 'ffn_norm': _jnp.float32, 'ffn_w_gu': _jnp.float32, 'ffn_w_down': _jnp.float32, 'final_norm': _jnp.float32}
MOMENT_SCALE = {'mix_norm_e': 2.090941e-01, 'w_in_e': 1.324646e-01, 'fox_f_bias': 6.487675e-01, 'gmlp_ln_g': 1.184555e-01, 'gmlp_ln_b': 1.243459e-01, 'gmlp_w_s': 8.242333e-02, 'gmlp_b_s': 1.185399e-01, 'w_out_e': 1.647419e-01, 'mix_norm_o': 1.311354e-01, 'conv_w_in': 8.699073e-02, 'conv_b_in': 1.241732e-01, 'conv_dw_w': 1.152217e-01, 'conv_dw_b': 2.410505e-01, 'conv_ln_g': 1.472912e-01, 'conv_ln_b': 1.482808e-01, 'conv_w_out': 1.133047e-01, 'conv_b_out': 2.218789e-01, 'xa_norm': 2.516829e-02, 'mem_norm': 3.466447e-02, 'xa_wq': 2.323032e-02, 'xa_wkv': 2.338346e-02, 'xa_wo': 2.351058e-02, 'ffn_norm': 1.603446e-01, 'ffn_w_gu': 6.732130e-02, 'ffn_w_down': 1.099907e-01, 'final_norm': 6.402116e+01}


def _to_microbatches(a, axis):
    t = _jnp.moveaxis(a, axis, 0)
    t = t.reshape((N_MICROBATCH, t.shape[0] // N_MICROBATCH) + t.shape[1:])
    return _jnp.moveaxis(t, 1, axis + 1)


def setup_inputs(seed: int = 0) -> dict:
    inp = _fwd_setup_inputs(seed)
    key = _jax.random.fold_in(_jax.random.key(seed), 7919)
    shape, _ = _output_shape()
    out = dict(inp)
    out["loss_target"] = _jax.random.normal(_jax.random.fold_in(key, 0), shape, _jnp.float32)
    for i, name in enumerate(TWIN_WEIGHTS):
        w = inp[name].astype(_jnp.float32)
        if MOMENT_SCALE is None:
            s = _jnp.sqrt(_jnp.mean(_jnp.square(w)) + 1e-30)
        else:
            s = MOMENT_SCALE[name]
        km, kv = _jax.random.split(_jax.random.fold_in(key, i + 1))
        out[name] = w
        out["m_" + name] = s * _jax.random.normal(km, w.shape, _jnp.float32)
        out["v_" + name] = (s * s) * _jax.random.uniform(kv, w.shape, _jnp.float32, 0.5, 1.5)
    if N_MICROBATCH > 1:
        for name, axis in PER_EXAMPLE_BATCH_AXIS.items():
            out[name] = _to_microbatches(out[name], axis)
    return {'x': out['x'], 'mem': out['mem'], 'mix_norm_e': out['mix_norm_e'], 'w_in_e': out['w_in_e'], 'fox_f_bias': out['fox_f_bias'], 'gmlp_ln_g': out['gmlp_ln_g'], 'gmlp_ln_b': out['gmlp_ln_b'], 'gmlp_w_s': out['gmlp_w_s'], 'gmlp_b_s': out['gmlp_b_s'], 'w_out_e': out['w_out_e'], 'mix_norm_o': out['mix_norm_o'], 'conv_w_in': out['conv_w_in'], 'conv_b_in': out['conv_b_in'], 'conv_dw_w': out['conv_dw_w'], 'conv_dw_b': out['conv_dw_b'], 'conv_ln_g': out['conv_ln_g'], 'conv_ln_b': out['conv_ln_b'], 'conv_w_out': out['conv_w_out'], 'conv_b_out': out['conv_b_out'], 'xa_norm': out['xa_norm'], 'mem_norm': out['mem_norm'], 'xa_wq': out['xa_wq'], 'xa_wkv': out['xa_wkv'], 'xa_wo': out['xa_wo'], 'ffn_norm': out['ffn_norm'], 'ffn_w_gu': out['ffn_w_gu'], 'ffn_w_down': out['ffn_w_down'], 'final_norm': out['final_norm'], 'loss_target': out['loss_target'], 'm_mix_norm_e': out['m_mix_norm_e'], 'm_w_in_e': out['m_w_in_e'], 'm_fox_f_bias': out['m_fox_f_bias'], 'm_gmlp_ln_g': out['m_gmlp_ln_g'], 'm_gmlp_ln_b': out['m_gmlp_ln_b'], 'm_gmlp_w_s': out['m_gmlp_w_s'], 'm_gmlp_b_s': out['m_gmlp_b_s'], 'm_w_out_e': out['m_w_out_e'], 'm_mix_norm_o': out['m_mix_norm_o'], 'm_conv_w_in': out['m_conv_w_in'], 'm_conv_b_in': out['m_conv_b_in'], 'm_conv_dw_w': out['m_conv_dw_w'], 'm_conv_dw_b': out['m_conv_dw_b'], 'm_conv_ln_g': out['m_conv_ln_g'], 'm_conv_ln_b': out['m_conv_ln_b'], 'm_conv_w_out': out['m_conv_w_out'], 'm_conv_b_out': out['m_conv_b_out'], 'm_xa_norm': out['m_xa_norm'], 'm_mem_norm': out['m_mem_norm'], 'm_xa_wq': out['m_xa_wq'], 'm_xa_wkv': out['m_xa_wkv'], 'm_xa_wo': out['m_xa_wo'], 'm_ffn_norm': out['m_ffn_norm'], 'm_ffn_w_gu': out['m_ffn_w_gu'], 'm_ffn_w_down': out['m_ffn_w_down'], 'm_final_norm': out['m_final_norm'], 'v_mix_norm_e': out['v_mix_norm_e'], 'v_w_in_e': out['v_w_in_e'], 'v_fox_f_bias': out['v_fox_f_bias'], 'v_gmlp_ln_g': out['v_gmlp_ln_g'], 'v_gmlp_ln_b': out['v_gmlp_ln_b'], 'v_gmlp_w_s': out['v_gmlp_w_s'], 'v_gmlp_b_s': out['v_gmlp_b_s'], 'v_w_out_e': out['v_w_out_e'], 'v_mix_norm_o': out['v_mix_norm_o'], 'v_conv_w_in': out['v_conv_w_in'], 'v_conv_b_in': out['v_conv_b_in'], 'v_conv_dw_w': out['v_conv_dw_w'], 'v_conv_dw_b': out['v_conv_dw_b'], 'v_conv_ln_g': out['v_conv_ln_g'], 'v_conv_ln_b': out['v_conv_ln_b'], 'v_conv_w_out': out['v_conv_w_out'], 'v_conv_b_out': out['v_conv_b_out'], 'v_xa_norm': out['v_xa_norm'], 'v_mem_norm': out['v_mem_norm'], 'v_xa_wq': out['v_xa_wq'], 'v_xa_wkv': out['v_xa_wkv'], 'v_xa_wo': out['v_xa_wo'], 'v_ffn_norm': out['v_ffn_norm'], 'v_ffn_w_gu': out['v_ffn_w_gu'], 'v_ffn_w_down': out['v_ffn_w_down'], 'v_final_norm': out['v_final_norm']}


def _loss(weights, diff, rest, loss_target):
    with _jax.named_scope("forward"):
        args = {**rest, TWIN_DIFF_INPUT: diff, **{k: w.astype(_WEIGHT_DTYPES[k]) for k, w in weights.items()}}
        y = _forward(args)
    with _jax.named_scope("loss_head"):
        err = _jnp.square(y.astype(_jnp.float32) - loss_target)
        return 0.5 * _jnp.sum(_jnp.mean(err, axis=-1)) if err.ndim else 0.5 * err


def _adamw(w, g, m, v):
    m = ADAM_B1 * m + (1.0 - ADAM_B1) * g
    v = ADAM_B2 * v + (1.0 - ADAM_B2) * _jnp.square(g)
    m_hat = m / (1.0 - ADAM_B1 ** ADAM_STEP)
    v_hat = v / (1.0 - ADAM_B2 ** ADAM_STEP)
    delta = -ADAM_LR * (m_hat / (_jnp.sqrt(v_hat) + ADAM_EPS) + ADAM_WD * w)
    return delta, m, v


def reference(x, mem, mix_norm_e, w_in_e, fox_f_bias, gmlp_ln_g, gmlp_ln_b, gmlp_w_s, gmlp_b_s, w_out_e, mix_norm_o, conv_w_in, conv_b_in, conv_dw_w, conv_dw_b, conv_ln_g, conv_ln_b, conv_w_out, conv_b_out, xa_norm, mem_norm, xa_wq, xa_wkv, xa_wo, ffn_norm, ffn_w_gu, ffn_w_down, final_norm, loss_target, m_mix_norm_e, m_w_in_e, m_fox_f_bias, m_gmlp_ln_g, m_gmlp_ln_b, m_gmlp_w_s, m_gmlp_b_s, m_w_out_e, m_mix_norm_o, m_conv_w_in, m_conv_b_in, m_conv_dw_w, m_conv_dw_b, m_conv_ln_g, m_conv_ln_b, m_conv_w_out, m_conv_b_out, m_xa_norm, m_mem_norm, m_xa_wq, m_xa_wkv, m_xa_wo, m_ffn_norm, m_ffn_w_gu, m_ffn_w_down, m_final_norm, v_mix_norm_e, v_w_in_e, v_fox_f_bias, v_gmlp_ln_g, v_gmlp_ln_b, v_gmlp_w_s, v_gmlp_b_s, v_w_out_e, v_mix_norm_o, v_conv_w_in, v_conv_b_in, v_conv_dw_w, v_conv_dw_b, v_conv_ln_g, v_conv_ln_b, v_conv_w_out, v_conv_b_out, v_xa_norm, v_mem_norm, v_xa_wq, v_xa_wkv, v_xa_wo, v_ffn_norm, v_ffn_w_gu, v_ffn_w_down, v_final_norm):
    given = dict(x=x, mem=mem, mix_norm_e=mix_norm_e, w_in_e=w_in_e, fox_f_bias=fox_f_bias, gmlp_ln_g=gmlp_ln_g, gmlp_ln_b=gmlp_ln_b, gmlp_w_s=gmlp_w_s, gmlp_b_s=gmlp_b_s, w_out_e=w_out_e, mix_norm_o=mix_norm_o, conv_w_in=conv_w_in, conv_b_in=conv_b_in, conv_dw_w=conv_dw_w, conv_dw_b=conv_dw_b, conv_ln_g=conv_ln_g, conv_ln_b=conv_ln_b, conv_w_out=conv_w_out, conv_b_out=conv_b_out, xa_norm=xa_norm, mem_norm=mem_norm, xa_wq=xa_wq, xa_wkv=xa_wkv, xa_wo=xa_wo, ffn_norm=ffn_norm, ffn_w_gu=ffn_w_gu, ffn_w_down=ffn_w_down, final_norm=final_norm, loss_target=loss_target, m_mix_norm_e=m_mix_norm_e, m_w_in_e=m_w_in_e, m_fox_f_bias=m_fox_f_bias, m_gmlp_ln_g=m_gmlp_ln_g, m_gmlp_ln_b=m_gmlp_ln_b, m_gmlp_w_s=m_gmlp_w_s, m_gmlp_b_s=m_gmlp_b_s, m_w_out_e=m_w_out_e, m_mix_norm_o=m_mix_norm_o, m_conv_w_in=m_conv_w_in, m_conv_b_in=m_conv_b_in, m_conv_dw_w=m_conv_dw_w, m_conv_dw_b=m_conv_dw_b, m_conv_ln_g=m_conv_ln_g, m_conv_ln_b=m_conv_ln_b, m_conv_w_out=m_conv_w_out, m_conv_b_out=m_conv_b_out, m_xa_norm=m_xa_norm, m_mem_norm=m_mem_norm, m_xa_wq=m_xa_wq, m_xa_wkv=m_xa_wkv, m_xa_wo=m_xa_wo, m_ffn_norm=m_ffn_norm, m_ffn_w_gu=m_ffn_w_gu, m_ffn_w_down=m_ffn_w_down, m_final_norm=m_final_norm, v_mix_norm_e=v_mix_norm_e, v_w_in_e=v_w_in_e, v_fox_f_bias=v_fox_f_bias, v_gmlp_ln_g=v_gmlp_ln_g, v_gmlp_ln_b=v_gmlp_ln_b, v_gmlp_w_s=v_gmlp_w_s, v_gmlp_b_s=v_gmlp_b_s, v_w_out_e=v_w_out_e, v_mix_norm_o=v_mix_norm_o, v_conv_w_in=v_conv_w_in, v_conv_b_in=v_conv_b_in, v_conv_dw_w=v_conv_dw_w, v_conv_dw_b=v_conv_dw_b, v_conv_ln_g=v_conv_ln_g, v_conv_ln_b=v_conv_ln_b, v_conv_w_out=v_conv_w_out, v_conv_b_out=v_conv_b_out, v_xa_norm=v_xa_norm, v_mem_norm=v_mem_norm, v_xa_wq=v_xa_wq, v_xa_wkv=v_xa_wkv, v_xa_wo=v_xa_wo, v_ffn_norm=v_ffn_norm, v_ffn_w_gu=v_ffn_w_gu, v_ffn_w_down=v_ffn_w_down, v_final_norm=v_final_norm)
    weights = {n: given[n] for n in TWIN_WEIGHTS}
    shared = {n: given[n] for n in SHARED_INPUTS}
    per_example = {n: given[n] for n in ['x', 'mem']}
    grad_fn = _jax.value_and_grad(_loss, argnums=(0, 1))

    def one_microbatch(ex, loss_target):
        ex = dict(ex)
        diff = ex.pop(TWIN_DIFF_INPUT)
        return grad_fn(weights, diff, {**shared, **ex}, loss_target)

    if N_MICROBATCH == 1:
        loss, (grad_w, grad_x) = one_microbatch(per_example, given["loss_target"])
    else:
        def body(carry, xs):
            loss_sum, grad_sum = carry
            l_k, (gw_k, gx_k) = one_microbatch(xs[0], xs[1])
            with _jax.named_scope("update"):
                return (loss_sum + l_k, _jax.tree.map(_jnp.add, grad_sum, gw_k)), gx_k

        init = (_jnp.zeros((), _jnp.float32), _jax.tree.map(_jnp.zeros_like, weights))
        (loss, grad_w), grad_x = _jax.lax.scan(body, init, (per_example, given["loss_target"]))
    with _jax.named_scope("update"):
        delta_w, new_m, new_v = {}, {}, {}
        for n in TWIN_WEIGHTS:
            delta_w[n], new_m[n], new_v[n] = _adamw(weights[n], grad_w[n], given["m_" + n], given["v_" + n])
    return (loss, grad_x, *[grad_w[n] for n in TWIN_WEIGHTS], *[delta_w[n] for n in TWIN_WEIGHTS],
            *[new_m[n] for n in TWIN_WEIGHTS], *[new_v[n] for n in TWIN_WEIGHTS])
```

```python
import functools
import math

import jax
import jax.numpy as jnp
from jax import lax
from jax.experimental import pallas as pl
from jax.experimental.pallas import tpu as pltpu

F32 = jnp.float32
BF = jnp.bfloat16
MESH = pl.DeviceIdType.MESH

D = 1024
FOXW = 512
HD = 64
GW = 512
CH = 128
NG = 8
FF = 2816
NMEM = 256
XH = 4
XD = 256
CK = 31
HALO = 32
EPS = 1e-6
IN_W = 2568
IN_WP = 2688
VMEM_LIMIT = 56 * 1024 * 1024

ADAM_LR, ADAM_B1, ADAM_B2, ADAM_EPS, ADAM_WD, ADAM_STEP = 0.001, 0.9, 0.999, 1e-08, 0.01, 10

WEIGHTS = ['mix_norm_e', 'w_in_e', 'fox_f_bias', 'gmlp_ln_g', 'gmlp_ln_b', 'gmlp_w_s', 'gmlp_b_s', 'w_out_e',
           'mix_norm_o', 'conv_w_in', 'conv_b_in', 'conv_dw_w', 'conv_dw_b', 'conv_ln_g', 'conv_ln_b',
           'conv_w_out', 'conv_b_out', 'xa_norm', 'mem_norm', 'xa_wq', 'xa_wkv', 'xa_wo', 'ffn_norm',
           'ffn_w_gu', 'ffn_w_down', 'final_norm']
SHARD_AXIS = {'mix_norm_e': None, 'w_in_e': 2, 'fox_f_bias': None, 'gmlp_ln_g': None, 'gmlp_ln_b': None,
              'gmlp_w_s': None, 'gmlp_b_s': None, 'w_out_e': 1, 'mix_norm_o': 1, 'conv_w_in': 2, 'conv_b_in': 1,
              'conv_dw_w': 2, 'conv_dw_b': 1, 'conv_ln_g': 1, 'conv_ln_b': 1, 'conv_w_out': 1, 'conv_b_out': 1,
              'xa_norm': None, 'mem_norm': None, 'xa_wq': 1, 'xa_wkv': 2, 'xa_wo': 1, 'ffn_norm': None,
              'ffn_w_gu': 2, 'ffn_w_down': 1, 'final_norm': None}
BIG = ['w_in_e', 'w_out_e', 'conv_w_in', 'conv_w_out', 'xa_wq', 'xa_wkv', 'xa_wo', 'ffn_w_gu', 'ffn_w_down']


def _pcall(body, **kw):
    return pl.pallas_call(body, **kw)


def _params(sem=None, **kw):
    return pltpu.CompilerParams(dimension_semantics=sem, vmem_limit_bytes=VMEM_LIMIT, **kw)


def _dot(a, b, dims):
    dn = {'nn': (((1,), (0,)), ((), ())), 'nt': (((1,), (1,)), ((), ())), 'tn': (((0,), (0,)), ((), ()))}[dims]
    return lax.dot_general(a, b, dn, preferred_element_type=F32)


def _sigmoid(x):
    return 1.0 / (1.0 + jnp.exp(-x))


def _rms_stats(xv):
    return lax.rsqrt(jnp.mean(xv * xv, axis=-1, keepdims=True) + EPS)


def _rms_bwd(xv, gain, dh):
    r = _rms_stats(xv)
    t = dh * gain
    dx = r * t - xv * (r * r * r * jnp.mean(t * xv, axis=-1, keepdims=True))
    return dx, dh * xv * r


def _fused_mm(name, *, dims, M, N, bm, bn, groups, epi, outs, x=None, gain=None, tiles=(), rows=(),
              h_out=False, reds=()):
    bm = min(bm, M)
    nI, nJ = M // bm, N // bn
    assert nI * bm == M and nJ * bn == N
    assert not reds or nJ == 1
    arrays, specs = [], []

    def add(arr, spec):
        arrays.append(arr)
        specs.append(spec)
        return len(arrays) - 1

    if x is not None:
        K0 = x.shape[1]
        add(x, pl.BlockSpec((bm, K0), lambda i, j: (i, 0)))
        add(gain, pl.BlockSpec((1, K0), lambda i, j: (0, 0)))
    plan = []
    for grp in groups:
        g = []
        for p in grp:
            ai = None
            if p['A'] is not None:
                ai = add(p['A'], pl.BlockSpec((bm, p['Ka']), lambda i, j, o=p.get('acoff', 0): (i, o)))
            ro, co = p.get('roff', 0), p.get('coff', 0)
            if dims == 'nn':
                bi = add(p['B'], pl.BlockSpec((p['Ka'], bn), lambda i, j, ro=ro, co=co: (ro, j + co)))
            else:
                bi = add(p['B'], pl.BlockSpec((bn, p['Ka']), lambda i, j, ro=ro, co=co: (j + ro, co)))
            g.append((ai, bi))
        plan.append(g)
    tile_idx = [add(a, pl.BlockSpec((bm, bn), lambda i, j, o=o: (i, j + o))) for a, o in tiles]
    row_idx = [add(a, pl.BlockSpec((1, bn), lambda i, j, o=o: (0, j + o))) for a, o in rows]
    n_in = len(arrays)

    out_shape = [jax.ShapeDtypeStruct((M, N), dt) for dt in outs]
    out_specs = [pl.BlockSpec((bm, bn), lambda i, j: (i, j)) for _ in outs]
    if h_out:
        out_shape.append(jax.ShapeDtypeStruct((M, x.shape[1]), BF))
        out_specs.append(pl.BlockSpec((bm, x.shape[1]), lambda i, j: (i, 0)))
    for shp in reds:
        out_shape.append(jax.ShapeDtypeStruct(shp, F32))
        out_specs.append(pl.BlockSpec(shp, lambda i, j: (0, 0)))
    n_main = len(outs)
    scratch = [pltpu.VMEM((bm, x.shape[1]), BF)] if x is not None else []

    def body(*refs):
        ins, out_refs, scr = refs[:n_in], refs[n_in:n_in + len(out_shape)], refs[n_in + len(out_shape):]
        i, j = pl.program_id(0), pl.program_id(1)
        if x is not None:
            hn_ref = scr[0]

            @pl.when(j == 0)
            def _():
                xv = ins[0][...]
                hn = (xv * _rms_stats(xv) * ins[1][...]).astype(BF)
                hn_ref[...] = hn
                if h_out:
                    out_refs[n_main][...] = hn

        accs = []
        for g in plan:
            acc = None
            for ai, bi in g:
                a = hn_ref[...] if ai is None else ins[ai][...]
                if a.dtype != BF:
                    a = a.astype(BF)
                d = _dot(a, ins[bi][...], dims)
                acc = d if acc is None else acc + d
            accs.append(acc)
        out_vals, red_vals = epi(accs, [ins[t][...] for t in tile_idx], [ins[r][...] for r in row_idx])
        for r, v in zip(out_refs[:n_main], out_vals):
            r[...] = v.astype(r.dtype)
        if reds:
            red_refs = out_refs[n_main + (1 if h_out else 0):]

            @pl.when(i == 0)
            def _():
                for r in red_refs:
                    r[...] = jnp.zeros(r.shape, F32)

            for r, v in zip(red_refs, red_vals):
                r[...] += v

    res = _pcall(body, name=name, grid=(nI, nJ), in_specs=specs, out_specs=out_specs, out_shape=out_shape,
                 scratch_shapes=scratch, compiler_params=_params(("arbitrary", "arbitrary")))(*arrays)
    return res


def _epi_plain(accs, tiles, rows):
    return [accs[0]], []


def _epi_resid(accs, tiles, rows):
    y = tiles[0] + accs[0]
    if rows:
        y = y + rows[0]
    return [y], []


def _epi_swiglu(accs, tiles, rows):
    g, u = accs
    return [g, u, g * _sigmoid(g) * u], []


def _epi_glu(accs, tiles, rows):
    a, g = accs[0] + rows[0], accs[1] + rows[1]
    return [a, g, a * _sigmoid(g)], []


def _epi_swiglu_bwd(accs, tiles, rows):
    da = accs[0]
    g, u = tiles[0].astype(F32), tiles[1].astype(F32)
    sg = _sigmoid(g)
    return [da * u * (sg * (1.0 + g * (1.0 - sg))), da * (g * sg)], []


def _epi_rms_bwd(accs, tiles, rows):
    dx, dgr = _rms_bwd(tiles[0], rows[0], accs[0])
    return [tiles[1] + dx], [jnp.sum(dgr, axis=0, keepdims=True)]


def _epi_rms_gain_only(accs, tiles, rows):
    _, dgr = _rms_bwd(tiles[0], rows[0], accs[0])
    return [], [jnp.sum(dgr, axis=0, keepdims=True)]


def _norm_mm(name, x, gain, W, *, N, coff=0, bn, out_dtype, bm=512, h_out=False):
    return _fused_mm(name, dims='nn', M=x.shape[0], N=N, bm=bm, bn=bn, x=x, gain=gain,
                     groups=[[dict(A=None, Ka=x.shape[1], B=W, coff=coff)]], epi=_epi_plain, outs=[out_dtype],
                     h_out=h_out)


def _mm_resid(name, pairs, resid, bias=None, bm=512):
    M = resid.shape[0]
    return _fused_mm(name, dims='nn', M=M, N=D, bm=bm, bn=D, groups=[pairs], epi=_epi_resid, outs=[F32],
                     tiles=[(resid, 0)], rows=[(bias, 0)] if bias is not None else [])[0]


def _mm_nt_plain(name, dy, W, bm=512):
    return _fused_mm(name, dims='nt', M=dy.shape[0], N=W.shape[0], bm=bm, bn=W.shape[0],
                     groups=[[dict(A=dy, Ka=dy.shape[1], B=W)]], epi=_epi_plain, outs=[BF])[0]


def _mm_nt_rms_bwd(name, pairs, x, gain, dx_in, bm=256):
    out = _fused_mm(name, dims='nt', M=x.shape[0], N=D, bm=bm, bn=D, groups=[pairs], epi=_epi_rms_bwd,
                    outs=[F32], tiles=[(x, 0), (dx_in, 0)], rows=[(gain, 0)], reds=[(1, D)])
    return out[0], out[1]


def _mm_tn(name, A, G, bk=1024):
    T, Ka, Kg = A.shape[0], A.shape[1], G.shape[1]
    bm = Ka if Ka <= 1024 else Ka // 2
    bn = Kg if Kg <= 1024 else Kg // 2
    if Kg == 1536:
        bn = 768
    bk = min(bk, T)
    nI, nJ, nK = Ka // bm, Kg // bn, T // bk

    def body(a_ref, g_ref, o_ref):
        @pl.when(pl.program_id(2) == 0)
        def _():
            o_ref[...] = jnp.zeros(o_ref.shape, F32)

        o_ref[...] += _dot(a_ref[...].astype(BF), g_ref[...].astype(BF), 'tn')

    return _pcall(body, name=name, grid=(nI, nJ, nK),
                  in_specs=[pl.BlockSpec((bk, bm), lambda i, j, k: (k, i)),
                            pl.BlockSpec((bk, bn), lambda i, j, k: (k, j))],
                  out_specs=pl.BlockSpec((bm, bn), lambda i, j, k: (i, j)),
                  out_shape=jax.ShapeDtypeStruct((Ka, Kg), F32),
                  compiler_params=_params(("arbitrary", "arbitrary", "arbitrary")))(A, G)


def _colsum(name, a, bt=512):
    M, N = a.shape
    bt = min(bt, M)

    def body(a_ref, o_ref):
        @pl.when(pl.program_id(0) == 0)
        def _():
            o_ref[...] = jnp.zeros(o_ref.shape, F32)

        o_ref[...] += jnp.sum(a_ref[...].astype(F32), axis=0, keepdims=True)

    return _pcall(body, name=name, grid=(M // bt,), in_specs=[pl.BlockSpec((bt, N), lambda i: (i, 0))],
                  out_specs=pl.BlockSpec((1, N), lambda i: (0, 0)), out_shape=jax.ShapeDtypeStruct((1, N), F32),
                  compiler_params=_params(("arbitrary",)))(a)


def _cumsum_rows(v):
    T = v.shape[0]
    row = lax.broadcasted_iota(jnp.int32, v.shape, 0)
    s = 1
    while s < T:
        v = v + jnp.where(row >= s, pltpu.roll(v, s, 0), 0.0)
        s *= 2
    return v


def _log_sigmoid(z):
    return jnp.minimum(z, 0.0) - jnp.log(1.0 + jnp.exp(-jnp.abs(z)))


def _fox_gate_fwd(fl, fbias, B, T):
    def body(fl_ref, b_ref, o_ref):
        o_ref[...] = _cumsum_rows(_log_sigmoid(fl_ref[...] + b_ref[...]))

    return _pcall(body, name="fox_gate_fwd", grid=(B,),
                  in_specs=[pl.BlockSpec((T, 128), lambda b: (b, 0)), pl.BlockSpec((1, 128), lambda b: (0, 0))],
                  out_specs=pl.BlockSpec((T, 128), lambda b: (b, 0)),
                  out_shape=jax.ShapeDtypeStruct((B * T, 128), F32), compiler_params=_params(("arbitrary",)))(fl, fbias)


def _fox_gate_bwd(fl, fbias, dcq, dck, B, T):
    def body(fl_ref, b_ref, dcq_ref, dck_ref, dfl_ref, db_ref):
        dc = dcq_ref[...] + dck_ref[...]
        rev = jnp.sum(dc, axis=0, keepdims=True) - _cumsum_rows(dc) + dc
        dfl = rev * _sigmoid(-(fl_ref[...] + b_ref[...]))
        dfl_ref[...] = dfl

        @pl.when(pl.program_id(0) == 0)
        def _():
            db_ref[...] = jnp.zeros(db_ref.shape, F32)

        db_ref[...] += jnp.sum(dfl, axis=0, keepdims=True)

    return _pcall(body, name="fox_gate_bwd", grid=(B,),
                  in_specs=[pl.BlockSpec((T, 128), lambda b: (b, 0)), pl.BlockSpec((1, 128), lambda b: (0, 0)),
                            pl.BlockSpec((T, 128), lambda b: (b, 0)), pl.BlockSpec((T, 128), lambda b: (b, 0))],
                  out_specs=[pl.BlockSpec((T, 128), lambda b: (b, 0)), pl.BlockSpec((1, 128), lambda b: (0, 0))],
                  out_shape=[jax.ShapeDtypeStruct((B * T, 128), F32), jax.ShapeDtypeStruct((1, 128), F32)],
                  compiler_params=_params(("arbitrary",)))(fl, fbias, dcq, dck)


NEG = -1e30


def _fox_fwd(qkv, cum4, B, T, qoff, bq, bk):
    nq, nkb = T // bq, T // bk
    N = B * T

    def body(q_ref, k_ref, v_ref, cum_ref, o_ref, lse_ref):
        hp, i = pl.program_id(1), pl.program_id(2)
        row = i * bq + lax.broadcasted_iota(jnp.int32, (bq, bk), 0)
        colb = lax.broadcasted_iota(jnp.int32, (bq, bk), 1)
        lane = lax.broadcasted_iota(jnp.int32, (bq, 128), 1)
        nblk = i + 1
        lse_tile = jnp.zeros((bq, 128), F32)
        for e in range(2):
            q = q_ref[:, e * HD:(e + 1) * HD]
            r = 2 * hp + e

            def step(j, carry):
                m, l, acc = carry
                ks = pl.multiple_of(j * bk, bk)
                k = k_ref[pl.ds(ks, bk), e * HD:(e + 1) * HD]
                v = v_ref[pl.ds(ks, bk), e * HD:(e + 1) * HD]
                ck = cum_ref[0, r, pl.ds(j, 1), :]
                s = _dot(q, k, 'nt') * 0.125 - ck
                s = jnp.where(row >= colb + j * bk, s, NEG)
                m_new = jnp.maximum(m, jnp.max(s, axis=1, keepdims=True))
                p = jnp.exp(s - m_new)
                alpha = jnp.exp(m - m_new)
                l = alpha * l + jnp.sum(p, axis=1, keepdims=True)
                acc = alpha * acc + _dot(p.astype(BF), v, 'nn')
                return m_new, l, acc

            m, l, acc = lax.fori_loop(0, nblk, step, (jnp.full((bq, 1), NEG, F32), jnp.zeros((bq, 1), F32),
                                                      jnp.zeros((bq, HD), F32)))
            o_ref[:, e * HD:(e + 1) * HD] = (acc / l).astype(BF)
            lse_tile = jnp.where(lane == e, m + jnp.log(l), lse_tile)
        lse_ref[...] = lse_tile

    return _pcall(body, name="fox_fwd", grid=(B, 4, nq),
                  in_specs=[pl.BlockSpec((bq, 128), lambda b, h, i: (b * nq + i, qoff + h)),
                            pl.BlockSpec((T, 128), lambda b, h, i: (b, qoff + 4 + h)),
                            pl.BlockSpec((T, 128), lambda b, h, i: (b, qoff + 8 + h)),
                            pl.BlockSpec((1, NG, nkb, bk), lambda b, h, i: (b, 0, 0, 0))],
                  out_specs=[pl.BlockSpec((bq, 128), lambda b, h, i: (b * nq + i, h)),
                             pl.BlockSpec((bq, 128), lambda b, h, i: (b * nq + i, h))],
                  out_shape=[jax.ShapeDtypeStruct((N, FOXW), BF), jax.ShapeDtypeStruct((N, FOXW), F32)],
                  compiler_params=_params(("arbitrary", "arbitrary", "arbitrary")))(qkv, qkv, qkv, cum4)


def _fox_bwd(qkv, cum4, o, lse, dcat, B, T, qoff, bq, bk):
    nq, nkb = T // bq, T // bk
    N = B * T

    def body(q_ref, k_ref, v_ref, cum_ref, o_ref, lse_ref, do_ref, dq_ref, dk_ref, dv_ref, dcum_ref, dcq_ref,
             dq_acc, dl_ref, rs_ref):
        hp = pl.program_id(1)
        rowb = lax.broadcasted_iota(jnp.int32, (bq, bk), 0)
        colb = lax.broadcasted_iota(jnp.int32, (bq, bk), 1)
        dcq_ref[...] = jnp.zeros(dcq_ref.shape, F32)
        for e in range(2):
            sl = slice(e * HD, (e + 1) * HD)
            r = 2 * hp + e
            dl_ref[...] = jnp.sum(do_ref[:, sl].astype(F32) * o_ref[:, sl].astype(F32), axis=1, keepdims=True)
            dq_acc[...] = jnp.zeros(dq_acc.shape, F32)
            rs_ref[...] = jnp.zeros(rs_ref.shape, F32)
            for j in range(nkb):
                k = k_ref[j * bk:(j + 1) * bk, sl]
                v = v_ref[j * bk:(j + 1) * bk, sl]
                ck = cum_ref[0, r, j:j + 1, :]

                def step(i, carry):
                    dk_a, dv_a, cs = carry
                    qs = pl.multiple_of(i * bq, bq)
                    q = q_ref[pl.ds(qs, bq), sl]
                    do = do_ref[pl.ds(qs, bq), sl]
                    lse_i = lse_ref[pl.ds(qs, bq), e:e + 1]
                    dl_i = dl_ref[pl.ds(qs, bq), :]
                    s = _dot(q, k, 'nt') * 0.125 - ck
                    p = jnp.where(rowb + i * bq >= colb + j * bk, jnp.exp(s - lse_i), 0.0)
                    pb = p.astype(BF)
                    dv_a = dv_a + _dot(pb, do, 'tn')
                    dp = _dot(do, v, 'nt')
                    ds = p * (dp - dl_i)
                    cs = cs + jnp.sum(ds, axis=0, keepdims=True)
                    rs_ref[pl.ds(qs, bq), :] += jnp.sum(ds, axis=1, keepdims=True)
                    dsb = ds.astype(BF)
                    dk_a = dk_a + _dot(dsb, q, 'tn')
                    dq_acc[pl.ds(qs, bq), :] += _dot(dsb, k, 'nn')
                    return dk_a, dv_a, cs

                dk_a, dv_a, cs = lax.fori_loop((j * bk) // bq, nq, step,
                                               (jnp.zeros((bk, HD), F32), jnp.zeros((bk, HD), F32),
                                                jnp.zeros((1, bk), F32)))
                dk_ref[j * bk:(j + 1) * bk, sl] = (dk_a * 0.125).astype(BF)
                dv_ref[j * bk:(j + 1) * bk, sl] = dv_a.astype(BF)
                dcum_ref[0, e, j:j + 1, :] = -cs
            dq_ref[:, sl] = (dq_acc[...] * 0.125).astype(BF)
            dcq_ref[:, e:e + 1] = rs_ref[...]

    seq = lambda off: pl.BlockSpec((T, 128), lambda b, h, off=off: (b, off + h))
    return _pcall(body, name="fox_bwd", grid=(B, 4),
                  in_specs=[seq(qoff), seq(qoff + 4), seq(qoff + 8),
                            pl.BlockSpec((1, NG, nkb, bk), lambda b, h: (b, 0, 0, 0)),
                            seq(0), seq(0), seq(0)],
                  out_specs=[seq(0), seq(0), seq(0), pl.BlockSpec((1, 2, nkb, bk), lambda b, h: (b, h, 0, 0)), seq(0)],
                  out_shape=[jax.ShapeDtypeStruct((N, FOXW), BF)] * 3 + [jax.ShapeDtypeStruct((B, NG, nkb, bk), F32),
                                                                         jax.ShapeDtypeStruct((N, FOXW), F32)],
                  scratch_shapes=[pltpu.VMEM((T, HD), F32), pltpu.VMEM((T, 1), F32), pltpu.VMEM((T, 1), F32)],
                  compiler_params=_params(("arbitrary", "arbitrary")))(qkv, qkv, qkv, cum4, o, lse, dcat)


_GC = math.sqrt(2.0 / math.pi)
_GA = 0.044715


def _gelu(z):
    return 0.5 * z * (1.0 + jnp.tanh(_GC * (z + _GA * z * z * z)))


def _gelu_grad(z):
    t = jnp.tanh(_GC * (z + _GA * z * z * z))
    return 0.5 * (1.0 + t) + 0.5 * z * (1.0 - t * t) * (_GC * (1.0 + 3.0 * _GA * z * z))


def _gmlp_common(z, lng, lnb):
    zg = _gelu(z)
    u, vg = zg[:, :GW], zg[:, GW:]
    mu = jnp.mean(vg, axis=-1, keepdims=True)
    xc = vg - mu
    rstd = lax.rsqrt(jnp.mean(xc * xc, axis=-1, keepdims=True) + EPS)
    xhat = xc * rstd
    return u, xhat, rstd, xhat * lng + lnb


def _tril_w(ws_ref):
    tri = lax.broadcasted_iota(jnp.int32, (CH, CH), 0) >= lax.broadcasted_iota(jnp.int32, (CH, CH), 1)
    return [jnp.where(tri, ws_ref[g], 0.0).astype(BF) for g in range(NG)], tri


def _split_pair(vp):
    lane = lax.broadcasted_iota(jnp.int32, vp.shape, 1)
    zero = jnp.zeros(vp.shape, vp.dtype)
    return jnp.concatenate([jnp.where(lane < HD, vp, zero), jnp.where(lane >= HD, vp, zero)], axis=0)


def _gmlp_mix(wt, vgn_b):
    outs = []
    for p in range(NG // 2):
        wcat = jnp.concatenate([wt[2 * p], wt[2 * p + 1]], axis=1)
        outs.append(_dot(wcat, _split_pair(vgn_b[:, 128 * p:128 * (p + 1)]), 'nn'))
    return jnp.concatenate(outs, axis=1)


def _gmlp_fwd(z, lng, lnb, ws, bfull, bt):
    N = z.shape[0]

    def body(z_ref, lng_ref, lnb_ref, ws_ref, bf_ref, o_ref):
        wt, _ = _tril_w(ws_ref)
        for c in range(bt // CH):
            rows = slice(c * CH, (c + 1) * CH)
            u, _, _, vgn = _gmlp_common(z_ref[rows, :], lng_ref[...], lnb_ref[...])
            mixed = _gmlp_mix(wt, vgn.astype(BF)) + bf_ref[...]
            o_ref[rows, :] = (u * mixed).astype(BF)

    full = lambda shp: pl.BlockSpec(shp, lambda i: (0,) * len(shp))
    return _pcall(body, name="gmlp_fwd", grid=(N // bt,),
                  in_specs=[pl.BlockSpec((bt, D), lambda i: (i, 0)), full((1, GW)), full((1, GW)),
                            full((NG, CH, CH)), full((CH, GW))],
                  out_specs=pl.BlockSpec((bt, GW), lambda i: (i, 0)), out_shape=jax.ShapeDtypeStruct((N, GW), BF),
                  compiler_params=_params(("arbitrary",)))(z, lng, lnb, ws, bfull)


def _gmlp_bwd(z, dcat, lng, lnb, ws, bfull, bt):
    N = z.shape[0]

    def body(z_ref, da_ref, lng_ref, lnb_ref, ws_ref, bf_ref, dz_ref, dg_ref, db_ref, dws_ref, dbf_ref):
        @pl.when(pl.program_id(0) == 0)
        def _():
            for r in (dg_ref, db_ref, dws_ref, dbf_ref):
                r[...] = jnp.zeros(r.shape, F32)

        wt, tri = _tril_w(ws_ref)
        lane = lax.broadcasted_iota(jnp.int32, (CH, 128), 1)
        for c in range(bt // CH):
            rows = slice(c * CH, (c + 1) * CH)
            zc = z_ref[rows, :]
            u, xhat, rstd, vgn = _gmlp_common(zc, lng_ref[...], lnb_ref[...])
            vgn_b = vgn.astype(BF)
            mixed = _gmlp_mix(wt, vgn_b) + bf_ref[...]
            da = da_ref[rows, :].astype(F32)
            dmix = da * u
            du = da * mixed
            dbf_ref[...] += dmix
            dvs = []
            for p in range(NG // 2):
                cols = slice(128 * p, 128 * (p + 1))
                dmp = dmix[:, cols].astype(BF)
                dwp = _dot(_split_pair(dmp), vgn_b[:, cols], 'nt')
                dws_ref[2 * p] += jnp.where(tri, dwp[:CH], 0.0)
                dws_ref[2 * p + 1] += jnp.where(tri, dwp[CH:], 0.0)
                dvs.append(jnp.where(lane < HD, _dot(wt[2 * p], dmp, 'tn'), _dot(wt[2 * p + 1], dmp, 'tn')))
            dvgn = jnp.concatenate(dvs, axis=1)
            dg_ref[...] += jnp.sum(dvgn * xhat, axis=0, keepdims=True)
            db_ref[...] += jnp.sum(dvgn, axis=0, keepdims=True)
            dxh = dvgn * lng_ref[...]
            dvg = rstd * (dxh - jnp.mean(dxh, axis=-1, keepdims=True)
                          - xhat * jnp.mean(dxh * xhat, axis=-1, keepdims=True))
            dz_ref[rows, :] = (jnp.concatenate([du, dvg], axis=1) * _gelu_grad(zc)).astype(BF)

    full = lambda shp: pl.BlockSpec(shp, lambda i: (0,) * len(shp))
    return _pcall(body, name="gmlp_bwd", grid=(N // bt,),
                  in_specs=[pl.BlockSpec((bt, D), lambda i: (i, 0)), pl.BlockSpec((bt, GW), lambda i: (i, 1)),
                            full((1, GW)), full((1, GW)), full((NG, CH, CH)), full((CH, GW))],
                  out_specs=[pl.BlockSpec((bt, D), lambda i: (i, 0)), full((1, GW)), full((1, GW)),
                             full((NG, CH, CH)), full((CH, GW))],
                  out_shape=[jax.ShapeDtypeStruct((N, D), BF), jax.ShapeDtypeStruct((1, GW), F32),
                             jax.ShapeDtypeStruct((1, GW), F32), jax.ShapeDtypeStruct((NG, CH, CH), F32),
                             jax.ShapeDtypeStruct((CH, GW), F32)],
                  compiler_params=_params(("arbitrary",)))(z, dcat, lng, lnb, ws, bfull)


def _group_sum(name, a):
    def body(a_ref, o_ref):
        lane = lax.broadcasted_iota(jnp.int32, (CH, 128), 1)
        out = jnp.zeros((CH, 128), F32)
        for g in range(NG):
            out = jnp.where(lane == g, jnp.sum(a_ref[:, g * HD:(g + 1) * HD], axis=1, keepdims=True), out)
        o_ref[...] = out

    return _pcall(body, name=name, out_shape=jax.ShapeDtypeStruct((CH, 128), F32))(a)


def _xattn_softmax(q_h, k_h):
    s = _dot(q_h, k_h, 'nt') * (XD ** -0.5)
    p = jnp.exp(s - jnp.max(s, axis=1, keepdims=True))
    return p / jnp.sum(p, axis=1, keepdims=True)


def _xattn_fwd(name, q, kv, B, T, bq):
    nq = T // bq

    def body(q_ref, kv_ref, o_ref):
        for h in range(XH):
            cols = slice(h * XD, (h + 1) * XD)
            p = _xattn_softmax(q_ref[:, cols], kv_ref[:, cols])
            o_ref[:, cols] = _dot(p.astype(BF), kv_ref[:, D + h * XD:D + (h + 1) * XD], 'nn').astype(BF)

    return _pcall(body, name=name, grid=(B, nq),
                  in_specs=[pl.BlockSpec((bq, D), lambda b, i: (b * nq + i, 0)),
                            pl.BlockSpec((NMEM, 2 * D), lambda b, i: (b, 0))],
                  out_specs=pl.BlockSpec((bq, D), lambda b, i: (b * nq + i, 0)),
                  out_shape=jax.ShapeDtypeStruct((B * T, D), BF), compiler_params=_params(("arbitrary", "arbitrary")))(q, kv)


def _xattn_bwd(name, q, kv, do, B, T, bq):
    nq = T // bq
    sc = XD ** -0.5

    def body(q_ref, kv_ref, do_ref, dq_ref, dkv_ref):
        @pl.when(pl.program_id(1) == 0)
        def _():
            dkv_ref[...] = jnp.zeros(dkv_ref.shape, F32)

        for h in range(XH):
            cols = slice(h * XD, (h + 1) * XD)
            vcols = slice(D + h * XD, D + (h + 1) * XD)
            qh, kh, doh = q_ref[:, cols], kv_ref[:, cols], do_ref[:, cols]
            p = _xattn_softmax(qh, kh)
            dp = _dot(doh, kv_ref[:, vcols], 'nt')
            ds = p * (dp - jnp.sum(p * dp, axis=1, keepdims=True))
            dsb = ds.astype(BF)
            dq_ref[:, cols] = (_dot(dsb, kh, 'nn') * sc).astype(BF)
            dkv_ref[:, cols] += _dot(dsb, qh, 'tn') * sc
            dkv_ref[:, vcols] += _dot(p.astype(BF), doh, 'tn')

    blk = pl.BlockSpec((bq, D), lambda b, i: (b * nq + i, 0))
    return _pcall(body, name=name, grid=(B, nq),
                  in_specs=[blk, pl.BlockSpec((NMEM, 2 * D), lambda b, i: (b, 0)), blk],
                  out_specs=[blk, pl.BlockSpec((NMEM, 2 * D), lambda b, i: (b, 0))],
                  out_shape=[jax.ShapeDtypeStruct((B * T, D), BF), jax.ShapeDtypeStruct((B * NMEM, 2 * D), F32)],
                  compiler_params=_params(("arbitrary", "arbitrary")))(q, kv, do)


def _ln_stats(v):
    mu = jnp.mean(v, axis=-1, keepdims=True)
    xc = v - mu
    rstd = lax.rsqrt(jnp.mean(xc * xc, axis=-1, keepdims=True) + EPS)
    return xc * rstd, rstd


def _conv_fwd(y, w32, wb, lng, lnb, B, T, bt):
    nt = T // bt
    hb = bt // HALO

    def body(y_ref, yp_ref, w_ref, wb_ref, lng_ref, lnb_ref, s_ref, yc_ref, win):
        i = pl.program_id(1)
        win[0:HALO, :] = jnp.where(i > 0, yp_ref[...], 0.0)
        win[HALO:, :] = y_ref[...]
        acc = jnp.zeros((bt, D), F32) + wb_ref[...]
        for j in range(CK):
            acc = acc + w_ref[j:j + 1, :] * win[HALO - (CK - 1) + j:HALO - (CK - 1) + j + bt, :]
        yc_ref[...] = acc
        xhat, _ = _ln_stats(acc)
        ln = xhat * lng_ref[...] + lnb_ref[...]
        s_ref[...] = (ln * _sigmoid(ln)).astype(BF)

    row = lambda n: pl.BlockSpec((n, D), lambda b, i: (0, 0))
    cur = pl.BlockSpec((bt, D), lambda b, i: (b * nt + i, 0))
    return _pcall(body, name="conv_fwd", grid=(B, nt),
                  in_specs=[cur, pl.BlockSpec((HALO, D), lambda b, i: (jnp.maximum((b * nt + i) * hb - 1, 0), 0)),
                            row(HALO), row(1), row(1), row(1)],
                  out_specs=[cur, cur],
                  out_shape=[jax.ShapeDtypeStruct((B * T, D), BF), jax.ShapeDtypeStruct((B * T, D), F32)],
                  scratch_shapes=[pltpu.VMEM((bt + HALO, D), F32)],
                  compiler_params=_params(("arbitrary", "arbitrary")))(y, y, w32, wb, lng, lnb)


def _conv_bwd(ds, yc, y, pa, pg, w32, lng, lnb, B, T, bt):
    nt = T // bt
    hb = bt // HALO
    nblk32 = B * T // HALO

    def ln_bwd(dsv, ycv, lng, lnb):
        xhat, rstd = _ln_stats(ycv)
        ln = xhat * lng + lnb
        sg = _sigmoid(ln)
        dln = dsv * (sg * (1.0 + ln * (1.0 - sg)))
        dxh = dln * lng
        dyc = rstd * (dxh - jnp.mean(dxh, axis=-1, keepdims=True)
                      - xhat * jnp.mean(dxh * xhat, axis=-1, keepdims=True))
        return dyc, dln, xhat

    def body(ds_ref, dsn_ref, yc_ref, ycn_ref, y_ref, yp_ref, pa_ref, pg_ref, w_ref, lng_ref, lnb_ref,
             dpa_ref, dpg_ref, dw_ref, dwb_ref, dlng_ref, dlnb_ref, dba_ref, dbg_ref, dwin, ywin):
        i = pl.program_id(1)

        @pl.when((pl.program_id(0) == 0) & (i == 0))
        def _():
            for r in (dw_ref, dwb_ref, dlng_ref, dlnb_ref, dba_ref, dbg_ref):
                r[...] = jnp.zeros(r.shape, F32)

        lng, lnb = lng_ref[...], lnb_ref[...]
        dyc, dln, xhat = ln_bwd(ds_ref[...].astype(F32), yc_ref[...], lng, lnb)
        dycn, _, _ = ln_bwd(dsn_ref[...].astype(F32), ycn_ref[...], lng, lnb)
        dwin[0:bt, :] = dyc
        dwin[bt:, :] = jnp.where(i < nt - 1, dycn, 0.0)
        ywin[0:HALO, :] = jnp.where(i > 0, yp_ref[...], 0.0)
        ywin[HALO:, :] = y_ref[...]
        dlng_ref[...] += jnp.sum(dln * xhat, axis=0, keepdims=True)
        dlnb_ref[...] += jnp.sum(dln, axis=0, keepdims=True)
        dwb_ref[...] += jnp.sum(dyc, axis=0, keepdims=True)
        dy = jnp.zeros((bt, D), F32)
        for j in range(CK):
            dy = dy + w_ref[j:j + 1, :] * dwin[CK - 1 - j:CK - 1 - j + bt, :]
            off = HALO - (CK - 1) + j
            dw_ref[j:j + 1, :] += jnp.sum(dyc * ywin[off:off + bt, :], axis=0, keepdims=True)
        a, g = pa_ref[...].astype(F32), pg_ref[...].astype(F32)
        sg = _sigmoid(g)
        da = dy * sg
        dg = dy * a * sg * (1.0 - sg)
        dpa_ref[...] = da.astype(BF)
        dpg_ref[...] = dg.astype(BF)
        dba_ref[...] += jnp.sum(da, axis=0, keepdims=True)
        dbg_ref[...] += jnp.sum(dg, axis=0, keepdims=True)

    cur = pl.BlockSpec((bt, D), lambda b, i: (b * nt + i, 0))
    nxt = pl.BlockSpec((HALO, D), lambda b, i: (jnp.minimum((b * nt + i + 1) * hb, nblk32 - 1), 0))
    prv = pl.BlockSpec((HALO, D), lambda b, i: (jnp.maximum((b * nt + i) * hb - 1, 0), 0))
    row = lambda n: pl.BlockSpec((n, D), lambda b, i: (0, 0))
    N = B * T
    return _pcall(body, name="conv_bwd", grid=(B, nt),
                  in_specs=[cur, nxt, cur, nxt, cur, prv, cur, cur, row(HALO), row(1), row(1)],
                  out_specs=[cur, cur, row(HALO), row(1), row(1), row(1), row(1), row(1)],
                  out_shape=[jax.ShapeDtypeStruct((N, D), BF)] * 2 + [jax.ShapeDtypeStruct((HALO, D), F32)]
                  + [jax.ShapeDtypeStruct((1, D), F32)] * 5,
                  scratch_shapes=[pltpu.VMEM((bt + HALO, D), F32), pltpu.VMEM((bt + HALO, D), F32)],
                  compiler_params=_params(("arbitrary", "arbitrary")))(ds, ds, yc, yc, y, y, pa, pg, w32, lng, lnb)


def _head(x, tgt, gain, bt=512):
    N = x.shape[0]
    bt = min(bt, N)

    def body(x_ref, t_ref, g_ref, dx_ref, loss_ref, dg_ref):
        @pl.when(pl.program_id(0) == 0)
        def _():
            loss_ref[...] = jnp.zeros(loss_ref.shape, F32)
            dg_ref[...] = jnp.zeros(dg_ref.shape, F32)

        xv = x_ref[...]
        gain = g_ref[...]
        err = xv * _rms_stats(xv) * gain - t_ref[...]
        loss_ref[...] += 0.5 * jnp.sum(jnp.mean(err * err, axis=-1, keepdims=True), axis=0, keepdims=True)
        dx, dgr = _rms_bwd(xv, gain, err * (1.0 / D))
        dx_ref[...] = dx
        dg_ref[...] += jnp.sum(dgr, axis=0, keepdims=True)

    blk = pl.BlockSpec((bt, D), lambda i: (i, 0))
    return _pcall(body, name="loss_head", grid=(N // bt,),
                  in_specs=[blk, blk, pl.BlockSpec((1, D), lambda i: (0, 0))],
                  out_specs=[blk, pl.BlockSpec((1, 128), lambda i: (0, 0)), pl.BlockSpec((1, D), lambda i: (0, 0))],
                  out_shape=[jax.ShapeDtypeStruct((N, D), F32), jax.ShapeDtypeStruct((1, 128), F32),
                             jax.ShapeDtypeStruct((1, D), F32)],
                  compiler_params=_params(("arbitrary",)))(x, tgt, gain)


def _local_step(x, mem, tgt, Wb, P):
    B, T, _ = x.shape
    N = B * T
    bq = bk = min(256, T)
    bt = min(512, T)
    x0 = x.reshape(N, D)
    mem2 = mem.reshape(B * NMEM, D)
    tgt2 = tgt.reshape(N, D)
    row = lambda v: v.reshape(1, -1)
    G = {}

    w_in = Wb['w_in_e'][0]
    w_inp = jnp.concatenate([w_in[:, 3 * FOXW + NG:], w_in[:, :3 * FOXW], w_in[:, 3 * FOXW:3 * FOXW + NG],
                             jnp.zeros((D, 128 - NG), BF)], axis=1)
    g_e = row(P['mix_norm_e'])
    z, h0 = _norm_mm("proj_z", x0, g_e, w_inp, N=D, coff=0, bn=D, out_dtype=F32, h_out=True)
    qkv = _norm_mm("proj_qkv", x0, g_e, w_inp, N=3 * FOXW, coff=2, bn=FOXW, out_dtype=BF)[0]
    fl = _norm_mm("proj_f", x0, g_e, w_inp, N=128, coff=20, bn=128, out_dtype=F32)[0]
    fbias = jnp.concatenate([P['fox_f_bias'].reshape(1, NG), jnp.zeros((1, 128 - NG), F32)], axis=1)
    cum = _fox_gate_fwd(fl, fbias, B, T)
    cum4 = cum[:, :NG].reshape(B, T, NG).transpose(0, 2, 1).reshape(B, NG, T // bk, bk)
    b_out, lse = _fox_fwd(qkv, cum4, B, T, 0, bq, bk)
    lng, lnb = row(P['gmlp_ln_g']), row(P['gmlp_ln_b'])
    ws = P['gmlp_w_s'][0]
    bfull = jnp.repeat(P['gmlp_b_s'][0].T, HD, axis=1)
    a_out = _gmlp_fwd(z, lng, lnb, ws, bfull, bt)
    w_out = Wb['w_out_e'][0]
    x1 = _mm_resid("mix_out", [dict(A=b_out, Ka=FOXW, B=w_out, roff=0), dict(A=a_out, Ka=GW, B=w_out, roff=1)], x0)

    def xa_ffn_fwd(l, xin):
        qx, hq = _norm_mm(f"xa_q{l}", xin, row(P['xa_norm'][l]), Wb['xa_wq'][l], N=D, bn=D, out_dtype=BF, h_out=True)
        kv, hm = _norm_mm(f"xa_kv{l}", mem2, row(P['mem_norm'][l]), Wb['xa_wkv'][l], N=2 * D, bn=D, out_dtype=BF,
                          h_out=True)
        o = _xattn_fwd(f"xattn_fwd{l}", qx, kv, B, T, bt)
        xm = _mm_resid(f"xa_o{l}", [dict(A=o, Ka=D, B=Wb['xa_wo'][l])], xin)
        wgu = Wb['ffn_w_gu'][l]
        g, u, a, hf = _fused_mm(f"ffn_gu{l}", dims='nn', M=N, N=FF, bm=min(512, N), bn=FF // 2, x=xm,
                                gain=row(P['ffn_norm'][l]),
                                groups=[[dict(A=None, Ka=D, B=wgu, coff=0)], [dict(A=None, Ka=D, B=wgu, coff=2)]],
                                epi=_epi_swiglu, outs=[BF, BF, BF], h_out=True)
        xo = _mm_resid(f"ffn_down{l}", [dict(A=a, Ka=FF, B=Wb['ffn_w_down'][l])], xm)
        return xo, dict(xin=xin, qx=qx, hq=hq, kv=kv, hm=hm, o=o, xm=xm, g=g, u=u, a=a, hf=hf)

    x3, S0 = xa_ffn_fwd(0, x1)
    w_cin = Wb['conv_w_in'][0]
    b_cin = row(P['conv_b_in'])
    pa, pg, y, hc = _fused_mm("conv_in", dims='nn', M=N, N=D, bm=min(512, N), bn=D, x=x3, gain=row(P['mix_norm_o']),
                              groups=[[dict(A=None, Ka=D, B=w_cin, coff=0)], [dict(A=None, Ka=D, B=w_cin, coff=1)]],
                              epi=_epi_glu, outs=[BF, BF, F32], rows=[(b_cin, 0), (b_cin, 1)], h_out=True)
    w32 = jnp.concatenate([P['conv_dw_w'][0], jnp.zeros((HALO - CK, D), F32)], axis=0)
    cbt = min(256, T)
    s, yc = _conv_fwd(y, w32, row(P['conv_dw_b']), row(P['conv_ln_g']), row(P['conv_ln_b']), B, T, cbt)
    x4 = _mm_resid("conv_out", [dict(A=s, Ka=D, B=Wb['conv_w_out'][0])], x3, bias=row(P['conv_b_out']))
    x6, S1 = xa_ffn_fwd(1, x4)
    dx, loss_t, dgf = _head(x6, tgt2, row(P['final_norm']))
    G['final_norm'] = dgf.reshape(D)

    def xa_ffn_bwd(l, S, dx):
        wgu, wdown = Wb['ffn_w_gu'][l], Wb['ffn_w_down'][l]
        dwdown = _mm_tn(f"dw_down{l}", S['a'], dx)
        dg, du = _fused_mm(f"ffn_dgu{l}", dims='nt', M=N, N=FF, bm=min(512, N), bn=FF // 2,
                           groups=[[dict(A=dx, Ka=D, B=wdown)]], epi=_epi_swiglu_bwd, outs=[BF, BF],
                           tiles=[(S['g'], 0), (S['u'], 0)])
        dwgu = jnp.concatenate([_mm_tn(f"dw_g{l}", S['hf'], dg), _mm_tn(f"dw_u{l}", S['hf'], du)], axis=1)
        dx, dgn = _mm_nt_rms_bwd(f"ffn_dx{l}", [dict(A=dg, Ka=FF, B=wgu, coff=0), dict(A=du, Ka=FF, B=wgu, coff=1)],
                                 S['xm'], row(P['ffn_norm'][l]), dx)
        dwo = _mm_tn(f"dw_o{l}", S['o'], dx)
        do = _mm_nt_plain(f"xa_do{l}", dx, Wb['xa_wo'][l])
        dq, dkv = _xattn_bwd(f"xattn_bwd{l}", S['qx'], S['kv'], do, B, T, bt)
        dwq = _mm_tn(f"dw_q{l}", S['hq'], dq)
        dwkv = _mm_tn(f"dw_kv{l}", S['hm'], dkv)
        dmn = _fused_mm(f"xa_dmem{l}", dims='nt', M=B * NMEM, N=D, bm=min(256, B * NMEM), bn=D,
                        groups=[[dict(A=dkv, Ka=2 * D, B=Wb['xa_wkv'][l])]], epi=_epi_rms_gain_only, outs=[],
                        tiles=[(mem2, 0)], rows=[(row(P['mem_norm'][l]), 0)], reds=[(1, D)])[0]
        dx, dxn = _mm_nt_rms_bwd(f"xa_dx{l}", [dict(A=dq, Ka=D, B=Wb['xa_wq'][l])], S['xin'],
                                 row(P['xa_norm'][l]), dx)
        return dx, dict(ffn_w_down=dwdown, ffn_w_gu=dwgu, ffn_norm=dgn.reshape(D), xa_wo=dwo, xa_wq=dwq,
                        xa_wkv=dwkv, mem_norm=dmn.reshape(D), xa_norm=dxn.reshape(D))

    dx, G1 = xa_ffn_bwd(1, S1, dx)
    G['conv_w_out'] = _mm_tn("dw_cout", s, dx)[None]
    G['conv_b_out'] = _colsum("db_cout", dx)
    dsv = _mm_nt_plain("conv_ds", dx, Wb['conv_w_out'][0])
    dpa, dpg, dw32, dwb, dlng, dlnb, dba, dbg = _conv_bwd(dsv, yc, y, pa, pg, w32, row(P['conv_ln_g']),
                                                          row(P['conv_ln_b']), B, T, cbt)
    G['conv_dw_w'] = dw32[:CK][None]
    G['conv_dw_b'], G['conv_ln_g'], G['conv_ln_b'] = dwb, dlng, dlnb
    G['conv_b_in'] = jnp.concatenate([dba, dbg], axis=1)
    G['conv_w_in'] = jnp.concatenate([_mm_tn("dw_cin_a", hc, dpa), _mm_tn("dw_cin_g", hc, dpg)], axis=1)[None]
    dx, dgo = _mm_nt_rms_bwd("conv_dx", [dict(A=dpa, Ka=D, B=w_cin, coff=0), dict(A=dpg, Ka=D, B=w_cin, coff=1)],
                             x3, row(P['mix_norm_o']), dx)
    G['mix_norm_o'] = dgo
    dx, G0 = xa_ffn_bwd(0, S0, dx)
    for k in G0:
        G[k] = jnp.stack([G0[k], G1[k]])
    dw_out = jnp.concatenate([_mm_tn("dw_out_b", b_out, dx), _mm_tn("dw_out_a", a_out, dx)], axis=0)
    G['w_out_e'] = dw_out[None]
    dcat = _mm_nt_plain("mix_dcat", dx, w_out)
    dq, dk, dv, dcum4, dcq4 = _fox_bwd(qkv, cum4, b_out, lse, dcat, B, T, 0, bq, bk)
    dz, dlg, dlb, dws, dbf = _gmlp_bwd(z, dcat, lng, lnb, ws, bfull, bt)
    G['gmlp_ln_g'], G['gmlp_ln_b'], G['gmlp_w_s'] = dlg, dlb, dws[None]
    G['gmlp_b_s'] = _group_sum("gmlp_db", dbf)[:, :NG].T[None]
    pad = jnp.zeros((N, 128 - NG), F32)
    dck = jnp.concatenate([dcum4.reshape(B, NG, T).transpose(0, 2, 1).reshape(N, NG), pad], axis=1)
    dcq = jnp.concatenate([dcq4.reshape(N, NG // 2, 128)[:, :, :2].reshape(N, NG), pad], axis=1)
    dfl, dfb = _fox_gate_bwd(fl, fbias, dcq, dck, B, T)
    G['fox_f_bias'] = dfb[:, :NG]
    dw_f = _mm_tn("dw_in_f", h0, dfl)
    G['w_in_e'] = jnp.concatenate([_mm_tn("dw_in_q", h0, dq), _mm_tn("dw_in_k", h0, dk), _mm_tn("dw_in_v", h0, dv),
                                   dw_f[:, :NG], _mm_tn("dw_in_z", h0, dz)], axis=1)[None]
    dx, dge = _mm_nt_rms_bwd("mix_dx", [dict(A=dz, Ka=D, B=w_inp, coff=0), dict(A=dq, Ka=FOXW, B=w_inp, coff=2),
                                        dict(A=dk, Ka=FOXW, B=w_inp, coff=3), dict(A=dv, Ka=FOXW, B=w_inp, coff=4),
                                        dict(A=dfl, Ka=128, B=w_inp, coff=20)], x0, g_e, dx)
    G['mix_norm_e'] = dge
    return loss_t[0, 0], dx.reshape(B, T, D), G


COLS = 1024
ANY = pl.BlockSpec(memory_space=pl.ANY)


def _coords():
    return lax.axis_index("x"), lax.axis_index("y"), lax.axis_index("c")


def _other_chips(x, y):
    return [(1 - x, y), (x, 1 - y), (1 - x, 1 - y)]


def _all_gather(name, v):
    _, R, C = v.shape

    def body(v_ref, out_ref, send_sems, recv_sems, local_sem):
        x, y, c = _coords()
        me = 2 * x + y
        sib = (x, y, 1 - c)
        chips = _other_chips(x, y)

        def rcopy(k, src, chip_idx, half, to):
            return pltpu.make_async_remote_copy(src_ref=src, dst_ref=out_ref.at[chip_idx, half],
                                                send_sem=send_sems.at[k], recv_sem=recv_sems.at[k],
                                                device_id=to, device_id_type=MESH)

        local = pltpu.make_async_copy(v_ref, out_ref.at[me], local_sem)
        local.start()
        first = [rcopy(j, v_ref.at[c], me, c, (cx, cy, c)) for j, (cx, cy) in enumerate(chips)]
        for cp in first:
            cp.start()
        passed = []
        for j, (cx, cy) in enumerate(chips):
            kj = 2 * cx + cy
            rcopy(j, v_ref.at[c], kj, c, sib).wait_recv()
            fwd = rcopy(3 + j, out_ref.at[kj, c], kj, c, sib)
            fwd.start()
            passed.append(fwd)
        for j, (cx, cy) in enumerate(chips):
            rcopy(3 + j, v_ref.at[c], 2 * cx + cy, 1 - c, sib).wait_recv()
        for cp in first + passed:
            cp.wait_send()
        local.wait()

    return _pcall(body, name=name, in_specs=[ANY], out_specs=ANY,
                  out_shape=jax.ShapeDtypeStruct((4, 2, R, C), v.dtype),
                  scratch_shapes=[pltpu.SemaphoreType.DMA((6,)), pltpu.SemaphoreType.DMA((6,)),
                                  pltpu.SemaphoreType.DMA(())])(v)


def _sibling_halves(name, p):
    _, _, R, C = p.shape

    def body(p_ref, out_ref, send_sems, recv_sems):
        x, y, c = _coords()
        cps = [pltpu.make_async_remote_copy(src_ref=p_ref.at[k, 1 - c], dst_ref=out_ref.at[k],
                                            send_sem=send_sems.at[k], recv_sem=recv_sems.at[k],
                                            device_id=(x, y, 1 - c), device_id_type=MESH) for k in range(4)]
        for cp in cps:
            cp.start()
        for cp in cps:
            cp.wait()

    return _pcall(body, name=name, in_specs=[ANY], out_specs=ANY, out_shape=jax.ShapeDtypeStruct((4, R, C), p.dtype),
                  scratch_shapes=[pltpu.SemaphoreType.DMA((4,)), pltpu.SemaphoreType.DMA((4,))])(p)


def _chip_exchange(name, q):
    _, R, C = q.shape

    def body(q_ref, out_ref, send_sems, recv_sems):
        x, y, c = _coords()
        cps = [pltpu.make_async_remote_copy(src_ref=q_ref.at[2 * cx + cy], dst_ref=out_ref.at[j],
                                            send_sem=send_sems.at[j], recv_sem=recv_sems.at[j],
                                            device_id=(cx, cy, c), device_id_type=MESH)
               for j, (cx, cy) in enumerate(_other_chips(x, y))]
        for cp in cps:
            cp.start()
        for cp in cps:
            cp.wait()

    return _pcall(body, name=name, in_specs=[ANY], out_specs=ANY, out_shape=jax.ShapeDtypeStruct((3, R, C), q.dtype),
                  scratch_shapes=[pltpu.SemaphoreType.DMA((3,)), pltpu.SemaphoreType.DMA((3,))])(q)


def _sibling_swap(name, h):
    R, C = h.shape

    def body(h_ref, out_ref, send_sem, recv_sem, local_sem):
        x, y, c = _coords()
        local = pltpu.make_async_copy(h_ref, out_ref.at[c], local_sem)
        local.start()
        send = pltpu.make_async_remote_copy(src_ref=h_ref, dst_ref=out_ref.at[c], send_sem=send_sem,
                                            recv_sem=recv_sem, device_id=(x, y, 1 - c), device_id_type=MESH)
        send.start()
        pltpu.make_async_remote_copy(src_ref=h_ref, dst_ref=out_ref.at[1 - c], send_sem=send_sem, recv_sem=recv_sem,
                                     device_id=(x, y, 1 - c), device_id_type=MESH).wait_recv()
        send.wait_send()
        local.wait()

    return _pcall(body, name=name, in_specs=[ANY], out_specs=ANY, out_shape=jax.ShapeDtypeStruct((2, R, C), h.dtype),
                  scratch_shapes=[pltpu.SemaphoreType.DMA(()), pltpu.SemaphoreType.DMA(()),
                                  pltpu.SemaphoreType.DMA(())])(h)


def _row_block(R):
    for br in (512, 400, 256, 200, 128, 64, 32, 16, 8):
        if R % br == 0:
            return br
    return R


def _add_own_half(name, p, recv, c_arr):
    _, _, R, C = p.shape
    br = _row_block(R)

    def body(c_ref, p_ref, r_ref, o_ref):
        o_ref[...] = p_ref[...] + r_ref[...]

    spec = pltpu.PrefetchScalarGridSpec(
        num_scalar_prefetch=1, grid=(4, R // br),
        in_specs=[pl.BlockSpec((None, None, br, C), lambda k, r, c_ref: (k, c_ref[0], r, 0)),
                  pl.BlockSpec((None, br, C), lambda k, r, c_ref: (k, r, 0))],
        out_specs=pl.BlockSpec((None, br, C), lambda k, r, c_ref: (k, r, 0)))
    return _pcall(body, name=name, grid_spec=spec, out_shape=jax.ShapeDtypeStruct((4, R, C), F32),
                  compiler_params=_params(("arbitrary", "arbitrary")))(c_arr, p, recv)


def _add_chips(name, q, recv, me_arr):
    _, R, C = q.shape
    br = _row_block(R)

    def body(me_ref, q_ref, r_ref, o_ref):
        o_ref[...] = ((q_ref[...] + r_ref[0]) + r_ref[1]) + r_ref[2]

    spec = pltpu.PrefetchScalarGridSpec(
        num_scalar_prefetch=1, grid=(R // br,),
        in_specs=[pl.BlockSpec((None, br, C), lambda r, me_ref: (me_ref[0], r, 0)),
                  pl.BlockSpec((3, br, C), lambda r, me_ref: (0, r, 0))],
        out_specs=pl.BlockSpec((br, C), lambda r, me_ref: (r, 0)))
    return _pcall(body, name=name, grid_spec=spec, out_shape=jax.ShapeDtypeStruct((R, C), F32),
                  compiler_params=_params(("arbitrary",)))(me_arr, q, recv)


def _reduce_scatter(p, c_arr, me_arr):
    got = _sibling_halves("rs_sibling_halves", p)
    q = _add_own_half("rs_add_pair", p, got, c_arr)
    got = _chip_exchange("rs_chip_exchange", q)
    h = _add_chips("rs_add_chips", q, got, me_arr)
    return _sibling_swap("rs_sibling_swap", h)


def _adamw(name, w, g, m, v):
    shape = w.shape
    cols = shape[-1]
    rows = w.size // cols
    w2, g2, m2, v2 = (a.reshape(rows, cols) for a in (w, g, m, v))
    bt = 128 if rows % 128 == 0 else rows

    def body(w_ref, g_ref, m_ref, v_ref, d_ref, nm_ref, nv_ref):
        gv = g_ref[...]
        nm = ADAM_B1 * m_ref[...] + (1.0 - ADAM_B1) * gv
        nv = ADAM_B2 * v_ref[...] + (1.0 - ADAM_B2) * (gv * gv)
        m_hat = nm / (1.0 - ADAM_B1 ** ADAM_STEP)
        v_hat = nv / (1.0 - ADAM_B2 ** ADAM_STEP)
        d_ref[...] = -ADAM_LR * (m_hat / (jnp.sqrt(v_hat) + ADAM_EPS) + ADAM_WD * w_ref[...])
        nm_ref[...] = nm
        nv_ref[...] = nv

    blk = pl.BlockSpec((bt, cols), lambda i: (i, 0))
    outs = _pcall(body, name=name, grid=(rows // bt,), in_specs=[blk] * 4, out_specs=[blk] * 3,
                  out_shape=[jax.ShapeDtypeStruct((rows, cols), F32)] * 3, compiler_params=_params(("arbitrary",)))(
        w2, g2, m2, v2)
    return [o.reshape(shape) for o in outs]


SMALL_SHARDED = ['mix_norm_o', 'conv_b_in', 'conv_dw_w', 'conv_dw_b', 'conv_ln_g', 'conv_ln_b', 'conv_b_out']
REPLICATED = [n for n in WEIGHTS if SHARD_AXIS[n] is None]
NCHIP = 4


def _halves(flat, tile_rows):
    unit = 2 * tile_rows * COLS
    total = -(-flat.size // unit) * unit
    return jnp.pad(flat, (0, total - flat.size)).reshape(2, total // (2 * COLS), COLS)


def _flat(arrays):
    return jnp.concatenate([a.reshape(-1) for a in arrays])


def _chip_block(a, axis, k):
    n = a.shape[axis] // NCHIP
    return lax.slice_in_dim(a, k * n, (k + 1) * n, axis=axis)


def _full_shape(n, shard_shape):
    s = list(shard_shape[n])
    s[SHARD_AXIS[n]] *= NCHIP
    return tuple(s)


def _unpack(flat, names, shapes):
    out, off = {}, 0
    for n in names:
        size = math.prod(shapes[n])
        out[n] = flat[off:off + size].reshape(shapes[n])
        off += size
    return out


def kernel(x, mem, mix_norm_e, w_in_e, fox_f_bias, gmlp_ln_g, gmlp_ln_b, gmlp_w_s, gmlp_b_s, w_out_e, mix_norm_o, conv_w_in, conv_b_in, conv_dw_w, conv_dw_b, conv_ln_g, conv_ln_b, conv_w_out, conv_b_out, xa_norm, mem_norm, xa_wq, xa_wkv, xa_wo, ffn_norm, ffn_w_gu, ffn_w_down, final_norm, loss_target, m_mix_norm_e, m_w_in_e, m_fox_f_bias, m_gmlp_ln_g, m_gmlp_ln_b, m_gmlp_w_s, m_gmlp_b_s, m_w_out_e, m_mix_norm_o, m_conv_w_in, m_conv_b_in, m_conv_dw_w, m_conv_dw_b, m_conv_ln_g, m_conv_ln_b, m_conv_w_out, m_conv_b_out, m_xa_norm, m_mem_norm, m_xa_wq, m_xa_wkv, m_xa_wo, m_ffn_norm, m_ffn_w_gu, m_ffn_w_down, m_final_norm, v_mix_norm_e, v_w_in_e, v_fox_f_bias, v_gmlp_ln_g, v_gmlp_ln_b, v_gmlp_w_s, v_gmlp_b_s, v_w_out_e, v_mix_norm_o, v_conv_w_in, v_conv_b_in, v_conv_dw_w, v_conv_dw_b, v_conv_ln_g, v_conv_ln_b, v_conv_w_out, v_conv_b_out, v_xa_norm, v_mem_norm, v_xa_wq, v_xa_wkv, v_xa_wo, v_ffn_norm, v_ffn_w_gu, v_ffn_w_down, v_final_norm):
    env = locals()
    w = {n: env[n] for n in WEIGHTS}
    m = {n: env["m_" + n] for n in WEIGHTS}
    v = {n: env["v_" + n] for n in WEIGHTS}
    shard_shape = {n: w[n].shape for n in WEIGHTS}
    xi, yi, ci = _coords()
    c_arr = jnp.reshape(ci, (1,)).astype(jnp.int32)
    me_arr = jnp.reshape(2 * xi + yi, (1,)).astype(jnp.int32)

    def gather_full(name, names, dtype, tile_rows):
        got = _all_gather(name, _halves(_flat([w[n].astype(dtype) for n in names]), tile_rows))
        got = got.reshape(NCHIP, -1)
        parts = [_unpack(got[k], names, shard_shape) for k in range(NCHIP)]
        return {n: jnp.concatenate([parts[k][n] for k in range(NCHIP)], axis=SHARD_AXIS[n]) for n in names}

    Wb = gather_full("gather_matrices", BIG, BF, 16)
    P = gather_full("gather_vectors", SMALL_SHARDED, F32, 8)
    P.update({n: w[n] for n in REPLICATED})

    loss_part, grad_x, G = _local_step(x, mem, loss_target, Wb, P)
    loss = lax.psum(loss_part, ("x", "y", "c"))

    rep = _flat([G[n] for n in REPLICATED])
    quarter = -(-rep.size // (NCHIP * 2 * 8 * COLS)) * (2 * 8 * COLS)
    rep = jnp.pad(rep, (0, NCHIP * quarter - rep.size)).reshape(NCHIP, quarter)
    segs = [_flat([_chip_block(G[n].reshape(_full_shape(n, shard_shape)), SHARD_AXIS[n], k)
                   for n in BIG + SMALL_SHARDED] + [rep[k]]) for k in range(NCHIP)]
    p = jnp.stack([_halves(s, 8) for s in segs])
    red = _reduce_scatter(p, c_arr, me_arr).reshape(-1)
    mine = _unpack(red, BIG + SMALL_SHARDED, shard_shape)
    off = sum(math.prod(shard_shape[n]) for n in BIG + SMALL_SHARDED)
    rep_all = _all_gather("gather_replicated_grads", red[off:off + quarter].reshape(2, quarter // (2 * COLS), COLS))
    mine.update(_unpack(rep_all.reshape(-1), REPLICATED, shard_shape))

    grads, deltas, new_m, new_v = [], [], [], []
    for n in WEIGHTS:
        d, nm, nv = _adamw("adamw_" + n, w[n], mine[n], m[n], v[n])
        grads.append(mine[n])
        deltas.append(d)
        new_m.append(nm)
        new_v.append(nv)
    return (loss, grad_x, *grads, *deltas, *new_m, *new_v)
```

```python
import functools
import math

import jax
import jax.numpy as jnp
from jax import lax
from jax.experimental import pallas as pl
from jax.experimental.pallas import tpu as pltpu

F32 = jnp.float32
BF = jnp.bfloat16
MESH = pl.DeviceIdType.MESH

D = 1024
FOXW = 512
HD = 64
GW = 512
CH = 128
NG = 8
FF = 2816
NMEM = 256
XH = 4
XD = 256
CK = 31
HALO = 32
EPS = 1e-6
IN_W = 2568
IN_WP = 2688
VMEM_LIMIT = 56 * 1024 * 1024

ADAM_LR, ADAM_B1, ADAM_B2, ADAM_EPS, ADAM_WD, ADAM_STEP = 0.001, 0.9, 0.999, 1e-08, 0.01, 10

WEIGHTS = ['mix_norm_e', 'w_in_e', 'fox_f_bias', 'gmlp_ln_g', 'gmlp_ln_b', 'gmlp_w_s', 'gmlp_b_s', 'w_out_e',
           'mix_norm_o', 'conv_w_in', 'conv_b_in', 'conv_dw_w', 'conv_dw_b', 'conv_ln_g', 'conv_ln_b',
           'conv_w_out', 'conv_b_out', 'xa_norm', 'mem_norm', 'xa_wq', 'xa_wkv', 'xa_wo', 'ffn_norm',
           'ffn_w_gu', 'ffn_w_down', 'final_norm']
SHARD_AXIS = {'mix_norm_e': None, 'w_in_e': 2, 'fox_f_bias': None, 'gmlp_ln_g': None, 'gmlp_ln_b': None,
              'gmlp_w_s': None, 'gmlp_b_s': None, 'w_out_e': 1, 'mix_norm_o': 1, 'conv_w_in': 2, 'conv_b_in': 1,
              'conv_dw_w': 2, 'conv_dw_b': 1, 'conv_ln_g': 1, 'conv_ln_b': 1, 'conv_w_out': 1, 'conv_b_out': 1,
              'xa_norm': None, 'mem_norm': None, 'xa_wq': 1, 'xa_wkv': 2, 'xa_wo': 1, 'ffn_norm': None,
              'ffn_w_gu': 2, 'ffn_w_down': 1, 'final_norm': None}
BIG = ['w_in_e', 'w_out_e', 'conv_w_in', 'conv_w_out', 'xa_wq', 'xa_wkv', 'xa_wo', 'ffn_w_gu', 'ffn_w_down']


def _pcall(body, **kw):
    return pl.pallas_call(body, **kw)


def _params(sem=None, **kw):
    return pltpu.CompilerParams(dimension_semantics=sem, vmem_limit_bytes=VMEM_LIMIT, **kw)


def _dot(a, b, dims):
    dn = {'nn': (((1,), (0,)), ((), ())), 'nt': (((1,), (1,)), ((), ())), 'tn': (((0,), (0,)), ((), ()))}[dims]
    return lax.dot_general(a, b, dn, preferred_element_type=F32)


def _sigmoid(x):
    return 1.0 / (1.0 + jnp.exp(-x))


def _rms_stats(xv):
    return lax.rsqrt(jnp.mean(xv * xv, axis=-1, keepdims=True) + EPS)


def _rms_bwd(xv, gain, dh):
    r = _rms_stats(xv)
    t = dh * gain
    dx = r * t - xv * (r * r * r * jnp.mean(t * xv, axis=-1, keepdims=True))
    return dx, dh * xv * r


def _fused_mm(name, *, dims, M, N, bm, bn, groups, epi, outs, x=None, gain=None, tiles=(), rows=(),
              h_out=False, reds=()):
    bm = min(bm, M)
    nI, nJ = M // bm, N // bn
    assert nI * bm == M and nJ * bn == N
    assert not reds or nJ == 1
    arrays, specs = [], []

    def add(arr, spec):
        arrays.append(arr)
        specs.append(spec)
        return len(arrays) - 1

    if x is not None:
        K0 = x.shape[1]
        add(x, pl.BlockSpec((bm, K0), lambda i, j: (i, 0)))
        add(gain, pl.BlockSpec((1, K0), lambda i, j: (0, 0)))
    plan = []
    for grp in groups:
        g = []
        for p in grp:
            ai = None
            if p['A'] is not None:
                ai = add(p['A'], pl.BlockSpec((bm, p['Ka']), lambda i, j, o=p.get('acoff', 0): (i, o)))
            ro, co = p.get('roff', 0), p.get('coff', 0)
            if dims == 'nn':
                bi = add(p['B'], pl.BlockSpec((p['Ka'], bn), lambda i, j, ro=ro, co=co: (ro, j + co)))
            else:
                bi = add(p['B'], pl.BlockSpec((bn, p['Ka']), lambda i, j, ro=ro, co=co: (j + ro, co)))
            g.append((ai, bi))
        plan.append(g)
    tile_idx = [add(a, pl.BlockSpec((bm, bn), lambda i, j, o=o: (i, j + o))) for a, o in tiles]
    row_idx = [add(a, pl.BlockSpec((1, bn), lambda i, j, o=o: (0, j + o))) for a, o in rows]
    n_in = len(arrays)

    out_shape = [jax.ShapeDtypeStruct((M, N), dt) for dt in outs]
    out_specs = [pl.BlockSpec((bm, bn), lambda i, j: (i, j)) for _ in outs]
    if h_out:
        out_shape.append(jax.ShapeDtypeStruct((M, x.shape[1]), BF))
        out_specs.append(pl.BlockSpec((bm, x.shape[1]), lambda i, j: (i, 0)))
    for shp in reds:
        out_shape.append(jax.ShapeDtypeStruct(shp, F32))
        out_specs.append(pl.BlockSpec(shp, lambda i, j: (0, 0)))
    n_main = len(outs)
    scratch = [pltpu.VMEM((bm, x.shape[1]), BF)] if x is not None else []

    def body(*refs):
        ins, out_refs, scr = refs[:n_in], refs[n_in:n_in + len(out_shape)], refs[n_in + len(out_shape):]
        i, j = pl.program_id(0), pl.program_id(1)
        if x is not None:
            hn_ref = scr[0]

            @pl.when(j == 0)
            def _():
                xv = ins[0][...]
                hn = (xv * _rms_stats(xv) * ins[1][...]).astype(BF)
                hn_ref[...] = hn
                if h_out:
                    out_refs[n_main][...] = hn

        accs = []
        for g in plan:
            acc = None
            for ai, bi in g:
                a = hn_ref[...] if ai is None else ins[ai][...]
                if a.dtype != BF:
                    a = a.astype(BF)
                d = _dot(a, ins[bi][...], dims)
                acc = d if acc is None else acc + d
            accs.append(acc)
        out_vals, red_vals = epi(accs, [ins[t][...] for t in tile_idx], [ins[r][...] for r in row_idx])
        for r, v in zip(out_refs[:n_main], out_vals):
            r[...] = v.astype(r.dtype)
        if reds:
            red_refs = out_refs[n_main + (1 if h_out else 0):]

            @pl.when(i == 0)
            def _():
                for r in red_refs:
                    r[...] = jnp.zeros(r.shape, F32)

            for r, v in zip(red_refs, red_vals):
                r[...] += v

    res = _pcall(body, name=name, grid=(nI, nJ), in_specs=specs, out_specs=out_specs, out_shape=out_shape,
                 scratch_shapes=scratch, compiler_params=_params(("arbitrary", "arbitrary")))(*arrays)
    return res


def _epi_plain(accs, tiles, rows):
    return [accs[0]], []


def _epi_resid(accs, tiles, rows):
    y = tiles[0] + accs[0]
    if rows:
        y = y + rows[0]
    return [y], []


def _epi_swiglu(accs, tiles, rows):
    g, u = accs
    return [g, u, g * _sigmoid(g) * u], []


def _epi_glu(accs, tiles, rows):
    a, g = accs[0] + rows[0], accs[1] + rows[1]
    return [a, g, a * _sigmoid(g)], []


def _epi_swiglu_bwd(accs, tiles, rows):
    da = accs[0]
    g, u = tiles[0].astype(F32), tiles[1].astype(F32)
    sg = _sigmoid(g)
    return [da * u * (sg * (1.0 + g * (1.0 - sg))), da * (g * sg)], []


def _epi_rms_bwd(accs, tiles, rows):
    dx, dgr = _rms_bwd(tiles[0], rows[0], accs[0])
    return [tiles[1] + dx], [jnp.sum(dgr, axis=0, keepdims=True)]


def _epi_rms_gain_only(accs, tiles, rows):
    _, dgr = _rms_bwd(tiles[0], rows[0], accs[0])
    return [], [jnp.sum(dgr, axis=0, keepdims=True)]


def _norm_mm(name, x, gain, W, *, N, coff=0, bn, out_dtype, bm=512, h_out=False):
    return _fused_mm(name, dims='nn', M=x.shape[0], N=N, bm=bm, bn=bn, x=x, gain=gain,
                     groups=[[dict(A=None, Ka=x.shape[1], B=W, coff=coff)]], epi=_epi_plain, outs=[out_dtype],
                     h_out=h_out)


def _mm_resid(name, pairs, resid, bias=None, bm=512):
    M = resid.shape[0]
    return _fused_mm(name, dims='nn', M=M, N=D, bm=bm, bn=D, groups=[pairs], epi=_epi_resid, outs=[F32],
                     tiles=[(resid, 0)], rows=[(bias, 0)] if bias is not None else [])[0]


def _mm_nt_plain(name, dy, W, bm=512):
    return _fused_mm(name, dims='nt', M=dy.shape[0], N=W.shape[0], bm=bm, bn=W.shape[0],
                     groups=[[dict(A=dy, Ka=dy.shape[1], B=W)]], epi=_epi_plain, outs=[BF])[0]


def _mm_nt_rms_bwd(name, pairs, x, gain, dx_in, bm=256):
    out = _fused_mm(name, dims='nt', M=x.shape[0], N=D, bm=bm, bn=D, groups=[pairs], epi=_epi_rms_bwd,
                    outs=[F32], tiles=[(x, 0), (dx_in, 0)], rows=[(gain, 0)], reds=[(1, D)])
    return out[0], out[1]


def _mm_tn(name, A, G, bk=1024):
    T, Ka, Kg = A.shape[0], A.shape[1], G.shape[1]
    bm = Ka if Ka <= 1024 else Ka // 2
    bn = Kg if Kg <= 1024 else Kg // 2
    if Kg == 1536:
        bn = 768
    bk = min(bk, T)
    nI, nJ, nK = Ka // bm, Kg // bn, T // bk

    def body(a_ref, g_ref, o_ref):
        @pl.when(pl.program_id(2) == 0)
        def _():
            o_ref[...] = jnp.zeros(o_ref.shape, F32)

        o_ref[...] += _dot(a_ref[...].astype(BF), g_ref[...].astype(BF), 'tn')

    return _pcall(body, name=name, grid=(nI, nJ, nK),
                  in_specs=[pl.BlockSpec((bk, bm), lambda i, j, k: (k, i)),
                            pl.BlockSpec((bk, bn), lambda i, j, k: (k, j))],
                  out_specs=pl.BlockSpec((bm, bn), lambda i, j, k: (i, j)),
                  out_shape=jax.ShapeDtypeStruct((Ka, Kg), F32),
                  compiler_params=_params(("arbitrary", "arbitrary", "arbitrary")))(A, G)


def _colsum(name, a, bt=512):
    M, N = a.shape
    bt = min(bt, M)

    def body(a_ref, o_ref):
        @pl.when(pl.program_id(0) == 0)
        def _():
            o_ref[...] = jnp.zeros(o_ref.shape, F32)

        o_ref[...] += jnp.sum(a_ref[...].astype(F32), axis=0, keepdims=True)

    return _pcall(body, name=name, grid=(M // bt,), in_specs=[pl.BlockSpec((bt, N), lambda i: (i, 0))],
                  out_specs=pl.BlockSpec((1, N), lambda i: (0, 0)), out_shape=jax.ShapeDtypeStruct((1, N), F32),
                  compiler_params=_params(("arbitrary",)))(a)


def _cumsum_rows(v):
    T = v.shape[0]
    row = lax.broadcasted_iota(jnp.int32, v.shape, 0)
    s = 1
    while s < T:
        v = v + jnp.where(row >= s, pltpu.roll(v, s, 0), 0.0)
        s *= 2
    return v


def _log_sigmoid(z):
    return jnp.minimum(z, 0.0) - jnp.log(1.0 + jnp.exp(-jnp.abs(z)))


def _fox_gate_fwd(fl, fbias, B, T):
    def body(fl_ref, b_ref, o_ref):
        o_ref[...] = _cumsum_rows(_log_sigmoid(fl_ref[...] + b_ref[...]))

    return _pcall(body, name="fox_gate_fwd", grid=(B,),
                  in_specs=[pl.BlockSpec((T, 128), lambda b: (b, 0)), pl.BlockSpec((1, 128), lambda b: (0, 0))],
                  out_specs=pl.BlockSpec((T, 128), lambda b: (b, 0)),
                  out_shape=jax.ShapeDtypeStruct((B * T, 128), F32), compiler_params=_params(("arbitrary",)))(fl, fbias)


def _fox_gate_bwd(fl, fbias, dcq, dck, B, T):
    def body(fl_ref, b_ref, dcq_ref, dck_ref, dfl_ref, db_ref):
        dc = dcq_ref[...] + dck_ref[...]
        rev = jnp.sum(dc, axis=0, keepdims=True) - _cumsum_rows(dc) + dc
        dfl = rev * _sigmoid(-(fl_ref[...] + b_ref[...]))
        dfl_ref[...] = dfl

        @pl.when(pl.program_id(0) == 0)
        def _():
            db_ref[...] = jnp.zeros(db_ref.shape, F32)

        db_ref[...] += jnp.sum(dfl, axis=0, keepdims=True)

    return _pcall(body, name="fox_gate_bwd", grid=(B,),
                  in_specs=[pl.BlockSpec((T, 128), lambda b: (b, 0)), pl.BlockSpec((1, 128), lambda b: (0, 0)),
                            pl.BlockSpec((T, 128), lambda b: (b, 0)), pl.BlockSpec((T, 128), lambda b: (b, 0))],
                  out_specs=[pl.BlockSpec((T, 128), lambda b: (b, 0)), pl.BlockSpec((1, 128), lambda b: (0, 0))],
                  out_shape=[jax.ShapeDtypeStruct((B * T, 128), F32), jax.ShapeDtypeStruct((1, 128), F32)],
                  compiler_params=_params(("arbitrary",)))(fl, fbias, dcq, dck)


NEG = -1e30


def _fox_fwd(qkv, cum4, B, T, qoff, bq, bk):
    nq, nkb = T // bq, T // bk
    N = B * T

    def body(q_ref, k_ref, v_ref, cum_ref, o_ref, lse_ref):
        hp, i = pl.program_id(1), pl.program_id(2)
        lane = lax.broadcasted_iota(jnp.int32, (bq, 128), 1)
        heads = [slice(e * HD, (e + 1) * HD) for e in range(2)]
        qs = [q_ref[:, sl] * 0.125 for sl in heads]

        def block(j, carry, diagonal):
            ks = pl.multiple_of(j * bk, bk)
            out = []
            for e, sl in enumerate(heads):
                m, l, acc = carry[e]
                s = _dot(qs[e], k_ref[pl.ds(ks, bk), sl], 'nt') - cum_ref[0, 2 * hp + e, pl.ds(j, 1), :]
                if diagonal:
                    keep = lax.broadcasted_iota(jnp.int32, (bq, bk), 0) >= lax.broadcasted_iota(jnp.int32, (bq, bk), 1)
                    s = jnp.where(keep, s, NEG)
                m_new = jnp.maximum(m, jnp.max(s, axis=1, keepdims=True))
                p = jnp.exp(s - m_new)
                alpha = jnp.exp(m - m_new)
                l = alpha * l + jnp.sum(p, axis=1, keepdims=True)
                acc = alpha * acc + _dot(p.astype(BF), v_ref[pl.ds(ks, bk), sl], 'nn')
                out.append((m_new, l, acc))
            return tuple(out)

        init = tuple((jnp.full((bq, 1), NEG, F32), jnp.zeros((bq, 1), F32), jnp.zeros((bq, HD), F32)) for _ in heads)
        carry = lax.fori_loop(0, i, lambda j, c: block(j, c, False), init)
        carry = block(i, carry, True)
        lse_tile = jnp.zeros((bq, 128), F32)
        for e, sl in enumerate(heads):
            m, l, acc = carry[e]
            o_ref[:, sl] = (acc / l).astype(BF)
            lse_tile = jnp.where(lane == e, m + jnp.log(l), lse_tile)
        lse_ref[...] = lse_tile

    return _pcall(body, name="fox_fwd", grid=(B, 4, nq),
                  in_specs=[pl.BlockSpec((bq, 128), lambda b, h, i: (b * nq + i, qoff + h)),
                            pl.BlockSpec((T, 128), lambda b, h, i: (b, qoff + 4 + h)),
                            pl.BlockSpec((T, 128), lambda b, h, i: (b, qoff + 8 + h)),
                            pl.BlockSpec((1, NG, nkb, bk), lambda b, h, i: (b, 0, 0, 0))],
                  out_specs=[pl.BlockSpec((bq, 128), lambda b, h, i: (b * nq + i, h)),
                             pl.BlockSpec((bq, 128), lambda b, h, i: (b * nq + i, h))],
                  out_shape=[jax.ShapeDtypeStruct((N, FOXW), BF), jax.ShapeDtypeStruct((N, FOXW), F32)],
                  compiler_params=_params(("arbitrary", "arbitrary", "arbitrary")))(qkv, qkv, qkv, cum4)


def _fox_bwd(qkv, cum4, o, lse, dcat, B, T, qoff, bq, bk):
    nq, nkb = T // bq, T // bk
    N = B * T

    def body(q_ref, k_ref, v_ref, cum_ref, o_ref, lse_ref, do_ref, dq_ref, dk_ref, dv_ref, dcum_ref, dcq_ref,
             dq_acc, dl_ref, rs_ref):
        hp = pl.program_id(1)
        heads = [slice(e * HD, (e + 1) * HD) for e in range(2)]
        keep = lax.broadcasted_iota(jnp.int32, (bq, bk), 0) >= lax.broadcasted_iota(jnp.int32, (bq, bk), 1)
        dcq_ref[...] = jnp.zeros(dcq_ref.shape, F32)
        dq_acc[...] = jnp.zeros(dq_acc.shape, F32)
        rs_ref[...] = jnp.zeros(rs_ref.shape, F32)
        for e, sl in enumerate(heads):
            dl_ref[e] = jnp.sum(do_ref[:, sl].astype(F32) * o_ref[:, sl].astype(F32), axis=1, keepdims=True)
        for j in range(nkb):
            krows = slice(j * bk, (j + 1) * bk)

            def tile(i, carry, diagonal):
                qs = i * bq if diagonal else pl.multiple_of(i * bq, bq)
                out = []
                for e, sl in enumerate(heads):
                    dk_a, dv_a, cs = carry[e]
                    q, k = q_ref[pl.ds(qs, bq), sl], k_ref[krows, sl]
                    do = do_ref[pl.ds(qs, bq), sl]
                    s = _dot(q, k, 'nt') * 0.125 - cum_ref[0, 2 * hp + e, j:j + 1, :]
                    p = jnp.exp(s - lse_ref[pl.ds(qs, bq), e:e + 1])
                    if diagonal:
                        p = jnp.where(keep, p, 0.0)
                    dv_a = dv_a + _dot(p.astype(BF), do, 'tn')
                    ds = p * (_dot(do, v_ref[krows, sl], 'nt') - dl_ref[e, pl.ds(qs, bq), :])
                    cs = cs + jnp.sum(ds, axis=0, keepdims=True)
                    rs_ref[e, pl.ds(qs, bq), :] += jnp.sum(ds, axis=1, keepdims=True)
                    dsb = ds.astype(BF)
                    dk_a = dk_a + _dot(dsb, q, 'tn')
                    dq_acc[e, pl.ds(qs, bq), :] += _dot(dsb, k, 'nn')
                    out.append((dk_a, dv_a, cs))
                return tuple(out)

            init = tuple((jnp.zeros((bk, HD), F32), jnp.zeros((bk, HD), F32), jnp.zeros((1, bk), F32)) for _ in heads)
            carry = lax.fori_loop(j + 1, nq, lambda i, c: tile(i, c, False), tile(j, init, True))
            for e, sl in enumerate(heads):
                dk_a, dv_a, cs = carry[e]
                dk_ref[krows, sl] = (dk_a * 0.125).astype(BF)
                dv_ref[krows, sl] = dv_a.astype(BF)
                dcum_ref[0, e, j:j + 1, :] = -cs
        for e, sl in enumerate(heads):
            dq_ref[:, sl] = (dq_acc[e] * 0.125).astype(BF)
            dcq_ref[:, e:e + 1] = rs_ref[e]

    seq = lambda off: pl.BlockSpec((T, 128), lambda b, h, off=off: (b, off + h))
    return _pcall(body, name="fox_bwd", grid=(B, 4),
                  in_specs=[seq(qoff), seq(qoff + 4), seq(qoff + 8),
                            pl.BlockSpec((1, NG, nkb, bk), lambda b, h: (b, 0, 0, 0)),
                            seq(0), seq(0), seq(0)],
                  out_specs=[seq(0), seq(0), seq(0), pl.BlockSpec((1, 2, nkb, bk), lambda b, h: (b, h, 0, 0)), seq(0)],
                  out_shape=[jax.ShapeDtypeStruct((N, FOXW), BF)] * 3 + [jax.ShapeDtypeStruct((B, NG, nkb, bk), F32),
                                                                         jax.ShapeDtypeStruct((N, FOXW), F32)],
                  scratch_shapes=[pltpu.VMEM((2, T, HD), F32), pltpu.VMEM((2, T, 1), F32), pltpu.VMEM((2, T, 1), F32)],
                  compiler_params=_params(("arbitrary", "arbitrary")))(qkv, qkv, qkv, cum4, o, lse, dcat)


_GC = math.sqrt(2.0 / math.pi)
_GA = 0.044715


def _gelu(z):
    return 0.5 * z * (1.0 + jnp.tanh(_GC * (z + _GA * z * z * z)))


def _gelu_grad(z):
    t = jnp.tanh(_GC * (z + _GA * z * z * z))
    return 0.5 * (1.0 + t) + 0.5 * z * (1.0 - t * t) * (_GC * (1.0 + 3.0 * _GA * z * z))


def _gmlp_common(z, lng, lnb):
    zg = _gelu(z)
    u, vg = zg[:, :GW], zg[:, GW:]
    mu = jnp.mean(vg, axis=-1, keepdims=True)
    xc = vg - mu
    rstd = lax.rsqrt(jnp.mean(xc * xc, axis=-1, keepdims=True) + EPS)
    xhat = xc * rstd
    return u, xhat, rstd, xhat * lng + lnb


def _tril_w(ws_ref):
    tri = lax.broadcasted_iota(jnp.int32, (CH, CH), 0) >= lax.broadcasted_iota(jnp.int32, (CH, CH), 1)
    return [jnp.where(tri, ws_ref[g], 0.0).astype(BF) for g in range(NG)], tri


def _split_pair(vp):
    lane = lax.broadcasted_iota(jnp.int32, vp.shape, 1)
    zero = jnp.zeros(vp.shape, vp.dtype)
    return jnp.concatenate([jnp.where(lane < HD, vp, zero), jnp.where(lane >= HD, vp, zero)], axis=0)


def _gmlp_mix(wt, vgn_b):
    outs = []
    for p in range(NG // 2):
        wcat = jnp.concatenate([wt[2 * p], wt[2 * p + 1]], axis=1)
        outs.append(_dot(wcat, _split_pair(vgn_b[:, 128 * p:128 * (p + 1)]), 'nn'))
    return jnp.concatenate(outs, axis=1)


def _gmlp_fwd(z, lng, lnb, ws, bfull, bt):
    N = z.shape[0]

    def body(z_ref, lng_ref, lnb_ref, ws_ref, bf_ref, o_ref):
        wt, _ = _tril_w(ws_ref)
        for c in range(bt // CH):
            rows = slice(c * CH, (c + 1) * CH)
            u, _, _, vgn = _gmlp_common(z_ref[rows, :], lng_ref[...], lnb_ref[...])
            mixed = _gmlp_mix(wt, vgn.astype(BF)) + bf_ref[...]
            o_ref[rows, :] = (u * mixed).astype(BF)

    full = lambda shp: pl.BlockSpec(shp, lambda i: (0,) * len(shp))
    return _pcall(body, name="gmlp_fwd", grid=(N // bt,),
                  in_specs=[pl.BlockSpec((bt, D), lambda i: (i, 0)), full((1, GW)), full((1, GW)),
                            full((NG, CH, CH)), full((CH, GW))],
                  out_specs=pl.BlockSpec((bt, GW), lambda i: (i, 0)), out_shape=jax.ShapeDtypeStruct((N, GW), BF),
                  compiler_params=_params(("arbitrary",)))(z, lng, lnb, ws, bfull)


def _gmlp_bwd(z, dcat, lng, lnb, ws, bfull, bt):
    N = z.shape[0]

    def body(z_ref, da_ref, lng_ref, lnb_ref, ws_ref, bf_ref, dz_ref, dg_ref, db_ref, dws_ref, dbf_ref):
        @pl.when(pl.program_id(0) == 0)
        def _():
            for r in (dg_ref, db_ref, dws_ref, dbf_ref):
                r[...] = jnp.zeros(r.shape, F32)

        wt, tri = _tril_w(ws_ref)
        lane = lax.broadcasted_iota(jnp.int32, (CH, 128), 1)
        for c in range(bt // CH):
            rows = slice(c * CH, (c + 1) * CH)
            zc = z_ref[rows, :]
            u, xhat, rstd, vgn = _gmlp_common(zc, lng_ref[...], lnb_ref[...])
            vgn_b = vgn.astype(BF)
            mixed = _gmlp_mix(wt, vgn_b) + bf_ref[...]
            da = da_ref[rows, :].astype(F32)
            dmix = da * u
            du = da * mixed
            dbf_ref[...] += dmix
            dvs = []
            for p in range(NG // 2):
                cols = slice(128 * p, 128 * (p + 1))
                dmp = dmix[:, cols].astype(BF)
                dwp = _dot(_split_pair(dmp), vgn_b[:, cols], 'nt')
                dws_ref[2 * p] += jnp.where(tri, dwp[:CH], 0.0)
                dws_ref[2 * p + 1] += jnp.where(tri, dwp[CH:], 0.0)
                dvs.append(jnp.where(lane < HD, _dot(wt[2 * p], dmp, 'tn'), _dot(wt[2 * p + 1], dmp, 'tn')))
            dvgn = jnp.concatenate(dvs, axis=1)
            dg_ref[...] += jnp.sum(dvgn * xhat, axis=0, keepdims=True)
            db_ref[...] += jnp.sum(dvgn, axis=0, keepdims=True)
            dxh = dvgn * lng_ref[...]
            dvg = rstd * (dxh - jnp.mean(dxh, axis=-1, keepdims=True)
                          - xhat * jnp.mean(dxh * xhat, axis=-1, keepdims=True))
            dz_ref[rows, :] = (jnp.concatenate([du, dvg], axis=1) * _gelu_grad(zc)).astype(BF)

    full = lambda shp: pl.BlockSpec(shp, lambda i: (0,) * len(shp))
    return _pcall(body, name="gmlp_bwd", grid=(N // bt,),
                  in_specs=[pl.BlockSpec((bt, D), lambda i: (i, 0)), pl.BlockSpec((bt, GW), lambda i: (i, 1)),
                            full((1, GW)), full((1, GW)), full((NG, CH, CH)), full((CH, GW))],
                  out_specs=[pl.BlockSpec((bt, D), lambda i: (i, 0)), full((1, GW)), full((1, GW)),
                             full((NG, CH, CH)), full((CH, GW))],
                  out_shape=[jax.ShapeDtypeStruct((N, D), BF), jax.ShapeDtypeStruct((1, GW), F32),
                             jax.ShapeDtypeStruct((1, GW), F32), jax.ShapeDtypeStruct((NG, CH, CH), F32),
                             jax.ShapeDtypeStruct((CH, GW), F32)],
                  compiler_params=_params(("arbitrary",)))(z, dcat, lng, lnb, ws, bfull)


def _group_sum(name, a):
    def body(a_ref, o_ref):
        lane = lax.broadcasted_iota(jnp.int32, (CH, 128), 1)
        out = jnp.zeros((CH, 128), F32)
        for g in range(NG):
            out = jnp.where(lane == g, jnp.sum(a_ref[:, g * HD:(g + 1) * HD], axis=1, keepdims=True), out)
        o_ref[...] = out

    return _pcall(body, name=name, out_shape=jax.ShapeDtypeStruct((CH, 128), F32))(a)


def _xattn_softmax(q_h, k_h):
    s = _dot(q_h, k_h, 'nt') * (XD ** -0.5)
    p = jnp.exp(s - jnp.max(s, axis=1, keepdims=True))
    return p / jnp.sum(p, axis=1, keepdims=True)


def _xattn_fwd(name, q, kv, B, T, bq):
    nq = T // bq

    def body(q_ref, kv_ref, o_ref):
        for h in range(XH):
            cols = slice(h * XD, (h + 1) * XD)
            p = _xattn_softmax(q_ref[:, cols], kv_ref[:, cols])
            o_ref[:, cols] = _dot(p.astype(BF), kv_ref[:, D + h * XD:D + (h + 1) * XD], 'nn').astype(BF)

    return _pcall(body, name=name, grid=(B, nq),
                  in_specs=[pl.BlockSpec((bq, D), lambda b, i: (b * nq + i, 0)),
                            pl.BlockSpec((NMEM, 2 * D), lambda b, i: (b, 0))],
                  out_specs=pl.BlockSpec((bq, D), lambda b, i: (b * nq + i, 0)),
                  out_shape=jax.ShapeDtypeStruct((B * T, D), BF), compiler_params=_params(("arbitrary", "arbitrary")))(q, kv)


def _xattn_bwd(name, q, kv, do, B, T, bq):
    nq = T // bq
    sc = XD ** -0.5

    def body(q_ref, kv_ref, do_ref, dq_ref, dkv_ref):
        @pl.when(pl.program_id(1) == 0)
        def _():
            dkv_ref[...] = jnp.zeros(dkv_ref.shape, F32)

        for h in range(XH):
            cols = slice(h * XD, (h + 1) * XD)
            vcols = slice(D + h * XD, D + (h + 1) * XD)
            qh, kh, doh = q_ref[:, cols], kv_ref[:, cols], do_ref[:, cols]
            p = _xattn_softmax(qh, kh)
            dp = _dot(doh, kv_ref[:, vcols], 'nt')
            ds = p * (dp - jnp.sum(p * dp, axis=1, keepdims=True))
            dsb = ds.astype(BF)
            dq_ref[:, cols] = (_dot(dsb, kh, 'nn') * sc).astype(BF)
            dkv_ref[:, cols] += _dot(dsb, qh, 'tn') * sc
            dkv_ref[:, vcols] += _dot(p.astype(BF), doh, 'tn')

    blk = pl.BlockSpec((bq, D), lambda b, i: (b * nq + i, 0))
    return _pcall(body, name=name, grid=(B, nq),
                  in_specs=[blk, pl.BlockSpec((NMEM, 2 * D), lambda b, i: (b, 0)), blk],
                  out_specs=[blk, pl.BlockSpec((NMEM, 2 * D), lambda b, i: (b, 0))],
                  out_shape=[jax.ShapeDtypeStruct((B * T, D), BF), jax.ShapeDtypeStruct((B * NMEM, 2 * D), F32)],
                  compiler_params=_params(("arbitrary", "arbitrary")))(q, kv, do)


def _ln_stats(v):
    mu = jnp.mean(v, axis=-1, keepdims=True)
    xc = v - mu
    rstd = lax.rsqrt(jnp.mean(xc * xc, axis=-1, keepdims=True) + EPS)
    return xc * rstd, rstd


SUB = 8


def _fill_phases(win, sh, rows):
    for b in range(1, SUB):
        sh[b - 1] = win[b:b + rows, :]


def _shifted(win, sh, o, bt):
    a, b = divmod(o, SUB)
    return win[SUB * a:SUB * a + bt, :] if b == 0 else sh[b - 1, SUB * a:SUB * a + bt, :]


def _sum_groups(name, a):
    R, C = a.shape[0] // SUB, a.shape[1]

    def body(a_ref, o_ref):
        o_ref[...] = jnp.sum(a_ref[...].reshape(R, SUB, C), axis=1)

    return _pcall(body, name=name, out_shape=jax.ShapeDtypeStruct((R, C), F32))(a)


def _conv_fwd(y, w32, wb, lng, lnb, B, T, bt):
    nt = T // bt
    hb = bt // HALO
    prows = bt + HALO - SUB

    def body(y_ref, yp_ref, w_ref, wb_ref, lng_ref, lnb_ref, s_ref, yc_ref, win, sh):
        i = pl.program_id(1)
        win[0:HALO, :] = jnp.where(i > 0, yp_ref[...], 0.0)
        win[HALO:, :] = y_ref[...]
        _fill_phases(win, sh, prows)
        acc = jnp.zeros((bt, D), F32) + wb_ref[...]
        for j in range(CK):
            acc = acc + w_ref[j:j + 1, :] * _shifted(win, sh, HALO - (CK - 1) + j, bt)
        yc_ref[...] = acc
        xhat, _ = _ln_stats(acc)
        ln = xhat * lng_ref[...] + lnb_ref[...]
        s_ref[...] = (ln * _sigmoid(ln)).astype(BF)

    row = lambda n: pl.BlockSpec((n, D), lambda b, i: (0, 0))
    cur = pl.BlockSpec((bt, D), lambda b, i: (b * nt + i, 0))
    return _pcall(body, name="conv_fwd", grid=(B, nt),
                  in_specs=[cur, pl.BlockSpec((HALO, D), lambda b, i: (jnp.maximum((b * nt + i) * hb - 1, 0), 0)),
                            row(HALO), row(1), row(1), row(1)],
                  out_specs=[cur, cur],
                  out_shape=[jax.ShapeDtypeStruct((B * T, D), BF), jax.ShapeDtypeStruct((B * T, D), F32)],
                  scratch_shapes=[pltpu.VMEM((bt + HALO, D), F32), pltpu.VMEM((SUB - 1, prows, D), F32)],
                  compiler_params=_params(("arbitrary", "arbitrary")))(y, y, w32, wb, lng, lnb)


def _conv_bwd(ds, yc, y, pa, pg, w32, lng, lnb, B, T, bt):
    nt = T // bt
    hb = bt // HALO
    nblk32 = B * T // HALO

    def ln_bwd(dsv, ycv, lng, lnb):
        xhat, rstd = _ln_stats(ycv)
        ln = xhat * lng + lnb
        sg = _sigmoid(ln)
        dln = dsv * (sg * (1.0 + ln * (1.0 - sg)))
        dxh = dln * lng
        dyc = rstd * (dxh - jnp.mean(dxh, axis=-1, keepdims=True)
                      - xhat * jnp.mean(dxh * xhat, axis=-1, keepdims=True))
        return dyc, dln, xhat

    prows = bt + HALO - SUB

    def body(ds_ref, dsn_ref, yc_ref, ycn_ref, y_ref, yp_ref, pa_ref, pg_ref, w_ref, lng_ref, lnb_ref,
             dpa_ref, dpg_ref, dw_ref, dwb_ref, dlng_ref, dlnb_ref, dba_ref, dbg_ref, dwin, ywin, dsh, ysh):
        i = pl.program_id(1)

        @pl.when((pl.program_id(0) == 0) & (i == 0))
        def _():
            for r in (dw_ref, dwb_ref, dlng_ref, dlnb_ref, dba_ref, dbg_ref):
                r[...] = jnp.zeros(r.shape, F32)

        lng, lnb = lng_ref[...], lnb_ref[...]
        dyc, dln, xhat = ln_bwd(ds_ref[...].astype(F32), yc_ref[...], lng, lnb)
        dycn, _, _ = ln_bwd(dsn_ref[...].astype(F32), ycn_ref[...], lng, lnb)
        dwin[0:bt, :] = dyc
        dwin[bt:, :] = jnp.where(i < nt - 1, dycn, 0.0)
        ywin[0:HALO, :] = jnp.where(i > 0, yp_ref[...], 0.0)
        ywin[HALO:, :] = y_ref[...]
        dlng_ref[...] += jnp.sum(dln * xhat, axis=0, keepdims=True)
        dlnb_ref[...] += jnp.sum(dln, axis=0, keepdims=True)
        dwb_ref[...] += jnp.sum(dyc, axis=0, keepdims=True)
        _fill_phases(dwin, dsh, prows)
        _fill_phases(ywin, ysh, prows)
        dy = jnp.zeros((bt, D), F32)
        for j in range(CK):
            dy = dy + w_ref[j:j + 1, :] * _shifted(dwin, dsh, CK - 1 - j, bt)
            prod = dyc * _shifted(ywin, ysh, HALO - (CK - 1) + j, bt)
            dw_ref[SUB * j:SUB * (j + 1), :] += jnp.sum(prod.reshape(bt // SUB, SUB, D), axis=0)
        a, g = pa_ref[...].astype(F32), pg_ref[...].astype(F32)
        sg = _sigmoid(g)
        da = dy * sg
        dg = dy * a * sg * (1.0 - sg)
        dpa_ref[...] = da.astype(BF)
        dpg_ref[...] = dg.astype(BF)
        dba_ref[...] += jnp.sum(da, axis=0, keepdims=True)
        dbg_ref[...] += jnp.sum(dg, axis=0, keepdims=True)

    cur = pl.BlockSpec((bt, D), lambda b, i: (b * nt + i, 0))
    nxt = pl.BlockSpec((HALO, D), lambda b, i: (jnp.minimum((b * nt + i + 1) * hb, nblk32 - 1), 0))
    prv = pl.BlockSpec((HALO, D), lambda b, i: (jnp.maximum((b * nt + i) * hb - 1, 0), 0))
    row = lambda n: pl.BlockSpec((n, D), lambda b, i: (0, 0))
    N = B * T
    return _pcall(body, name="conv_bwd", grid=(B, nt),
                  in_specs=[cur, nxt, cur, nxt, cur, prv, cur, cur, row(HALO), row(1), row(1)],
                  out_specs=[cur, cur, row(HALO * SUB), row(1), row(1), row(1), row(1), row(1)],
                  out_shape=[jax.ShapeDtypeStruct((N, D), BF)] * 2 + [jax.ShapeDtypeStruct((HALO * SUB, D), F32)]
                  + [jax.ShapeDtypeStruct((1, D), F32)] * 5,
                  scratch_shapes=[pltpu.VMEM((bt + HALO, D), F32), pltpu.VMEM((bt + HALO, D), F32),
                                  pltpu.VMEM((SUB - 1, prows, D), F32), pltpu.VMEM((SUB - 1, prows, D), F32)],
                  compiler_params=_params(("arbitrary", "arbitrary")))(ds, ds, yc, yc, y, y, pa, pg, w32, lng, lnb)


def _head(x, tgt, gain, bt=512):
    N = x.shape[0]
    bt = min(bt, N)

    def body(x_ref, t_ref, g_ref, dx_ref, loss_ref, dg_ref):
        @pl.when(pl.program_id(0) == 0)
        def _():
            loss_ref[...] = jnp.zeros(loss_ref.shape, F32)
            dg_ref[...] = jnp.zeros(dg_ref.shape, F32)

        xv = x_ref[...]
        gain = g_ref[...]
        err = xv * _rms_stats(xv) * gain - t_ref[...]
        loss_ref[...] += 0.5 * jnp.sum(jnp.mean(err * err, axis=-1, keepdims=True), axis=0, keepdims=True)
        dx, dgr = _rms_bwd(xv, gain, err * (1.0 / D))
        dx_ref[...] = dx
        dg_ref[...] += jnp.sum(dgr, axis=0, keepdims=True)

    blk = pl.BlockSpec((bt, D), lambda i: (i, 0))
    return _pcall(body, name="loss_head", grid=(N // bt,),
                  in_specs=[blk, blk, pl.BlockSpec((1, D), lambda i: (0, 0))],
                  out_specs=[blk, pl.BlockSpec((1, 128), lambda i: (0, 0)), pl.BlockSpec((1, D), lambda i: (0, 0))],
                  out_shape=[jax.ShapeDtypeStruct((N, D), F32), jax.ShapeDtypeStruct((1, 128), F32),
                             jax.ShapeDtypeStruct((1, D), F32)],
                  compiler_params=_params(("arbitrary",)))(x, tgt, gain)


def _local_step(x, mem, tgt, Wb, P):
    B, T, _ = x.shape
    N = B * T
    bq = bk = min(512, T)
    bt = min(512, T)
    x0 = x.reshape(N, D)
    mem2 = mem.reshape(B * NMEM, D)
    tgt2 = tgt.reshape(N, D)
    row = lambda v: v.reshape(1, -1)
    G = {}

    w_in = Wb['w_in_e'][0]
    w_inp = jnp.concatenate([w_in[:, 3 * FOXW + NG:], w_in[:, :3 * FOXW], w_in[:, 3 * FOXW:3 * FOXW + NG],
                             jnp.zeros((D, 128 - NG), BF)], axis=1)
    g_e = row(P['mix_norm_e'])
    z, h0 = _norm_mm("proj_z", x0, g_e, w_inp, N=D, coff=0, bn=D, out_dtype=F32, h_out=True)
    qkv = _norm_mm("proj_qkv", x0, g_e, w_inp, N=3 * FOXW, coff=2, bn=FOXW, out_dtype=BF)[0]
    fl = _norm_mm("proj_f", x0, g_e, w_inp, N=128, coff=20, bn=128, out_dtype=F32)[0]
    fbias = jnp.concatenate([P['fox_f_bias'].reshape(1, NG), jnp.zeros((1, 128 - NG), F32)], axis=1)
    cum = _fox_gate_fwd(fl, fbias, B, T)
    cum4 = cum[:, :NG].reshape(B, T, NG).transpose(0, 2, 1).reshape(B, NG, T // bk, bk)
    b_out, lse = _fox_fwd(qkv, cum4, B, T, 0, bq, bk)
    lng, lnb = row(P['gmlp_ln_g']), row(P['gmlp_ln_b'])
    ws = P['gmlp_w_s'][0]
    bfull = jnp.repeat(P['gmlp_b_s'][0].T, HD, axis=1)
    a_out = _gmlp_fwd(z, lng, lnb, ws, bfull, bt)
    w_out = Wb['w_out_e'][0]
    x1 = _mm_resid("mix_out", [dict(A=b_out, Ka=FOXW, B=w_out, roff=0), dict(A=a_out, Ka=GW, B=w_out, roff=1)], x0)

    def xa_ffn_fwd(l, xin):
        qx, hq = _norm_mm(f"xa_q{l}", xin, row(P['xa_norm'][l]), Wb['xa_wq'][l], N=D, bn=D, out_dtype=BF, h_out=True)
        kv, hm = _norm_mm(f"xa_kv{l}", mem2, row(P['mem_norm'][l]), Wb['xa_wkv'][l], N=2 * D, bn=D, out_dtype=BF,
                          h_out=True)
        o = _xattn_fwd(f"xattn_fwd{l}", qx, kv, B, T, bt)
        xm = _mm_resid(f"xa_o{l}", [dict(A=o, Ka=D, B=Wb['xa_wo'][l])], xin)
        wgu = Wb['ffn_w_gu'][l]
        g, u, a, hf = _fused_mm(f"ffn_gu{l}", dims='nn', M=N, N=FF, bm=min(512, N), bn=FF // 2, x=xm,
                                gain=row(P['ffn_norm'][l]),
                                groups=[[dict(A=None, Ka=D, B=wgu, coff=0)], [dict(A=None, Ka=D, B=wgu, coff=2)]],
                                epi=_epi_swiglu, outs=[BF, BF, BF], h_out=True)
        xo = _mm_resid(f"ffn_down{l}", [dict(A=a, Ka=FF, B=Wb['ffn_w_down'][l])], xm)
        return xo, dict(xin=xin, qx=qx, hq=hq, kv=kv, hm=hm, o=o, xm=xm, g=g, u=u, a=a, hf=hf)

    x3, S0 = xa_ffn_fwd(0, x1)
    w_cin = Wb['conv_w_in'][0]
    b_cin = row(P['conv_b_in'])
    pa, pg, y, hc = _fused_mm("conv_in", dims='nn', M=N, N=D, bm=min(512, N), bn=D, x=x3, gain=row(P['mix_norm_o']),
                              groups=[[dict(A=None, Ka=D, B=w_cin, coff=0)], [dict(A=None, Ka=D, B=w_cin, coff=1)]],
                              epi=_epi_glu, outs=[BF, BF, F32], rows=[(b_cin, 0), (b_cin, 1)], h_out=True)
    w32 = jnp.concatenate([P['conv_dw_w'][0], jnp.zeros((HALO - CK, D), F32)], axis=0)
    cbt = min(256, T)
    s, yc = _conv_fwd(y, w32, row(P['conv_dw_b']), row(P['conv_ln_g']), row(P['conv_ln_b']), B, T, cbt)
    x4 = _mm_resid("conv_out", [dict(A=s, Ka=D, B=Wb['conv_w_out'][0])], x3, bias=row(P['conv_b_out']))
    x6, S1 = xa_ffn_fwd(1, x4)
    dx, loss_t, dgf = _head(x6, tgt2, row(P['final_norm']))
    G['final_norm'] = dgf.reshape(D)

    def xa_ffn_bwd(l, S, dx):
        wgu, wdown = Wb['ffn_w_gu'][l], Wb['ffn_w_down'][l]
        dwdown = _mm_tn(f"dw_down{l}", S['a'], dx)
        dg, du = _fused_mm(f"ffn_dgu{l}", dims='nt', M=N, N=FF, bm=min(512, N), bn=FF // 2,
                           groups=[[dict(A=dx, Ka=D, B=wdown)]], epi=_epi_swiglu_bwd, outs=[BF, BF],
                           tiles=[(S['g'], 0), (S['u'], 0)])
        dwgu = jnp.concatenate([_mm_tn(f"dw_g{l}", S['hf'], dg), _mm_tn(f"dw_u{l}", S['hf'], du)], axis=1)
        dx, dgn = _mm_nt_rms_bwd(f"ffn_dx{l}", [dict(A=dg, Ka=FF, B=wgu, coff=0), dict(A=du, Ka=FF, B=wgu, coff=1)],
                                 S['xm'], row(P['ffn_norm'][l]), dx)
        dwo = _mm_tn(f"dw_o{l}", S['o'], dx)
        do = _mm_nt_plain(f"xa_do{l}", dx, Wb['xa_wo'][l])
        dq, dkv = _xattn_bwd(f"xattn_bwd{l}", S['qx'], S['kv'], do, B, T, bt)
        dwq = _mm_tn(f"dw_q{l}", S['hq'], dq)
        dwkv = _mm_tn(f"dw_kv{l}", S['hm'], dkv)
        dmn = _fused_mm(f"xa_dmem{l}", dims='nt', M=B * NMEM, N=D, bm=min(256, B * NMEM), bn=D,
                        groups=[[dict(A=dkv, Ka=2 * D, B=Wb['xa_wkv'][l])]], epi=_epi_rms_gain_only, outs=[],
                        tiles=[(mem2, 0)], rows=[(row(P['mem_norm'][l]), 0)], reds=[(1, D)])[0]
        dx, dxn = _mm_nt_rms_bwd(f"xa_dx{l}", [dict(A=dq, Ka=D, B=Wb['xa_wq'][l])], S['xin'],
                                 row(P['xa_norm'][l]), dx)
        return dx, dict(ffn_w_down=dwdown, ffn_w_gu=dwgu, ffn_norm=dgn.reshape(D), xa_wo=dwo, xa_wq=dwq,
                        xa_wkv=dwkv, mem_norm=dmn.reshape(D), xa_norm=dxn.reshape(D))

    dx, G1 = xa_ffn_bwd(1, S1, dx)
    G['conv_w_out'] = _mm_tn("dw_cout", s, dx)[None]
    G['conv_b_out'] = _colsum("db_cout", dx)
    dsv = _mm_nt_plain("conv_ds", dx, Wb['conv_w_out'][0])
    dpa, dpg, dw32, dwb, dlng, dlnb, dba, dbg = _conv_bwd(dsv, yc, y, pa, pg, w32, row(P['conv_ln_g']),
                                                          row(P['conv_ln_b']), B, T, cbt)
    G['conv_dw_w'] = _sum_groups("conv_dw_sum", dw32)[:CK][None]
    G['conv_dw_b'], G['conv_ln_g'], G['conv_ln_b'] = dwb, dlng, dlnb
    G['conv_b_in'] = jnp.concatenate([dba, dbg], axis=1)
    G['conv_w_in'] = jnp.concatenate([_mm_tn("dw_cin_a", hc, dpa), _mm_tn("dw_cin_g", hc, dpg)], axis=1)[None]
    dx, dgo = _mm_nt_rms_bwd("conv_dx", [dict(A=dpa, Ka=D, B=w_cin, coff=0), dict(A=dpg, Ka=D, B=w_cin, coff=1)],
                             x3, row(P['mix_norm_o']), dx)
    G['mix_norm_o'] = dgo
    dx, G0 = xa_ffn_bwd(0, S0, dx)
    for k in G0:
        G[k] = [G0[k], G1[k]]
    dw_out = jnp.concatenate([_mm_tn("dw_out_b", b_out, dx), _mm_tn("dw_out_a", a_out, dx)], axis=0)
    G['w_out_e'] = dw_out[None]
    dcat = _mm_nt_plain("mix_dcat", dx, w_out)
    dq, dk, dv, dcum4, dcq4 = _fox_bwd(qkv, cum4, b_out, lse, dcat, B, T, 0, bq, bk)
    dz, dlg, dlb, dws, dbf = _gmlp_bwd(z, dcat, lng, lnb, ws, bfull, bt)
    G['gmlp_ln_g'], G['gmlp_ln_b'], G['gmlp_w_s'] = dlg, dlb, dws[None]
    G['gmlp_b_s'] = _group_sum("gmlp_db", dbf)[:, :NG].T[None]
    pad = jnp.zeros((N, 128 - NG), F32)
    dck = jnp.concatenate([dcum4.reshape(B, NG, T).transpose(0, 2, 1).reshape(N, NG), pad], axis=1)
    dcq = jnp.concatenate([dcq4.reshape(N, NG // 2, 128)[:, :, :2].reshape(N, NG), pad], axis=1)
    dfl, dfb = _fox_gate_bwd(fl, fbias, dcq, dck, B, T)
    G['fox_f_bias'] = dfb[:, :NG]
    dw_f = _mm_tn("dw_in_f", h0, dfl)
    G['w_in_e'] = jnp.concatenate([_mm_tn("dw_in_q", h0, dq), _mm_tn("dw_in_k", h0, dk), _mm_tn("dw_in_v", h0, dv),
                                   dw_f[:, :NG], _mm_tn("dw_in_z", h0, dz)], axis=1)[None]
    dx, dge = _mm_nt_rms_bwd("mix_dx", [dict(A=dz, Ka=D, B=w_inp, coff=0), dict(A=dq, Ka=FOXW, B=w_inp, coff=2),
                                        dict(A=dk, Ka=FOXW, B=w_inp, coff=3), dict(A=dv, Ka=FOXW, B=w_inp, coff=4),
                                        dict(A=dfl, Ka=128, B=w_inp, coff=20)], x0, g_e, dx)
    G['mix_norm_e'] = dge
    return loss_t[0, 0], dx.reshape(B, T, D), G


COLS = 1024
ANY = pl.BlockSpec(memory_space=pl.ANY)


def _coords():
    return lax.axis_index("x"), lax.axis_index("y"), lax.axis_index("c")


def _other_chips(x, y):
    return [(1 - x, y), (x, 1 - y), (1 - x, 1 - y)]


def _all_gather(name, v):
    _, R, C = v.shape

    def body(v_ref, out_ref, send_sems, recv_sems, local_sem):
        x, y, c = _coords()
        me = 2 * x + y
        sib = (x, y, 1 - c)
        chips = _other_chips(x, y)

        def rcopy(k, src, chip_idx, half, to):
            return pltpu.make_async_remote_copy(src_ref=src, dst_ref=out_ref.at[chip_idx, half],
                                                send_sem=send_sems.at[k], recv_sem=recv_sems.at[k],
                                                device_id=to, device_id_type=MESH)

        local = pltpu.make_async_copy(v_ref, out_ref.at[me], local_sem)
        local.start()
        first = [rcopy(j, v_ref.at[c], me, c, (cx, cy, c)) for j, (cx, cy) in enumerate(chips)]
        for cp in first:
            cp.start()
        passed = []
        for j, (cx, cy) in enumerate(chips):
            kj = 2 * cx + cy
            rcopy(j, v_ref.at[c], kj, c, sib).wait_recv()
            fwd = rcopy(3 + j, out_ref.at[kj, c], kj, c, sib)
            fwd.start()
            passed.append(fwd)
        for j, (cx, cy) in enumerate(chips):
            rcopy(3 + j, v_ref.at[c], 2 * cx + cy, 1 - c, sib).wait_recv()
        for cp in first + passed:
            cp.wait_send()
        local.wait()

    return _pcall(body, name=name, in_specs=[ANY], out_specs=ANY,
                  out_shape=jax.ShapeDtypeStruct((4, 2, R, C), v.dtype),
                  scratch_shapes=[pltpu.SemaphoreType.DMA((6,)), pltpu.SemaphoreType.DMA((6,)),
                                  pltpu.SemaphoreType.DMA(())])(v)


def _sem_pairs(n):
    return [pltpu.SemaphoreType.DMA((n,)), pltpu.SemaphoreType.DMA((n,))]


def _sibling_halves(name, ps):
    n = len(ps)

    def body(*refs):
        p_refs, out_refs, (send_sems, recv_sems) = refs[:n], refs[n:2 * n], refs[2 * n:]
        x, y, c = _coords()
        cps = [pltpu.make_async_remote_copy(src_ref=p_refs[a].at[k, 1 - c], dst_ref=out_refs[a].at[k],
                                            send_sem=send_sems.at[4 * a + k], recv_sem=recv_sems.at[4 * a + k],
                                            device_id=(x, y, 1 - c), device_id_type=MESH)
               for a in range(n) for k in range(NCHIP)]
        for cp in cps:
            cp.start()
        for cp in cps:
            cp.wait()

    return _pcall(body, name=name, in_specs=[ANY] * n, out_specs=[ANY] * n,
                  out_shape=[jax.ShapeDtypeStruct((NCHIP,) + p.shape[2:], p.dtype) for p in ps],
                  scratch_shapes=_sem_pairs(NCHIP * n))(*ps)


def _chip_exchange(name, qs):
    n = len(qs)

    def body(*refs):
        q_refs, out_refs, (send_sems, recv_sems) = refs[:n], refs[n:2 * n], refs[2 * n:]
        x, y, c = _coords()
        cps = [pltpu.make_async_remote_copy(src_ref=q_refs[a].at[2 * cx + cy], dst_ref=out_refs[a].at[j],
                                            send_sem=send_sems.at[3 * a + j], recv_sem=recv_sems.at[3 * a + j],
                                            device_id=(cx, cy, c), device_id_type=MESH)
               for a in range(n) for j, (cx, cy) in enumerate(_other_chips(x, y))]
        for cp in cps:
            cp.start()
        for cp in cps:
            cp.wait()

    return _pcall(body, name=name, in_specs=[ANY] * n, out_specs=[ANY] * n,
                  out_shape=[jax.ShapeDtypeStruct((3,) + q.shape[1:], q.dtype) for q in qs],
                  scratch_shapes=_sem_pairs(3 * n))(*qs)


def _sibling_swap(name, hs):
    n = len(hs)

    def body(*refs):
        h_refs, out_refs, (send_sems, recv_sems, local_sems) = refs[:n], refs[n:2 * n], refs[2 * n:]
        x, y, c = _coords()
        sib = (x, y, 1 - c)
        locals_ = [pltpu.make_async_copy(h_refs[a], out_refs[a].at[c], local_sems.at[a]) for a in range(n)]
        sends = [pltpu.make_async_remote_copy(src_ref=h_refs[a], dst_ref=out_refs[a].at[c], send_sem=send_sems.at[a],
                                              recv_sem=recv_sems.at[a], device_id=sib, device_id_type=MESH)
                 for a in range(n)]
        for cp in locals_ + sends:
            cp.start()
        for a in range(n):
            pltpu.make_async_remote_copy(src_ref=h_refs[a], dst_ref=out_refs[a].at[1 - c], send_sem=send_sems.at[a],
                                         recv_sem=recv_sems.at[a], device_id=sib, device_id_type=MESH).wait_recv()
        for cp in sends:
            cp.wait_send()
        for cp in locals_:
            cp.wait()

    return _pcall(body, name=name, in_specs=[ANY] * n, out_specs=[ANY] * n,
                  out_shape=[jax.ShapeDtypeStruct((2,) + h.shape, h.dtype) for h in hs],
                  scratch_shapes=_sem_pairs(n) + [pltpu.SemaphoreType.DMA((n,))])(*hs)


def _row_block(R):
    for br in (512, 256, 128, 64, 32, 16, 8):
        if R % br == 0:
            return br
    return R


def _add_own_half(name, p, recv, c_arr, out_dtype):
    _, _, R, C = p.shape
    br = _row_block(R)

    def body(c_ref, p_ref, r_ref, o_ref):
        o_ref[...] = (p_ref[...].astype(F32) + r_ref[...].astype(F32)).astype(o_ref.dtype)

    spec = pltpu.PrefetchScalarGridSpec(
        num_scalar_prefetch=1, grid=(NCHIP, R // br),
        in_specs=[pl.BlockSpec((None, None, br, C), lambda k, r, c_ref: (k, c_ref[0], r, 0)),
                  pl.BlockSpec((None, br, C), lambda k, r, c_ref: (k, r, 0))],
        out_specs=pl.BlockSpec((None, br, C), lambda k, r, c_ref: (k, r, 0)))
    return _pcall(body, name=name, grid_spec=spec, out_shape=jax.ShapeDtypeStruct((NCHIP, R, C), out_dtype),
                  compiler_params=_params(("arbitrary", "arbitrary")))(c_arr, p, recv)


def _add_chips(name, q, recv, me_arr):
    _, R, C = q.shape
    br = _row_block(R)

    def body(me_ref, q_ref, r_ref, o_ref):
        o_ref[...] = ((q_ref[...].astype(F32) + r_ref[0].astype(F32)) + r_ref[1].astype(F32)) + r_ref[2].astype(F32)

    spec = pltpu.PrefetchScalarGridSpec(
        num_scalar_prefetch=1, grid=(R // br,),
        in_specs=[pl.BlockSpec((None, br, C), lambda r, me_ref: (me_ref[0], r, 0)),
                  pl.BlockSpec((3, br, C), lambda r, me_ref: (0, r, 0))],
        out_specs=pl.BlockSpec((br, C), lambda r, me_ref: (r, 0)))
    return _pcall(body, name=name, grid_spec=spec, out_shape=jax.ShapeDtypeStruct((R, C), F32),
                  compiler_params=_params(("arbitrary",)))(me_arr, q, recv)


def _reduce_scatter(p_big, p_small, c_arr, me_arr):
    got_b, got_s = _sibling_halves("rs_sibling_halves", [p_big, p_small])
    q_b = _add_own_half("rs_add_pair_big", p_big, got_b, c_arr, BF)
    q_s = _add_own_half("rs_add_pair_small", p_small, got_s, c_arr, F32)
    got_b, got_s = _chip_exchange("rs_chip_exchange", [q_b, q_s])
    h_b = _add_chips("rs_add_chips_big", q_b, got_b, me_arr)
    h_s = _add_chips("rs_add_chips_small", q_s, got_s, me_arr)
    return _sibling_swap("rs_sibling_swap", [h_b, h_s])


def _adamw(name, w, g, m, v):
    shape = w.shape
    cols = shape[-1]
    rows = w.size // cols
    w2, g2, m2, v2 = (a.reshape(rows, cols) for a in (w, g, m, v))
    bt = 128 if rows % 128 == 0 else rows

    def body(w_ref, g_ref, m_ref, v_ref, d_ref, nm_ref, nv_ref):
        gv = g_ref[...]
        nm = ADAM_B1 * m_ref[...] + (1.0 - ADAM_B1) * gv
        nv = ADAM_B2 * v_ref[...] + (1.0 - ADAM_B2) * (gv * gv)
        m_hat = nm / (1.0 - ADAM_B1 ** ADAM_STEP)
        v_hat = nv / (1.0 - ADAM_B2 ** ADAM_STEP)
        d_ref[...] = -ADAM_LR * (m_hat / (jnp.sqrt(v_hat) + ADAM_EPS) + ADAM_WD * w_ref[...])
        nm_ref[...] = nm
        nv_ref[...] = nv

    blk = pl.BlockSpec((bt, cols), lambda i: (i, 0))
    outs = _pcall(body, name=name, grid=(rows // bt,), in_specs=[blk] * 4, out_specs=[blk] * 3,
                  out_shape=[jax.ShapeDtypeStruct((rows, cols), F32)] * 3, compiler_params=_params(("arbitrary",)))(
        w2, g2, m2, v2)
    return [o.reshape(shape) for o in outs]


SMALL_SHARDED = ['mix_norm_o', 'conv_b_in', 'conv_dw_w', 'conv_dw_b', 'conv_ln_g', 'conv_ln_b', 'conv_b_out']
REPLICATED = [n for n in WEIGHTS if SHARD_AXIS[n] is None]
NCHIP = 4


def _halves(flat, tile_rows):
    unit = 2 * tile_rows * COLS
    total = -(-flat.size // unit) * unit
    return jnp.pad(flat, (0, total - flat.size)).reshape(2, total // (2 * COLS), COLS)


def _flat(arrays):
    return jnp.concatenate([a.reshape(-1) for a in arrays])


def _chip_block(a, axis, k):
    n = a.shape[axis] // NCHIP
    return lax.slice_in_dim(a, k * n, (k + 1) * n, axis=axis)


def _full_shape(n, shard_shape):
    s = list(shard_shape[n])
    s[SHARD_AXIS[n]] *= NCHIP
    return tuple(s)


def _unpack(flat, names, shapes):
    out, off = {}, 0
    for n in names:
        size = math.prod(shapes[n])
        out[n] = flat[off:off + size].reshape(shapes[n])
        off += size
    return out


def kernel(x, mem, mix_norm_e, w_in_e, fox_f_bias, gmlp_ln_g, gmlp_ln_b, gmlp_w_s, gmlp_b_s, w_out_e, mix_norm_o, conv_w_in, conv_b_in, conv_dw_w, conv_dw_b, conv_ln_g, conv_ln_b, conv_w_out, conv_b_out, xa_norm, mem_norm, xa_wq, xa_wkv, xa_wo, ffn_norm, ffn_w_gu, ffn_w_down, final_norm, loss_target, m_mix_norm_e, m_w_in_e, m_fox_f_bias, m_gmlp_ln_g, m_gmlp_ln_b, m_gmlp_w_s, m_gmlp_b_s, m_w_out_e, m_mix_norm_o, m_conv_w_in, m_conv_b_in, m_conv_dw_w, m_conv_dw_b, m_conv_ln_g, m_conv_ln_b, m_conv_w_out, m_conv_b_out, m_xa_norm, m_mem_norm, m_xa_wq, m_xa_wkv, m_xa_wo, m_ffn_norm, m_ffn_w_gu, m_ffn_w_down, m_final_norm, v_mix_norm_e, v_w_in_e, v_fox_f_bias, v_gmlp_ln_g, v_gmlp_ln_b, v_gmlp_w_s, v_gmlp_b_s, v_w_out_e, v_mix_norm_o, v_conv_w_in, v_conv_b_in, v_conv_dw_w, v_conv_dw_b, v_conv_ln_g, v_conv_ln_b, v_conv_w_out, v_conv_b_out, v_xa_norm, v_mem_norm, v_xa_wq, v_xa_wkv, v_xa_wo, v_ffn_norm, v_ffn_w_gu, v_ffn_w_down, v_final_norm):
    env = locals()
    w = {n: env[n] for n in WEIGHTS}
    m = {n: env["m_" + n] for n in WEIGHTS}
    v = {n: env["v_" + n] for n in WEIGHTS}
    shard_shape = {n: w[n].shape for n in WEIGHTS}
    xi, yi, ci = _coords()
    c_arr = jnp.reshape(ci, (1,)).astype(jnp.int32)
    me_arr = jnp.reshape(2 * xi + yi, (1,)).astype(jnp.int32)

    def gather_full(name, names, dtype, tile_rows):
        got = _all_gather(name, _halves(_flat([w[n].astype(dtype) for n in names]), tile_rows))
        got = got.reshape(NCHIP, -1)
        parts = [_unpack(got[k], names, shard_shape) for k in range(NCHIP)]
        return {n: jnp.concatenate([parts[k][n] for k in range(NCHIP)], axis=SHARD_AXIS[n]) for n in names}

    Wb = gather_full("gather_matrices", BIG, BF, 16)
    P = gather_full("gather_vectors", SMALL_SHARDED, F32, 8)
    P.update({n: w[n] for n in REPLICATED})

    loss_part, grad_x, G = _local_step(x, mem, loss_target, Wb, P)
    loss = lax.psum(loss_part, ("x", "y", "c"))

    def layers(n):
        return G[n] if isinstance(G[n], list) else ([G[n]] if G[n].ndim == 1 else [G[n][l] for l in range(G[n].shape[0])])

    def by_chip(names, dtype, extra, row_multiple):
        segs = [[_chip_block(a, SHARD_AXIS[n] - 1, k).reshape(-1).astype(dtype) for n in names for a in layers(n)]
                + ([extra[k]] if extra is not None else []) for k in range(NCHIP)]
        size = sum(piece.size for piece in segs[0])
        unit = 2 * row_multiple * COLS
        total = -(-size // unit) * unit
        pieces = [piece for seg in segs for piece in seg + [jnp.zeros((total - size,), dtype)]]
        return jnp.concatenate(pieces).reshape(NCHIP, 2, total // (2 * COLS), COLS)

    rep = _flat([a for n in REPLICATED for a in layers(n)])
    quarter = -(-rep.size // (NCHIP * 2 * 8 * COLS)) * (2 * 8 * COLS)
    rep = jnp.pad(rep, (0, NCHIP * quarter - rep.size)).reshape(NCHIP, quarter)
    p_big = by_chip(BIG, BF, None, 256)
    p_small = by_chip(SMALL_SHARDED, F32, rep, 8)
    red_big, red_small = _reduce_scatter(p_big, p_small, c_arr, me_arr)
    mine = _unpack(red_big.reshape(-1), BIG, shard_shape)
    red_small = red_small.reshape(-1)
    mine.update(_unpack(red_small, SMALL_SHARDED, shard_shape))
    off = sum(math.prod(shard_shape[n]) for n in SMALL_SHARDED)
    rep_all = _all_gather("gather_replicated_grads", red_small[off:off + quarter].reshape(2, quarter // (2 * COLS), COLS))
    mine.update(_unpack(rep_all.reshape(-1), REPLICATED, shard_shape))

    grads, deltas, new_m, new_v = [], [], [], []
    for n in WEIGHTS:
        d, nm, nv = _adamw("adamw_" + n, w[n], mine[n], m[n], v[n])
        grads.append(mine[n])
        deltas.append(d)
        new_m.append(nm)
        new_v.append(nv)
    return (loss, grad_x, *grads, *deltas, *new_m, *new_v)
```

```python
import functools
import math

import jax
import jax.numpy as jnp
from jax import lax
from jax.experimental import pallas as pl
from jax.experimental.pallas import tpu as pltpu

F32 = jnp.float32
BF = jnp.bfloat16
MESH = pl.DeviceIdType.MESH

D = 1024
FOXW = 512
HD = 64
GW = 512
CH = 128
NG = 8
FF = 2816
NMEM = 256
XH = 4
XD = 256
CK = 31
HALO = 32
EPS = 1e-6
IN_W = 2568
IN_WP = 2688
VMEM_LIMIT = 56 * 1024 * 1024

ADAM_LR, ADAM_B1, ADAM_B2, ADAM_EPS, ADAM_WD, ADAM_STEP = 0.001, 0.9, 0.999, 1e-08, 0.01, 10

WEIGHTS = ['mix_norm_e', 'w_in_e', 'fox_f_bias', 'gmlp_ln_g', 'gmlp_ln_b', 'gmlp_w_s', 'gmlp_b_s', 'w_out_e',
           'mix_norm_o', 'conv_w_in', 'conv_b_in', 'conv_dw_w', 'conv_dw_b', 'conv_ln_g', 'conv_ln_b',
           'conv_w_out', 'conv_b_out', 'xa_norm', 'mem_norm', 'xa_wq', 'xa_wkv', 'xa_wo', 'ffn_norm',
           'ffn_w_gu', 'ffn_w_down', 'final_norm']
SHARD_AXIS = {'mix_norm_e': None, 'w_in_e': 2, 'fox_f_bias': None, 'gmlp_ln_g': None, 'gmlp_ln_b': None,
              'gmlp_w_s': None, 'gmlp_b_s': None, 'w_out_e': 1, 'mix_norm_o': 1, 'conv_w_in': 2, 'conv_b_in': 1,
              'conv_dw_w': 2, 'conv_dw_b': 1, 'conv_ln_g': 1, 'conv_ln_b': 1, 'conv_w_out': 1, 'conv_b_out': 1,
              'xa_norm': None, 'mem_norm': None, 'xa_wq': 1, 'xa_wkv': 2, 'xa_wo': 1, 'ffn_norm': None,
              'ffn_w_gu': 2, 'ffn_w_down': 1, 'final_norm': None}
BIG = ['w_in_e', 'w_out_e', 'conv_w_in', 'conv_w_out', 'xa_wq', 'xa_wkv', 'xa_wo', 'ffn_w_gu', 'ffn_w_down']


def _pcall(body, **kw):
    return pl.pallas_call(body, **kw)


def _params(sem=None, **kw):
    return pltpu.CompilerParams(dimension_semantics=sem, vmem_limit_bytes=VMEM_LIMIT, **kw)


def _dot(a, b, dims):
    dn = {'nn': (((1,), (0,)), ((), ())), 'nt': (((1,), (1,)), ((), ())), 'tn': (((0,), (0,)), ((), ()))}[dims]
    return lax.dot_general(a, b, dn, preferred_element_type=F32)


def _sigmoid(x):
    return 1.0 / (1.0 + jnp.exp(-x))


def _rms_stats(xv):
    return lax.rsqrt(jnp.mean(xv * xv, axis=-1, keepdims=True) + EPS)


def _rms_bwd(xv, gain, dh):
    r = _rms_stats(xv)
    t = dh * gain
    dx = r * t - xv * (r * r * r * jnp.mean(t * xv, axis=-1, keepdims=True))
    return dx, dh * xv * r


def _fused_mm(name, *, dims, M, N, bm, bn, groups, epi, outs, x=None, gain=None, tiles=(), rows=(),
              h_out=False, reds=()):
    bm = min(bm, M)
    nI, nJ = M // bm, N // bn
    assert nI * bm == M and nJ * bn == N
    assert not reds or nJ == 1
    arrays, specs = [], []

    def add(arr, spec):
        arrays.append(arr)
        specs.append(spec)
        return len(arrays) - 1

    if x is not None:
        K0 = x.shape[1]
        add(x, pl.BlockSpec((bm, K0), lambda i, j: (i, 0)))
        add(gain, pl.BlockSpec((1, K0), lambda i, j: (0, 0)))
    plan = []
    for grp in groups:
        g = []
        for p in grp:
            ai = None
            if p['A'] is not None:
                ai = add(p['A'], pl.BlockSpec((bm, p['Ka']), lambda i, j, o=p.get('acoff', 0): (i, o)))
            ro, co = p.get('roff', 0), p.get('coff', 0)
            if dims == 'nn':
                bi = add(p['B'], pl.BlockSpec((p['Ka'], bn), lambda i, j, ro=ro, co=co: (ro, j + co)))
            else:
                bi = add(p['B'], pl.BlockSpec((bn, p['Ka']), lambda i, j, ro=ro, co=co: (j + ro, co)))
            g.append((ai, bi))
        plan.append(g)
    tile_idx = [add(a, pl.BlockSpec((bm, bn), lambda i, j, o=o: (i, j + o))) for a, o in tiles]
    row_idx = [add(a, pl.BlockSpec((1, bn), lambda i, j, o=o: (0, j + o))) for a, o in rows]
    n_in = len(arrays)

    out_shape = [jax.ShapeDtypeStruct((M, N), dt) for dt in outs]
    out_specs = [pl.BlockSpec((bm, bn), lambda i, j: (i, j)) for _ in outs]
    if h_out:
        out_shape.append(jax.ShapeDtypeStruct((M, x.shape[1]), BF))
        out_specs.append(pl.BlockSpec((bm, x.shape[1]), lambda i, j: (i, 0)))
    for shp in reds:
        out_shape.append(jax.ShapeDtypeStruct(shp, F32))
        out_specs.append(pl.BlockSpec(shp, lambda i, j: (0, 0)))
    n_main = len(outs)
    scratch = [pltpu.VMEM((bm, x.shape[1]), BF)] if x is not None else []

    def body(*refs):
        ins, out_refs, scr = refs[:n_in], refs[n_in:n_in + len(out_shape)], refs[n_in + len(out_shape):]
        i, j = pl.program_id(0), pl.program_id(1)
        if x is not None:
            hn_ref = scr[0]

            @pl.when(j == 0)
            def _():
                xv = ins[0][...]
                hn = (xv * _rms_stats(xv) * ins[1][...]).astype(BF)
                hn_ref[...] = hn
                if h_out:
                    out_refs[n_main][...] = hn

        accs = []
        for g in plan:
            acc = None
            for ai, bi in g:
                a = hn_ref[...] if ai is None else ins[ai][...]
                if a.dtype != BF:
                    a = a.astype(BF)
                d = _dot(a, ins[bi][...], dims)
                acc = d if acc is None else acc + d
            accs.append(acc)
        out_vals, red_vals = epi(accs, [ins[t][...] for t in tile_idx], [ins[r][...] for r in row_idx])
        for r, v in zip(out_refs[:n_main], out_vals):
            r[...] = v.astype(r.dtype)
        if reds:
            red_refs = out_refs[n_main + (1 if h_out else 0):]

            @pl.when(i == 0)
            def _():
                for r in red_refs:
                    r[...] = jnp.zeros(r.shape, F32)

            for r, v in zip(red_refs, red_vals):
                r[...] += v

    res = _pcall(body, name=name, grid=(nI, nJ), in_specs=specs, out_specs=out_specs, out_shape=out_shape,
                 scratch_shapes=scratch, compiler_params=_params(("arbitrary", "arbitrary")))(*arrays)
    return res


def _epi_plain(accs, tiles, rows):
    return [accs[0]], []


def _epi_resid(accs, tiles, rows):
    y = tiles[0] + accs[0]
    if rows:
        y = y + rows[0]
    return [y], []


def _epi_swiglu(accs, tiles, rows):
    g, u = accs
    return [g, u, g * _sigmoid(g) * u], []


def _epi_glu(accs, tiles, rows):
    a, g = accs[0] + rows[0], accs[1] + rows[1]
    return [a, g, a * _sigmoid(g)], []


def _epi_swiglu_bwd(accs, tiles, rows):
    da = accs[0]
    g, u = tiles[0].astype(F32), tiles[1].astype(F32)
    sg = _sigmoid(g)
    return [da * u * (sg * (1.0 + g * (1.0 - sg))), da * (g * sg)], []


def _epi_rms_bwd(accs, tiles, rows):
    dx, dgr = _rms_bwd(tiles[0], rows[0], accs[0])
    return [tiles[1] + dx], [jnp.sum(dgr, axis=0, keepdims=True)]


def _epi_rms_gain_only(accs, tiles, rows):
    _, dgr = _rms_bwd(tiles[0], rows[0], accs[0])
    return [], [jnp.sum(dgr, axis=0, keepdims=True)]


def _norm_mm(name, x, gain, W, *, N, coff=0, bn, out_dtype, bm=512, h_out=False):
    return _fused_mm(name, dims='nn', M=x.shape[0], N=N, bm=bm, bn=bn, x=x, gain=gain,
                     groups=[[dict(A=None, Ka=x.shape[1], B=W, coff=coff)]], epi=_epi_plain, outs=[out_dtype],
                     h_out=h_out)


def _mm_resid(name, pairs, resid, bias=None, bm=512):
    M = resid.shape[0]
    return _fused_mm(name, dims='nn', M=M, N=D, bm=bm, bn=D, groups=[pairs], epi=_epi_resid, outs=[F32],
                     tiles=[(resid, 0)], rows=[(bias, 0)] if bias is not None else [])[0]


def _mm_nt_plain(name, dy, W, bm=512):
    return _fused_mm(name, dims='nt', M=dy.shape[0], N=W.shape[0], bm=bm, bn=W.shape[0],
                     groups=[[dict(A=dy, Ka=dy.shape[1], B=W)]], epi=_epi_plain, outs=[BF])[0]


def _mm_nt_rms_bwd(name, pairs, x, gain, dx_in, bm=256):
    out = _fused_mm(name, dims='nt', M=x.shape[0], N=D, bm=bm, bn=D, groups=[pairs], epi=_epi_rms_bwd,
                    outs=[F32], tiles=[(x, 0), (dx_in, 0)], rows=[(gain, 0)], reds=[(1, D)])
    return out[0], out[1]


def _mm_tn(name, A, G, bk=1024, parts=1):
    T, Ka, Kg = A.shape[0], A.shape[1], G.shape[1]
    w = Kg // parts
    bm = Ka if Ka <= 1024 else Ka // 2
    bn = w if w <= 1408 else w // 2
    bk = min(bk, T)
    per = w // bn
    nI, nJ, nK = Ka // bm, Kg // bn, T // bk

    def body(a_ref, g_ref, o_ref, acc):
        k = pl.program_id(2)

        @pl.when(k == 0)
        def _():
            acc[...] = jnp.zeros(acc.shape, F32)

        acc[...] += _dot(a_ref[...].astype(BF), g_ref[...].astype(BF), 'tn')

        @pl.when(k == nK - 1)
        def _():
            o_ref[...] = acc[...].astype(BF)

    return _pcall(body, name=name, grid=(nI, nJ, nK),
                  in_specs=[pl.BlockSpec((bk, bm), lambda i, j, k: (k, i)),
                            pl.BlockSpec((bk, bn), lambda i, j, k: (k, j))],
                  out_specs=pl.BlockSpec((None, bm, bn), lambda i, j, k: (j // per, i, j % per)),
                  out_shape=jax.ShapeDtypeStruct((parts, Ka, w), BF),
                  scratch_shapes=[pltpu.VMEM((bm, bn), F32)],
                  compiler_params=_params(("arbitrary", "arbitrary", "arbitrary")))(A, G)


def _colsum(name, a, bt=512):
    M, N = a.shape
    bt = min(bt, M)

    def body(a_ref, o_ref):
        @pl.when(pl.program_id(0) == 0)
        def _():
            o_ref[...] = jnp.zeros(o_ref.shape, F32)

        o_ref[...] += jnp.sum(a_ref[...].astype(F32), axis=0, keepdims=True)

    return _pcall(body, name=name, grid=(M // bt,), in_specs=[pl.BlockSpec((bt, N), lambda i: (i, 0))],
                  out_specs=pl.BlockSpec((1, N), lambda i: (0, 0)), out_shape=jax.ShapeDtypeStruct((1, N), F32),
                  compiler_params=_params(("arbitrary",)))(a)


def _cumsum_rows(v):
    T = v.shape[0]
    row = lax.broadcasted_iota(jnp.int32, v.shape, 0)
    s = 1
    while s < T:
        v = v + jnp.where(row >= s, pltpu.roll(v, s, 0), 0.0)
        s *= 2
    return v


def _log_sigmoid(z):
    return jnp.minimum(z, 0.0) - jnp.log(1.0 + jnp.exp(-jnp.abs(z)))


def _fox_gate_fwd(fl, fbias, B, T):
    def body(fl_ref, b_ref, o_ref):
        o_ref[...] = _cumsum_rows(_log_sigmoid(fl_ref[...] + b_ref[...]))

    return _pcall(body, name="fox_gate_fwd", grid=(B,),
                  in_specs=[pl.BlockSpec((T, 128), lambda b: (b, 0)), pl.BlockSpec((1, 128), lambda b: (0, 0))],
                  out_specs=pl.BlockSpec((T, 128), lambda b: (b, 0)),
                  out_shape=jax.ShapeDtypeStruct((B * T, 128), F32), compiler_params=_params(("arbitrary",)))(fl, fbias)


def _fox_gate_bwd(fl, fbias, dcq, dck, B, T):
    def body(fl_ref, b_ref, dcq_ref, dck_ref, dfl_ref, db_ref):
        dc = dcq_ref[...] + dck_ref[...]
        rev = jnp.sum(dc, axis=0, keepdims=True) - _cumsum_rows(dc) + dc
        dfl = rev * _sigmoid(-(fl_ref[...] + b_ref[...]))
        dfl_ref[...] = dfl

        @pl.when(pl.program_id(0) == 0)
        def _():
            db_ref[...] = jnp.zeros(db_ref.shape, F32)

        db_ref[...] += jnp.sum(dfl, axis=0, keepdims=True)

    return _pcall(body, name="fox_gate_bwd", grid=(B,),
                  in_specs=[pl.BlockSpec((T, 128), lambda b: (b, 0)), pl.BlockSpec((1, 128), lambda b: (0, 0)),
                            pl.BlockSpec((T, 128), lambda b: (b, 0)), pl.BlockSpec((T, 128), lambda b: (b, 0))],
                  out_specs=[pl.BlockSpec((T, 128), lambda b: (b, 0)), pl.BlockSpec((1, 128), lambda b: (0, 0))],
                  out_shape=[jax.ShapeDtypeStruct((B * T, 128), F32), jax.ShapeDtypeStruct((1, 128), F32)],
                  compiler_params=_params(("arbitrary",)))(fl, fbias, dcq, dck)


NEG = -1e30


def _fox_fwd(qkv, cum4, B, T, qoff, bq, bk):
    nq, nkb = T // bq, T // bk
    N = B * T

    def body(q_ref, k_ref, v_ref, cum_ref, o_ref, lse_ref):
        hp, i = pl.program_id(1), pl.program_id(2)
        lane = lax.broadcasted_iota(jnp.int32, (bq, 128), 1)
        heads = [slice(e * HD, (e + 1) * HD) for e in range(2)]
        qs = [q_ref[:, sl] * 0.125 for sl in heads]

        def block(j, carry, diagonal):
            ks = pl.multiple_of(j * bk, bk)
            out = []
            for e, sl in enumerate(heads):
                m, l, acc = carry[e]
                s = _dot(qs[e], k_ref[pl.ds(ks, bk), sl], 'nt') - cum_ref[0, 2 * hp + e, pl.ds(j, 1), :]
                if diagonal:
                    keep = lax.broadcasted_iota(jnp.int32, (bq, bk), 0) >= lax.broadcasted_iota(jnp.int32, (bq, bk), 1)
                    s = jnp.where(keep, s, NEG)
                m_new = jnp.maximum(m, jnp.max(s, axis=1, keepdims=True))
                p = jnp.exp(s - m_new)
                alpha = jnp.exp(m - m_new)
                l = alpha * l + jnp.sum(p, axis=1, keepdims=True)
                acc = alpha * acc + _dot(p.astype(BF), v_ref[pl.ds(ks, bk), sl], 'nn')
                out.append((m_new, l, acc))
            return tuple(out)

        init = tuple((jnp.full((bq, 1), NEG, F32), jnp.zeros((bq, 1), F32), jnp.zeros((bq, HD), F32)) for _ in heads)
        carry = lax.fori_loop(0, i, lambda j, c: block(j, c, False), init)
        carry = block(i, carry, True)
        lse_tile = jnp.zeros((bq, 128), F32)
        for e, sl in enumerate(heads):
            m, l, acc = carry[e]
            o_ref[:, sl] = (acc / l).astype(BF)
            lse_tile = jnp.where(lane == e, m + jnp.log(l), lse_tile)
        lse_ref[...] = lse_tile

    return _pcall(body, name="fox_fwd", grid=(B, 4, nq),
                  in_specs=[pl.BlockSpec((bq, 128), lambda b, h, i: (b * nq + i, qoff + h)),
                            pl.BlockSpec((T, 128), lambda b, h, i: (b, qoff + 4 + h)),
                            pl.BlockSpec((T, 128), lambda b, h, i: (b, qoff + 8 + h)),
                            pl.BlockSpec((1, NG, nkb, bk), lambda b, h, i: (b, 0, 0, 0))],
                  out_specs=[pl.BlockSpec((bq, 128), lambda b, h, i: (b * nq + i, h)),
                             pl.BlockSpec((bq, 128), lambda b, h, i: (b * nq + i, h))],
                  out_shape=[jax.ShapeDtypeStruct((N, FOXW), BF), jax.ShapeDtypeStruct((N, FOXW), F32)],
                  compiler_params=_params(("arbitrary", "arbitrary", "arbitrary")))(qkv, qkv, qkv, cum4)


def _fox_bwd(qkv, cum4, o, lse, dcat, B, T, qoff, bq, bk):
    nq, nkb = T // bq, T // bk
    N = B * T

    def body(q_ref, k_ref, v_ref, cum_ref, o_ref, lse_ref, do_ref, dq_ref, dk_ref, dv_ref, dcum_ref, dcq_ref,
             dq_acc, dl_ref, rs_ref):
        hp = pl.program_id(1)
        heads = [slice(e * HD, (e + 1) * HD) for e in range(2)]
        keep = lax.broadcasted_iota(jnp.int32, (bq, bk), 0) >= lax.broadcasted_iota(jnp.int32, (bq, bk), 1)
        dcq_ref[...] = jnp.zeros(dcq_ref.shape, F32)
        dq_acc[...] = jnp.zeros(dq_acc.shape, F32)
        rs_ref[...] = jnp.zeros(rs_ref.shape, F32)
        for e, sl in enumerate(heads):
            dl_ref[e] = jnp.sum(do_ref[:, sl].astype(F32) * o_ref[:, sl].astype(F32), axis=1, keepdims=True)
        for j in range(nkb):
            krows = slice(j * bk, (j + 1) * bk)

            def tile(i, carry, diagonal):
                qs = i * bq if diagonal else pl.multiple_of(i * bq, bq)
                out = []
                for e, sl in enumerate(heads):
                    dk_a, dv_a, cs = carry[e]
                    q, k = q_ref[pl.ds(qs, bq), sl], k_ref[krows, sl]
                    do = do_ref[pl.ds(qs, bq), sl]
                    s = _dot(q, k, 'nt') * 0.125 - cum_ref[0, 2 * hp + e, j:j + 1, :]
                    p = jnp.exp(s - lse_ref[pl.ds(qs, bq), e:e + 1])
                    if diagonal:
                        p = jnp.where(keep, p, 0.0)
                    dv_a = dv_a + _dot(p.astype(BF), do, 'tn')
                    ds = p * (_dot(do, v_ref[krows, sl], 'nt') - dl_ref[e, pl.ds(qs, bq), :])
                    cs = cs + jnp.sum(ds, axis=0, keepdims=True)
                    rs_ref[e, pl.ds(qs, bq), :] += jnp.sum(ds, axis=1, keepdims=True)
                    dsb = ds.astype(BF)
                    dk_a = dk_a + _dot(dsb, q, 'tn')
                    dq_acc[e, pl.ds(qs, bq), :] += _dot(dsb, k, 'nn')
                    out.append((dk_a, dv_a, cs))
                return tuple(out)

            init = tuple((jnp.zeros((bk, HD), F32), jnp.zeros((bk, HD), F32), jnp.zeros((1, bk), F32)) for _ in heads)
            carry = lax.fori_loop(j + 1, nq, lambda i, c: tile(i, c, False), tile(j, init, True))
            for e, sl in enumerate(heads):
                dk_a, dv_a, cs = carry[e]
                dk_ref[krows, sl] = (dk_a * 0.125).astype(BF)
                dv_ref[krows, sl] = dv_a.astype(BF)
                dcum_ref[0, e, j:j + 1, :] = -cs
        for e, sl in enumerate(heads):
            dq_ref[:, sl] = (dq_acc[e] * 0.125).astype(BF)
            dcq_ref[:, e:e + 1] = rs_ref[e]

    seq = lambda off: pl.BlockSpec((T, 128), lambda b, h, off=off: (b, off + h))
    return _pcall(body, name="fox_bwd", grid=(B, 4),
                  in_specs=[seq(qoff), seq(qoff + 4), seq(qoff + 8),
                            pl.BlockSpec((1, NG, nkb, bk), lambda b, h: (b, 0, 0, 0)),
                            seq(0), seq(0), seq(0)],
                  out_specs=[seq(0), seq(0), seq(0), pl.BlockSpec((1, 2, nkb, bk), lambda b, h: (b, h, 0, 0)), seq(0)],
                  out_shape=[jax.ShapeDtypeStruct((N, FOXW), BF)] * 3 + [jax.ShapeDtypeStruct((B, NG, nkb, bk), F32),
                                                                         jax.ShapeDtypeStruct((N, FOXW), F32)],
                  scratch_shapes=[pltpu.VMEM((2, T, HD), F32), pltpu.VMEM((2, T, 1), F32), pltpu.VMEM((2, T, 1), F32)],
                  compiler_params=_params(("arbitrary", "arbitrary")))(qkv, qkv, qkv, cum4, o, lse, dcat)


_GC = math.sqrt(2.0 / math.pi)
_GA = 0.044715


def _gelu(z):
    return 0.5 * z * (1.0 + jnp.tanh(_GC * (z + _GA * z * z * z)))


def _gelu_grad(z):
    t = jnp.tanh(_GC * (z + _GA * z * z * z))
    return 0.5 * (1.0 + t) + 0.5 * z * (1.0 - t * t) * (_GC * (1.0 + 3.0 * _GA * z * z))


def _gmlp_common(z, lng, lnb):
    zg = _gelu(z)
    u, vg = zg[:, :GW], zg[:, GW:]
    mu = jnp.mean(vg, axis=-1, keepdims=True)
    xc = vg - mu
    rstd = lax.rsqrt(jnp.mean(xc * xc, axis=-1, keepdims=True) + EPS)
    xhat = xc * rstd
    return u, xhat, rstd, xhat * lng + lnb


def _tril_w(ws_ref):
    tri = lax.broadcasted_iota(jnp.int32, (CH, CH), 0) >= lax.broadcasted_iota(jnp.int32, (CH, CH), 1)
    return [jnp.where(tri, ws_ref[g], 0.0).astype(BF) for g in range(NG)], tri


def _split_pair(vp):
    lane = lax.broadcasted_iota(jnp.int32, vp.shape, 1)
    zero = jnp.zeros(vp.shape, vp.dtype)
    return jnp.concatenate([jnp.where(lane < HD, vp, zero), jnp.where(lane >= HD, vp, zero)], axis=0)


def _gmlp_mix(wt, vgn_b):
    outs = []
    for p in range(NG // 2):
        wcat = jnp.concatenate([wt[2 * p], wt[2 * p + 1]], axis=1)
        outs.append(_dot(wcat, _split_pair(vgn_b[:, 128 * p:128 * (p + 1)]), 'nn'))
    return jnp.concatenate(outs, axis=1)


def _gmlp_fwd(z, lng, lnb, ws, bfull, bt):
    N = z.shape[0]

    def body(z_ref, lng_ref, lnb_ref, ws_ref, bf_ref, o_ref):
        wt, _ = _tril_w(ws_ref)
        for c in range(bt // CH):
            rows = slice(c * CH, (c + 1) * CH)
            u, _, _, vgn = _gmlp_common(z_ref[rows, :], lng_ref[...], lnb_ref[...])
            mixed = _gmlp_mix(wt, vgn.astype(BF)) + bf_ref[...]
            o_ref[rows, :] = (u * mixed).astype(BF)

    full = lambda shp: pl.BlockSpec(shp, lambda i: (0,) * len(shp))
    return _pcall(body, name="gmlp_fwd", grid=(N // bt,),
                  in_specs=[pl.BlockSpec((bt, D), lambda i: (i, 0)), full((1, GW)), full((1, GW)),
                            full((NG, CH, CH)), full((CH, GW))],
                  out_specs=pl.BlockSpec((bt, GW), lambda i: (i, 0)), out_shape=jax.ShapeDtypeStruct((N, GW), BF),
                  compiler_params=_params(("arbitrary",)))(z, lng, lnb, ws, bfull)


def _gmlp_bwd(z, dcat, lng, lnb, ws, bfull, bt):
    N = z.shape[0]

    def body(z_ref, da_ref, lng_ref, lnb_ref, ws_ref, bf_ref, dz_ref, dg_ref, db_ref, dws_ref, dbf_ref):
        @pl.when(pl.program_id(0) == 0)
        def _():
            for r in (dg_ref, db_ref, dws_ref, dbf_ref):
                r[...] = jnp.zeros(r.shape, F32)

        wt, tri = _tril_w(ws_ref)
        lane = lax.broadcasted_iota(jnp.int32, (CH, 128), 1)
        for c in range(bt // CH):
            rows = slice(c * CH, (c + 1) * CH)
            zc = z_ref[rows, :]
            u, xhat, rstd, vgn = _gmlp_common(zc, lng_ref[...], lnb_ref[...])
            vgn_b = vgn.astype(BF)
            mixed = _gmlp_mix(wt, vgn_b) + bf_ref[...]
            da = da_ref[rows, :].astype(F32)
            dmix = da * u
            du = da * mixed
            dbf_ref[...] += dmix
            dvs = []
            for p in range(NG // 2):
                cols = slice(128 * p, 128 * (p + 1))
                dmp = dmix[:, cols].astype(BF)
                dwp = _dot(_split_pair(dmp), vgn_b[:, cols], 'nt')
                dws_ref[2 * p] += jnp.where(tri, dwp[:CH], 0.0)
                dws_ref[2 * p + 1] += jnp.where(tri, dwp[CH:], 0.0)
                dvs.append(jnp.where(lane < HD, _dot(wt[2 * p], dmp, 'tn'), _dot(wt[2 * p + 1], dmp, 'tn')))
            dvgn = jnp.concatenate(dvs, axis=1)
            dg_ref[...] += jnp.sum(dvgn * xhat, axis=0, keepdims=True)
            db_ref[...] += jnp.sum(dvgn, axis=0, keepdims=True)
            dxh = dvgn * lng_ref[...]
            dvg = rstd * (dxh - jnp.mean(dxh, axis=-1, keepdims=True)
                          - xhat * jnp.mean(dxh * xhat, axis=-1, keepdims=True))
            dz_ref[rows, :] = (jnp.concatenate([du, dvg], axis=1) * _gelu_grad(zc)).astype(BF)

    full = lambda shp: pl.BlockSpec(shp, lambda i: (0,) * len(shp))
    return _pcall(body, name="gmlp_bwd", grid=(N // bt,),
                  in_specs=[pl.BlockSpec((bt, D), lambda i: (i, 0)), pl.BlockSpec((bt, GW), lambda i: (i, 1)),
                            full((1, GW)), full((1, GW)), full((NG, CH, CH)), full((CH, GW))],
                  out_specs=[pl.BlockSpec((bt, D), lambda i: (i, 0)), full((1, GW)), full((1, GW)),
                             full((NG, CH, CH)), full((CH, GW))],
                  out_shape=[jax.ShapeDtypeStruct((N, D), BF), jax.ShapeDtypeStruct((1, GW), F32),
                             jax.ShapeDtypeStruct((1, GW), F32), jax.ShapeDtypeStruct((NG, CH, CH), F32),
                             jax.ShapeDtypeStruct((CH, GW), F32)],
                  compiler_params=_params(("arbitrary",)))(z, dcat, lng, lnb, ws, bfull)


def _group_sum(name, a):
    def body(a_ref, o_ref):
        lane = lax.broadcasted_iota(jnp.int32, (CH, 128), 1)
        out = jnp.zeros((CH, 128), F32)
        for g in range(NG):
            out = jnp.where(lane == g, jnp.sum(a_ref[:, g * HD:(g + 1) * HD], axis=1, keepdims=True), out)
        o_ref[...] = out

    return _pcall(body, name=name, out_shape=jax.ShapeDtypeStruct((CH, 128), F32))(a)


def _xattn_softmax(q_h, k_h):
    s = _dot(q_h, k_h, 'nt') * (XD ** -0.5)
    p = jnp.exp(s - jnp.max(s, axis=1, keepdims=True))
    return p / jnp.sum(p, axis=1, keepdims=True)


def _xattn_fwd(name, q, kv, B, T, bq):
    nq = T // bq

    def body(q_ref, kv_ref, o_ref):
        for h in range(XH):
            cols = slice(h * XD, (h + 1) * XD)
            p = _xattn_softmax(q_ref[:, cols], kv_ref[:, cols])
            o_ref[:, cols] = _dot(p.astype(BF), kv_ref[:, D + h * XD:D + (h + 1) * XD], 'nn').astype(BF)

    return _pcall(body, name=name, grid=(B, nq),
                  in_specs=[pl.BlockSpec((bq, D), lambda b, i: (b * nq + i, 0)),
                            pl.BlockSpec((NMEM, 2 * D), lambda b, i: (b, 0))],
                  out_specs=pl.BlockSpec((bq, D), lambda b, i: (b * nq + i, 0)),
                  out_shape=jax.ShapeDtypeStruct((B * T, D), BF), compiler_params=_params(("arbitrary", "arbitrary")))(q, kv)


def _xattn_bwd(name, q, kv, do, B, T, bq):
    nq = T // bq
    sc = XD ** -0.5

    def body(q_ref, kv_ref, do_ref, dq_ref, dkv_ref):
        @pl.when(pl.program_id(1) == 0)
        def _():
            dkv_ref[...] = jnp.zeros(dkv_ref.shape, F32)

        for h in range(XH):
            cols = slice(h * XD, (h + 1) * XD)
            vcols = slice(D + h * XD, D + (h + 1) * XD)
            qh, kh, doh = q_ref[:, cols], kv_ref[:, cols], do_ref[:, cols]
            p = _xattn_softmax(qh, kh)
            dp = _dot(doh, kv_ref[:, vcols], 'nt')
            ds = p * (dp - jnp.sum(p * dp, axis=1, keepdims=True))
            dsb = ds.astype(BF)
            dq_ref[:, cols] = (_dot(dsb, kh, 'nn') * sc).astype(BF)
            dkv_ref[:, cols] += _dot(dsb, qh, 'tn') * sc
            dkv_ref[:, vcols] += _dot(p.astype(BF), doh, 'tn')

    blk = pl.BlockSpec((bq, D), lambda b, i: (b * nq + i, 0))
    return _pcall(body, name=name, grid=(B, nq),
                  in_specs=[blk, pl.BlockSpec((NMEM, 2 * D), lambda b, i: (b, 0)), blk],
                  out_specs=[blk, pl.BlockSpec((NMEM, 2 * D), lambda b, i: (b, 0))],
                  out_shape=[jax.ShapeDtypeStruct((B * T, D), BF), jax.ShapeDtypeStruct((B * NMEM, 2 * D), F32)],
                  compiler_params=_params(("arbitrary", "arbitrary")))(q, kv, do)


def _ln_stats(v):
    mu = jnp.mean(v, axis=-1, keepdims=True)
    xc = v - mu
    rstd = lax.rsqrt(jnp.mean(xc * xc, axis=-1, keepdims=True) + EPS)
    return xc * rstd, rstd


SUB = 8


def _fill_phases(win, sh, rows):
    for b in range(1, SUB):
        sh[b - 1] = win[b:b + rows, :]


LANES = 128
RC = 32


def _shifted_chunk(win, sh, o, r0, cols, rc=RC):
    a, b = divmod(o, SUB)
    start = r0 + SUB * a
    rows = slice(start, start + rc) if isinstance(r0, int) else pl.ds(pl.multiple_of(start, SUB), rc)
    return win[rows, cols] if b == 0 else sh[b - 1, rows, cols]


def _sum8(v):
    return jnp.sum(v.reshape(v.shape[0] // SUB, SUB, v.shape[1]), axis=0)


def _sum_groups(name, a):
    R, C = a.shape[0] // SUB, a.shape[1]

    def body(a_ref, o_ref):
        o_ref[...] = jnp.sum(a_ref[...].reshape(R, SUB, C), axis=1)

    return _pcall(body, name=name, out_shape=jax.ShapeDtypeStruct((R, C), F32))(a)


def _conv_fwd(y, w32, wb, lng, lnb, B, T, bt):
    nt = T // bt
    hb = bt // HALO
    prows = bt + HALO - SUB

    def body(y_ref, yp_ref, w_ref, wb_ref, lng_ref, lnb_ref, s_ref, yc_ref, win, sh):
        i = pl.program_id(1)
        win[0:HALO, :] = jnp.where(i > 0, yp_ref[...], 0.0)
        win[HALO:, :] = y_ref[...]
        _fill_phases(win, sh, prows)
        for cs in range(D // LANES):
            cols = slice(cs * LANES, (cs + 1) * LANES)

            rc = 2 * RC
            for r0 in range(0, bt, rc):
                accs = [jnp.zeros((rc, LANES), F32) + wb_ref[:, cols], jnp.zeros((rc, LANES), F32)]
                for j in range(CK):
                    accs[j % 2] = accs[j % 2] + w_ref[j:j + 1, cols] * _shifted_chunk(win, sh, HALO - (CK - 1) + j, r0,
                                                                                  cols, rc)
                yc_ref[r0:r0 + rc, cols] = accs[0] + accs[1]
        acc = yc_ref[...]
        xhat, _ = _ln_stats(acc)
        ln = xhat * lng_ref[...] + lnb_ref[...]
        s_ref[...] = (ln * _sigmoid(ln)).astype(BF)

    row = lambda n: pl.BlockSpec((n, D), lambda b, i: (0, 0))
    cur = pl.BlockSpec((bt, D), lambda b, i: (b * nt + i, 0))
    return _pcall(body, name="conv_fwd", grid=(B, nt),
                  in_specs=[cur, pl.BlockSpec((HALO, D), lambda b, i: (jnp.maximum((b * nt + i) * hb - 1, 0), 0)),
                            row(HALO), row(1), row(1), row(1)],
                  out_specs=[cur, cur],
                  out_shape=[jax.ShapeDtypeStruct((B * T, D), BF), jax.ShapeDtypeStruct((B * T, D), F32)],
                  scratch_shapes=[pltpu.VMEM((bt + HALO, D), F32), pltpu.VMEM((SUB - 1, prows, D), F32)],
                  compiler_params=_params(("arbitrary", "arbitrary")))(y, y, w32, wb, lng, lnb)


def _conv_bwd(ds, yc, y, pa, pg, w32, lng, lnb, B, T, bt):
    nt = T // bt
    hb = bt // HALO
    nblk32 = B * T // HALO

    def ln_bwd(dsv, ycv, lng, lnb):
        xhat, rstd = _ln_stats(ycv)
        ln = xhat * lng + lnb
        sg = _sigmoid(ln)
        dln = dsv * (sg * (1.0 + ln * (1.0 - sg)))
        dxh = dln * lng
        dyc = rstd * (dxh - jnp.mean(dxh, axis=-1, keepdims=True)
                      - xhat * jnp.mean(dxh * xhat, axis=-1, keepdims=True))
        return dyc, dln, xhat

    prows = bt + HALO - SUB

    def body(ds_ref, dsn_ref, yc_ref, ycn_ref, y_ref, yp_ref, pa_ref, pg_ref, w_ref, lng_ref, lnb_ref,
             dpa_ref, dpg_ref, dw_ref, dwb_ref, dlng_ref, dlnb_ref, dba_ref, dbg_ref, dwin, ywin, dsh, ysh):
        i = pl.program_id(1)

        @pl.when((pl.program_id(0) == 0) & (i == 0))
        def _():
            for r in (dw_ref, dwb_ref, dlng_ref, dlnb_ref, dba_ref, dbg_ref):
                r[...] = jnp.zeros(r.shape, F32)

        lng, lnb = lng_ref[...], lnb_ref[...]
        dyc, dln, xhat = ln_bwd(ds_ref[...].astype(F32), yc_ref[...], lng, lnb)
        dycn, _, _ = ln_bwd(dsn_ref[...].astype(F32), ycn_ref[...], lng, lnb)
        dwin[0:bt, :] = dyc
        dwin[bt:, :] = jnp.where(i < nt - 1, dycn, 0.0)
        ywin[0:HALO, :] = jnp.where(i > 0, yp_ref[...], 0.0)
        ywin[HALO:, :] = y_ref[...]
        dlng_ref[...] += jnp.sum(dln * xhat, axis=0, keepdims=True)
        dlnb_ref[...] += jnp.sum(dln, axis=0, keepdims=True)
        dwb_ref[...] += jnp.sum(dyc, axis=0, keepdims=True)
        _fill_phases(dwin, dsh, prows)
        _fill_phases(ywin, ysh, prows)
        zero8 = jnp.zeros((SUB, LANES), F32)
        for cs in range(D // LANES):
            cols = slice(cs * LANES, (cs + 1) * LANES)

            def chunk(r, carry, cols=cols):
                dw_acc, sum_a, sum_g = carry
                r0 = pl.multiple_of(r * RC, RC)
                rows = pl.ds(r0, RC)
                dyc_c = dwin[rows, cols]
                dys = [jnp.zeros((RC, LANES), F32), jnp.zeros((RC, LANES), F32)]
                dw_new = []
                for j in range(CK):
                    dys[j % 2] = dys[j % 2] + w_ref[j:j + 1, cols] * _shifted_chunk(dwin, dsh, CK - 1 - j, r0, cols)
                    dw_new.append(dw_acc[j] + _sum8(dyc_c * _shifted_chunk(ywin, ysh, HALO - (CK - 1) + j, r0, cols)))
                dy = dys[0] + dys[1]
                a, g = pa_ref[rows, cols].astype(F32), pg_ref[rows, cols].astype(F32)
                sg = _sigmoid(g)
                da = dy * sg
                dg = dy * a * sg * (1.0 - sg)
                dpa_ref[rows, cols] = da.astype(BF)
                dpg_ref[rows, cols] = dg.astype(BF)
                return tuple(dw_new), sum_a + _sum8(da), sum_g + _sum8(dg)

            dw_acc, sum_a, sum_g = lax.fori_loop(0, bt // RC, chunk, ((zero8,) * CK, zero8, zero8))
            for j in range(CK):
                dw_ref[SUB * j:SUB * (j + 1), cols] += dw_acc[j]
            dba_ref[:, cols] += jnp.sum(sum_a, axis=0, keepdims=True)
            dbg_ref[:, cols] += jnp.sum(sum_g, axis=0, keepdims=True)

    cur = pl.BlockSpec((bt, D), lambda b, i: (b * nt + i, 0))
    nxt = pl.BlockSpec((HALO, D), lambda b, i: (jnp.minimum((b * nt + i + 1) * hb, nblk32 - 1), 0))
    prv = pl.BlockSpec((HALO, D), lambda b, i: (jnp.maximum((b * nt + i) * hb - 1, 0), 0))
    row = lambda n: pl.BlockSpec((n, D), lambda b, i: (0, 0))
    N = B * T
    return _pcall(body, name="conv_bwd", grid=(B, nt),
                  in_specs=[cur, nxt, cur, nxt, cur, prv, cur, cur, row(HALO), row(1), row(1)],
                  out_specs=[cur, cur, row(HALO * SUB), row(1), row(1), row(1), row(1), row(1)],
                  out_shape=[jax.ShapeDtypeStruct((N, D), BF)] * 2 + [jax.ShapeDtypeStruct((HALO * SUB, D), F32)]
                  + [jax.ShapeDtypeStruct((1, D), F32)] * 5,
                  scratch_shapes=[pltpu.VMEM((bt + HALO, D), F32), pltpu.VMEM((bt + HALO, D), F32),
                                  pltpu.VMEM((SUB - 1, prows, D), F32), pltpu.VMEM((SUB - 1, prows, D), F32)],
                  compiler_params=_params(("arbitrary", "arbitrary")))(ds, ds, yc, yc, y, y, pa, pg, w32, lng, lnb)


def _head(x, tgt, gain, bt=512):
    N = x.shape[0]
    bt = min(bt, N)

    def body(x_ref, t_ref, g_ref, dx_ref, loss_ref, dg_ref):
        @pl.when(pl.program_id(0) == 0)
        def _():
            loss_ref[...] = jnp.zeros(loss_ref.shape, F32)
            dg_ref[...] = jnp.zeros(dg_ref.shape, F32)

        xv = x_ref[...]
        gain = g_ref[...]
        err = xv * _rms_stats(xv) * gain - t_ref[...]
        loss_ref[...] += 0.5 * jnp.sum(jnp.mean(err * err, axis=-1, keepdims=True), axis=0, keepdims=True)
        dx, dgr = _rms_bwd(xv, gain, err * (1.0 / D))
        dx_ref[...] = dx
        dg_ref[...] += jnp.sum(dgr, axis=0, keepdims=True)

    blk = pl.BlockSpec((bt, D), lambda i: (i, 0))
    return _pcall(body, name="loss_head", grid=(N // bt,),
                  in_specs=[blk, blk, pl.BlockSpec((1, D), lambda i: (0, 0))],
                  out_specs=[blk, pl.BlockSpec((1, 128), lambda i: (0, 0)), pl.BlockSpec((1, D), lambda i: (0, 0))],
                  out_shape=[jax.ShapeDtypeStruct((N, D), F32), jax.ShapeDtypeStruct((1, 128), F32),
                             jax.ShapeDtypeStruct((1, D), F32)],
                  compiler_params=_params(("arbitrary",)))(x, tgt, gain)


def _local_step(x, mem, tgt, Wb, P):
    B, T, _ = x.shape
    N = B * T
    bq = bk = min(512, T)
    bt = min(512, T)
    x0 = x.reshape(N, D)
    mem2 = mem.reshape(B * NMEM, D)
    tgt2 = tgt.reshape(N, D)
    row = lambda v: v.reshape(1, -1)
    G = {}

    w_in = Wb['w_in_e'][0]
    w_inp = jnp.concatenate([w_in[:, 3 * FOXW + NG:], w_in[:, :3 * FOXW], w_in[:, 3 * FOXW:3 * FOXW + NG],
                             jnp.zeros((D, 128 - NG), BF)], axis=1)
    g_e = row(P['mix_norm_e'])
    z, h0 = _norm_mm("proj_z", x0, g_e, w_inp, N=D, coff=0, bn=D, out_dtype=F32, h_out=True)
    qkv = _norm_mm("proj_qkv", x0, g_e, w_inp, N=3 * FOXW, coff=2, bn=FOXW, out_dtype=BF)[0]
    fl = _norm_mm("proj_f", x0, g_e, w_inp, N=128, coff=20, bn=128, out_dtype=F32)[0]
    fbias = jnp.concatenate([P['fox_f_bias'].reshape(1, NG), jnp.zeros((1, 128 - NG), F32)], axis=1)
    cum = _fox_gate_fwd(fl, fbias, B, T)
    cum4 = cum[:, :NG].reshape(B, T, NG).transpose(0, 2, 1).reshape(B, NG, T // bk, bk)
    b_out, lse = _fox_fwd(qkv, cum4, B, T, 0, bq, bk)
    lng, lnb = row(P['gmlp_ln_g']), row(P['gmlp_ln_b'])
    ws = P['gmlp_w_s'][0]
    bfull = jnp.repeat(P['gmlp_b_s'][0].T, HD, axis=1)
    a_out = _gmlp_fwd(z, lng, lnb, ws, bfull, bt)
    w_out = Wb['w_out_e'][0]
    x1 = _mm_resid("mix_out", [dict(A=b_out, Ka=FOXW, B=w_out, roff=0), dict(A=a_out, Ka=GW, B=w_out, roff=1)], x0)

    def xa_ffn_fwd(l, xin):
        qx, hq = _norm_mm(f"xa_q{l}", xin, row(P['xa_norm'][l]), Wb['xa_wq'][l], N=D, bn=D, out_dtype=BF, h_out=True)
        kv, hm = _norm_mm(f"xa_kv{l}", mem2, row(P['mem_norm'][l]), Wb['xa_wkv'][l], N=2 * D, bn=D, out_dtype=BF,
                          h_out=True)
        o = _xattn_fwd(f"xattn_fwd{l}", qx, kv, B, T, bt)
        xm = _mm_resid(f"xa_o{l}", [dict(A=o, Ka=D, B=Wb['xa_wo'][l])], xin)
        wgu = Wb['ffn_w_gu'][l]
        g, u, a, hf = _fused_mm(f"ffn_gu{l}", dims='nn', M=N, N=FF, bm=min(512, N), bn=FF // 2, x=xm,
                                gain=row(P['ffn_norm'][l]),
                                groups=[[dict(A=None, Ka=D, B=wgu, coff=0)], [dict(A=None, Ka=D, B=wgu, coff=2)]],
                                epi=_epi_swiglu, outs=[BF, BF, BF], h_out=True)
        xo = _mm_resid(f"ffn_down{l}", [dict(A=a, Ka=FF, B=Wb['ffn_w_down'][l])], xm)
        return xo, dict(xin=xin, qx=qx, hq=hq, kv=kv, hm=hm, o=o, xm=xm, g=g, u=u, a=a, hf=hf)

    x3, S0 = xa_ffn_fwd(0, x1)
    w_cin = Wb['conv_w_in'][0]
    b_cin = row(P['conv_b_in'])
    pa, pg, y, hc = _fused_mm("conv_in", dims='nn', M=N, N=D, bm=min(512, N), bn=D, x=x3, gain=row(P['mix_norm_o']),
                              groups=[[dict(A=None, Ka=D, B=w_cin, coff=0)], [dict(A=None, Ka=D, B=w_cin, coff=1)]],
                              epi=_epi_glu, outs=[BF, BF, F32], rows=[(b_cin, 0), (b_cin, 1)], h_out=True)
    w32 = jnp.concatenate([P['conv_dw_w'][0], jnp.zeros((HALO - CK, D), F32)], axis=0)
    cbt = min(256, T)
    s, yc = _conv_fwd(y, w32, row(P['conv_dw_b']), row(P['conv_ln_g']), row(P['conv_ln_b']), B, T, cbt)
    x4 = _mm_resid("conv_out", [dict(A=s, Ka=D, B=Wb['conv_w_out'][0])], x3, bias=row(P['conv_b_out']))
    x6, S1 = xa_ffn_fwd(1, x4)
    dx, loss_t, dgf = _head(x6, tgt2, row(P['final_norm']))
    G['final_norm'] = dgf.reshape(D)

    def by_rows(dw):
        return dw.reshape(NCHIP, dw.shape[1] // NCHIP, dw.shape[2])

    def xa_ffn_bwd(l, S, dx):
        wgu, wdown = Wb['ffn_w_gu'][l], Wb['ffn_w_down'][l]
        dwdown = by_rows(_mm_tn(f"dw_down{l}", S['a'], dx))
        dg, du = _fused_mm(f"ffn_dgu{l}", dims='nt', M=N, N=FF, bm=min(512, N), bn=FF // 2,
                           groups=[[dict(A=dx, Ka=D, B=wdown)]], epi=_epi_swiglu_bwd, outs=[BF, BF],
                           tiles=[(S['g'], 0), (S['u'], 0)])
        dwgu = jnp.concatenate([_mm_tn(f"dw_g{l}", S['hf'], dg, parts=2), _mm_tn(f"dw_u{l}", S['hf'], du, parts=2)])
        dx, dgn = _mm_nt_rms_bwd(f"ffn_dx{l}", [dict(A=dg, Ka=FF, B=wgu, coff=0), dict(A=du, Ka=FF, B=wgu, coff=1)],
                                 S['xm'], row(P['ffn_norm'][l]), dx)
        dwo = by_rows(_mm_tn(f"dw_o{l}", S['o'], dx))
        do = _mm_nt_plain(f"xa_do{l}", dx, Wb['xa_wo'][l])
        dq, dkv = _xattn_bwd(f"xattn_bwd{l}", S['qx'], S['kv'], do, B, T, bt)
        dwq = by_rows(_mm_tn(f"dw_q{l}", S['hq'], dq))
        dwkv = _mm_tn(f"dw_kv{l}", S['hm'], dkv, parts=NCHIP)
        dmn = _fused_mm(f"xa_dmem{l}", dims='nt', M=B * NMEM, N=D, bm=min(256, B * NMEM), bn=D,
                        groups=[[dict(A=dkv, Ka=2 * D, B=Wb['xa_wkv'][l])]], epi=_epi_rms_gain_only, outs=[],
                        tiles=[(mem2, 0)], rows=[(row(P['mem_norm'][l]), 0)], reds=[(1, D)])[0]
        dx, dxn = _mm_nt_rms_bwd(f"xa_dx{l}", [dict(A=dq, Ka=D, B=Wb['xa_wq'][l])], S['xin'],
                                 row(P['xa_norm'][l]), dx)
        return dx, dict(ffn_w_down=dwdown, ffn_w_gu=dwgu, ffn_norm=dgn.reshape(D), xa_wo=dwo, xa_wq=dwq,
                        xa_wkv=dwkv, mem_norm=dmn.reshape(D), xa_norm=dxn.reshape(D))

    dx, G1 = xa_ffn_bwd(1, S1, dx)
    G['conv_w_out'] = [by_rows(_mm_tn("dw_cout", s, dx))]
    G['conv_b_out'] = _colsum("db_cout", dx)
    dsv = _mm_nt_plain("conv_ds", dx, Wb['conv_w_out'][0])
    dpa, dpg, dw32, dwb, dlng, dlnb, dba, dbg = _conv_bwd(dsv, yc, y, pa, pg, w32, row(P['conv_ln_g']),
                                                          row(P['conv_ln_b']), B, T, cbt)
    G['conv_dw_w'] = _sum_groups("conv_dw_sum", dw32)[:CK][None]
    G['conv_dw_b'], G['conv_ln_g'], G['conv_ln_b'] = dwb, dlng, dlnb
    G['conv_b_in'] = jnp.concatenate([dba, dbg], axis=1)
    G['conv_w_in'] = [jnp.concatenate([_mm_tn("dw_cin_a", hc, dpa, parts=2), _mm_tn("dw_cin_g", hc, dpg, parts=2)])]
    dx, dgo = _mm_nt_rms_bwd("conv_dx", [dict(A=dpa, Ka=D, B=w_cin, coff=0), dict(A=dpg, Ka=D, B=w_cin, coff=1)],
                             x3, row(P['mix_norm_o']), dx)
    G['mix_norm_o'] = dgo
    dx, G0 = xa_ffn_bwd(0, S0, dx)
    for k in G0:
        G[k] = [G0[k], G1[k]]
    G['w_out_e'] = [by_rows(jnp.concatenate([_mm_tn("dw_out_b", b_out, dx), _mm_tn("dw_out_a", a_out, dx)], axis=1))]
    dcat = _mm_nt_plain("mix_dcat", dx, w_out)
    dq, dk, dv, dcum4, dcq4 = _fox_bwd(qkv, cum4, b_out, lse, dcat, B, T, 0, bq, bk)
    dz, dlg, dlb, dws, dbf = _gmlp_bwd(z, dcat, lng, lnb, ws, bfull, bt)
    G['gmlp_ln_g'], G['gmlp_ln_b'], G['gmlp_w_s'] = dlg, dlb, dws[None]
    G['gmlp_b_s'] = _group_sum("gmlp_db", dbf)[:, :NG].T[None]
    pad = jnp.zeros((N, 128 - NG), F32)
    dck = jnp.concatenate([dcum4.reshape(B, NG, T).transpose(0, 2, 1).reshape(N, NG), pad], axis=1)
    dcq = jnp.concatenate([dcq4.reshape(N, NG // 2, 128)[:, :, :2].reshape(N, NG), pad], axis=1)
    dfl, dfb = _fox_gate_bwd(fl, fbias, dcq, dck, B, T)
    G['fox_f_bias'] = dfb[:, :NG]
    dw_in = jnp.concatenate([_mm_tn("dw_in_q", h0, dq)[0], _mm_tn("dw_in_k", h0, dk)[0], _mm_tn("dw_in_v", h0, dv)[0],
                             _mm_tn("dw_in_f", h0, dfl)[0][:, :NG], _mm_tn("dw_in_z", h0, dz)[0]], axis=1)
    G['w_in_e'] = [dw_in.reshape(D, NCHIP, IN_W // NCHIP).transpose(1, 0, 2)]
    dx, dge = _mm_nt_rms_bwd("mix_dx", [dict(A=dz, Ka=D, B=w_inp, coff=0), dict(A=dq, Ka=FOXW, B=w_inp, coff=2),
                                        dict(A=dk, Ka=FOXW, B=w_inp, coff=3), dict(A=dv, Ka=FOXW, B=w_inp, coff=4),
                                        dict(A=dfl, Ka=128, B=w_inp, coff=20)], x0, g_e, dx)
    G['mix_norm_e'] = dge
    return loss_t[0, 0], dx.reshape(B, T, D), G


COLS = 1024
ANY = pl.BlockSpec(memory_space=pl.ANY)


def _coords():
    return lax.axis_index("x"), lax.axis_index("y"), lax.axis_index("c")


def _other_chips(x, y):
    return [(1 - x, y), (x, 1 - y), (1 - x, 1 - y)]


def _all_gather(name, shards, me):
    n = len(shards)
    bufs = [lax.dynamic_update_slice(lax.empty((NCHIP,) + v.shape, v.dtype), v[None], (me, 0, 0, 0)) for v in shards]

    def body(*refs):
        out_refs, (send_sems, recv_sems) = refs[n:2 * n], refs[2 * n:]
        x, y, c = _coords()
        mine = 2 * x + y
        sib = (x, y, 1 - c)
        chips = _other_chips(x, y)

        def rcopy(a, k, chip_idx, half, to):
            blk = out_refs[a].at[chip_idx, half]
            return pltpu.make_async_remote_copy(src_ref=blk, dst_ref=blk, send_sem=send_sems.at[6 * a + k],
                                                recv_sem=recv_sems.at[6 * a + k], device_id=to, device_id_type=MESH)

        first = [rcopy(a, j, mine, c, (cx, cy, c)) for a in range(n) for j, (cx, cy) in enumerate(chips)]
        for cp in first:
            cp.start()
        passed = []
        for a in range(n):
            for j, (cx, cy) in enumerate(chips):
                kj = 2 * cx + cy
                rcopy(a, j, kj, c, sib).wait_recv()
                fwd = rcopy(a, 3 + j, kj, c, sib)
                fwd.start()
                passed.append(fwd)
        for a in range(n):
            for j, (cx, cy) in enumerate(chips):
                rcopy(a, 3 + j, 2 * cx + cy, 1 - c, sib).wait_recv()
        for cp in first + passed:
            cp.wait_send()

    return _pcall(body, name=name, in_specs=[ANY] * n, out_specs=[ANY] * n,
                  out_shape=[jax.ShapeDtypeStruct(b.shape, b.dtype) for b in bufs],
                  input_output_aliases={a: a for a in range(n)}, scratch_shapes=_sem_pairs(6 * n))(*bufs)


def _sem_pairs(n):
    return [pltpu.SemaphoreType.DMA((n,)), pltpu.SemaphoreType.DMA((n,))]


def _sibling_halves(name, ps):
    n = len(ps)

    def body(*refs):
        p_refs, out_refs, (send_sems, recv_sems) = refs[:n], refs[n:2 * n], refs[2 * n:]
        x, y, c = _coords()
        cps = [pltpu.make_async_remote_copy(src_ref=p_refs[a].at[k, 1 - c], dst_ref=out_refs[a].at[k],
                                            send_sem=send_sems.at[4 * a + k], recv_sem=recv_sems.at[4 * a + k],
                                            device_id=(x, y, 1 - c), device_id_type=MESH)
               for a in range(n) for k in range(NCHIP)]
        for cp in cps:
            cp.start()
        for cp in cps:
            cp.wait()

    return _pcall(body, name=name, in_specs=[ANY] * n, out_specs=[ANY] * n,
                  out_shape=[jax.ShapeDtypeStruct((NCHIP,) + p.shape[2:], p.dtype) for p in ps],
                  scratch_shapes=_sem_pairs(NCHIP * n))(*ps)


def _chip_exchange(name, qs):
    n = len(qs)

    def body(*refs):
        q_refs, out_refs, (send_sems, recv_sems) = refs[:n], refs[n:2 * n], refs[2 * n:]
        x, y, c = _coords()
        cps = [pltpu.make_async_remote_copy(src_ref=q_refs[a].at[2 * cx + cy], dst_ref=out_refs[a].at[j],
                                            send_sem=send_sems.at[3 * a + j], recv_sem=recv_sems.at[3 * a + j],
                                            device_id=(cx, cy, c), device_id_type=MESH)
               for a in range(n) for j, (cx, cy) in enumerate(_other_chips(x, y))]
        for cp in cps:
            cp.start()
        for cp in cps:
            cp.wait()

    return _pcall(body, name=name, in_specs=[ANY] * n, out_specs=[ANY] * n,
                  out_shape=[jax.ShapeDtypeStruct((3,) + q.shape[1:], q.dtype) for q in qs],
                  scratch_shapes=_sem_pairs(3 * n))(*qs)


def _sibling_swap(name, hs):
    n = len(hs)

    def body(*refs):
        out_refs, (send_sems, recv_sems) = refs[n:2 * n], refs[2 * n:]
        x, y, c = _coords()
        sib = (x, y, 1 - c)
        sends = [pltpu.make_async_remote_copy(src_ref=out_refs[a].at[c], dst_ref=out_refs[a].at[c],
                                              send_sem=send_sems.at[a], recv_sem=recv_sems.at[a], device_id=sib,
                                              device_id_type=MESH) for a in range(n)]
        for cp in sends:
            cp.start()
        for a in range(n):
            theirs = out_refs[a].at[1 - c]
            pltpu.make_async_remote_copy(src_ref=theirs, dst_ref=theirs, send_sem=send_sems.at[a],
                                         recv_sem=recv_sems.at[a], device_id=sib, device_id_type=MESH).wait_recv()
        for cp in sends:
            cp.wait_send()

    return _pcall(body, name=name, in_specs=[ANY] * n, out_specs=[ANY] * n,
                  out_shape=[jax.ShapeDtypeStruct(h.shape, h.dtype) for h in hs],
                  input_output_aliases={a: a for a in range(n)}, scratch_shapes=_sem_pairs(n))(*hs)


ADD_BLOCK_BYTES = 2 * 1024 * 1024


def _row_block(R, C):
    if R * C * 4 <= ADD_BLOCK_BYTES:
        return R
    for br in (512, 256, 128, 64, 32, 16, 8):
        if R % br == 0 and br * C * 4 <= ADD_BLOCK_BYTES:
            return br
    return R


def _add_own_half(name, p, recv, c_arr, out_dtype):
    _, _, R, C = p.shape
    br = _row_block(R, C)

    def body(c_ref, p_ref, r_ref, o_ref):
        o_ref[...] = (p_ref[...].astype(F32) + r_ref[...].astype(F32)).astype(o_ref.dtype)

    spec = pltpu.PrefetchScalarGridSpec(
        num_scalar_prefetch=1, grid=(NCHIP, R // br),
        in_specs=[pl.BlockSpec((None, None, br, C), lambda k, r, c_ref: (k, c_ref[0], r, 0)),
                  pl.BlockSpec((None, br, C), lambda k, r, c_ref: (k, r, 0))],
        out_specs=pl.BlockSpec((None, br, C), lambda k, r, c_ref: (k, r, 0)))
    return _pcall(body, name=name, grid_spec=spec, out_shape=jax.ShapeDtypeStruct((NCHIP, R, C), out_dtype),
                  compiler_params=_params(("arbitrary", "arbitrary")))(c_arr, p, recv)


def _add_chips(name, q, recv, idx_arr):
    _, R, C = q.shape
    br = _row_block(R, C)

    def body(idx_ref, q_ref, r_ref, o_ref):
        o_ref[...] = ((q_ref[...].astype(F32) + r_ref[0].astype(F32)) + r_ref[1].astype(F32)) + r_ref[2].astype(F32)

    spec = pltpu.PrefetchScalarGridSpec(
        num_scalar_prefetch=1, grid=(R // br,),
        in_specs=[pl.BlockSpec((None, br, C), lambda r, idx: (idx[0], r, 0)),
                  pl.BlockSpec((3, br, C), lambda r, idx: (0, r, 0))],
        out_specs=pl.BlockSpec((None, br, C), lambda r, idx: (idx[1], r, 0)))
    return _pcall(body, name=name, grid_spec=spec, out_shape=jax.ShapeDtypeStruct((2, R, C), F32),
                  compiler_params=_params(("arbitrary",)))(idx_arr, q, recv)


def _reduce_scatter(tag, ps, c_arr, idx_arr):
    got = _sibling_halves(f"rs_sibling_halves_{tag}", ps)
    qs = [_add_own_half(f"rs_add_pair_{tag}{a}", p, g, c_arr, p.dtype) for a, (p, g) in enumerate(zip(ps, got))]
    got = _chip_exchange(f"rs_chip_exchange_{tag}", qs)
    hs = [_add_chips(f"rs_add_chips_{tag}{a}", q, g, idx_arr) for a, (q, g) in enumerate(zip(qs, got))]
    return _sibling_swap(f"rs_sibling_swap_{tag}", hs)


def _adamw(name, w, g, m, v):
    shape = w.shape
    cols = shape[-1]
    rows = w.size // cols
    w2, g2, m2, v2 = (a.reshape(rows, cols) for a in (w, g, m, v))
    bt = 128 if rows % 128 == 0 else rows

    def body(w_ref, g_ref, m_ref, v_ref, d_ref, nm_ref, nv_ref):
        gv = g_ref[...]
        nm = ADAM_B1 * m_ref[...] + (1.0 - ADAM_B1) * gv
        nv = ADAM_B2 * v_ref[...] + (1.0 - ADAM_B2) * (gv * gv)
        m_hat = nm / (1.0 - ADAM_B1 ** ADAM_STEP)
        v_hat = nv / (1.0 - ADAM_B2 ** ADAM_STEP)
        d_ref[...] = -ADAM_LR * (m_hat / (jnp.sqrt(v_hat) + ADAM_EPS) + ADAM_WD * w_ref[...])
        nm_ref[...] = nm
        nv_ref[...] = nv

    blk = pl.BlockSpec((bt, cols), lambda i: (i, 0))
    outs = _pcall(body, name=name, grid=(rows // bt,), in_specs=[blk] * 4, out_specs=[blk] * 3,
                  out_shape=[jax.ShapeDtypeStruct((rows, cols), F32)] * 3, compiler_params=_params(("arbitrary",)))(
        w2, g2, m2, v2)
    return [o.reshape(shape) for o in outs]


SMALL_SHARDED = ['mix_norm_o', 'conv_b_in', 'conv_dw_w', 'conv_dw_b', 'conv_ln_g', 'conv_ln_b', 'conv_b_out']
REPLICATED = [n for n in WEIGHTS if SHARD_AXIS[n] is None]
NCHIP = 4


def _halves(flat, tile_rows):
    unit = 2 * tile_rows * COLS
    total = -(-flat.size // unit) * unit
    return jnp.pad(flat, (0, total - flat.size)).reshape(2, total // (2 * COLS), COLS)


def _flat(arrays):
    return jnp.concatenate([a.reshape(-1) for a in arrays])


def _chip_block(a, axis, k):
    n = a.shape[axis] // NCHIP
    return lax.slice_in_dim(a, k * n, (k + 1) * n, axis=axis)


def _full_shape(n, shard_shape):
    s = list(shard_shape[n])
    s[SHARD_AXIS[n]] *= NCHIP
    return tuple(s)


def _unpack(flat, names, shapes):
    out, off = {}, 0
    for n in names:
        size = math.prod(shapes[n])
        out[n] = flat[off:off + size].reshape(shapes[n])
        off += size
    return out


def kernel(x, mem, mix_norm_e, w_in_e, fox_f_bias, gmlp_ln_g, gmlp_ln_b, gmlp_w_s, gmlp_b_s, w_out_e, mix_norm_o, conv_w_in, conv_b_in, conv_dw_w, conv_dw_b, conv_ln_g, conv_ln_b, conv_w_out, conv_b_out, xa_norm, mem_norm, xa_wq, xa_wkv, xa_wo, ffn_norm, ffn_w_gu, ffn_w_down, final_norm, loss_target, m_mix_norm_e, m_w_in_e, m_fox_f_bias, m_gmlp_ln_g, m_gmlp_ln_b, m_gmlp_w_s, m_gmlp_b_s, m_w_out_e, m_mix_norm_o, m_conv_w_in, m_conv_b_in, m_conv_dw_w, m_conv_dw_b, m_conv_ln_g, m_conv_ln_b, m_conv_w_out, m_conv_b_out, m_xa_norm, m_mem_norm, m_xa_wq, m_xa_wkv, m_xa_wo, m_ffn_norm, m_ffn_w_gu, m_ffn_w_down, m_final_norm, v_mix_norm_e, v_w_in_e, v_fox_f_bias, v_gmlp_ln_g, v_gmlp_ln_b, v_gmlp_w_s, v_gmlp_b_s, v_w_out_e, v_mix_norm_o, v_conv_w_in, v_conv_b_in, v_conv_dw_w, v_conv_dw_b, v_conv_ln_g, v_conv_ln_b, v_conv_w_out, v_conv_b_out, v_xa_norm, v_mem_norm, v_xa_wq, v_xa_wkv, v_xa_wo, v_ffn_norm, v_ffn_w_gu, v_ffn_w_down, v_final_norm):
    env = locals()
    w = {n: env[n] for n in WEIGHTS}
    m = {n: env["m_" + n] for n in WEIGHTS}
    v = {n: env["v_" + n] for n in WEIGHTS}
    shard_shape = {n: w[n].shape for n in WEIGHTS}
    xi, yi, ci = _coords()
    me = 2 * xi + yi
    c_arr = jnp.reshape(ci, (1,)).astype(jnp.int32)
    idx_arr = jnp.stack([me, ci]).astype(jnp.int32)

    def two_halves(a):
        return a.reshape(2, a.shape[0] // 2, a.shape[1])

    got = _all_gather("gather_matrices", [two_halves(w[n][l].astype(BF)) for n in BIG for l in range(w[n].shape[0])], me)
    Wb, at = {}, 0
    for n in BIG:
        Wb[n] = []
        for l in range(w[n].shape[0]):
            rows, cols = w[n].shape[1:]
            g = got[at].reshape(NCHIP, rows, cols)
            at += 1
            Wb[n].append(g.reshape(NCHIP * rows, cols) if SHARD_AXIS[n] == 1
                         else g.transpose(1, 0, 2).reshape(rows, NCHIP * cols))
    vec = _all_gather("gather_vectors", [_halves(_flat([w[n] for n in SMALL_SHARDED]), 8)], me)[0].reshape(NCHIP, -1)
    parts = [_unpack(vec[k], SMALL_SHARDED, shard_shape) for k in range(NCHIP)]
    P = {n: jnp.concatenate([parts[k][n] for k in range(NCHIP)], axis=SHARD_AXIS[n]) for n in SMALL_SHARDED}
    P.update({n: w[n] for n in REPLICATED})

    loss_part, grad_x, G = _local_step(x, mem, loss_target, Wb, P)
    loss = lax.psum(loss_part, ("x", "y", "c"))

    def layers(n):
        return G[n] if isinstance(G[n], list) else ([G[n]] if G[n].ndim == 1 else [G[n][l] for l in range(G[n].shape[0])])

    rep = _flat([a for n in REPLICATED for a in layers(n)])
    quarter = -(-rep.size // (NCHIP * 2 * 8 * COLS)) * (2 * 8 * COLS)
    rep = jnp.pad(rep, (0, NCHIP * quarter - rep.size)).reshape(NCHIP, quarter)
    segs = [[_chip_block(a, SHARD_AXIS[n] - 1, k).reshape(-1) for n in SMALL_SHARDED for a in layers(n)] + [rep[k]]
            for k in range(NCHIP)]
    size = sum(piece.size for piece in segs[0])
    total = -(-size // (2 * 8 * COLS)) * (2 * 8 * COLS)
    p_small = jnp.concatenate([piece for seg in segs for piece in seg + [jnp.zeros((total - size,), F32)]])
    p_small = p_small.reshape(NCHIP, 2, total // (2 * COLS), COLS)
    p_big = [g.reshape(NCHIP, 2, g.shape[1] // 2, g.shape[2]) for n in BIG for g in G[n]]
    red = _reduce_scatter("grads", p_big + [p_small], c_arr, idx_arr)
    mine, at = {}, 0
    for n in BIG:
        nl = shard_shape[n][0]
        mine[n] = jnp.stack([r.reshape(shard_shape[n][1:]) for r in red[at:at + nl]])
        at += nl
    red_small = red[-1].reshape(-1)
    mine.update(_unpack(red_small, SMALL_SHARDED, shard_shape))
    off = sum(math.prod(shard_shape[n]) for n in SMALL_SHARDED)
    rep_all = _all_gather("gather_replicated_grads",
                          [red_small[off:off + quarter].reshape(2, quarter // (2 * COLS), COLS)], me)[0]
    mine.update(_unpack(rep_all.reshape(-1), REPLICATED, shard_shape))

    grads, deltas, new_m, new_v = [], [], [], []
    for n in WEIGHTS:
        d, nm, nv = _adamw("adamw_" + n, w[n], mine[n], m[n], v[n])
        grads.append(mine[n])
        deltas.append(d)
        new_m.append(nm)
        new_v.append(nv)
    return (loss, grad_x, *grads, *deltas, *new_m, *new_v)
```

```python
import functools
import math

import jax
import jax.numpy as jnp
from jax import lax
from jax.experimental import pallas as pl
from jax.experimental.pallas import tpu as pltpu

F32 = jnp.float32
BF = jnp.bfloat16
MESH = pl.DeviceIdType.MESH

D = 1024
FOXW = 512
HD = 64
GW = 512
CH = 128
NG = 8
FF = 2816
NMEM = 256
XH = 4
XD = 256
CK = 31
HALO = 32
EPS = 1e-6
IN_W = 2568
IN_WP = 2688
VMEM_LIMIT = 56 * 1024 * 1024

ADAM_LR, ADAM_B1, ADAM_B2, ADAM_EPS, ADAM_WD, ADAM_STEP = 0.001, 0.9, 0.999, 1e-08, 0.01, 10

WEIGHTS = ['mix_norm_e', 'w_in_e', 'fox_f_bias', 'gmlp_ln_g', 'gmlp_ln_b', 'gmlp_w_s', 'gmlp_b_s', 'w_out_e',
           'mix_norm_o', 'conv_w_in', 'conv_b_in', 'conv_dw_w', 'conv_dw_b', 'conv_ln_g', 'conv_ln_b',
           'conv_w_out', 'conv_b_out', 'xa_norm', 'mem_norm', 'xa_wq', 'xa_wkv', 'xa_wo', 'ffn_norm',
           'ffn_w_gu', 'ffn_w_down', 'final_norm']
SHARD_AXIS = {'mix_norm_e': None, 'w_in_e': 2, 'fox_f_bias': None, 'gmlp_ln_g': None, 'gmlp_ln_b': None,
              'gmlp_w_s': None, 'gmlp_b_s': None, 'w_out_e': 1, 'mix_norm_o': 1, 'conv_w_in': 2, 'conv_b_in': 1,
              'conv_dw_w': 2, 'conv_dw_b': 1, 'conv_ln_g': 1, 'conv_ln_b': 1, 'conv_w_out': 1, 'conv_b_out': 1,
              'xa_norm': None, 'mem_norm': None, 'xa_wq': 1, 'xa_wkv': 2, 'xa_wo': 1, 'ffn_norm': None,
              'ffn_w_gu': 2, 'ffn_w_down': 1, 'final_norm': None}
BIG = ['w_in_e', 'w_out_e', 'conv_w_in', 'conv_w_out', 'xa_wq', 'xa_wkv', 'xa_wo', 'ffn_w_gu', 'ffn_w_down']


def _pcall(body, **kw):
    return pl.pallas_call(body, **kw)


def _params(sem=None, **kw):
    return pltpu.CompilerParams(dimension_semantics=sem, vmem_limit_bytes=VMEM_LIMIT, **kw)


def _dot(a, b, dims):
    dn = {'nn': (((1,), (0,)), ((), ())), 'nt': (((1,), (1,)), ((), ())), 'tn': (((0,), (0,)), ((), ()))}[dims]
    return lax.dot_general(a, b, dn, preferred_element_type=F32)


def _sigmoid(x):
    return 1.0 / (1.0 + jnp.exp(-x))


def _rms_stats(xv):
    return lax.rsqrt(jnp.mean(xv * xv, axis=-1, keepdims=True) + EPS)


def _rms_bwd(xv, gain, dh):
    r = _rms_stats(xv)
    t = dh * gain
    dx = r * t - xv * (r * r * r * jnp.mean(t * xv, axis=-1, keepdims=True))
    return dx, dh * xv * r


def _fused_mm(name, *, dims, M, N, bm, bn, groups, epi, outs, x=None, gain=None, tiles=(), rows=(),
              h_out=False, reds=()):
    bm = min(bm, M)
    nI, nJ = M // bm, N // bn
    assert nI * bm == M and nJ * bn == N
    assert not reds or nJ == 1
    arrays, specs = [], []

    def add(arr, spec):
        arrays.append(arr)
        specs.append(spec)
        return len(arrays) - 1

    if x is not None:
        K0 = x.shape[1]
        add(x, pl.BlockSpec((bm, K0), lambda i, j: (i, 0)))
        add(gain, pl.BlockSpec((1, K0), lambda i, j: (0, 0)))
    plan = []
    for grp in groups:
        g = []
        for p in grp:
            ai = None
            if p['A'] is not None:
                ai = add(p['A'], pl.BlockSpec((bm, p['Ka']), lambda i, j, o=p.get('acoff', 0): (i, o)))
            ro, co = p.get('roff', 0), p.get('coff', 0)
            if dims == 'nn':
                bi = add(p['B'], pl.BlockSpec((p['Ka'], bn), lambda i, j, ro=ro, co=co: (ro, j + co)))
            else:
                bi = add(p['B'], pl.BlockSpec((bn, p['Ka']), lambda i, j, ro=ro, co=co: (j + ro, co)))
            g.append((ai, bi))
        plan.append(g)
    tile_idx = [add(a, pl.BlockSpec((bm, bn), lambda i, j, o=o: (i, j + o))) for a, o in tiles]
    row_idx = [add(a, pl.BlockSpec((1, bn), lambda i, j, o=o: (0, j + o))) for a, o in rows]
    n_in = len(arrays)

    out_shape = [jax.ShapeDtypeStruct((M, N), dt) for dt in outs]
    out_specs = [pl.BlockSpec((bm, bn), lambda i, j: (i, j)) for _ in outs]
    if h_out:
        out_shape.append(jax.ShapeDtypeStruct((M, x.shape[1]), BF))
        out_specs.append(pl.BlockSpec((bm, x.shape[1]), lambda i, j: (i, 0)))
    for shp in reds:
        out_shape.append(jax.ShapeDtypeStruct(shp, F32))
        out_specs.append(pl.BlockSpec(shp, lambda i, j: (0, 0)))
    n_main = len(outs)
    scratch = [pltpu.VMEM((bm, x.shape[1]), BF)] if x is not None else []

    def body(*refs):
        ins, out_refs, scr = refs[:n_in], refs[n_in:n_in + len(out_shape)], refs[n_in + len(out_shape):]
        i, j = pl.program_id(0), pl.program_id(1)
        if x is not None:
            hn_ref = scr[0]

            @pl.when(j == 0)
            def _():
                xv = ins[0][...]
                hn = (xv * _rms_stats(xv) * ins[1][...]).astype(BF)
                hn_ref[...] = hn
                if h_out:
                    out_refs[n_main][...] = hn

        accs = []
        for g in plan:
            acc = None
            for ai, bi in g:
                a = hn_ref[...] if ai is None else ins[ai][...]
                if a.dtype != BF:
                    a = a.astype(BF)
                d = _dot(a, ins[bi][...], dims)
                acc = d if acc is None else acc + d
            accs.append(acc)
        out_vals, red_vals = epi(accs, [ins[t][...] for t in tile_idx], [ins[r][...] for r in row_idx])
        for r, v in zip(out_refs[:n_main], out_vals):
            r[...] = v.astype(r.dtype)
        if reds:
            red_refs = out_refs[n_main + (1 if h_out else 0):]

            @pl.when(i == 0)
            def _():
                for r in red_refs:
                    r[...] = jnp.zeros(r.shape, F32)

            for r, v in zip(red_refs, red_vals):
                r[...] += v

    res = _pcall(body, name=name, grid=(nI, nJ), in_specs=specs, out_specs=out_specs, out_shape=out_shape,
                 scratch_shapes=scratch, compiler_params=_params(("arbitrary", "arbitrary")))(*arrays)
    return res


def _epi_plain(accs, tiles, rows):
    return [accs[0]], []


def _epi_resid(accs, tiles, rows):
    y = tiles[0] + accs[0]
    if rows:
        y = y + rows[0]
    return [y], []


def _epi_swiglu(accs, tiles, rows):
    g, u = accs
    return [g, u, g * _sigmoid(g) * u], []


def _epi_glu(accs, tiles, rows):
    a, g = accs[0] + rows[0], accs[1] + rows[1]
    return [a, g, a * _sigmoid(g)], []


def _epi_swiglu_bwd(accs, tiles, rows):
    da = accs[0]
    g, u = tiles[0].astype(F32), tiles[1].astype(F32)
    sg = _sigmoid(g)
    return [da * u * (sg * (1.0 + g * (1.0 - sg))), da * (g * sg)], []


def _epi_rms_bwd(accs, tiles, rows):
    dx, dgr = _rms_bwd(tiles[0], rows[0], accs[0])
    return [tiles[1] + dx], [jnp.sum(dgr, axis=0, keepdims=True)]


def _epi_rms_gain_only(accs, tiles, rows):
    _, dgr = _rms_bwd(tiles[0], rows[0], accs[0])
    return [], [jnp.sum(dgr, axis=0, keepdims=True)]


def _norm_mm(name, x, gain, W, *, N, coff=0, bn, out_dtype, bm=512, h_out=False):
    return _fused_mm(name, dims='nn', M=x.shape[0], N=N, bm=bm, bn=bn, x=x, gain=gain,
                     groups=[[dict(A=None, Ka=x.shape[1], B=W, coff=coff)]], epi=_epi_plain, outs=[out_dtype],
                     h_out=h_out)


def _mm_resid(name, pairs, resid, bias=None, bm=512):
    M = resid.shape[0]
    return _fused_mm(name, dims='nn', M=M, N=D, bm=bm, bn=D, groups=[pairs], epi=_epi_resid, outs=[F32],
                     tiles=[(resid, 0)], rows=[(bias, 0)] if bias is not None else [])[0]


def _mm_nt_plain(name, dy, W, bm=512):
    return _fused_mm(name, dims='nt', M=dy.shape[0], N=W.shape[0], bm=bm, bn=W.shape[0],
                     groups=[[dict(A=dy, Ka=dy.shape[1], B=W)]], epi=_epi_plain, outs=[BF])[0]


def _mm_nt_rms_bwd(name, pairs, x, gain, dx_in, bm=256):
    out = _fused_mm(name, dims='nt', M=x.shape[0], N=D, bm=bm, bn=D, groups=[pairs], epi=_epi_rms_bwd,
                    outs=[F32], tiles=[(x, 0), (dx_in, 0)], rows=[(gain, 0)], reds=[(1, D)])
    return out[0], out[1]


def _mm_tn(name, A, G, bk=1024, parts=1):
    T, Ka, Kg = A.shape[0], A.shape[1], G.shape[1]
    w = Kg // parts
    bm = Ka if Ka <= 1024 else Ka // 2
    bn = w if w <= 1408 else w // 2
    bk = min(bk, T)
    per = w // bn
    nI, nJ, nK = Ka // bm, Kg // bn, T // bk

    def body(a_ref, g_ref, o_ref, acc):
        k = pl.program_id(2)

        @pl.when(k == 0)
        def _():
            acc[...] = jnp.zeros(acc.shape, F32)

        acc[...] += _dot(a_ref[...].astype(BF), g_ref[...].astype(BF), 'tn')

        @pl.when(k == nK - 1)
        def _():
            o_ref[...] = acc[...].astype(BF)

    return _pcall(body, name=name, grid=(nI, nJ, nK),
                  in_specs=[pl.BlockSpec((bk, bm), lambda i, j, k: (k, i)),
                            pl.BlockSpec((bk, bn), lambda i, j, k: (k, j))],
                  out_specs=pl.BlockSpec((None, bm, bn), lambda i, j, k: (j // per, i, j % per)),
                  out_shape=jax.ShapeDtypeStruct((parts, Ka, w), BF),
                  scratch_shapes=[pltpu.VMEM((bm, bn), F32)],
                  compiler_params=_params(("arbitrary", "arbitrary", "arbitrary")))(A, G)


def _colsum(name, a, bt=512):
    M, N = a.shape
    bt = min(bt, M)

    def body(a_ref, o_ref):
        @pl.when(pl.program_id(0) == 0)
        def _():
            o_ref[...] = jnp.zeros(o_ref.shape, F32)

        o_ref[...] += jnp.sum(a_ref[...].astype(F32), axis=0, keepdims=True)

    return _pcall(body, name=name, grid=(M // bt,), in_specs=[pl.BlockSpec((bt, N), lambda i: (i, 0))],
                  out_specs=pl.BlockSpec((1, N), lambda i: (0, 0)), out_shape=jax.ShapeDtypeStruct((1, N), F32),
                  compiler_params=_params(("arbitrary",)))(a)


def _cumsum_rows(v):
    T = v.shape[0]
    row = lax.broadcasted_iota(jnp.int32, v.shape, 0)
    s = 1
    while s < T:
        v = v + jnp.where(row >= s, pltpu.roll(v, s, 0), 0.0)
        s *= 2
    return v


def _log_sigmoid(z):
    return jnp.minimum(z, 0.0) - jnp.log(1.0 + jnp.exp(-jnp.abs(z)))


def _fox_gate_fwd(fl, fbias, B, T):
    def body(fl_ref, b_ref, o_ref):
        o_ref[...] = _cumsum_rows(_log_sigmoid(fl_ref[...] + b_ref[...]))

    return _pcall(body, name="fox_gate_fwd", grid=(B,),
                  in_specs=[pl.BlockSpec((T, 128), lambda b: (b, 0)), pl.BlockSpec((1, 128), lambda b: (0, 0))],
                  out_specs=pl.BlockSpec((T, 128), lambda b: (b, 0)),
                  out_shape=jax.ShapeDtypeStruct((B * T, 128), F32), compiler_params=_params(("arbitrary",)))(fl, fbias)


def _fox_gate_bwd(fl, fbias, dcq, dck, B, T):
    def body(fl_ref, b_ref, dcq_ref, dck_ref, dfl_ref, db_ref):
        dc = dcq_ref[...] + dck_ref[...]
        rev = jnp.sum(dc, axis=0, keepdims=True) - _cumsum_rows(dc) + dc
        dfl = rev * _sigmoid(-(fl_ref[...] + b_ref[...]))
        dfl_ref[...] = dfl

        @pl.when(pl.program_id(0) == 0)
        def _():
            db_ref[...] = jnp.zeros(db_ref.shape, F32)

        db_ref[...] += jnp.sum(dfl, axis=0, keepdims=True)

    return _pcall(body, name="fox_gate_bwd", grid=(B,),
                  in_specs=[pl.BlockSpec((T, 128), lambda b: (b, 0)), pl.BlockSpec((1, 128), lambda b: (0, 0)),
                            pl.BlockSpec((T, 128), lambda b: (b, 0)), pl.BlockSpec((T, 128), lambda b: (b, 0))],
                  out_specs=[pl.BlockSpec((T, 128), lambda b: (b, 0)), pl.BlockSpec((1, 128), lambda b: (0, 0))],
                  out_shape=[jax.ShapeDtypeStruct((B * T, 128), F32), jax.ShapeDtypeStruct((1, 128), F32)],
                  compiler_params=_params(("arbitrary",)))(fl, fbias, dcq, dck)


def _carried_call(body, *, name, grid, in_specs, out_specs, out_shape, scratch_shapes, operands, carry):
    if carry is None:
        return _pcall(body, name=name, grid=grid, in_specs=in_specs, out_specs=out_specs, out_shape=out_shape,
                      scratch_shapes=scratch_shapes, compiler_params=_params(("arbitrary",) * len(grid)))(*operands), []
    n, n_in, n_out, n_scr = len(carry['inputs']), len(in_specs), len(out_specs), len(scratch_shapes)

    def wrapped(*refs):
        ins, cin = refs[:n_in], refs[n_in:n_in + n]
        outs, cout = refs[n_in + n:n_in + n + n_out], refs[n_in + n + n_out:n_in + 2 * n + n_out]
        scr = refs[n_in + 2 * n + n_out:]
        send_sems, recv_sems = scr[n_scr:]
        ids = [pl.program_id(d) for d in range(len(grid))]
        first = functools.reduce(jnp.logical_and, [i == 0 for i in ids])
        last = functools.reduce(jnp.logical_and, [i == g - 1 for i, g in zip(ids, grid)])

        @pl.when(first)
        def _():
            for cp in carry['copies'](cin, cout, send_sems, recv_sems):
                cp.start()

        body(*ins, *outs, *scr[:n_scr])

        @pl.when(last)
        def _():
            for cp in carry['copies'](cin, cout, send_sems, recv_sems):
                cp.wait()

    aliases = {n_in + a: n_out + a for a in range(n)} if carry['in_place'] else {}
    res = _pcall(wrapped, name=name, grid=grid, in_specs=list(in_specs) + [ANY] * n,
                 out_specs=list(out_specs) + [ANY] * n, out_shape=list(out_shape) + carry['out_shape'],
                 scratch_shapes=list(scratch_shapes) + _sem_pairs(carry['nsem']), input_output_aliases=aliases,
                 compiler_params=_params(("arbitrary",) * len(grid)))(*operands, *carry['inputs'])
    return res[:n_out], res[n_out:]


def _carry_gather(bufs):
    n = len(bufs)

    def copies(in_refs, out_refs, send_sems, recv_sems):
        x, y, c = _coords()
        cps = []
        for a in range(n):
            blk = out_refs[a].at[2 * x + y, c]
            cps += [pltpu.make_async_remote_copy(src_ref=blk, dst_ref=blk, send_sem=send_sems.at[3 * a + j],
                                                 recv_sem=recv_sems.at[3 * a + j], device_id=(cx, cy, c),
                                                 device_id_type=MESH) for j, (cx, cy) in enumerate(_other_chips(x, y))]
        return cps

    return dict(inputs=bufs, out_shape=[jax.ShapeDtypeStruct(b.shape, b.dtype) for b in bufs], in_place=True,
                nsem=3 * n, copies=copies)


def _carry_exchange(qs):
    n = len(qs)

    def copies(in_refs, out_refs, send_sems, recv_sems):
        x, y, c = _coords()
        return [pltpu.make_async_remote_copy(src_ref=in_refs[a].at[2 * cx + cy], dst_ref=out_refs[a].at[j],
                                             send_sem=send_sems.at[3 * a + j], recv_sem=recv_sems.at[3 * a + j],
                                             device_id=(cx, cy, c), device_id_type=MESH)
                for a in range(n) for j, (cx, cy) in enumerate(_other_chips(x, y))]

    return dict(inputs=qs, out_shape=[jax.ShapeDtypeStruct((3,) + q.shape[1:], q.dtype) for q in qs], in_place=False,
                nsem=3 * n, copies=copies)


NEG = -1e30


def _fox_fwd(qkv, cum4, B, T, qoff, bq, bk, carry=None):
    nq, nkb = T // bq, T // bk
    N = B * T

    def body(q_ref, k_ref, v_ref, cum_ref, o_ref, lse_ref):
        hp, i = pl.program_id(1), pl.program_id(2)
        lane = lax.broadcasted_iota(jnp.int32, (bq, 128), 1)
        heads = [slice(e * HD, (e + 1) * HD) for e in range(2)]
        qs = [q_ref[:, sl] * 0.125 for sl in heads]

        def block(j, carry, diagonal):
            ks = pl.multiple_of(j * bk, bk)
            out = []
            for e, sl in enumerate(heads):
                m, l, acc = carry[e]
                s = _dot(qs[e], k_ref[pl.ds(ks, bk), sl], 'nt') - cum_ref[0, 2 * hp + e, pl.ds(j, 1), :]
                if diagonal:
                    keep = lax.broadcasted_iota(jnp.int32, (bq, bk), 0) >= lax.broadcasted_iota(jnp.int32, (bq, bk), 1)
                    s = jnp.where(keep, s, NEG)
                m_new = jnp.maximum(m, jnp.max(s, axis=1, keepdims=True))
                p = jnp.exp(s - m_new)
                alpha = jnp.exp(m - m_new)
                l = alpha * l + jnp.sum(p, axis=1, keepdims=True)
                acc = alpha * acc + _dot(p.astype(BF), v_ref[pl.ds(ks, bk), sl], 'nn')
                out.append((m_new, l, acc))
            return tuple(out)

        init = tuple((jnp.full((bq, 1), NEG, F32), jnp.zeros((bq, 1), F32), jnp.zeros((bq, HD), F32)) for _ in heads)
        carry = lax.fori_loop(0, i, lambda j, c: block(j, c, False), init)
        carry = block(i, carry, True)
        lse_tile = jnp.zeros((bq, 128), F32)
        for e, sl in enumerate(heads):
            m, l, acc = carry[e]
            o_ref[:, sl] = (acc / l).astype(BF)
            lse_tile = jnp.where(lane == e, m + jnp.log(l), lse_tile)
        lse_ref[...] = lse_tile

    return _carried_call(body, name="fox_fwd", grid=(B, 4, nq),
                         in_specs=[pl.BlockSpec((bq, 128), lambda b, h, i: (b * nq + i, qoff + h)),
                                   pl.BlockSpec((T, 128), lambda b, h, i: (b, qoff + 4 + h)),
                                   pl.BlockSpec((T, 128), lambda b, h, i: (b, qoff + 8 + h)),
                                   pl.BlockSpec((1, NG, nkb, bk), lambda b, h, i: (b, 0, 0, 0))],
                         out_specs=[pl.BlockSpec((bq, 128), lambda b, h, i: (b * nq + i, h)),
                                    pl.BlockSpec((bq, 128), lambda b, h, i: (b * nq + i, h))],
                         out_shape=[jax.ShapeDtypeStruct((N, FOXW), BF), jax.ShapeDtypeStruct((N, FOXW), F32)],
                         scratch_shapes=[], operands=(qkv, qkv, qkv, cum4), carry=carry)


def _fox_bwd(qkv, cum4, o, lse, dcat, B, T, qoff, bq, bk, carry=None):
    nq, nkb = T // bq, T // bk
    N = B * T

    def body(q_ref, k_ref, v_ref, cum_ref, o_ref, lse_ref, do_ref, dq_ref, dk_ref, dv_ref, dcum_ref, dcq_ref,
             dq_acc, dl_ref, rs_ref):
        hp = pl.program_id(1)
        heads = [slice(e * HD, (e + 1) * HD) for e in range(2)]
        keep = lax.broadcasted_iota(jnp.int32, (bq, bk), 0) >= lax.broadcasted_iota(jnp.int32, (bq, bk), 1)
        dcq_ref[...] = jnp.zeros(dcq_ref.shape, F32)
        dq_acc[...] = jnp.zeros(dq_acc.shape, F32)
        rs_ref[...] = jnp.zeros(rs_ref.shape, F32)
        for e, sl in enumerate(heads):
            dl_ref[e] = jnp.sum(do_ref[:, sl].astype(F32) * o_ref[:, sl].astype(F32), axis=1, keepdims=True)
        for j in range(nkb):
            krows = slice(j * bk, (j + 1) * bk)

            def tile(i, carry, diagonal):
                qs = i * bq if diagonal else pl.multiple_of(i * bq, bq)
                out = []
                for e, sl in enumerate(heads):
                    dk_a, dv_a, cs = carry[e]
                    q, k = q_ref[pl.ds(qs, bq), sl], k_ref[krows, sl]
                    do = do_ref[pl.ds(qs, bq), sl]
                    s = _dot(q, k, 'nt') * 0.125 - cum_ref[0, 2 * hp + e, j:j + 1, :]
                    p = jnp.exp(s - lse_ref[pl.ds(qs, bq), e:e + 1])
                    if diagonal:
                        p = jnp.where(keep, p, 0.0)
                    dv_a = dv_a + _dot(p.astype(BF), do, 'tn')
                    ds = p * (_dot(do, v_ref[krows, sl], 'nt') - dl_ref[e, pl.ds(qs, bq), :])
                    cs = cs + jnp.sum(ds, axis=0, keepdims=True)
                    rs_ref[e, pl.ds(qs, bq), :] += jnp.sum(ds, axis=1, keepdims=True)
                    dsb = ds.astype(BF)
                    dk_a = dk_a + _dot(dsb, q, 'tn')
                    dq_acc[e, pl.ds(qs, bq), :] += _dot(dsb, k, 'nn')
                    out.append((dk_a, dv_a, cs))
                return tuple(out)

            init = tuple((jnp.zeros((bk, HD), F32), jnp.zeros((bk, HD), F32), jnp.zeros((1, bk), F32)) for _ in heads)
            carry = lax.fori_loop(j + 1, nq, lambda i, c: tile(i, c, False), tile(j, init, True))
            for e, sl in enumerate(heads):
                dk_a, dv_a, cs = carry[e]
                dk_ref[krows, sl] = (dk_a * 0.125).astype(BF)
                dv_ref[krows, sl] = dv_a.astype(BF)
                dcum_ref[0, e, j:j + 1, :] = -cs
        for e, sl in enumerate(heads):
            dq_ref[:, sl] = (dq_acc[e] * 0.125).astype(BF)
            dcq_ref[:, e:e + 1] = rs_ref[e]

    seq = lambda off: pl.BlockSpec((T, 128), lambda b, h, off=off: (b, off + h))
    return _carried_call(body, name="fox_bwd", grid=(B, 4),
                         in_specs=[seq(qoff), seq(qoff + 4), seq(qoff + 8),
                                   pl.BlockSpec((1, NG, nkb, bk), lambda b, h: (b, 0, 0, 0)),
                                   seq(0), seq(0), seq(0)],
                         out_specs=[seq(0), seq(0), seq(0),
                                    pl.BlockSpec((1, 2, nkb, bk), lambda b, h: (b, h, 0, 0)), seq(0)],
                         out_shape=[jax.ShapeDtypeStruct((N, FOXW), BF)] * 3
                         + [jax.ShapeDtypeStruct((B, NG, nkb, bk), F32), jax.ShapeDtypeStruct((N, FOXW), F32)],
                         scratch_shapes=[pltpu.VMEM((2, T, HD), F32), pltpu.VMEM((2, T, 1), F32),
                                         pltpu.VMEM((2, T, 1), F32)],
                         operands=(qkv, qkv, qkv, cum4, o, lse, dcat), carry=carry)


_GC = math.sqrt(2.0 / math.pi)
_GA = 0.044715


def _gelu(z):
    return 0.5 * z * (1.0 + jnp.tanh(_GC * (z + _GA * z * z * z)))


def _gelu_grad(z):
    t = jnp.tanh(_GC * (z + _GA * z * z * z))
    return 0.5 * (1.0 + t) + 0.5 * z * (1.0 - t * t) * (_GC * (1.0 + 3.0 * _GA * z * z))


def _gmlp_common(z, lng, lnb):
    zg = _gelu(z)
    u, vg = zg[:, :GW], zg[:, GW:]
    mu = jnp.mean(vg, axis=-1, keepdims=True)
    xc = vg - mu
    rstd = lax.rsqrt(jnp.mean(xc * xc, axis=-1, keepdims=True) + EPS)
    xhat = xc * rstd
    return u, xhat, rstd, xhat * lng + lnb


def _tril_w(ws_ref):
    tri = lax.broadcasted_iota(jnp.int32, (CH, CH), 0) >= lax.broadcasted_iota(jnp.int32, (CH, CH), 1)
    return [jnp.where(tri, ws_ref[g], 0.0).astype(BF) for g in range(NG)], tri


def _split_pair(vp):
    lane = lax.broadcasted_iota(jnp.int32, vp.shape, 1)
    zero = jnp.zeros(vp.shape, vp.dtype)
    return jnp.concatenate([jnp.where(lane < HD, vp, zero), jnp.where(lane >= HD, vp, zero)], axis=0)


def _gmlp_mix(wt, vgn_b):
    outs = []
    for p in range(NG // 2):
        wcat = jnp.concatenate([wt[2 * p], wt[2 * p + 1]], axis=1)
        outs.append(_dot(wcat, _split_pair(vgn_b[:, 128 * p:128 * (p + 1)]), 'nn'))
    return jnp.concatenate(outs, axis=1)


def _gmlp_fwd(z, lng, lnb, ws, bfull, bt):
    N = z.shape[0]

    def body(z_ref, lng_ref, lnb_ref, ws_ref, bf_ref, o_ref):
        wt, _ = _tril_w(ws_ref)
        for c in range(bt // CH):
            rows = slice(c * CH, (c + 1) * CH)
            u, _, _, vgn = _gmlp_common(z_ref[rows, :], lng_ref[...], lnb_ref[...])
            mixed = _gmlp_mix(wt, vgn.astype(BF)) + bf_ref[...]
            o_ref[rows, :] = (u * mixed).astype(BF)

    full = lambda shp: pl.BlockSpec(shp, lambda i: (0,) * len(shp))
    return _pcall(body, name="gmlp_fwd", grid=(N // bt,),
                  in_specs=[pl.BlockSpec((bt, D), lambda i: (i, 0)), full((1, GW)), full((1, GW)),
                            full((NG, CH, CH)), full((CH, GW))],
                  out_specs=pl.BlockSpec((bt, GW), lambda i: (i, 0)), out_shape=jax.ShapeDtypeStruct((N, GW), BF),
                  compiler_params=_params(("arbitrary",)))(z, lng, lnb, ws, bfull)


def _gmlp_bwd(z, dcat, lng, lnb, ws, bfull, bt):
    N = z.shape[0]

    def body(z_ref, da_ref, lng_ref, lnb_ref, ws_ref, bf_ref, dz_ref, dg_ref, db_ref, dws_ref, dbf_ref):
        @pl.when(pl.program_id(0) == 0)
        def _():
            for r in (dg_ref, db_ref, dws_ref, dbf_ref):
                r[...] = jnp.zeros(r.shape, F32)

        wt, tri = _tril_w(ws_ref)
        lane = lax.broadcasted_iota(jnp.int32, (CH, 128), 1)
        for c in range(bt // CH):
            rows = slice(c * CH, (c + 1) * CH)
            zc = z_ref[rows, :]
            u, xhat, rstd, vgn = _gmlp_common(zc, lng_ref[...], lnb_ref[...])
            vgn_b = vgn.astype(BF)
            mixed = _gmlp_mix(wt, vgn_b) + bf_ref[...]
            da = da_ref[rows, :].astype(F32)
            dmix = da * u
            du = da * mixed
            dbf_ref[...] += dmix
            dvs = []
            for p in range(NG // 2):
                cols = slice(128 * p, 128 * (p + 1))
                dmp = dmix[:, cols].astype(BF)
                dwp = _dot(_split_pair(dmp), vgn_b[:, cols], 'nt')
                dws_ref[2 * p] += jnp.where(tri, dwp[:CH], 0.0)
                dws_ref[2 * p + 1] += jnp.where(tri, dwp[CH:], 0.0)
                dvs.append(jnp.where(lane < HD, _dot(wt[2 * p], dmp, 'tn'), _dot(wt[2 * p + 1], dmp, 'tn')))
            dvgn = jnp.concatenate(dvs, axis=1)
            dg_ref[...] += jnp.sum(dvgn * xhat, axis=0, keepdims=True)
            db_ref[...] += jnp.sum(dvgn, axis=0, keepdims=True)
            dxh = dvgn * lng_ref[...]
            dvg = rstd * (dxh - jnp.mean(dxh, axis=-1, keepdims=True)
                          - xhat * jnp.mean(dxh * xhat, axis=-1, keepdims=True))
            dz_ref[rows, :] = (jnp.concatenate([du, dvg], axis=1) * _gelu_grad(zc)).astype(BF)

    full = lambda shp: pl.BlockSpec(shp, lambda i: (0,) * len(shp))
    return _pcall(body, name="gmlp_bwd", grid=(N // bt,),
                  in_specs=[pl.BlockSpec((bt, D), lambda i: (i, 0)), pl.BlockSpec((bt, GW), lambda i: (i, 1)),
                            full((1, GW)), full((1, GW)), full((NG, CH, CH)), full((CH, GW))],
                  out_specs=[pl.BlockSpec((bt, D), lambda i: (i, 0)), full((1, GW)), full((1, GW)),
                             full((NG, CH, CH)), full((CH, GW))],
                  out_shape=[jax.ShapeDtypeStruct((N, D), BF), jax.ShapeDtypeStruct((1, GW), F32),
                             jax.ShapeDtypeStruct((1, GW), F32), jax.ShapeDtypeStruct((NG, CH, CH), F32),
                             jax.ShapeDtypeStruct((CH, GW), F32)],
                  compiler_params=_params(("arbitrary",)))(z, dcat, lng, lnb, ws, bfull)


def _group_sum(name, a):
    def body(a_ref, o_ref):
        lane = lax.broadcasted_iota(jnp.int32, (CH, 128), 1)
        out = jnp.zeros((CH, 128), F32)
        for g in range(NG):
            out = jnp.where(lane == g, jnp.sum(a_ref[:, g * HD:(g + 1) * HD], axis=1, keepdims=True), out)
        o_ref[...] = out

    return _pcall(body, name=name, out_shape=jax.ShapeDtypeStruct((CH, 128), F32))(a)


def _xattn_softmax(q_h, k_h):
    s = _dot(q_h, k_h, 'nt') * (XD ** -0.5)
    p = jnp.exp(s - jnp.max(s, axis=1, keepdims=True))
    return p / jnp.sum(p, axis=1, keepdims=True)


def _xattn_fwd(name, q, kv, B, T, bq):
    nq = T // bq

    def body(q_ref, kv_ref, o_ref):
        for h in range(XH):
            cols = slice(h * XD, (h + 1) * XD)
            p = _xattn_softmax(q_ref[:, cols], kv_ref[:, cols])
            o_ref[:, cols] = _dot(p.astype(BF), kv_ref[:, D + h * XD:D + (h + 1) * XD], 'nn').astype(BF)

    return _pcall(body, name=name, grid=(B, nq),
                  in_specs=[pl.BlockSpec((bq, D), lambda b, i: (b * nq + i, 0)),
                            pl.BlockSpec((NMEM, 2 * D), lambda b, i: (b, 0))],
                  out_specs=pl.BlockSpec((bq, D), lambda b, i: (b * nq + i, 0)),
                  out_shape=jax.ShapeDtypeStruct((B * T, D), BF), compiler_params=_params(("arbitrary", "arbitrary")))(q, kv)


def _xattn_bwd(name, q, kv, do, B, T, bq):
    nq = T // bq
    sc = XD ** -0.5

    def body(q_ref, kv_ref, do_ref, dq_ref, dkv_ref):
        @pl.when(pl.program_id(1) == 0)
        def _():
            dkv_ref[...] = jnp.zeros(dkv_ref.shape, F32)

        for h in range(XH):
            cols = slice(h * XD, (h + 1) * XD)
            vcols = slice(D + h * XD, D + (h + 1) * XD)
            qh, kh, doh = q_ref[:, cols], kv_ref[:, cols], do_ref[:, cols]
            p = _xattn_softmax(qh, kh)
            dp = _dot(doh, kv_ref[:, vcols], 'nt')
            ds = p * (dp - jnp.sum(p * dp, axis=1, keepdims=True))
            dsb = ds.astype(BF)
            dq_ref[:, cols] = (_dot(dsb, kh, 'nn') * sc).astype(BF)
            dkv_ref[:, cols] += _dot(dsb, qh, 'tn') * sc
            dkv_ref[:, vcols] += _dot(p.astype(BF), doh, 'tn')

    blk = pl.BlockSpec((bq, D), lambda b, i: (b * nq + i, 0))
    return _pcall(body, name=name, grid=(B, nq),
                  in_specs=[blk, pl.BlockSpec((NMEM, 2 * D), lambda b, i: (b, 0)), blk],
                  out_specs=[blk, pl.BlockSpec((NMEM, 2 * D), lambda b, i: (b, 0))],
                  out_shape=[jax.ShapeDtypeStruct((B * T, D), BF), jax.ShapeDtypeStruct((B * NMEM, 2 * D), F32)],
                  compiler_params=_params(("arbitrary", "arbitrary")))(q, kv, do)


def _ln_stats(v):
    mu = jnp.mean(v, axis=-1, keepdims=True)
    xc = v - mu
    rstd = lax.rsqrt(jnp.mean(xc * xc, axis=-1, keepdims=True) + EPS)
    return xc * rstd, rstd


SUB = 8


def _fill_phases(win, sh, rows):
    for b in range(1, SUB):
        sh[b - 1] = win[b:b + rows, :]


LANES = 128
RC = 32


def _shifted_chunk(win, sh, o, r0, cols, rc=RC):
    a, b = divmod(o, SUB)
    start = r0 + SUB * a
    rows = slice(start, start + rc) if isinstance(r0, int) else pl.ds(pl.multiple_of(start, SUB), rc)
    return win[rows, cols] if b == 0 else sh[b - 1, rows, cols]


def _sum8(v):
    return jnp.sum(v.reshape(v.shape[0] // SUB, SUB, v.shape[1]), axis=0)


def _sum_groups(name, a):
    R, C = a.shape[0] // SUB, a.shape[1]

    def body(a_ref, o_ref):
        o_ref[...] = jnp.sum(a_ref[...].reshape(R, SUB, C), axis=1)

    return _pcall(body, name=name, out_shape=jax.ShapeDtypeStruct((R, C), F32))(a)


def _conv_fwd(y, w32, wb, lng, lnb, B, T, bt):
    nt = T // bt
    hb = bt // HALO
    prows = bt + HALO - SUB

    def body(y_ref, yp_ref, w_ref, wb_ref, lng_ref, lnb_ref, s_ref, yc_ref, win, sh):
        i = pl.program_id(1)
        win[0:HALO, :] = jnp.where(i > 0, yp_ref[...], 0.0)
        win[HALO:, :] = y_ref[...]
        _fill_phases(win, sh, prows)
        for cs in range(D // LANES):
            cols = slice(cs * LANES, (cs + 1) * LANES)

            rc = 2 * RC
            for r0 in range(0, bt, rc):
                accs = [jnp.zeros((rc, LANES), F32) + wb_ref[:, cols], jnp.zeros((rc, LANES), F32)]
                for j in range(CK):
                    accs[j % 2] = accs[j % 2] + w_ref[j:j + 1, cols] * _shifted_chunk(win, sh, HALO - (CK - 1) + j, r0,
                                                                                  cols, rc)
                yc_ref[r0:r0 + rc, cols] = accs[0] + accs[1]
        acc = yc_ref[...]
        xhat, _ = _ln_stats(acc)
        ln = xhat * lng_ref[...] + lnb_ref[...]
        s_ref[...] = (ln * _sigmoid(ln)).astype(BF)

    row = lambda n: pl.BlockSpec((n, D), lambda b, i: (0, 0))
    cur = pl.BlockSpec((bt, D), lambda b, i: (b * nt + i, 0))
    return _pcall(body, name="conv_fwd", grid=(B, nt),
                  in_specs=[cur, pl.BlockSpec((HALO, D), lambda b, i: (jnp.maximum((b * nt + i) * hb - 1, 0), 0)),
                            row(HALO), row(1), row(1), row(1)],
                  out_specs=[cur, cur],
                  out_shape=[jax.ShapeDtypeStruct((B * T, D), BF), jax.ShapeDtypeStruct((B * T, D), F32)],
                  scratch_shapes=[pltpu.VMEM((bt + HALO, D), F32), pltpu.VMEM((SUB - 1, prows, D), F32)],
                  compiler_params=_params(("arbitrary", "arbitrary")))(y, y, w32, wb, lng, lnb)


def _conv_bwd(ds, yc, y, pa, pg, w32, lng, lnb, B, T, bt):
    nt = T // bt
    hb = bt // HALO
    nblk32 = B * T // HALO

    def ln_bwd(dsv, ycv, lng, lnb):
        xhat, rstd = _ln_stats(ycv)
        ln = xhat * lng + lnb
        sg = _sigmoid(ln)
        dln = dsv * (sg * (1.0 + ln * (1.0 - sg)))
        dxh = dln * lng
        dyc = rstd * (dxh - jnp.mean(dxh, axis=-1, keepdims=True)
                      - xhat * jnp.mean(dxh * xhat, axis=-1, keepdims=True))
        return dyc, dln, xhat

    prows = bt + HALO - SUB

    def body(ds_ref, dsn_ref, yc_ref, ycn_ref, y_ref, yp_ref, pa_ref, pg_ref, w_ref, lng_ref, lnb_ref,
             dpa_ref, dpg_ref, dw_ref, dwb_ref, dlng_ref, dlnb_ref, dba_ref, dbg_ref, dwin, ywin, dsh, ysh):
        i = pl.program_id(1)

        @pl.when((pl.program_id(0) == 0) & (i == 0))
        def _():
            for r in (dw_ref, dwb_ref, dlng_ref, dlnb_ref, dba_ref, dbg_ref):
                r[...] = jnp.zeros(r.shape, F32)

        lng, lnb = lng_ref[...], lnb_ref[...]
        dyc, dln, xhat = ln_bwd(ds_ref[...].astype(F32), yc_ref[...], lng, lnb)
        dycn, _, _ = ln_bwd(dsn_ref[...].astype(F32), ycn_ref[...], lng, lnb)
        dwin[0:bt, :] = dyc
        dwin[bt:, :] = jnp.where(i < nt - 1, dycn, 0.0)
        ywin[0:HALO, :] = jnp.where(i > 0, yp_ref[...], 0.0)
        ywin[HALO:, :] = y_ref[...]
        dlng_ref[...] += jnp.sum(dln * xhat, axis=0, keepdims=True)
        dlnb_ref[...] += jnp.sum(dln, axis=0, keepdims=True)
        dwb_ref[...] += jnp.sum(dyc, axis=0, keepdims=True)
        _fill_phases(dwin, dsh, prows)
        _fill_phases(ywin, ysh, prows)
        zero8 = jnp.zeros((SUB, LANES), F32)
        for cs in range(D // LANES):
            cols = slice(cs * LANES, (cs + 1) * LANES)

            def chunk(r, carry, cols=cols):
                dw_acc, sum_a, sum_g = carry
                r0 = pl.multiple_of(r * RC, RC)
                rows = pl.ds(r0, RC)
                dyc_c = dwin[rows, cols]
                dys = [jnp.zeros((RC, LANES), F32), jnp.zeros((RC, LANES), F32)]
                dw_new = []
                for j in range(CK):
                    dys[j % 2] = dys[j % 2] + w_ref[j:j + 1, cols] * _shifted_chunk(dwin, dsh, CK - 1 - j, r0, cols)
                    dw_new.append(dw_acc[j] + _sum8(dyc_c * _shifted_chunk(ywin, ysh, HALO - (CK - 1) + j, r0, cols)))
                dy = dys[0] + dys[1]
                a, g = pa_ref[rows, cols].astype(F32), pg_ref[rows, cols].astype(F32)
                sg = _sigmoid(g)
                da = dy * sg
                dg = dy * a * sg * (1.0 - sg)
                dpa_ref[rows, cols] = da.astype(BF)
                dpg_ref[rows, cols] = dg.astype(BF)
                return tuple(dw_new), sum_a + _sum8(da), sum_g + _sum8(dg)

            dw_acc, sum_a, sum_g = lax.fori_loop(0, bt // RC, chunk, ((zero8,) * CK, zero8, zero8))
            for j in range(CK):
                dw_ref[SUB * j:SUB * (j + 1), cols] += dw_acc[j]
            dba_ref[:, cols] += jnp.sum(sum_a, axis=0, keepdims=True)
            dbg_ref[:, cols] += jnp.sum(sum_g, axis=0, keepdims=True)

    cur = pl.BlockSpec((bt, D), lambda b, i: (b * nt + i, 0))
    nxt = pl.BlockSpec((HALO, D), lambda b, i: (jnp.minimum((b * nt + i + 1) * hb, nblk32 - 1), 0))
    prv = pl.BlockSpec((HALO, D), lambda b, i: (jnp.maximum((b * nt + i) * hb - 1, 0), 0))
    row = lambda n: pl.BlockSpec((n, D), lambda b, i: (0, 0))
    N = B * T
    return _pcall(body, name="conv_bwd", grid=(B, nt),
                  in_specs=[cur, nxt, cur, nxt, cur, prv, cur, cur, row(HALO), row(1), row(1)],
                  out_specs=[cur, cur, row(HALO * SUB), row(1), row(1), row(1), row(1), row(1)],
                  out_shape=[jax.ShapeDtypeStruct((N, D), BF)] * 2 + [jax.ShapeDtypeStruct((HALO * SUB, D), F32)]
                  + [jax.ShapeDtypeStruct((1, D), F32)] * 5,
                  scratch_shapes=[pltpu.VMEM((bt + HALO, D), F32), pltpu.VMEM((bt + HALO, D), F32),
                                  pltpu.VMEM((SUB - 1, prows, D), F32), pltpu.VMEM((SUB - 1, prows, D), F32)],
                  compiler_params=_params(("arbitrary", "arbitrary")))(ds, ds, yc, yc, y, y, pa, pg, w32, lng, lnb)


def _head(x, tgt, gain, bt=512):
    N = x.shape[0]
    bt = min(bt, N)

    def body(x_ref, t_ref, g_ref, dx_ref, loss_ref, dg_ref):
        @pl.when(pl.program_id(0) == 0)
        def _():
            loss_ref[...] = jnp.zeros(loss_ref.shape, F32)
            dg_ref[...] = jnp.zeros(dg_ref.shape, F32)

        xv = x_ref[...]
        gain = g_ref[...]
        err = xv * _rms_stats(xv) * gain - t_ref[...]
        loss_ref[...] += 0.5 * jnp.sum(jnp.mean(err * err, axis=-1, keepdims=True), axis=0, keepdims=True)
        dx, dgr = _rms_bwd(xv, gain, err * (1.0 / D))
        dx_ref[...] = dx
        dg_ref[...] += jnp.sum(dgr, axis=0, keepdims=True)

    blk = pl.BlockSpec((bt, D), lambda i: (i, 0))
    return _pcall(body, name="loss_head", grid=(N // bt,),
                  in_specs=[blk, blk, pl.BlockSpec((1, D), lambda i: (0, 0))],
                  out_specs=[blk, pl.BlockSpec((1, 128), lambda i: (0, 0)), pl.BlockSpec((1, D), lambda i: (0, 0))],
                  out_shape=[jax.ShapeDtypeStruct((N, D), F32), jax.ShapeDtypeStruct((1, 128), F32),
                             jax.ShapeDtypeStruct((1, D), F32)],
                  compiler_params=_params(("arbitrary",)))(x, tgt, gain)


def _local_step(x, mem, tgt, Wb, P, hooks=None):
    B, T, _ = x.shape
    N = B * T
    bq = bk = min(512, T)
    bt = min(512, T)
    x0 = x.reshape(N, D)
    mem2 = mem.reshape(B * NMEM, D)
    tgt2 = tgt.reshape(N, D)
    row = lambda v: v.reshape(1, -1)
    G = {}

    w_in = Wb['w_in_e'][0]
    w_inp = jnp.concatenate([w_in[:, 3 * FOXW + NG:], w_in[:, :3 * FOXW], w_in[:, 3 * FOXW:3 * FOXW + NG],
                             jnp.zeros((D, 128 - NG), BF)], axis=1)
    g_e = row(P['mix_norm_e'])
    z, h0 = _norm_mm("proj_z", x0, g_e, w_inp, N=D, coff=0, bn=D, out_dtype=F32, h_out=True)
    qkv = _norm_mm("proj_qkv", x0, g_e, w_inp, N=3 * FOXW, coff=2, bn=FOXW, out_dtype=BF)[0]
    fl = _norm_mm("proj_f", x0, g_e, w_inp, N=128, coff=20, bn=128, out_dtype=F32)[0]
    fbias = jnp.concatenate([P['fox_f_bias'].reshape(1, NG), jnp.zeros((1, 128 - NG), F32)], axis=1)
    cum = _fox_gate_fwd(fl, fbias, B, T)
    cum4 = cum[:, :NG].reshape(B, T, NG).transpose(0, 2, 1).reshape(B, NG, T // bk, bk)
    (b_out, lse), arrived = _fox_fwd(qkv, cum4, B, T, 0, bq, bk, carry=hooks['fwd_carry']() if hooks else None)
    if hooks:
        Wb = {**Wb, **hooks['fwd_done'](arrived)}
    lng, lnb = row(P['gmlp_ln_g']), row(P['gmlp_ln_b'])
    ws = P['gmlp_w_s'][0]
    bfull = jnp.repeat(P['gmlp_b_s'][0].T, HD, axis=1)
    a_out = _gmlp_fwd(z, lng, lnb, ws, bfull, bt)
    w_out = Wb['w_out_e'][0]
    x1 = _mm_resid("mix_out", [dict(A=b_out, Ka=FOXW, B=w_out, roff=0), dict(A=a_out, Ka=GW, B=w_out, roff=1)], x0)

    def xa_ffn_fwd(l, xin):
        qx, hq = _norm_mm(f"xa_q{l}", xin, row(P['xa_norm'][l]), Wb['xa_wq'][l], N=D, bn=D, out_dtype=BF, h_out=True)
        kv, hm = _norm_mm(f"xa_kv{l}", mem2, row(P['mem_norm'][l]), Wb['xa_wkv'][l], N=2 * D, bn=D, out_dtype=BF,
                          h_out=True)
        o = _xattn_fwd(f"xattn_fwd{l}", qx, kv, B, T, bt)
        xm = _mm_resid(f"xa_o{l}", [dict(A=o, Ka=D, B=Wb['xa_wo'][l])], xin)
        wgu = Wb['ffn_w_gu'][l]
        g, u, a, hf = _fused_mm(f"ffn_gu{l}", dims='nn', M=N, N=FF, bm=min(512, N), bn=FF // 2, x=xm,
                                gain=row(P['ffn_norm'][l]),
                                groups=[[dict(A=None, Ka=D, B=wgu, coff=0)], [dict(A=None, Ka=D, B=wgu, coff=2)]],
                                epi=_epi_swiglu, outs=[BF, BF, BF], h_out=True)
        xo = _mm_resid(f"ffn_down{l}", [dict(A=a, Ka=FF, B=Wb['ffn_w_down'][l])], xm)
        return xo, dict(xin=xin, qx=qx, hq=hq, kv=kv, hm=hm, o=o, xm=xm, g=g, u=u, a=a, hf=hf)

    x3, S0 = xa_ffn_fwd(0, x1)
    w_cin = Wb['conv_w_in'][0]
    b_cin = row(P['conv_b_in'])
    pa, pg, y, hc = _fused_mm("conv_in", dims='nn', M=N, N=D, bm=min(512, N), bn=D, x=x3, gain=row(P['mix_norm_o']),
                              groups=[[dict(A=None, Ka=D, B=w_cin, coff=0)], [dict(A=None, Ka=D, B=w_cin, coff=1)]],
                              epi=_epi_glu, outs=[BF, BF, F32], rows=[(b_cin, 0), (b_cin, 1)], h_out=True)
    w32 = jnp.concatenate([P['conv_dw_w'][0], jnp.zeros((HALO - CK, D), F32)], axis=0)
    cbt = min(256, T)
    s, yc = _conv_fwd(y, w32, row(P['conv_dw_b']), row(P['conv_ln_g']), row(P['conv_ln_b']), B, T, cbt)
    x4 = _mm_resid("conv_out", [dict(A=s, Ka=D, B=Wb['conv_w_out'][0])], x3, bias=row(P['conv_b_out']))
    x6, S1 = xa_ffn_fwd(1, x4)
    dx, loss_t, dgf = _head(x6, tgt2, row(P['final_norm']))
    G['final_norm'] = dgf.reshape(D)

    def by_rows(dw):
        return dw.reshape(NCHIP, dw.shape[1] // NCHIP, dw.shape[2])

    def xa_ffn_bwd(l, S, dx):
        wgu, wdown = Wb['ffn_w_gu'][l], Wb['ffn_w_down'][l]
        dwdown = by_rows(_mm_tn(f"dw_down{l}", S['a'], dx))
        dg, du = _fused_mm(f"ffn_dgu{l}", dims='nt', M=N, N=FF, bm=min(512, N), bn=FF // 2,
                           groups=[[dict(A=dx, Ka=D, B=wdown)]], epi=_epi_swiglu_bwd, outs=[BF, BF],
                           tiles=[(S['g'], 0), (S['u'], 0)])
        dwgu = jnp.concatenate([_mm_tn(f"dw_g{l}", S['hf'], dg, parts=2), _mm_tn(f"dw_u{l}", S['hf'], du, parts=2)])
        dx, dgn = _mm_nt_rms_bwd(f"ffn_dx{l}", [dict(A=dg, Ka=FF, B=wgu, coff=0), dict(A=du, Ka=FF, B=wgu, coff=1)],
                                 S['xm'], row(P['ffn_norm'][l]), dx)
        dwo = by_rows(_mm_tn(f"dw_o{l}", S['o'], dx))
        do = _mm_nt_plain(f"xa_do{l}", dx, Wb['xa_wo'][l])
        dq, dkv = _xattn_bwd(f"xattn_bwd{l}", S['qx'], S['kv'], do, B, T, bt)
        dwq = by_rows(_mm_tn(f"dw_q{l}", S['hq'], dq))
        dwkv = _mm_tn(f"dw_kv{l}", S['hm'], dkv, parts=NCHIP)
        dmn = _fused_mm(f"xa_dmem{l}", dims='nt', M=B * NMEM, N=D, bm=min(256, B * NMEM), bn=D,
                        groups=[[dict(A=dkv, Ka=2 * D, B=Wb['xa_wkv'][l])]], epi=_epi_rms_gain_only, outs=[],
                        tiles=[(mem2, 0)], rows=[(row(P['mem_norm'][l]), 0)], reds=[(1, D)])[0]
        dx, dxn = _mm_nt_rms_bwd(f"xa_dx{l}", [dict(A=dq, Ka=D, B=Wb['xa_wq'][l])], S['xin'],
                                 row(P['xa_norm'][l]), dx)
        return dx, dict(ffn_w_down=dwdown, ffn_w_gu=dwgu, ffn_norm=dgn.reshape(D), xa_wo=dwo, xa_wq=dwq,
                        xa_wkv=dwkv, mem_norm=dmn.reshape(D), xa_norm=dxn.reshape(D))

    dx, G1 = xa_ffn_bwd(1, S1, dx)
    G['conv_w_out'] = [by_rows(_mm_tn("dw_cout", s, dx))]
    G['conv_b_out'] = _colsum("db_cout", dx)
    dsv = _mm_nt_plain("conv_ds", dx, Wb['conv_w_out'][0])
    dpa, dpg, dw32, dwb, dlng, dlnb, dba, dbg = _conv_bwd(dsv, yc, y, pa, pg, w32, row(P['conv_ln_g']),
                                                          row(P['conv_ln_b']), B, T, cbt)
    G['conv_dw_w'] = _sum_groups("conv_dw_sum", dw32)[:CK][None]
    G['conv_dw_b'], G['conv_ln_g'], G['conv_ln_b'] = dwb, dlng, dlnb
    G['conv_b_in'] = jnp.concatenate([dba, dbg], axis=1)
    G['conv_w_in'] = [jnp.concatenate([_mm_tn("dw_cin_a", hc, dpa, parts=2), _mm_tn("dw_cin_g", hc, dpg, parts=2)])]
    dx, dgo = _mm_nt_rms_bwd("conv_dx", [dict(A=dpa, Ka=D, B=w_cin, coff=0), dict(A=dpg, Ka=D, B=w_cin, coff=1)],
                             x3, row(P['mix_norm_o']), dx)
    G['mix_norm_o'] = dgo
    dx, G0 = xa_ffn_bwd(0, S0, dx)
    for k in G0:
        G[k] = [G0[k], G1[k]]
    G['w_out_e'] = [by_rows(jnp.concatenate([_mm_tn("dw_out_b", b_out, dx), _mm_tn("dw_out_a", a_out, dx)], axis=1))]
    dcat = _mm_nt_plain("mix_dcat", dx, w_out)
    (dq, dk, dv, dcum4, dcq4), arrived = _fox_bwd(qkv, cum4, b_out, lse, dcat, B, T, 0, bq, bk,
                                                  carry=hooks['bwd_carry'](G) if hooks else None)
    if hooks:
        hooks['bwd_done'](arrived)
    dz, dlg, dlb, dws, dbf = _gmlp_bwd(z, dcat, lng, lnb, ws, bfull, bt)
    G['gmlp_ln_g'], G['gmlp_ln_b'], G['gmlp_w_s'] = dlg, dlb, dws[None]
    G['gmlp_b_s'] = _group_sum("gmlp_db", dbf)[:, :NG].T[None]
    pad = jnp.zeros((N, 128 - NG), F32)
    dck = jnp.concatenate([dcum4.reshape(B, NG, T).transpose(0, 2, 1).reshape(N, NG), pad], axis=1)
    dcq = jnp.concatenate([dcq4.reshape(N, NG // 2, 128)[:, :, :2].reshape(N, NG), pad], axis=1)
    dfl, dfb = _fox_gate_bwd(fl, fbias, dcq, dck, B, T)
    G['fox_f_bias'] = dfb[:, :NG]
    dw_in = jnp.concatenate([_mm_tn("dw_in_q", h0, dq)[0], _mm_tn("dw_in_k", h0, dk)[0], _mm_tn("dw_in_v", h0, dv)[0],
                             _mm_tn("dw_in_f", h0, dfl)[0][:, :NG], _mm_tn("dw_in_z", h0, dz)[0]], axis=1)
    G['w_in_e'] = [dw_in.reshape(D, NCHIP, IN_W // NCHIP).transpose(1, 0, 2)]
    dx, dge = _mm_nt_rms_bwd("mix_dx", [dict(A=dz, Ka=D, B=w_inp, coff=0), dict(A=dq, Ka=FOXW, B=w_inp, coff=2),
                                        dict(A=dk, Ka=FOXW, B=w_inp, coff=3), dict(A=dv, Ka=FOXW, B=w_inp, coff=4),
                                        dict(A=dfl, Ka=128, B=w_inp, coff=20)], x0, g_e, dx)
    G['mix_norm_e'] = dge
    return loss_t[0, 0], dx.reshape(B, T, D), G


COLS = 1024
ANY = pl.BlockSpec(memory_space=pl.ANY)


def _coords():
    return lax.axis_index("x"), lax.axis_index("y"), lax.axis_index("c")


def _other_chips(x, y):
    return [(1 - x, y), (x, 1 - y), (1 - x, 1 - y)]


def _own_slot(v, me):
    return lax.dynamic_update_slice(lax.empty((NCHIP,) + v.shape, v.dtype), v[None], (me,) + (0,) * v.ndim)


def _all_gather(name, shards, me):
    n = len(shards)
    bufs = [_own_slot(v, me) for v in shards]

    def body(*refs):
        out_refs, (send_sems, recv_sems) = refs[n:2 * n], refs[2 * n:]
        x, y, c = _coords()
        mine = 2 * x + y
        sib = (x, y, 1 - c)
        chips = _other_chips(x, y)

        def rcopy(a, k, chip_idx, half, to):
            blk = out_refs[a].at[chip_idx, half]
            return pltpu.make_async_remote_copy(src_ref=blk, dst_ref=blk, send_sem=send_sems.at[6 * a + k],
                                                recv_sem=recv_sems.at[6 * a + k], device_id=to, device_id_type=MESH)

        first = [rcopy(a, j, mine, c, (cx, cy, c)) for a in range(n) for j, (cx, cy) in enumerate(chips)]
        for cp in first:
            cp.start()
        passed = []
        for a in range(n):
            for j, (cx, cy) in enumerate(chips):
                kj = 2 * cx + cy
                rcopy(a, j, kj, c, sib).wait_recv()
                fwd = rcopy(a, 3 + j, kj, c, sib)
                fwd.start()
                passed.append(fwd)
        for a in range(n):
            for j, (cx, cy) in enumerate(chips):
                rcopy(a, 3 + j, 2 * cx + cy, 1 - c, sib).wait_recv()
        for cp in first + passed:
            cp.wait_send()

    return _pcall(body, name=name, in_specs=[ANY] * n, out_specs=[ANY] * n,
                  out_shape=[jax.ShapeDtypeStruct(b.shape, b.dtype) for b in bufs],
                  input_output_aliases={a: a for a in range(n)}, scratch_shapes=_sem_pairs(6 * n))(*bufs)


def _sem_pairs(n):
    return [pltpu.SemaphoreType.DMA((n,)), pltpu.SemaphoreType.DMA((n,))]


def _gather_forward(name, bufs):
    n = len(bufs)

    def body(*refs):
        out_refs, (send_sems, recv_sems) = refs[n:2 * n], refs[2 * n:]
        x, y, c = _coords()

        def cp(a, j, kj, half):
            blk = out_refs[a].at[kj, half]
            return pltpu.make_async_remote_copy(src_ref=blk, dst_ref=blk, send_sem=send_sems.at[3 * a + j],
                                                recv_sem=recv_sems.at[3 * a + j], device_id=(x, y, 1 - c),
                                                device_id_type=MESH)

        chips = [2 * cx + cy for cx, cy in _other_chips(x, y)]
        sends = [cp(a, j, kj, c) for a in range(n) for j, kj in enumerate(chips)]
        for s in sends:
            s.start()
        for a in range(n):
            for j, kj in enumerate(chips):
                cp(a, j, kj, 1 - c).wait_recv()
        for s in sends:
            s.wait_send()

    return _pcall(body, name=name, in_specs=[ANY] * n, out_specs=[ANY] * n,
                  out_shape=[jax.ShapeDtypeStruct(b.shape, b.dtype) for b in bufs],
                  input_output_aliases={a: a for a in range(n)}, scratch_shapes=_sem_pairs(3 * n))(*bufs)


def _sibling_halves(name, ps):
    n = len(ps)

    def body(*refs):
        p_refs, out_refs, (send_sems, recv_sems) = refs[:n], refs[n:2 * n], refs[2 * n:]
        x, y, c = _coords()
        cps = [pltpu.make_async_remote_copy(src_ref=p_refs[a].at[k, 1 - c], dst_ref=out_refs[a].at[k],
                                            send_sem=send_sems.at[4 * a + k], recv_sem=recv_sems.at[4 * a + k],
                                            device_id=(x, y, 1 - c), device_id_type=MESH)
               for a in range(n) for k in range(NCHIP)]
        for cp in cps:
            cp.start()
        for cp in cps:
            cp.wait()

    return _pcall(body, name=name, in_specs=[ANY] * n, out_specs=[ANY] * n,
                  out_shape=[jax.ShapeDtypeStruct((NCHIP,) + p.shape[2:], p.dtype) for p in ps],
                  scratch_shapes=_sem_pairs(NCHIP * n))(*ps)


def _chip_exchange(name, qs):
    n = len(qs)

    def body(*refs):
        q_refs, out_refs, (send_sems, recv_sems) = refs[:n], refs[n:2 * n], refs[2 * n:]
        x, y, c = _coords()
        cps = [pltpu.make_async_remote_copy(src_ref=q_refs[a].at[2 * cx + cy], dst_ref=out_refs[a].at[j],
                                            send_sem=send_sems.at[3 * a + j], recv_sem=recv_sems.at[3 * a + j],
                                            device_id=(cx, cy, c), device_id_type=MESH)
               for a in range(n) for j, (cx, cy) in enumerate(_other_chips(x, y))]
        for cp in cps:
            cp.start()
        for cp in cps:
            cp.wait()

    return _pcall(body, name=name, in_specs=[ANY] * n, out_specs=[ANY] * n,
                  out_shape=[jax.ShapeDtypeStruct((3,) + q.shape[1:], q.dtype) for q in qs],
                  scratch_shapes=_sem_pairs(3 * n))(*qs)


def _sibling_swap(name, hs):
    n = len(hs)

    def body(*refs):
        out_refs, (send_sems, recv_sems) = refs[n:2 * n], refs[2 * n:]
        x, y, c = _coords()
        sib = (x, y, 1 - c)
        sends = [pltpu.make_async_remote_copy(src_ref=out_refs[a].at[c], dst_ref=out_refs[a].at[c],
                                              send_sem=send_sems.at[a], recv_sem=recv_sems.at[a], device_id=sib,
                                              device_id_type=MESH) for a in range(n)]
        for cp in sends:
            cp.start()
        for a in range(n):
            theirs = out_refs[a].at[1 - c]
            pltpu.make_async_remote_copy(src_ref=theirs, dst_ref=theirs, send_sem=send_sems.at[a],
                                         recv_sem=recv_sems.at[a], device_id=sib, device_id_type=MESH).wait_recv()
        for cp in sends:
            cp.wait_send()

    return _pcall(body, name=name, in_specs=[ANY] * n, out_specs=[ANY] * n,
                  out_shape=[jax.ShapeDtypeStruct(h.shape, h.dtype) for h in hs],
                  input_output_aliases={a: a for a in range(n)}, scratch_shapes=_sem_pairs(n))(*hs)


ADD_BLOCK_BYTES = 2 * 1024 * 1024


def _row_block(R, C):
    if R * C * 4 <= ADD_BLOCK_BYTES:
        return R
    for br in (512, 256, 128, 64, 32, 16, 8):
        if R % br == 0 and br * C * 4 <= ADD_BLOCK_BYTES:
            return br
    return R


def _add_own_half(name, p, recv, c_arr, out_dtype):
    _, _, R, C = p.shape
    br = _row_block(R, C)

    def body(c_ref, p_ref, r_ref, o_ref):
        o_ref[...] = (p_ref[...].astype(F32) + r_ref[...].astype(F32)).astype(o_ref.dtype)

    spec = pltpu.PrefetchScalarGridSpec(
        num_scalar_prefetch=1, grid=(NCHIP, R // br),
        in_specs=[pl.BlockSpec((None, None, br, C), lambda k, r, c_ref: (k, c_ref[0], r, 0)),
                  pl.BlockSpec((None, br, C), lambda k, r, c_ref: (k, r, 0))],
        out_specs=pl.BlockSpec((None, br, C), lambda k, r, c_ref: (k, r, 0)))
    return _pcall(body, name=name, grid_spec=spec, out_shape=jax.ShapeDtypeStruct((NCHIP, R, C), out_dtype),
                  compiler_params=_params(("arbitrary", "arbitrary")))(c_arr, p, recv)


def _add_chips(name, q, recv, idx_arr):
    _, R, C = q.shape
    br = _row_block(R, C)

    def body(idx_ref, q_ref, r_ref, o_ref):
        o_ref[...] = ((q_ref[...].astype(F32) + r_ref[0].astype(F32)) + r_ref[1].astype(F32)) + r_ref[2].astype(F32)

    spec = pltpu.PrefetchScalarGridSpec(
        num_scalar_prefetch=1, grid=(R // br,),
        in_specs=[pl.BlockSpec((None, br, C), lambda r, idx: (idx[0], r, 0)),
                  pl.BlockSpec((3, br, C), lambda r, idx: (0, r, 0))],
        out_specs=pl.BlockSpec((None, br, C), lambda r, idx: (idx[1], r, 0)))
    return _pcall(body, name=name, grid_spec=spec, out_shape=jax.ShapeDtypeStruct((2, R, C), F32),
                  compiler_params=_params(("arbitrary",)))(idx_arr, q, recv)


EARLY = ['w_in_e', 'w_out_e']
LATE = [n for n in BIG if n not in EARLY]


def _adamw(name, w, g, m, v):
    shape = w.shape
    cols = shape[-1]
    rows = w.size // cols
    w2, g2, m2, v2 = (a.reshape(rows, cols) for a in (w, g, m, v))
    bt = next((b for b in (256, 128) if rows % b == 0), rows)

    def body(w_ref, g_ref, m_ref, v_ref, d_ref, nm_ref, nv_ref):
        gv = g_ref[...]
        nm = ADAM_B1 * m_ref[...] + (1.0 - ADAM_B1) * gv
        nv = ADAM_B2 * v_ref[...] + (1.0 - ADAM_B2) * (gv * gv)
        m_hat = nm / (1.0 - ADAM_B1 ** ADAM_STEP)
        v_hat = nv / (1.0 - ADAM_B2 ** ADAM_STEP)
        d_ref[...] = -ADAM_LR * (m_hat / (jnp.sqrt(v_hat) + ADAM_EPS) + ADAM_WD * w_ref[...])
        nm_ref[...] = nm
        nv_ref[...] = nv

    blk = pl.BlockSpec((bt, cols), lambda i: (i, 0))
    outs = _pcall(body, name=name, grid=(rows // bt,), in_specs=[blk] * 4, out_specs=[blk] * 3,
                  out_shape=[jax.ShapeDtypeStruct((rows, cols), F32)] * 3, compiler_params=_params(("arbitrary",)))(
        w2, g2, m2, v2)
    return [o.reshape(shape) for o in outs]


SMALL_SHARDED = ['mix_norm_o', 'conv_b_in', 'conv_dw_w', 'conv_dw_b', 'conv_ln_g', 'conv_ln_b', 'conv_b_out']
REPLICATED = [n for n in WEIGHTS if SHARD_AXIS[n] is None]
NCHIP = 4


def _halves(flat, tile_rows):
    unit = 2 * tile_rows * COLS
    total = -(-flat.size // unit) * unit
    return jnp.pad(flat, (0, total - flat.size)).reshape(2, total // (2 * COLS), COLS)


def _flat(arrays):
    return jnp.concatenate([a.reshape(-1) for a in arrays])


def _chip_block(a, axis, k):
    n = a.shape[axis] // NCHIP
    return lax.slice_in_dim(a, k * n, (k + 1) * n, axis=axis)


def _full_shape(n, shard_shape):
    s = list(shard_shape[n])
    s[SHARD_AXIS[n]] *= NCHIP
    return tuple(s)


def _unpack(flat, names, shapes):
    out, off = {}, 0
    for n in names:
        size = math.prod(shapes[n])
        out[n] = flat[off:off + size].reshape(shapes[n])
        off += size
    return out


def kernel(x, mem, mix_norm_e, w_in_e, fox_f_bias, gmlp_ln_g, gmlp_ln_b, gmlp_w_s, gmlp_b_s, w_out_e, mix_norm_o, conv_w_in, conv_b_in, conv_dw_w, conv_dw_b, conv_ln_g, conv_ln_b, conv_w_out, conv_b_out, xa_norm, mem_norm, xa_wq, xa_wkv, xa_wo, ffn_norm, ffn_w_gu, ffn_w_down, final_norm, loss_target, m_mix_norm_e, m_w_in_e, m_fox_f_bias, m_gmlp_ln_g, m_gmlp_ln_b, m_gmlp_w_s, m_gmlp_b_s, m_w_out_e, m_mix_norm_o, m_conv_w_in, m_conv_b_in, m_conv_dw_w, m_conv_dw_b, m_conv_ln_g, m_conv_ln_b, m_conv_w_out, m_conv_b_out, m_xa_norm, m_mem_norm, m_xa_wq, m_xa_wkv, m_xa_wo, m_ffn_norm, m_ffn_w_gu, m_ffn_w_down, m_final_norm, v_mix_norm_e, v_w_in_e, v_fox_f_bias, v_gmlp_ln_g, v_gmlp_ln_b, v_gmlp_w_s, v_gmlp_b_s, v_w_out_e, v_mix_norm_o, v_conv_w_in, v_conv_b_in, v_conv_dw_w, v_conv_dw_b, v_conv_ln_g, v_conv_ln_b, v_conv_w_out, v_conv_b_out, v_xa_norm, v_mem_norm, v_xa_wq, v_xa_wkv, v_xa_wo, v_ffn_norm, v_ffn_w_gu, v_ffn_w_down, v_final_norm):
    env = locals()
    w = {n: env[n] for n in WEIGHTS}
    m = {n: env["m_" + n] for n in WEIGHTS}
    v = {n: env["v_" + n] for n in WEIGHTS}
    shard_shape = {n: w[n].shape for n in WEIGHTS}
    xi, yi, ci = _coords()
    me = 2 * xi + yi
    c_arr = jnp.reshape(ci, (1,)).astype(jnp.int32)
    idx_arr = jnp.stack([me, ci]).astype(jnp.int32)

    def two_halves(a):
        return a.reshape(2, a.shape[0] // 2, a.shape[1])

    def shards(names):
        return [two_halves(w[n][l].astype(BF)) for n in names for l in range(w[n].shape[0])]

    def full_matrices(names, gathered):
        out, at = {}, 0
        for n in names:
            out[n] = []
            for l in range(w[n].shape[0]):
                rows, cols = w[n].shape[1:]
                g = gathered[at].reshape(NCHIP, rows, cols)
                at += 1
                out[n].append(g.reshape(NCHIP * rows, cols) if SHARD_AXIS[n] == 1
                              else g.transpose(1, 0, 2).reshape(rows, NCHIP * cols))
        return out

    Wb = full_matrices(EARLY, _all_gather("gather_mixer", shards(EARLY), me))
    vec = _all_gather("gather_vectors", [_halves(_flat([w[n] for n in SMALL_SHARDED]), 8)], me)[0].reshape(NCHIP, -1)
    parts = [_unpack(vec[k], SMALL_SHARDED, shard_shape) for k in range(NCHIP)]
    P = {n: jnp.concatenate([parts[k][n] for k in range(NCHIP)], axis=SHARD_AXIS[n]) for n in SMALL_SHARDED}
    P.update({n: w[n] for n in REPLICATED})
    late_bufs = [_own_slot(v_, me) for v_ in shards(LATE)]
    state = {}

    def by_halves(G, names):
        return [g.reshape(NCHIP, 2, g.shape[1] // 2, g.shape[2]) for n in names for g in G[n]]

    def bwd_carry(G):
        ps = by_halves(G, LATE + ['w_out_e'])
        got = _sibling_halves("rs_sibling_halves_late", ps)
        state['qs'] = [_add_own_half(f"rs_add_pair_late{a}", p, g, c_arr, BF) for a, (p, g) in enumerate(zip(ps, got))]
        return _carry_exchange(state['qs'])

    hooks = dict(fwd_carry=lambda: _carry_gather(late_bufs),
                 fwd_done=lambda arrived: full_matrices(LATE, _gather_forward("gather_forward", arrived)),
                 bwd_carry=bwd_carry, bwd_done=lambda arrived: state.update(got=arrived))
    loss_part, grad_x, G = _local_step(x, mem, loss_target, Wb, P, hooks)
    loss = lax.psum(loss_part, ("x", "y", "c"))

    def layers(n):
        return G[n] if isinstance(G[n], list) else ([G[n]] if G[n].ndim == 1 else [G[n][l] for l in range(G[n].shape[0])])

    rep = _flat([a for n in REPLICATED for a in layers(n)])
    quarter = -(-rep.size // (NCHIP * 2 * 8 * COLS)) * (2 * 8 * COLS)
    rep = jnp.pad(rep, (0, NCHIP * quarter - rep.size)).reshape(NCHIP, quarter)
    segs = [[_chip_block(a, SHARD_AXIS[n] - 1, k).reshape(-1) for n in SMALL_SHARDED for a in layers(n)] + [rep[k]]
            for k in range(NCHIP)]
    size = sum(piece.size for piece in segs[0])
    total = -(-size // (2 * 8 * COLS)) * (2 * 8 * COLS)
    p_small = jnp.concatenate([piece for seg in segs for piece in seg + [jnp.zeros((total - size,), F32)]])
    p_small = p_small.reshape(NCHIP, 2, total // (2 * COLS), COLS)
    hs = [_add_chips(f"rs_add_chips_late{a}", q, g, idx_arr) for a, (q, g) in enumerate(zip(state['qs'], state['got']))]
    ps = by_halves(G, ['w_in_e']) + [p_small]
    got = _sibling_halves("rs_sibling_halves_last", ps)
    qs = [_add_own_half(f"rs_add_pair_last{a}", p, g, c_arr, p.dtype) for a, (p, g) in enumerate(zip(ps, got))]
    got = _chip_exchange("rs_chip_exchange_last", qs)
    hs += [_add_chips(f"rs_add_chips_last{a}", q, g, idx_arr) for a, (q, g) in enumerate(zip(qs, got))]
    red = _sibling_swap("rs_sibling_swap", hs)
    mine, at = {}, 0
    for n in LATE + ['w_out_e', 'w_in_e']:
        nl = shard_shape[n][0]
        mine[n] = jnp.stack([r.reshape(shard_shape[n][1:]) for r in red[at:at + nl]])
        at += nl
    red_small = red[-1].reshape(-1)
    mine.update(_unpack(red_small, SMALL_SHARDED, shard_shape))
    off = sum(math.prod(shard_shape[n]) for n in SMALL_SHARDED)
    rep_all = _all_gather("gather_replicated_grads",
                          [red_small[off:off + quarter].reshape(2, quarter // (2 * COLS), COLS)], me)[0]
    mine.update(_unpack(rep_all.reshape(-1), REPLICATED, shard_shape))

    grads, deltas, new_m, new_v = [], [], [], []
    for n in WEIGHTS:
        d, nm, nv = _adamw("adamw_" + n, w[n], mine[n], m[n], v[n])
        grads.append(mine[n])
        deltas.append(d)
        new_m.append(nm)
        new_v.append(nv)
    return (loss, grad_x, *grads, *deltas, *new_m, *new_v)
```

```python
import functools
import math

import jax
import jax.numpy as jnp
from jax import lax
from jax.experimental import pallas as pl
from jax.experimental.pallas import tpu as pltpu

F32 = jnp.float32
BF = jnp.bfloat16
MESH = pl.DeviceIdType.MESH

D = 1024
FOXW = 512
HD = 64
GW = 512
CH = 128
NG = 8
FF = 2816
NMEM = 256
XH = 4
XD = 256
CK = 31
HALO = 32
EPS = 1e-6
IN_W = 2568
IN_WP = 2688
VMEM_LIMIT = 56 * 1024 * 1024

ADAM_LR, ADAM_B1, ADAM_B2, ADAM_EPS, ADAM_WD, ADAM_STEP = 0.001, 0.9, 0.999, 1e-08, 0.01, 10

WEIGHTS = ['mix_norm_e', 'w_in_e', 'fox_f_bias', 'gmlp_ln_g', 'gmlp_ln_b', 'gmlp_w_s', 'gmlp_b_s', 'w_out_e',
           'mix_norm_o', 'conv_w_in', 'conv_b_in', 'conv_dw_w', 'conv_dw_b', 'conv_ln_g', 'conv_ln_b',
           'conv_w_out', 'conv_b_out', 'xa_norm', 'mem_norm', 'xa_wq', 'xa_wkv', 'xa_wo', 'ffn_norm',
           'ffn_w_gu', 'ffn_w_down', 'final_norm']
SHARD_AXIS = {'mix_norm_e': None, 'w_in_e': 2, 'fox_f_bias': None, 'gmlp_ln_g': None, 'gmlp_ln_b': None,
              'gmlp_w_s': None, 'gmlp_b_s': None, 'w_out_e': 1, 'mix_norm_o': 1, 'conv_w_in': 2, 'conv_b_in': 1,
              'conv_dw_w': 2, 'conv_dw_b': 1, 'conv_ln_g': 1, 'conv_ln_b': 1, 'conv_w_out': 1, 'conv_b_out': 1,
              'xa_norm': None, 'mem_norm': None, 'xa_wq': 1, 'xa_wkv': 2, 'xa_wo': 1, 'ffn_norm': None,
              'ffn_w_gu': 2, 'ffn_w_down': 1, 'final_norm': None}
BIG = ['w_in_e', 'w_out_e', 'conv_w_in', 'conv_w_out', 'xa_wq', 'xa_wkv', 'xa_wo', 'ffn_w_gu', 'ffn_w_down']


def _pcall(body, **kw):
    return pl.pallas_call(body, **kw)


def _params(sem=None, **kw):
    return pltpu.CompilerParams(dimension_semantics=sem, vmem_limit_bytes=VMEM_LIMIT, **kw)


def _dot(a, b, dims):
    dn = {'nn': (((1,), (0,)), ((), ())), 'nt': (((1,), (1,)), ((), ())), 'tn': (((0,), (0,)), ((), ()))}[dims]
    return lax.dot_general(a, b, dn, preferred_element_type=F32)


def _sigmoid(x):
    return 1.0 / (1.0 + jnp.exp(-x))


def _rms_stats(xv):
    return lax.rsqrt(jnp.mean(xv * xv, axis=-1, keepdims=True) + EPS)


def _rms_bwd(xv, gain, dh):
    r = _rms_stats(xv)
    t = dh * gain
    dx = r * t - xv * (r * r * r * jnp.mean(t * xv, axis=-1, keepdims=True))
    return dx, dh * xv * r


def _fused_mm(name, *, dims, M, N, bm, bn, groups, epi, outs, x=None, gain=None, tiles=(), rows=(),
              h_out=False, reds=()):
    bm = min(bm, M)
    nI, nJ = M // bm, N // bn
    assert nI * bm == M and nJ * bn == N
    assert not reds or nJ == 1
    arrays, specs = [], []

    def add(arr, spec):
        arrays.append(arr)
        specs.append(spec)
        return len(arrays) - 1

    if x is not None:
        K0 = x.shape[1]
        add(x, pl.BlockSpec((bm, K0), lambda i, j: (i, 0)))
        add(gain, pl.BlockSpec((1, K0), lambda i, j: (0, 0)))
    plan = []
    for grp in groups:
        g = []
        for p in grp:
            ai = None
            if p['A'] is not None:
                ai = add(p['A'], pl.BlockSpec((bm, p['Ka']), lambda i, j, o=p.get('acoff', 0): (i, o)))
            ro, co = p.get('roff', 0), p.get('coff', 0)
            if dims == 'nn':
                bi = add(p['B'], pl.BlockSpec((p['Ka'], bn), lambda i, j, ro=ro, co=co: (ro, j + co)))
            else:
                bi = add(p['B'], pl.BlockSpec((bn, p['Ka']), lambda i, j, ro=ro, co=co: (j + ro, co)))
            g.append((ai, bi))
        plan.append(g)
    tile_idx = [add(a, pl.BlockSpec((bm, bn), lambda i, j, o=o: (i, j + o))) for a, o in tiles]
    row_idx = [add(a, pl.BlockSpec((1, bn), lambda i, j, o=o: (0, j + o))) for a, o in rows]
    n_in = len(arrays)

    out_shape = [jax.ShapeDtypeStruct((M, N), dt) for dt in outs]
    out_specs = [pl.BlockSpec((bm, bn), lambda i, j: (i, j)) for _ in outs]
    if h_out:
        out_shape.append(jax.ShapeDtypeStruct((M, x.shape[1]), BF))
        out_specs.append(pl.BlockSpec((bm, x.shape[1]), lambda i, j: (i, 0)))
    for shp in reds:
        out_shape.append(jax.ShapeDtypeStruct(shp, F32))
        out_specs.append(pl.BlockSpec(shp, lambda i, j: (0, 0)))
    n_main = len(outs)
    scratch = [pltpu.VMEM((bm, x.shape[1]), BF)] if x is not None else []

    def body(*refs):
        ins, out_refs, scr = refs[:n_in], refs[n_in:n_in + len(out_shape)], refs[n_in + len(out_shape):]
        i, j = pl.program_id(0), pl.program_id(1)
        if x is not None:
            hn_ref = scr[0]

            @pl.when(j == 0)
            def _():
                xv = ins[0][...]
                hn = (xv * _rms_stats(xv) * ins[1][...]).astype(BF)
                hn_ref[...] = hn
                if h_out:
                    out_refs[n_main][...] = hn

        accs = []
        for g in plan:
            acc = None
            for ai, bi in g:
                a = hn_ref[...] if ai is None else ins[ai][...]
                if a.dtype != BF:
                    a = a.astype(BF)
                d = _dot(a, ins[bi][...], dims)
                acc = d if acc is None else acc + d
            accs.append(acc)
        out_vals, red_vals = epi(accs, [ins[t][...] for t in tile_idx], [ins[r][...] for r in row_idx])
        for r, v in zip(out_refs[:n_main], out_vals):
            r[...] = v.astype(r.dtype)
        if reds:
            red_refs = out_refs[n_main + (1 if h_out else 0):]

            @pl.when(i == 0)
            def _():
                for r in red_refs:
                    r[...] = jnp.zeros(r.shape, F32)

            for r, v in zip(red_refs, red_vals):
                r[...] += v

    res = _pcall(body, name=name, grid=(nI, nJ), in_specs=specs, out_specs=out_specs, out_shape=out_shape,
                 scratch_shapes=scratch, compiler_params=_params(("arbitrary", "arbitrary")))(*arrays)
    return res


def _epi_plain(accs, tiles, rows):
    return [accs[0]], []


def _epi_resid(accs, tiles, rows):
    y = tiles[0] + accs[0]
    if rows:
        y = y + rows[0]
    return [y], []


def _epi_swiglu(accs, tiles, rows):
    g, u = accs
    return [g, u, g * _sigmoid(g) * u], []


def _epi_glu(accs, tiles, rows):
    a, g = accs[0] + rows[0], accs[1] + rows[1]
    return [a, g, a * _sigmoid(g)], []


def _epi_swiglu_bwd(accs, tiles, rows):
    da = accs[0]
    g, u = tiles[0].astype(F32), tiles[1].astype(F32)
    sg = _sigmoid(g)
    return [da * u * (sg * (1.0 + g * (1.0 - sg))), da * (g * sg)], []


def _epi_rms_bwd(accs, tiles, rows):
    dx, dgr = _rms_bwd(tiles[0], rows[0], accs[0])
    return [tiles[1] + dx], [jnp.sum(dgr, axis=0, keepdims=True)]


def _epi_rms_gain_only(accs, tiles, rows):
    _, dgr = _rms_bwd(tiles[0], rows[0], accs[0])
    return [], [jnp.sum(dgr, axis=0, keepdims=True)]


def _norm_mm(name, x, gain, W, *, N, coff=0, bn, out_dtype, bm=512, h_out=False):
    return _fused_mm(name, dims='nn', M=x.shape[0], N=N, bm=bm, bn=bn, x=x, gain=gain,
                     groups=[[dict(A=None, Ka=x.shape[1], B=W, coff=coff)]], epi=_epi_plain, outs=[out_dtype],
                     h_out=h_out)


def _mm_resid(name, pairs, resid, bias=None, bm=512):
    M = resid.shape[0]
    return _fused_mm(name, dims='nn', M=M, N=D, bm=bm, bn=D, groups=[pairs], epi=_epi_resid, outs=[F32],
                     tiles=[(resid, 0)], rows=[(bias, 0)] if bias is not None else [])[0]


def _mm_nt_plain(name, dy, W, bm=512):
    return _fused_mm(name, dims='nt', M=dy.shape[0], N=W.shape[0], bm=bm, bn=W.shape[0],
                     groups=[[dict(A=dy, Ka=dy.shape[1], B=W)]], epi=_epi_plain, outs=[BF])[0]


def _mm_nt_rms_bwd(name, pairs, x, gain, dx_in, bm=256):
    out = _fused_mm(name, dims='nt', M=x.shape[0], N=D, bm=bm, bn=D, groups=[pairs], epi=_epi_rms_bwd,
                    outs=[F32], tiles=[(x, 0), (dx_in, 0)], rows=[(gain, 0)], reds=[(1, D)])
    return out[0], out[1]


def _mm_tn(name, A, G, bk=1024, parts=1):
    T, Ka, Kg = A.shape[0], A.shape[1], G.shape[1]
    w = Kg // parts
    bm = Ka if Ka <= 1024 else Ka // 2
    bn = w if w <= 1408 else w // 2
    bk = min(bk, T)
    per = w // bn
    nI, nJ, nK = Ka // bm, Kg // bn, T // bk

    def body(a_ref, g_ref, o_ref, acc):
        k = pl.program_id(2)

        @pl.when(k == 0)
        def _():
            acc[...] = jnp.zeros(acc.shape, F32)

        acc[...] += _dot(a_ref[...].astype(BF), g_ref[...].astype(BF), 'tn')

        @pl.when(k == nK - 1)
        def _():
            o_ref[...] = acc[...].astype(BF)

    return _pcall(body, name=name, grid=(nI, nJ, nK),
                  in_specs=[pl.BlockSpec((bk, bm), lambda i, j, k: (k, i)),
                            pl.BlockSpec((bk, bn), lambda i, j, k: (k, j))],
                  out_specs=pl.BlockSpec((None, bm, bn), lambda i, j, k: (j // per, i, j % per)),
                  out_shape=jax.ShapeDtypeStruct((parts, Ka, w), BF),
                  scratch_shapes=[pltpu.VMEM((bm, bn), F32)],
                  compiler_params=_params(("arbitrary", "arbitrary", "arbitrary")))(A, G)


def _colsum(name, a, bt=512):
    M, N = a.shape
    bt = min(bt, M)

    def body(a_ref, o_ref):
        @pl.when(pl.program_id(0) == 0)
        def _():
            o_ref[...] = jnp.zeros(o_ref.shape, F32)

        o_ref[...] += jnp.sum(a_ref[...].astype(F32), axis=0, keepdims=True)

    return _pcall(body, name=name, grid=(M // bt,), in_specs=[pl.BlockSpec((bt, N), lambda i: (i, 0))],
                  out_specs=pl.BlockSpec((1, N), lambda i: (0, 0)), out_shape=jax.ShapeDtypeStruct((1, N), F32),
                  compiler_params=_params(("arbitrary",)))(a)


def _cumsum_rows(v):
    T = v.shape[0]
    row = lax.broadcasted_iota(jnp.int32, v.shape, 0)
    s = 1
    while s < T:
        v = v + jnp.where(row >= s, pltpu.roll(v, s, 0), 0.0)
        s *= 2
    return v


def _log_sigmoid(z):
    return jnp.minimum(z, 0.0) - jnp.log(1.0 + jnp.exp(-jnp.abs(z)))


def _fox_gate_fwd(fl, fbias, B, T):
    def body(fl_ref, b_ref, o_ref):
        o_ref[...] = _cumsum_rows(_log_sigmoid(fl_ref[...] + b_ref[...]))

    return _pcall(body, name="fox_gate_fwd", grid=(B,),
                  in_specs=[pl.BlockSpec((T, 128), lambda b: (b, 0)), pl.BlockSpec((1, 128), lambda b: (0, 0))],
                  out_specs=pl.BlockSpec((T, 128), lambda b: (b, 0)),
                  out_shape=jax.ShapeDtypeStruct((B * T, 128), F32), compiler_params=_params(("arbitrary",)))(fl, fbias)


def _fox_gate_bwd(fl, fbias, dcq, dck, B, T):
    def body(fl_ref, b_ref, dcq_ref, dck_ref, dfl_ref, db_ref):
        dc = dcq_ref[...] + dck_ref[...]
        rev = jnp.sum(dc, axis=0, keepdims=True) - _cumsum_rows(dc) + dc
        dfl = rev * _sigmoid(-(fl_ref[...] + b_ref[...]))
        dfl_ref[...] = dfl

        @pl.when(pl.program_id(0) == 0)
        def _():
            db_ref[...] = jnp.zeros(db_ref.shape, F32)

        db_ref[...] += jnp.sum(dfl, axis=0, keepdims=True)

    return _pcall(body, name="fox_gate_bwd", grid=(B,),
                  in_specs=[pl.BlockSpec((T, 128), lambda b: (b, 0)), pl.BlockSpec((1, 128), lambda b: (0, 0)),
                            pl.BlockSpec((T, 128), lambda b: (b, 0)), pl.BlockSpec((T, 128), lambda b: (b, 0))],
                  out_specs=[pl.BlockSpec((T, 128), lambda b: (b, 0)), pl.BlockSpec((1, 128), lambda b: (0, 0))],
                  out_shape=[jax.ShapeDtypeStruct((B * T, 128), F32), jax.ShapeDtypeStruct((1, 128), F32)],
                  compiler_params=_params(("arbitrary",)))(fl, fbias, dcq, dck)


def _carried_call(body, *, name, grid, in_specs, out_specs, out_shape, scratch_shapes, operands, carry):
    if carry is None:
        return _pcall(body, name=name, grid=grid, in_specs=in_specs, out_specs=out_specs, out_shape=out_shape,
                      scratch_shapes=scratch_shapes, compiler_params=_params(("arbitrary",) * len(grid)))(*operands), []
    n, n_in, n_out, n_scr = len(carry['inputs']), len(in_specs), len(out_specs), len(scratch_shapes)

    def wrapped(*refs):
        ins, cin = refs[:n_in], refs[n_in:n_in + n]
        outs, cout = refs[n_in + n:n_in + n + n_out], refs[n_in + n + n_out:n_in + 2 * n + n_out]
        scr = refs[n_in + 2 * n + n_out:]
        send_sems, recv_sems = scr[n_scr:]
        ids = [pl.program_id(d) for d in range(len(grid))]
        first = functools.reduce(jnp.logical_and, [i == 0 for i in ids])
        last = functools.reduce(jnp.logical_and, [i == g - 1 for i, g in zip(ids, grid)])

        @pl.when(first)
        def _():
            for cp in carry['copies'](cin, cout, send_sems, recv_sems):
                cp.start()

        body(*ins, *outs, *scr[:n_scr])

        @pl.when(last)
        def _():
            for cp in carry['copies'](cin, cout, send_sems, recv_sems):
                cp.wait()

    aliases = {n_in + a: n_out + a for a in range(n)} if carry['in_place'] else {}
    res = _pcall(wrapped, name=name, grid=grid, in_specs=list(in_specs) + [ANY] * n,
                 out_specs=list(out_specs) + [ANY] * n, out_shape=list(out_shape) + carry['out_shape'],
                 scratch_shapes=list(scratch_shapes) + _sem_pairs(carry['nsem']), input_output_aliases=aliases,
                 compiler_params=_params(("arbitrary",) * len(grid)))(*operands, *carry['inputs'])
    return res[:n_out], res[n_out:]


def _carry_gather(bufs):
    n = len(bufs)

    def copies(in_refs, out_refs, send_sems, recv_sems):
        x, y, c = _coords()
        cps = []
        for a in range(n):
            blk = out_refs[a].at[2 * x + y, c]
            cps += [pltpu.make_async_remote_copy(src_ref=blk, dst_ref=blk, send_sem=send_sems.at[3 * a + j],
                                                 recv_sem=recv_sems.at[3 * a + j], device_id=(cx, cy, c),
                                                 device_id_type=MESH) for j, (cx, cy) in enumerate(_other_chips(x, y))]
        return cps

    return dict(inputs=bufs, out_shape=[jax.ShapeDtypeStruct(b.shape, b.dtype) for b in bufs], in_place=True,
                nsem=3 * n, copies=copies)


def _carry_exchange(qs):
    n = len(qs)

    def copies(in_refs, out_refs, send_sems, recv_sems):
        x, y, c = _coords()
        return [pltpu.make_async_remote_copy(src_ref=in_refs[a].at[2 * cx + cy], dst_ref=out_refs[a].at[j],
                                             send_sem=send_sems.at[3 * a + j], recv_sem=recv_sems.at[3 * a + j],
                                             device_id=(cx, cy, c), device_id_type=MESH)
                for a in range(n) for j, (cx, cy) in enumerate(_other_chips(x, y))]

    return dict(inputs=qs, out_shape=[jax.ShapeDtypeStruct((3,) + q.shape[1:], q.dtype) for q in qs], in_place=False,
                nsem=3 * n, copies=copies)


NEG = -1e30


def _fox_fwd(qkv, cum4, B, T, qoff, bq, bk, carry=None):
    nq, nkb = T // bq, T // bk
    N = B * T

    def body(q_ref, k_ref, v_ref, cum_ref, o_ref, lse_ref):
        hp, i = pl.program_id(1), pl.program_id(2)
        lane = lax.broadcasted_iota(jnp.int32, (bq, 128), 1)
        heads = [slice(e * HD, (e + 1) * HD) for e in range(2)]
        qs = [q_ref[:, sl] * 0.125 for sl in heads]

        def block(j, carry, diagonal):
            ks = pl.multiple_of(j * bk, bk)
            out = []
            for e, sl in enumerate(heads):
                m, l, acc = carry[e]
                s = _dot(qs[e], k_ref[pl.ds(ks, bk), sl], 'nt') - cum_ref[0, 2 * hp + e, pl.ds(j, 1), :]
                if diagonal:
                    keep = lax.broadcasted_iota(jnp.int32, (bq, bk), 0) >= lax.broadcasted_iota(jnp.int32, (bq, bk), 1)
                    s = jnp.where(keep, s, NEG)
                m_new = jnp.maximum(m, jnp.max(s, axis=1, keepdims=True))
                p = jnp.exp(s - m_new)
                alpha = jnp.exp(m - m_new)
                l = alpha * l + jnp.sum(p, axis=1, keepdims=True)
                acc = alpha * acc + _dot(p.astype(BF), v_ref[pl.ds(ks, bk), sl], 'nn')
                out.append((m_new, l, acc))
            return tuple(out)

        init = tuple((jnp.full((bq, 1), NEG, F32), jnp.zeros((bq, 1), F32), jnp.zeros((bq, HD), F32)) for _ in heads)
        carry = lax.fori_loop(0, i, lambda j, c: block(j, c, False), init)
        carry = block(i, carry, True)
        lse_tile = jnp.zeros((bq, 128), F32)
        for e, sl in enumerate(heads):
            m, l, acc = carry[e]
            o_ref[:, sl] = (acc / l).astype(BF)
            lse_tile = jnp.where(lane == e, m + jnp.log(l), lse_tile)
        lse_ref[...] = lse_tile

    return _carried_call(body, name="fox_fwd", grid=(B, 4, nq),
                         in_specs=[pl.BlockSpec((bq, 128), lambda b, h, i: (b * nq + i, qoff + h)),
                                   pl.BlockSpec((T, 128), lambda b, h, i: (b, qoff + 4 + h)),
                                   pl.BlockSpec((T, 128), lambda b, h, i: (b, qoff + 8 + h)),
                                   pl.BlockSpec((1, NG, nkb, bk), lambda b, h, i: (b, 0, 0, 0))],
                         out_specs=[pl.BlockSpec((bq, 128), lambda b, h, i: (b * nq + i, h)),
                                    pl.BlockSpec((bq, 128), lambda b, h, i: (b * nq + i, h))],
                         out_shape=[jax.ShapeDtypeStruct((N, FOXW), BF), jax.ShapeDtypeStruct((N, FOXW), F32)],
                         scratch_shapes=[], operands=(qkv, qkv, qkv, cum4), carry=carry)


def _fox_bwd(qkv, cum4, o, lse, dcat, B, T, qoff, bq, bk, carry=None):
    nq, nkb = T // bq, T // bk
    N = B * T

    def body(q_ref, k_ref, v_ref, cum_ref, o_ref, lse_ref, do_ref, dq_ref, dk_ref, dv_ref, dcum_ref, dcq_ref,
             dq_acc, dl_ref, rs_ref):
        hp = pl.program_id(1)
        heads = [slice(e * HD, (e + 1) * HD) for e in range(2)]
        keep = lax.broadcasted_iota(jnp.int32, (bq, bk), 0) >= lax.broadcasted_iota(jnp.int32, (bq, bk), 1)
        dcq_ref[...] = jnp.zeros(dcq_ref.shape, F32)
        dq_acc[...] = jnp.zeros(dq_acc.shape, F32)
        rs_ref[...] = jnp.zeros(rs_ref.shape, F32)
        for e, sl in enumerate(heads):
            dl_ref[e] = jnp.sum(do_ref[:, sl].astype(F32) * o_ref[:, sl].astype(F32), axis=1, keepdims=True)
        for j in range(nkb):
            krows = slice(j * bk, (j + 1) * bk)

            def tile(i, carry, diagonal):
                qs = i * bq if diagonal else pl.multiple_of(i * bq, bq)
                out = []
                for e, sl in enumerate(heads):
                    dk_a, dv_a, cs = carry[e]
                    q, k = q_ref[pl.ds(qs, bq), sl], k_ref[krows, sl]
                    do = do_ref[pl.ds(qs, bq), sl]
                    s = _dot(q, k, 'nt') * 0.125 - cum_ref[0, 2 * hp + e, j:j + 1, :]
                    p = jnp.exp(s - lse_ref[pl.ds(qs, bq), e:e + 1])
                    if diagonal:
                        p = jnp.where(keep, p, 0.0)
                    dv_a = dv_a + _dot(p.astype(BF), do, 'tn')
                    ds = p * (_dot(do, v_ref[krows, sl], 'nt') - dl_ref[e, pl.ds(qs, bq), :])
                    cs = cs + jnp.sum(ds, axis=0, keepdims=True)
                    rs_ref[e, pl.ds(qs, bq), :] += jnp.sum(ds, axis=1, keepdims=True)
                    dsb = ds.astype(BF)
                    dk_a = dk_a + _dot(dsb, q, 'tn')
                    dq_acc[e, pl.ds(qs, bq), :] += _dot(dsb, k, 'nn')
                    out.append((dk_a, dv_a, cs))
                return tuple(out)

            init = tuple((jnp.zeros((bk, HD), F32), jnp.zeros((bk, HD), F32), jnp.zeros((1, bk), F32)) for _ in heads)
            carry = lax.fori_loop(j + 1, nq, lambda i, c: tile(i, c, False), tile(j, init, True))
            for e, sl in enumerate(heads):
                dk_a, dv_a, cs = carry[e]
                dk_ref[krows, sl] = (dk_a * 0.125).astype(BF)
                dv_ref[krows, sl] = dv_a.astype(BF)
                dcum_ref[0, e, j:j + 1, :] = -cs
        for e, sl in enumerate(heads):
            dq_ref[:, sl] = (dq_acc[e] * 0.125).astype(BF)
            dcq_ref[:, e:e + 1] = rs_ref[e]

    seq = lambda off: pl.BlockSpec((T, 128), lambda b, h, off=off: (b, off + h))
    return _carried_call(body, name="fox_bwd", grid=(B, 4),
                         in_specs=[seq(qoff), seq(qoff + 4), seq(qoff + 8),
                                   pl.BlockSpec((1, NG, nkb, bk), lambda b, h: (b, 0, 0, 0)),
                                   seq(0), seq(0), seq(0)],
                         out_specs=[seq(0), seq(0), seq(0),
                                    pl.BlockSpec((1, 2, nkb, bk), lambda b, h: (b, h, 0, 0)), seq(0)],
                         out_shape=[jax.ShapeDtypeStruct((N, FOXW), BF)] * 3
                         + [jax.ShapeDtypeStruct((B, NG, nkb, bk), F32), jax.ShapeDtypeStruct((N, FOXW), F32)],
                         scratch_shapes=[pltpu.VMEM((2, T, HD), F32), pltpu.VMEM((2, T, 1), F32),
                                         pltpu.VMEM((2, T, 1), F32)],
                         operands=(qkv, qkv, qkv, cum4, o, lse, dcat), carry=carry)


_GC = math.sqrt(2.0 / math.pi)
_GA = 0.044715


def _gelu(z):
    return 0.5 * z * (1.0 + jnp.tanh(_GC * (z + _GA * z * z * z)))


def _gelu_grad(z):
    t = jnp.tanh(_GC * (z + _GA * z * z * z))
    return 0.5 * (1.0 + t) + 0.5 * z * (1.0 - t * t) * (_GC * (1.0 + 3.0 * _GA * z * z))


def _gmlp_common(z, lng, lnb):
    zg = _gelu(z)
    u, vg = zg[:, :GW], zg[:, GW:]
    mu = jnp.mean(vg, axis=-1, keepdims=True)
    xc = vg - mu
    rstd = lax.rsqrt(jnp.mean(xc * xc, axis=-1, keepdims=True) + EPS)
    xhat = xc * rstd
    return u, xhat, rstd, xhat * lng + lnb


def _tril_w(ws_ref):
    tri = lax.broadcasted_iota(jnp.int32, (CH, CH), 0) >= lax.broadcasted_iota(jnp.int32, (CH, CH), 1)
    return [jnp.where(tri, ws_ref[g], 0.0).astype(BF) for g in range(NG)], tri


def _split_pair(vp):
    lane = lax.broadcasted_iota(jnp.int32, vp.shape, 1)
    zero = jnp.zeros(vp.shape, vp.dtype)
    return jnp.concatenate([jnp.where(lane < HD, vp, zero), jnp.where(lane >= HD, vp, zero)], axis=0)


def _gmlp_mix(wt, vgn_b):
    outs = []
    for p in range(NG // 2):
        wcat = jnp.concatenate([wt[2 * p], wt[2 * p + 1]], axis=1)
        outs.append(_dot(wcat, _split_pair(vgn_b[:, 128 * p:128 * (p + 1)]), 'nn'))
    return jnp.concatenate(outs, axis=1)


def _gmlp_fwd(z, lng, lnb, ws, bfull, bt):
    N = z.shape[0]

    def body(z_ref, lng_ref, lnb_ref, ws_ref, bf_ref, o_ref):
        wt, _ = _tril_w(ws_ref)
        for c in range(bt // CH):
            rows = slice(c * CH, (c + 1) * CH)
            u, _, _, vgn = _gmlp_common(z_ref[rows, :], lng_ref[...], lnb_ref[...])
            mixed = _gmlp_mix(wt, vgn.astype(BF)) + bf_ref[...]
            o_ref[rows, :] = (u * mixed).astype(BF)

    full = lambda shp: pl.BlockSpec(shp, lambda i: (0,) * len(shp))
    return _pcall(body, name="gmlp_fwd", grid=(N // bt,),
                  in_specs=[pl.BlockSpec((bt, D), lambda i: (i, 0)), full((1, GW)), full((1, GW)),
                            full((NG, CH, CH)), full((CH, GW))],
                  out_specs=pl.BlockSpec((bt, GW), lambda i: (i, 0)), out_shape=jax.ShapeDtypeStruct((N, GW), BF),
                  compiler_params=_params(("arbitrary",)))(z, lng, lnb, ws, bfull)


def _gmlp_bwd(z, dcat, lng, lnb, ws, bfull, bt):
    N = z.shape[0]

    def body(z_ref, da_ref, lng_ref, lnb_ref, ws_ref, bf_ref, dz_ref, dg_ref, db_ref, dws_ref, dbf_ref):
        @pl.when(pl.program_id(0) == 0)
        def _():
            for r in (dg_ref, db_ref, dws_ref, dbf_ref):
                r[...] = jnp.zeros(r.shape, F32)

        wt, tri = _tril_w(ws_ref)
        lane = lax.broadcasted_iota(jnp.int32, (CH, 128), 1)
        for c in range(bt // CH):
            rows = slice(c * CH, (c + 1) * CH)
            zc = z_ref[rows, :]
            u, xhat, rstd, vgn = _gmlp_common(zc, lng_ref[...], lnb_ref[...])
            vgn_b = vgn.astype(BF)
            mixed = _gmlp_mix(wt, vgn_b) + bf_ref[...]
            da = da_ref[rows, :].astype(F32)
            dmix = da * u
            du = da * mixed
            dbf_ref[...] += dmix
            dvs = []
            for p in range(NG // 2):
                cols = slice(128 * p, 128 * (p + 1))
                dmp = dmix[:, cols].astype(BF)
                dwp = _dot(_split_pair(dmp), vgn_b[:, cols], 'nt')
                dws_ref[2 * p] += jnp.where(tri, dwp[:CH], 0.0)
                dws_ref[2 * p + 1] += jnp.where(tri, dwp[CH:], 0.0)
                dvs.append(jnp.where(lane < HD, _dot(wt[2 * p], dmp, 'tn'), _dot(wt[2 * p + 1], dmp, 'tn')))
            dvgn = jnp.concatenate(dvs, axis=1)
            dg_ref[...] += jnp.sum(dvgn * xhat, axis=0, keepdims=True)
            db_ref[...] += jnp.sum(dvgn, axis=0, keepdims=True)
            dxh = dvgn * lng_ref[...]
            dvg = rstd * (dxh - jnp.mean(dxh, axis=-1, keepdims=True)
                          - xhat * jnp.mean(dxh * xhat, axis=-1, keepdims=True))
            dz_ref[rows, :] = (jnp.concatenate([du, dvg], axis=1) * _gelu_grad(zc)).astype(BF)

    full = lambda shp: pl.BlockSpec(shp, lambda i: (0,) * len(shp))
    return _pcall(body, name="gmlp_bwd", grid=(N // bt,),
                  in_specs=[pl.BlockSpec((bt, D), lambda i: (i, 0)), pl.BlockSpec((bt, GW), lambda i: (i, 1)),
                            full((1, GW)), full((1, GW)), full((NG, CH, CH)), full((CH, GW))],
                  out_specs=[pl.BlockSpec((bt, D), lambda i: (i, 0)), full((1, GW)), full((1, GW)),
                             full((NG, CH, CH)), full((CH, GW))],
                  out_shape=[jax.ShapeDtypeStruct((N, D), BF), jax.ShapeDtypeStruct((1, GW), F32),
                             jax.ShapeDtypeStruct((1, GW), F32), jax.ShapeDtypeStruct((NG, CH, CH), F32),
                             jax.ShapeDtypeStruct((CH, GW), F32)],
                  compiler_params=_params(("arbitrary",)))(z, dcat, lng, lnb, ws, bfull)


def _group_sum(name, a):
    def body(a_ref, o_ref):
        lane = lax.broadcasted_iota(jnp.int32, (CH, 128), 1)
        out = jnp.zeros((CH, 128), F32)
        for g in range(NG):
            out = jnp.where(lane == g, jnp.sum(a_ref[:, g * HD:(g + 1) * HD], axis=1, keepdims=True), out)
        o_ref[...] = out

    return _pcall(body, name=name, out_shape=jax.ShapeDtypeStruct((CH, 128), F32))(a)


def _xattn_softmax(q_h, k_h):
    s = _dot(q_h, k_h, 'nt') * (XD ** -0.5)
    p = jnp.exp(s - jnp.max(s, axis=1, keepdims=True))
    return p / jnp.sum(p, axis=1, keepdims=True)


def _xattn_fwd(name, q, kv, B, T, bq):
    nq = T // bq

    def body(q_ref, kv_ref, o_ref):
        for h in range(XH):
            cols = slice(h * XD, (h + 1) * XD)
            p = _xattn_softmax(q_ref[:, cols], kv_ref[:, cols])
            o_ref[:, cols] = _dot(p.astype(BF), kv_ref[:, D + h * XD:D + (h + 1) * XD], 'nn').astype(BF)

    return _pcall(body, name=name, grid=(B, nq),
                  in_specs=[pl.BlockSpec((bq, D), lambda b, i: (b * nq + i, 0)),
                            pl.BlockSpec((NMEM, 2 * D), lambda b, i: (b, 0))],
                  out_specs=pl.BlockSpec((bq, D), lambda b, i: (b * nq + i, 0)),
                  out_shape=jax.ShapeDtypeStruct((B * T, D), BF), compiler_params=_params(("arbitrary", "arbitrary")))(q, kv)


def _xattn_bwd(name, q, kv, do, B, T, bq):
    nq = T // bq
    sc = XD ** -0.5

    def body(q_ref, kv_ref, do_ref, dq_ref, dkv_ref):
        @pl.when(pl.program_id(1) == 0)
        def _():
            dkv_ref[...] = jnp.zeros(dkv_ref.shape, F32)

        for h in range(XH):
            cols = slice(h * XD, (h + 1) * XD)
            vcols = slice(D + h * XD, D + (h + 1) * XD)
            qh, kh, doh = q_ref[:, cols], kv_ref[:, cols], do_ref[:, cols]
            p = _xattn_softmax(qh, kh)
            dp = _dot(doh, kv_ref[:, vcols], 'nt')
            ds = p * (dp - jnp.sum(p * dp, axis=1, keepdims=True))
            dsb = ds.astype(BF)
            dq_ref[:, cols] = (_dot(dsb, kh, 'nn') * sc).astype(BF)
            dkv_ref[:, cols] += _dot(dsb, qh, 'tn') * sc
            dkv_ref[:, vcols] += _dot(p.astype(BF), doh, 'tn')

    blk = pl.BlockSpec((bq, D), lambda b, i: (b * nq + i, 0))
    return _pcall(body, name=name, grid=(B, nq),
                  in_specs=[blk, pl.BlockSpec((NMEM, 2 * D), lambda b, i: (b, 0)), blk],
                  out_specs=[blk, pl.BlockSpec((NMEM, 2 * D), lambda b, i: (b, 0))],
                  out_shape=[jax.ShapeDtypeStruct((B * T, D), BF), jax.ShapeDtypeStruct((B * NMEM, 2 * D), F32)],
                  compiler_params=_params(("arbitrary", "arbitrary")))(q, kv, do)


def _ln_stats(v):
    mu = jnp.mean(v, axis=-1, keepdims=True)
    xc = v - mu
    rstd = lax.rsqrt(jnp.mean(xc * xc, axis=-1, keepdims=True) + EPS)
    return xc * rstd, rstd


SUB = 8


LANES = 128
NSTRIP = D // LANES


def _fill_window(win, parts):
    for s in range(NSTRIP):
        for r0, val in parts:
            win[s, r0:r0 + val.shape[0], :] = val[:, s * LANES:(s + 1) * LANES]


def _fill_phases(win, sh, rows):
    for b in range(1, SUB):
        for s in range(NSTRIP):
            sh[b - 1, s] = win[s, b:b + rows, :]


def _fill_taps(w8, w_ref):
    for s in range(NSTRIP):
        for j in range(CK):
            w8[s, SUB * j:SUB * (j + 1), :] = jnp.broadcast_to(w_ref[j:j + 1, s * LANES:(s + 1) * LANES], (SUB, LANES))


def _row_groups(win, sh, s):
    cache = {}

    def get(o, t):
        a, b = divmod(o, SUB)
        key = (b, t + a)
        if key not in cache:
            rows = slice(SUB * (t + a), SUB * (t + a + 1))
            cache[key] = win[s, rows, :] if b == 0 else sh[b - 1, s, rows, :]
        return cache[key]

    return get


def _from_strips(ref):
    return jnp.concatenate([ref[s] for s in range(NSTRIP)], axis=1)


def _sum_groups(name, a):
    R, C = a.shape[0] // SUB, a.shape[1]

    def body(a_ref, o_ref):
        o_ref[...] = jnp.sum(a_ref[...].reshape(R, SUB, C), axis=1)

    return _pcall(body, name=name, out_shape=jax.ShapeDtypeStruct((R, C), F32))(a)


def _conv_fwd(y, w32, wb, lng, lnb, B, T, bt):
    nt = T // bt
    hb = bt // HALO
    prows = bt + HALO - SUB

    def body(y_ref, yp_ref, w_ref, wb_ref, lng_ref, lnb_ref, s_ref, yc_ref, win, sh, w8, out):
        i = pl.program_id(1)
        _fill_window(win, [(0, jnp.where(i > 0, yp_ref[...], 0.0)), (HALO, y_ref[...])])
        _fill_phases(win, sh, prows)

        @pl.when((pl.program_id(0) == 0) & (i == 0))
        def _():
            _fill_taps(w8, w_ref)

        def strip(s, carry):
            get = _row_groups(win, sh, s)
            for t in range(bt // SUB):
                accs = [jnp.zeros((SUB, LANES), F32), jnp.zeros((SUB, LANES), F32)]
                for j in range(CK):
                    accs[j % 2] = accs[j % 2] + w8[s, SUB * j:SUB * (j + 1), :] * get(HALO - (CK - 1) + j, t)
                out[s, SUB * t:SUB * (t + 1), :] = accs[0] + accs[1]
            return carry

        lax.fori_loop(0, NSTRIP, strip, 0)
        acc = _from_strips(out) + wb_ref[...]
        yc_ref[...] = acc
        xhat, _ = _ln_stats(acc)
        ln = xhat * lng_ref[...] + lnb_ref[...]
        s_ref[...] = (ln * _sigmoid(ln)).astype(BF)

    row = lambda n: pl.BlockSpec((n, D), lambda b, i: (0, 0))
    cur = pl.BlockSpec((bt, D), lambda b, i: (b * nt + i, 0))
    return _pcall(body, name="conv_fwd", grid=(B, nt),
                  in_specs=[cur, pl.BlockSpec((HALO, D), lambda b, i: (jnp.maximum((b * nt + i) * hb - 1, 0), 0)),
                            row(HALO), row(1), row(1), row(1)],
                  out_specs=[cur, cur],
                  out_shape=[jax.ShapeDtypeStruct((B * T, D), BF), jax.ShapeDtypeStruct((B * T, D), F32)],
                  scratch_shapes=[pltpu.VMEM((NSTRIP, bt + HALO, LANES), F32),
                                  pltpu.VMEM((SUB - 1, NSTRIP, prows, LANES), F32),
                                  pltpu.VMEM((NSTRIP, HALO * SUB, LANES), F32), pltpu.VMEM((NSTRIP, bt, LANES), F32)],
                  compiler_params=_params(("arbitrary", "arbitrary")))(y, y, w32, wb, lng, lnb)


def _conv_bwd(ds, yc, y, pa, pg, w32, lng, lnb, B, T, bt):
    nt = T // bt
    hb = bt // HALO
    nblk32 = B * T // HALO

    def ln_bwd(dsv, ycv, lng, lnb):
        xhat, rstd = _ln_stats(ycv)
        ln = xhat * lng + lnb
        sg = _sigmoid(ln)
        dln = dsv * (sg * (1.0 + ln * (1.0 - sg)))
        dxh = dln * lng
        dyc = rstd * (dxh - jnp.mean(dxh, axis=-1, keepdims=True)
                      - xhat * jnp.mean(dxh * xhat, axis=-1, keepdims=True))
        return dyc, dln, xhat

    prows = bt + HALO - SUB

    def body(ds_ref, dsn_ref, yc_ref, ycn_ref, y_ref, yp_ref, pa_ref, pg_ref, w_ref, lng_ref, lnb_ref,
             dpa_ref, dpg_ref, dw_ref, dwb_ref, dlng_ref, dlnb_ref, dba_ref, dbg_ref,
             dwin, ywin, dsh, ysh, w8, dy_out, dw_out):
        i = pl.program_id(1)

        @pl.when((pl.program_id(0) == 0) & (i == 0))
        def _():
            for r in (dw_ref, dwb_ref, dlng_ref, dlnb_ref, dba_ref, dbg_ref):
                r[...] = jnp.zeros(r.shape, F32)
            _fill_taps(w8, w_ref)

        lng, lnb = lng_ref[...], lnb_ref[...]
        dyc, dln, xhat = ln_bwd(ds_ref[...].astype(F32), yc_ref[...], lng, lnb)
        dycn, _, _ = ln_bwd(dsn_ref[...].astype(F32), ycn_ref[...], lng, lnb)
        _fill_window(dwin, [(0, dyc), (bt, jnp.where(i < nt - 1, dycn, 0.0))])
        _fill_window(ywin, [(0, jnp.where(i > 0, yp_ref[...], 0.0)), (HALO, y_ref[...])])
        dlng_ref[...] += jnp.sum(dln * xhat, axis=0, keepdims=True)
        dlnb_ref[...] += jnp.sum(dln, axis=0, keepdims=True)
        dwb_ref[...] += jnp.sum(dyc, axis=0, keepdims=True)
        _fill_phases(dwin, dsh, prows)
        _fill_phases(ywin, ysh, prows)

        def strip(s, carry):
            get_d, get_y = _row_groups(dwin, dsh, s), _row_groups(ywin, ysh, s)
            dw_acc = [jnp.zeros((SUB, LANES), F32) for _ in range(CK)]
            for t in range(bt // SUB):
                dyc_g = get_d(0, t)
                dys = [jnp.zeros((SUB, LANES), F32), jnp.zeros((SUB, LANES), F32)]
                for j in range(CK):
                    dys[j % 2] = dys[j % 2] + w8[s, SUB * j:SUB * (j + 1), :] * get_d(CK - 1 - j, t)
                    dw_acc[j] = dw_acc[j] + dyc_g * get_y(HALO - (CK - 1) + j, t)
                dy_out[s, SUB * t:SUB * (t + 1), :] = dys[0] + dys[1]
            for j in range(CK):
                dw_out[s, SUB * j:SUB * (j + 1), :] = dw_acc[j]
            return carry

        lax.fori_loop(0, NSTRIP, strip, 0)
        dw_ref[0:CK * SUB, :] += _from_strips(dw_out)
        dy = _from_strips(dy_out)
        a, g = pa_ref[...].astype(F32), pg_ref[...].astype(F32)
        sg = _sigmoid(g)
        da = dy * sg
        dg = dy * a * sg * (1.0 - sg)
        dpa_ref[...] = da.astype(BF)
        dpg_ref[...] = dg.astype(BF)
        dba_ref[...] += jnp.sum(da, axis=0, keepdims=True)
        dbg_ref[...] += jnp.sum(dg, axis=0, keepdims=True)

    cur = pl.BlockSpec((bt, D), lambda b, i: (b * nt + i, 0))
    nxt = pl.BlockSpec((HALO, D), lambda b, i: (jnp.minimum((b * nt + i + 1) * hb, nblk32 - 1), 0))
    prv = pl.BlockSpec((HALO, D), lambda b, i: (jnp.maximum((b * nt + i) * hb - 1, 0), 0))
    row = lambda n: pl.BlockSpec((n, D), lambda b, i: (0, 0))
    N = B * T
    return _pcall(body, name="conv_bwd", grid=(B, nt),
                  in_specs=[cur, nxt, cur, nxt, cur, prv, cur, cur, row(HALO), row(1), row(1)],
                  out_specs=[cur, cur, row(HALO * SUB), row(1), row(1), row(1), row(1), row(1)],
                  out_shape=[jax.ShapeDtypeStruct((N, D), BF)] * 2 + [jax.ShapeDtypeStruct((HALO * SUB, D), F32)]
                  + [jax.ShapeDtypeStruct((1, D), F32)] * 5,
                  scratch_shapes=[pltpu.VMEM((NSTRIP, bt + HALO, LANES), F32), pltpu.VMEM((NSTRIP, bt + HALO, LANES), F32),
                                  pltpu.VMEM((SUB - 1, NSTRIP, prows, LANES), F32),
                                  pltpu.VMEM((SUB - 1, NSTRIP, prows, LANES), F32),
                                  pltpu.VMEM((NSTRIP, HALO * SUB, LANES), F32), pltpu.VMEM((NSTRIP, bt, LANES), F32),
                                  pltpu.VMEM((NSTRIP, CK * SUB, LANES), F32)],
                  compiler_params=_params(("arbitrary", "arbitrary")))(ds, ds, yc, yc, y, y, pa, pg, w32, lng, lnb)


def _head(x, tgt, gain, bt=512):
    N = x.shape[0]
    bt = min(bt, N)

    def body(x_ref, t_ref, g_ref, dx_ref, loss_ref, dg_ref):
        @pl.when(pl.program_id(0) == 0)
        def _():
            loss_ref[...] = jnp.zeros(loss_ref.shape, F32)
            dg_ref[...] = jnp.zeros(dg_ref.shape, F32)

        xv = x_ref[...]
        gain = g_ref[...]
        err = xv * _rms_stats(xv) * gain - t_ref[...]
        loss_ref[...] += 0.5 * jnp.sum(jnp.mean(err * err, axis=-1, keepdims=True), axis=0, keepdims=True)
        dx, dgr = _rms_bwd(xv, gain, err * (1.0 / D))
        dx_ref[...] = dx
        dg_ref[...] += jnp.sum(dgr, axis=0, keepdims=True)

    blk = pl.BlockSpec((bt, D), lambda i: (i, 0))
    return _pcall(body, name="loss_head", grid=(N // bt,),
                  in_specs=[blk, blk, pl.BlockSpec((1, D), lambda i: (0, 0))],
                  out_specs=[blk, pl.BlockSpec((1, 128), lambda i: (0, 0)), pl.BlockSpec((1, D), lambda i: (0, 0))],
                  out_shape=[jax.ShapeDtypeStruct((N, D), F32), jax.ShapeDtypeStruct((1, 128), F32),
                             jax.ShapeDtypeStruct((1, D), F32)],
                  compiler_params=_params(("arbitrary",)))(x, tgt, gain)


def _local_step(x, mem, tgt, Wb, P, hooks=None):
    B, T, _ = x.shape
    N = B * T
    bq = bk = min(512, T)
    bt = min(512, T)
    x0 = x.reshape(N, D)
    mem2 = mem.reshape(B * NMEM, D)
    tgt2 = tgt.reshape(N, D)
    row = lambda v: v.reshape(1, -1)
    G = {}

    w_in = Wb['w_in_e'][0]
    w_inp = jnp.concatenate([w_in[:, 3 * FOXW + NG:], w_in[:, :3 * FOXW], w_in[:, 3 * FOXW:3 * FOXW + NG],
                             jnp.zeros((D, 128 - NG), BF)], axis=1)
    g_e = row(P['mix_norm_e'])
    z, h0 = _norm_mm("proj_z", x0, g_e, w_inp, N=D, coff=0, bn=D, out_dtype=F32, h_out=True)
    qkv = _norm_mm("proj_qkv", x0, g_e, w_inp, N=3 * FOXW, coff=2, bn=FOXW, out_dtype=BF)[0]
    fl = _norm_mm("proj_f", x0, g_e, w_inp, N=128, coff=20, bn=128, out_dtype=F32)[0]
    fbias = jnp.concatenate([P['fox_f_bias'].reshape(1, NG), jnp.zeros((1, 128 - NG), F32)], axis=1)
    cum = _fox_gate_fwd(fl, fbias, B, T)
    cum4 = cum[:, :NG].reshape(B, T, NG).transpose(0, 2, 1).reshape(B, NG, T // bk, bk)
    (b_out, lse), arrived = _fox_fwd(qkv, cum4, B, T, 0, bq, bk, carry=hooks['fwd_carry']() if hooks else None)
    if hooks:
        Wb = {**Wb, **hooks['fwd_done'](arrived)}
    lng, lnb = row(P['gmlp_ln_g']), row(P['gmlp_ln_b'])
    ws = P['gmlp_w_s'][0]
    bfull = jnp.repeat(P['gmlp_b_s'][0].T, HD, axis=1)
    a_out = _gmlp_fwd(z, lng, lnb, ws, bfull, bt)
    w_out = Wb['w_out_e'][0]
    x1 = _mm_resid("mix_out", [dict(A=b_out, Ka=FOXW, B=w_out, roff=0), dict(A=a_out, Ka=GW, B=w_out, roff=1)], x0)

    def xa_ffn_fwd(l, xin):
        qx, hq = _norm_mm(f"xa_q{l}", xin, row(P['xa_norm'][l]), Wb['xa_wq'][l], N=D, bn=D, out_dtype=BF, h_out=True)
        kv, hm = _norm_mm(f"xa_kv{l}", mem2, row(P['mem_norm'][l]), Wb['xa_wkv'][l], N=2 * D, bn=D, out_dtype=BF,
                          h_out=True)
        o = _xattn_fwd(f"xattn_fwd{l}", qx, kv, B, T, bt)
        xm = _mm_resid(f"xa_o{l}", [dict(A=o, Ka=D, B=Wb['xa_wo'][l])], xin)
        wgu = Wb['ffn_w_gu'][l]
        g, u, a, hf = _fused_mm(f"ffn_gu{l}", dims='nn', M=N, N=FF, bm=min(512, N), bn=FF // 2, x=xm,
                                gain=row(P['ffn_norm'][l]),
                                groups=[[dict(A=None, Ka=D, B=wgu, coff=0)], [dict(A=None, Ka=D, B=wgu, coff=2)]],
                                epi=_epi_swiglu, outs=[BF, BF, BF], h_out=True)
        xo = _mm_resid(f"ffn_down{l}", [dict(A=a, Ka=FF, B=Wb['ffn_w_down'][l])], xm)
        return xo, dict(xin=xin, qx=qx, hq=hq, kv=kv, hm=hm, o=o, xm=xm, g=g, u=u, a=a, hf=hf)

    x3, S0 = xa_ffn_fwd(0, x1)
    w_cin = Wb['conv_w_in'][0]
    b_cin = row(P['conv_b_in'])
    pa, pg, y, hc = _fused_mm("conv_in", dims='nn', M=N, N=D, bm=min(512, N), bn=D, x=x3, gain=row(P['mix_norm_o']),
                              groups=[[dict(A=None, Ka=D, B=w_cin, coff=0)], [dict(A=None, Ka=D, B=w_cin, coff=1)]],
                              epi=_epi_glu, outs=[BF, BF, F32], rows=[(b_cin, 0), (b_cin, 1)], h_out=True)
    w32 = jnp.concatenate([P['conv_dw_w'][0], jnp.zeros((HALO - CK, D), F32)], axis=0)
    cbt = min(256, T)
    s, yc = _conv_fwd(y, w32, row(P['conv_dw_b']), row(P['conv_ln_g']), row(P['conv_ln_b']), B, T, cbt)
    x4 = _mm_resid("conv_out", [dict(A=s, Ka=D, B=Wb['conv_w_out'][0])], x3, bias=row(P['conv_b_out']))
    x6, S1 = xa_ffn_fwd(1, x4)
    dx, loss_t, dgf = _head(x6, tgt2, row(P['final_norm']))
    G['final_norm'] = dgf.reshape(D)

    def by_rows(dw):
        return dw.reshape(NCHIP, dw.shape[1] // NCHIP, dw.shape[2])

    def xa_ffn_bwd(l, S, dx):
        wgu, wdown = Wb['ffn_w_gu'][l], Wb['ffn_w_down'][l]
        dwdown = by_rows(_mm_tn(f"dw_down{l}", S['a'], dx))
        dg, du = _fused_mm(f"ffn_dgu{l}", dims='nt', M=N, N=FF, bm=min(512, N), bn=FF // 2,
                           groups=[[dict(A=dx, Ka=D, B=wdown)]], epi=_epi_swiglu_bwd, outs=[BF, BF],
                           tiles=[(S['g'], 0), (S['u'], 0)])
        dwgu = jnp.concatenate([_mm_tn(f"dw_g{l}", S['hf'], dg, parts=2), _mm_tn(f"dw_u{l}", S['hf'], du, parts=2)])
        dx, dgn = _mm_nt_rms_bwd(f"ffn_dx{l}", [dict(A=dg, Ka=FF, B=wgu, coff=0), dict(A=du, Ka=FF, B=wgu, coff=1)],
                                 S['xm'], row(P['ffn_norm'][l]), dx)
        dwo = by_rows(_mm_tn(f"dw_o{l}", S['o'], dx))
        do = _mm_nt_plain(f"xa_do{l}", dx, Wb['xa_wo'][l])
        dq, dkv = _xattn_bwd(f"xattn_bwd{l}", S['qx'], S['kv'], do, B, T, bt)
        dwq = by_rows(_mm_tn(f"dw_q{l}", S['hq'], dq))
        dwkv = _mm_tn(f"dw_kv{l}", S['hm'], dkv, parts=NCHIP)
        dmn = _fused_mm(f"xa_dmem{l}", dims='nt', M=B * NMEM, N=D, bm=min(256, B * NMEM), bn=D,
                        groups=[[dict(A=dkv, Ka=2 * D, B=Wb['xa_wkv'][l])]], epi=_epi_rms_gain_only, outs=[],
                        tiles=[(mem2, 0)], rows=[(row(P['mem_norm'][l]), 0)], reds=[(1, D)])[0]
        dx, dxn = _mm_nt_rms_bwd(f"xa_dx{l}", [dict(A=dq, Ka=D, B=Wb['xa_wq'][l])], S['xin'],
                                 row(P['xa_norm'][l]), dx)
        return dx, dict(ffn_w_down=dwdown, ffn_w_gu=dwgu, ffn_norm=dgn.reshape(D), xa_wo=dwo, xa_wq=dwq,
                        xa_wkv=dwkv, mem_norm=dmn.reshape(D), xa_norm=dxn.reshape(D))

    dx, G1 = xa_ffn_bwd(1, S1, dx)
    G['conv_w_out'] = [by_rows(_mm_tn("dw_cout", s, dx))]
    G['conv_b_out'] = _colsum("db_cout", dx)
    dsv = _mm_nt_plain("conv_ds", dx, Wb['conv_w_out'][0])
    dpa, dpg, dw32, dwb, dlng, dlnb, dba, dbg = _conv_bwd(dsv, yc, y, pa, pg, w32, row(P['conv_ln_g']),
                                                          row(P['conv_ln_b']), B, T, cbt)
    G['conv_dw_w'] = _sum_groups("conv_dw_sum", dw32)[:CK][None]
    G['conv_dw_b'], G['conv_ln_g'], G['conv_ln_b'] = dwb, dlng, dlnb
    G['conv_b_in'] = jnp.concatenate([dba, dbg], axis=1)
    G['conv_w_in'] = [jnp.concatenate([_mm_tn("dw_cin_a", hc, dpa, parts=2), _mm_tn("dw_cin_g", hc, dpg, parts=2)])]
    dx, dgo = _mm_nt_rms_bwd("conv_dx", [dict(A=dpa, Ka=D, B=w_cin, coff=0), dict(A=dpg, Ka=D, B=w_cin, coff=1)],
                             x3, row(P['mix_norm_o']), dx)
    G['mix_norm_o'] = dgo
    dx, G0 = xa_ffn_bwd(0, S0, dx)
    for k in G0:
        G[k] = [G0[k], G1[k]]
    G['w_out_e'] = [by_rows(jnp.concatenate([_mm_tn("dw_out_b", b_out, dx), _mm_tn("dw_out_a", a_out, dx)], axis=1))]
    dcat = _mm_nt_plain("mix_dcat", dx, w_out)
    (dq, dk, dv, dcum4, dcq4), arrived = _fox_bwd(qkv, cum4, b_out, lse, dcat, B, T, 0, bq, bk,
                                                  carry=hooks['bwd_carry'](G) if hooks else None)
    if hooks:
        hooks['bwd_done'](arrived)
    dz, dlg, dlb, dws, dbf = _gmlp_bwd(z, dcat, lng, lnb, ws, bfull, bt)
    G['gmlp_ln_g'], G['gmlp_ln_b'], G['gmlp_w_s'] = dlg, dlb, dws[None]
    G['gmlp_b_s'] = _group_sum("gmlp_db", dbf)[:, :NG].T[None]
    pad = jnp.zeros((N, 128 - NG), F32)
    dck = jnp.concatenate([dcum4.reshape(B, NG, T).transpose(0, 2, 1).reshape(N, NG), pad], axis=1)
    dcq = jnp.concatenate([dcq4.reshape(N, NG // 2, 128)[:, :, :2].reshape(N, NG), pad], axis=1)
    dfl, dfb = _fox_gate_bwd(fl, fbias, dcq, dck, B, T)
    G['fox_f_bias'] = dfb[:, :NG]
    dw_in = jnp.concatenate([_mm_tn("dw_in_q", h0, dq)[0], _mm_tn("dw_in_k", h0, dk)[0], _mm_tn("dw_in_v", h0, dv)[0],
                             _mm_tn("dw_in_f", h0, dfl)[0][:, :NG], _mm_tn("dw_in_z", h0, dz)[0]], axis=1)
    G['w_in_e'] = [dw_in.reshape(D, NCHIP, IN_W // NCHIP).transpose(1, 0, 2)]
    dx, dge = _mm_nt_rms_bwd("mix_dx", [dict(A=dz, Ka=D, B=w_inp, coff=0), dict(A=dq, Ka=FOXW, B=w_inp, coff=2),
                                        dict(A=dk, Ka=FOXW, B=w_inp, coff=3), dict(A=dv, Ka=FOXW, B=w_inp, coff=4),
                                        dict(A=dfl, Ka=128, B=w_inp, coff=20)], x0, g_e, dx)
    G['mix_norm_e'] = dge
    return loss_t[0, 0], dx.reshape(B, T, D), G


COLS = 1024
ANY = pl.BlockSpec(memory_space=pl.ANY)


def _coords():
    return lax.axis_index("x"), lax.axis_index("y"), lax.axis_index("c")


def _other_chips(x, y):
    return [(1 - x, y), (x, 1 - y), (1 - x, 1 - y)]


def _own_slot(v, me):
    return lax.dynamic_update_slice(lax.empty((NCHIP,) + v.shape, v.dtype), v[None], (me,) + (0,) * v.ndim)


def _all_gather(name, shards, me):
    n = len(shards)
    bufs = [_own_slot(v, me) for v in shards]

    def body(*refs):
        out_refs, (send_sems, recv_sems) = refs[n:2 * n], refs[2 * n:]
        x, y, c = _coords()
        mine = 2 * x + y
        sib = (x, y, 1 - c)
        chips = _other_chips(x, y)

        def rcopy(a, k, chip_idx, half, to):
            blk = out_refs[a].at[chip_idx, half]
            return pltpu.make_async_remote_copy(src_ref=blk, dst_ref=blk, send_sem=send_sems.at[6 * a + k],
                                                recv_sem=recv_sems.at[6 * a + k], device_id=to, device_id_type=MESH)

        first = [rcopy(a, j, mine, c, (cx, cy, c)) for a in range(n) for j, (cx, cy) in enumerate(chips)]
        for cp in first:
            cp.start()
        passed = []
        for a in range(n):
            for j, (cx, cy) in enumerate(chips):
                kj = 2 * cx + cy
                rcopy(a, j, kj, c, sib).wait_recv()
                fwd = rcopy(a, 3 + j, kj, c, sib)
                fwd.start()
                passed.append(fwd)
        for a in range(n):
            for j, (cx, cy) in enumerate(chips):
                rcopy(a, 3 + j, 2 * cx + cy, 1 - c, sib).wait_recv()
        for cp in first + passed:
            cp.wait_send()

    return _pcall(body, name=name, in_specs=[ANY] * n, out_specs=[ANY] * n,
                  out_shape=[jax.ShapeDtypeStruct(b.shape, b.dtype) for b in bufs],
                  input_output_aliases={a: a for a in range(n)}, scratch_shapes=_sem_pairs(6 * n))(*bufs)


def _sem_pairs(n):
    return [pltpu.SemaphoreType.DMA((n,)), pltpu.SemaphoreType.DMA((n,))]


def _gather_forward(name, bufs):
    n = len(bufs)

    def body(*refs):
        out_refs, (send_sems, recv_sems) = refs[n:2 * n], refs[2 * n:]
        x, y, c = _coords()

        def cp(a, j, kj, half):
            blk = out_refs[a].at[kj, half]
            return pltpu.make_async_remote_copy(src_ref=blk, dst_ref=blk, send_sem=send_sems.at[3 * a + j],
                                                recv_sem=recv_sems.at[3 * a + j], device_id=(x, y, 1 - c),
                                                device_id_type=MESH)

        chips = [2 * cx + cy for cx, cy in _other_chips(x, y)]
        sends = [cp(a, j, kj, c) for a in range(n) for j, kj in enumerate(chips)]
        for s in sends:
            s.start()
        for a in range(n):
            for j, kj in enumerate(chips):
                cp(a, j, kj, 1 - c).wait_recv()
        for s in sends:
            s.wait_send()

    return _pcall(body, name=name, in_specs=[ANY] * n, out_specs=[ANY] * n,
                  out_shape=[jax.ShapeDtypeStruct(b.shape, b.dtype) for b in bufs],
                  input_output_aliases={a: a for a in range(n)}, scratch_shapes=_sem_pairs(3 * n))(*bufs)


def _sibling_halves(name, ps):
    n = len(ps)

    def body(*refs):
        p_refs, out_refs, (send_sems, recv_sems) = refs[:n], refs[n:2 * n], refs[2 * n:]
        x, y, c = _coords()
        cps = [pltpu.make_async_remote_copy(src_ref=p_refs[a].at[k, 1 - c], dst_ref=out_refs[a].at[k],
                                            send_sem=send_sems.at[4 * a + k], recv_sem=recv_sems.at[4 * a + k],
                                            device_id=(x, y, 1 - c), device_id_type=MESH)
               for a in range(n) for k in range(NCHIP)]
        for cp in cps:
            cp.start()
        for cp in cps:
            cp.wait()

    return _pcall(body, name=name, in_specs=[ANY] * n, out_specs=[ANY] * n,
                  out_shape=[jax.ShapeDtypeStruct((NCHIP,) + p.shape[2:], p.dtype) for p in ps],
                  scratch_shapes=_sem_pairs(NCHIP * n))(*ps)


def _chip_exchange(name, qs):
    n = len(qs)

    def body(*refs):
        q_refs, out_refs, (send_sems, recv_sems) = refs[:n], refs[n:2 * n], refs[2 * n:]
        x, y, c = _coords()
        cps = [pltpu.make_async_remote_copy(src_ref=q_refs[a].at[2 * cx + cy], dst_ref=out_refs[a].at[j],
                                            send_sem=send_sems.at[3 * a + j], recv_sem=recv_sems.at[3 * a + j],
                                            device_id=(cx, cy, c), device_id_type=MESH)
               for a in range(n) for j, (cx, cy) in enumerate(_other_chips(x, y))]
        for cp in cps:
            cp.start()
        for cp in cps:
            cp.wait()

    return _pcall(body, name=name, in_specs=[ANY] * n, out_specs=[ANY] * n,
                  out_shape=[jax.ShapeDtypeStruct((3,) + q.shape[1:], q.dtype) for q in qs],
                  scratch_shapes=_sem_pairs(3 * n))(*qs)


def _sibling_swap(name, hs):
    n = len(hs)

    def body(*refs):
        out_refs, (send_sems, recv_sems) = refs[n:2 * n], refs[2 * n:]
        x, y, c = _coords()
        sib = (x, y, 1 - c)
        sends = [pltpu.make_async_remote_copy(src_ref=out_refs[a].at[c], dst_ref=out_refs[a].at[c],
                                              send_sem=send_sems.at[a], recv_sem=recv_sems.at[a], device_id=sib,
                                              device_id_type=MESH) for a in range(n)]
        for cp in sends:
            cp.start()
        for a in range(n):
            theirs = out_refs[a].at[1 - c]
            pltpu.make_async_remote_copy(src_ref=theirs, dst_ref=theirs, send_sem=send_sems.at[a],
                                         recv_sem=recv_sems.at[a], device_id=sib, device_id_type=MESH).wait_recv()
        for cp in sends:
            cp.wait_send()

    return _pcall(body, name=name, in_specs=[ANY] * n, out_specs=[ANY] * n,
                  out_shape=[jax.ShapeDtypeStruct(h.shape, h.dtype) for h in hs],
                  input_output_aliases={a: a for a in range(n)}, scratch_shapes=_sem_pairs(n))(*hs)


ADD_BLOCK_BYTES = 2 * 1024 * 1024


def _row_block(R, C):
    if R * C * 4 <= ADD_BLOCK_BYTES:
        return R
    for br in (512, 256, 128, 64, 32, 16, 8):
        if R % br == 0 and br * C * 4 <= ADD_BLOCK_BYTES:
            return br
    return R


def _add_own_half(name, p, recv, c_arr, out_dtype):
    _, _, R, C = p.shape
    br = _row_block(R, C)

    def body(c_ref, p_ref, r_ref, o_ref):
        o_ref[...] = (p_ref[...].astype(F32) + r_ref[...].astype(F32)).astype(o_ref.dtype)

    spec = pltpu.PrefetchScalarGridSpec(
        num_scalar_prefetch=1, grid=(NCHIP, R // br),
        in_specs=[pl.BlockSpec((None, None, br, C), lambda k, r, c_ref: (k, c_ref[0], r, 0)),
                  pl.BlockSpec((None, br, C), lambda k, r, c_ref: (k, r, 0))],
        out_specs=pl.BlockSpec((None, br, C), lambda k, r, c_ref: (k, r, 0)))
    return _pcall(body, name=name, grid_spec=spec, out_shape=jax.ShapeDtypeStruct((NCHIP, R, C), out_dtype),
                  compiler_params=_params(("arbitrary", "arbitrary")))(c_arr, p, recv)


def _add_chips(name, q, recv, idx_arr):
    _, R, C = q.shape
    br = _row_block(R, C)

    def body(idx_ref, q_ref, r_ref, o_ref):
        o_ref[...] = ((q_ref[...].astype(F32) + r_ref[0].astype(F32)) + r_ref[1].astype(F32)) + r_ref[2].astype(F32)

    spec = pltpu.PrefetchScalarGridSpec(
        num_scalar_prefetch=1, grid=(R // br,),
        in_specs=[pl.BlockSpec((None, br, C), lambda r, idx: (idx[0], r, 0)),
                  pl.BlockSpec((3, br, C), lambda r, idx: (0, r, 0))],
        out_specs=pl.BlockSpec((None, br, C), lambda r, idx: (idx[1], r, 0)))
    return _pcall(body, name=name, grid_spec=spec, out_shape=jax.ShapeDtypeStruct((2, R, C), F32),
                  compiler_params=_params(("arbitrary",)))(idx_arr, q, recv)


EARLY = ['w_in_e', 'w_out_e']
LATE = [n for n in BIG if n not in EARLY]


def _adamw(name, w, g, m, v):
    shape = w.shape
    cols = shape[-1]
    rows = w.size // cols
    w2, g2, m2, v2 = (a.reshape(rows, cols) for a in (w, g, m, v))
    bt = next((b for b in (256, 128) if rows % b == 0), rows)

    def body(w_ref, g_ref, m_ref, v_ref, d_ref, nm_ref, nv_ref):
        gv = g_ref[...]
        nm = ADAM_B1 * m_ref[...] + (1.0 - ADAM_B1) * gv
        nv = ADAM_B2 * v_ref[...] + (1.0 - ADAM_B2) * (gv * gv)
        m_hat = nm / (1.0 - ADAM_B1 ** ADAM_STEP)
        v_hat = nv / (1.0 - ADAM_B2 ** ADAM_STEP)
        d_ref[...] = -ADAM_LR * (m_hat / (jnp.sqrt(v_hat) + ADAM_EPS) + ADAM_WD * w_ref[...])
        nm_ref[...] = nm
        nv_ref[...] = nv

    blk = pl.BlockSpec((bt, cols), lambda i: (i, 0))
    outs = _pcall(body, name=name, grid=(rows // bt,), in_specs=[blk] * 4, out_specs=[blk] * 3,
                  out_shape=[jax.ShapeDtypeStruct((rows, cols), F32)] * 3, compiler_params=_params(("arbitrary",)))(
        w2, g2, m2, v2)
    return [o.reshape(shape) for o in outs]


SMALL_SHARDED = ['mix_norm_o', 'conv_b_in', 'conv_dw_w', 'conv_dw_b', 'conv_ln_g', 'conv_ln_b', 'conv_b_out']
REPLICATED = [n for n in WEIGHTS if SHARD_AXIS[n] is None]
NCHIP = 4


def _halves(flat, tile_rows):
    unit = 2 * tile_rows * COLS
    total = -(-flat.size // unit) * unit
    return jnp.pad(flat, (0, total - flat.size)).reshape(2, total // (2 * COLS), COLS)


def _flat(arrays):
    return jnp.concatenate([a.reshape(-1) for a in arrays])


def _chip_block(a, axis, k):
    n = a.shape[axis] // NCHIP
    return lax.slice_in_dim(a, k * n, (k + 1) * n, axis=axis)


def _full_shape(n, shard_shape):
    s = list(shard_shape[n])
    s[SHARD_AXIS[n]] *= NCHIP
    return tuple(s)


def _unpack(flat, names, shapes):
    out, off = {}, 0
    for n in names:
        size = math.prod(shapes[n])
        out[n] = flat[off:off + size].reshape(shapes[n])
        off += size
    return out


def kernel(x, mem, mix_norm_e, w_in_e, fox_f_bias, gmlp_ln_g, gmlp_ln_b, gmlp_w_s, gmlp_b_s, w_out_e, mix_norm_o, conv_w_in, conv_b_in, conv_dw_w, conv_dw_b, conv_ln_g, conv_ln_b, conv_w_out, conv_b_out, xa_norm, mem_norm, xa_wq, xa_wkv, xa_wo, ffn_norm, ffn_w_gu, ffn_w_down, final_norm, loss_target, m_mix_norm_e, m_w_in_e, m_fox_f_bias, m_gmlp_ln_g, m_gmlp_ln_b, m_gmlp_w_s, m_gmlp_b_s, m_w_out_e, m_mix_norm_o, m_conv_w_in, m_conv_b_in, m_conv_dw_w, m_conv_dw_b, m_conv_ln_g, m_conv_ln_b, m_conv_w_out, m_conv_b_out, m_xa_norm, m_mem_norm, m_xa_wq, m_xa_wkv, m_xa_wo, m_ffn_norm, m_ffn_w_gu, m_ffn_w_down, m_final_norm, v_mix_norm_e, v_w_in_e, v_fox_f_bias, v_gmlp_ln_g, v_gmlp_ln_b, v_gmlp_w_s, v_gmlp_b_s, v_w_out_e, v_mix_norm_o, v_conv_w_in, v_conv_b_in, v_conv_dw_w, v_conv_dw_b, v_conv_ln_g, v_conv_ln_b, v_conv_w_out, v_conv_b_out, v_xa_norm, v_mem_norm, v_xa_wq, v_xa_wkv, v_xa_wo, v_ffn_norm, v_ffn_w_gu, v_ffn_w_down, v_final_norm):
    env = locals()
    w = {n: env[n] for n in WEIGHTS}
    m = {n: env["m_" + n] for n in WEIGHTS}
    v = {n: env["v_" + n] for n in WEIGHTS}
    shard_shape = {n: w[n].shape for n in WEIGHTS}
    xi, yi, ci = _coords()
    me = 2 * xi + yi
    c_arr = jnp.reshape(ci, (1,)).astype(jnp.int32)
    idx_arr = jnp.stack([me, ci]).astype(jnp.int32)

    def two_halves(a):
        return a.reshape(2, a.shape[0] // 2, a.shape[1])

    def shards(names):
        return [two_halves(w[n][l].astype(BF)) for n in names for l in range(w[n].shape[0])]

    def full_matrices(names, gathered):
        out, at = {}, 0
        for n in names:
            out[n] = []
            for l in range(w[n].shape[0]):
                rows, cols = w[n].shape[1:]
                g = gathered[at].reshape(NCHIP, rows, cols)
                at += 1
                out[n].append(g.reshape(NCHIP * rows, cols) if SHARD_AXIS[n] == 1
                              else g.transpose(1, 0, 2).reshape(rows, NCHIP * cols))
        return out

    Wb = full_matrices(EARLY, _all_gather("gather_mixer", shards(EARLY), me))
    vec = _all_gather("gather_vectors", [_halves(_flat([w[n] for n in SMALL_SHARDED]), 8)], me)[0].reshape(NCHIP, -1)
    parts = [_unpack(vec[k], SMALL_SHARDED, shard_shape) for k in range(NCHIP)]
    P = {n: jnp.concatenate([parts[k][n] for k in range(NCHIP)], axis=SHARD_AXIS[n]) for n in SMALL_SHARDED}
    P.update({n: w[n] for n in REPLICATED})
    late_bufs = [_own_slot(v_, me) for v_ in shards(LATE)]
    state = {}

    def by_halves(G, names):
        return [g.reshape(NCHIP, 2, g.shape[1] // 2, g.shape[2]) for n in names for g in G[n]]

    def bwd_carry(G):
        ps = by_halves(G, LATE + ['w_out_e'])
        got = _sibling_halves("rs_sibling_halves_late", ps)
        state['qs'] = [_add_own_half(f"rs_add_pair_late{a}", p, g, c_arr, BF) for a, (p, g) in enumerate(zip(ps, got))]
        return _carry_exchange(state['qs'])

    hooks = dict(fwd_carry=lambda: _carry_gather(late_bufs),
                 fwd_done=lambda arrived: full_matrices(LATE, _gather_forward("gather_forward", arrived)),
                 bwd_carry=bwd_carry, bwd_done=lambda arrived: state.update(got=arrived))
    loss_part, grad_x, G = _local_step(x, mem, loss_target, Wb, P, hooks)
    loss = lax.psum(loss_part, ("x", "y", "c"))

    def layers(n):
        return G[n] if isinstance(G[n], list) else ([G[n]] if G[n].ndim == 1 else [G[n][l] for l in range(G[n].shape[0])])

    rep = _flat([a for n in REPLICATED for a in layers(n)])
    quarter = -(-rep.size // (NCHIP * 2 * 8 * COLS)) * (2 * 8 * COLS)
    rep = jnp.pad(rep, (0, NCHIP * quarter - rep.size)).reshape(NCHIP, quarter)
    segs = [[_chip_block(a, SHARD_AXIS[n] - 1, k).reshape(-1) for n in SMALL_SHARDED for a in layers(n)] + [rep[k]]
            for k in range(NCHIP)]
    size = sum(piece.size for piece in segs[0])
    total = -(-size // (2 * 8 * COLS)) * (2 * 8 * COLS)
    p_small = jnp.concatenate([piece for seg in segs for piece in seg + [jnp.zeros((total - size,), F32)]])
    p_small = p_small.reshape(NCHIP, 2, total // (2 * COLS), COLS)
    hs = [_add_chips(f"rs_add_chips_late{a}", q, g, idx_arr) for a, (q, g) in enumerate(zip(state['qs'], state['got']))]
    ps = by_halves(G, ['w_in_e']) + [p_small]
    got = _sibling_halves("rs_sibling_halves_last", ps)
    qs = [_add_own_half(f"rs_add_pair_last{a}", p, g, c_arr, p.dtype) for a, (p, g) in enumerate(zip(ps, got))]
    got = _chip_exchange("rs_chip_exchange_last", qs)
    hs += [_add_chips(f"rs_add_chips_last{a}", q, g, idx_arr) for a, (q, g) in enumerate(zip(qs, got))]
    red = _sibling_swap("rs_sibling_swap", hs)
    mine, at = {}, 0
    for n in LATE + ['w_out_e', 'w_in_e']:
        nl = shard_shape[n][0]
        mine[n] = jnp.stack([r.reshape(shard_shape[n][1:]) for r in red[at:at + nl]])
        at += nl
    red_small = red[-1].reshape(-1)
    mine.update(_unpack(red_small, SMALL_SHARDED, shard_shape))
    off = sum(math.prod(shard_shape[n]) for n in SMALL_SHARDED)
    rep_all = _all_gather("gather_replicated_grads",
                          [red_small[off:off + quarter].reshape(2, quarter // (2 * COLS), COLS)], me)[0]
    mine.update(_unpack(rep_all.reshape(-1), REPLICATED, shard_shape))

    grads, deltas, new_m, new_v = [], [], [], []
    for n in WEIGHTS:
        d, nm, nv = _adamw("adamw_" + n, w[n], mine[n], m[n], v[n])
        grads.append(mine[n])
        deltas.append(d)
        new_m.append(nm)
        new_v.append(nv)
    return (loss, grad_x, *grads, *deltas, *new_m, *new_v)
```

```python
import functools
import math

import jax
import jax.numpy as jnp
from jax import lax
from jax.experimental import pallas as pl
from jax.experimental.pallas import tpu as pltpu

F32 = jnp.float32
BF = jnp.bfloat16
MESH = pl.DeviceIdType.MESH

D = 1024
FOXW = 512
HD = 64
GW = 512
CH = 128
NG = 8
FF = 2816
NMEM = 256
XH = 4
XD = 256
CK = 31
HALO = 32
EPS = 1e-6
IN_W = 2568
IN_WP = 2688
VMEM_LIMIT = 56 * 1024 * 1024

ADAM_LR, ADAM_B1, ADAM_B2, ADAM_EPS, ADAM_WD, ADAM_STEP = 0.001, 0.9, 0.999, 1e-08, 0.01, 10

WEIGHTS = ['mix_norm_e', 'w_in_e', 'fox_f_bias', 'gmlp_ln_g', 'gmlp_ln_b', 'gmlp_w_s', 'gmlp_b_s', 'w_out_e',
           'mix_norm_o', 'conv_w_in', 'conv_b_in', 'conv_dw_w', 'conv_dw_b', 'conv_ln_g', 'conv_ln_b',
           'conv_w_out', 'conv_b_out', 'xa_norm', 'mem_norm', 'xa_wq', 'xa_wkv', 'xa_wo', 'ffn_norm',
           'ffn_w_gu', 'ffn_w_down', 'final_norm']
SHARD_AXIS = {'mix_norm_e': None, 'w_in_e': 2, 'fox_f_bias': None, 'gmlp_ln_g': None, 'gmlp_ln_b': None,
              'gmlp_w_s': None, 'gmlp_b_s': None, 'w_out_e': 1, 'mix_norm_o': 1, 'conv_w_in': 2, 'conv_b_in': 1,
              'conv_dw_w': 2, 'conv_dw_b': 1, 'conv_ln_g': 1, 'conv_ln_b': 1, 'conv_w_out': 1, 'conv_b_out': 1,
              'xa_norm': None, 'mem_norm': None, 'xa_wq': 1, 'xa_wkv': 2, 'xa_wo': 1, 'ffn_norm': None,
              'ffn_w_gu': 2, 'ffn_w_down': 1, 'final_norm': None}
BIG = ['w_in_e', 'w_out_e', 'conv_w_in', 'conv_w_out', 'xa_wq', 'xa_wkv', 'xa_wo', 'ffn_w_gu', 'ffn_w_down']


def _pcall(body, **kw):
    return pl.pallas_call(body, **kw)


def _params(sem=None, **kw):
    return pltpu.CompilerParams(dimension_semantics=sem, vmem_limit_bytes=VMEM_LIMIT, **kw)


def _dot(a, b, dims):
    dn = {'nn': (((1,), (0,)), ((), ())), 'nt': (((1,), (1,)), ((), ())), 'tn': (((0,), (0,)), ((), ()))}[dims]
    return lax.dot_general(a, b, dn, preferred_element_type=F32)


def _sigmoid(x):
    return 1.0 / (1.0 + jnp.exp(-x))


def _rms_stats(xv):
    return lax.rsqrt(jnp.mean(xv * xv, axis=-1, keepdims=True) + EPS)


def _rms_bwd(xv, gain, dh):
    r = _rms_stats(xv)
    t = dh * gain
    dx = r * t - xv * (r * r * r * jnp.mean(t * xv, axis=-1, keepdims=True))
    return dx, dh * xv * r


def _fused_mm(name, *, dims, M, N, bm, bn, groups, epi, outs, x=None, gain=None, tiles=(), rows=(),
              h_out=False, reds=(), carry=None):
    bm = min(bm, M)
    nI, nJ = M // bm, N // bn
    assert nI * bm == M and nJ * bn == N
    assert not reds or nJ == 1
    arrays, specs = [], []

    def add(arr, spec):
        arrays.append(arr)
        specs.append(spec)
        return len(arrays) - 1

    if x is not None:
        K0 = x.shape[1]
        add(x, pl.BlockSpec((bm, K0), lambda i, j: (i, 0)))
        add(gain, pl.BlockSpec((1, K0), lambda i, j: (0, 0)))
    plan = []
    for grp in groups:
        g = []
        for p in grp:
            ai = None
            if p['A'] is not None:
                ai = add(p['A'], pl.BlockSpec((bm, p['Ka']), lambda i, j, o=p.get('acoff', 0): (i, o)))
            ro, co = p.get('roff', 0), p.get('coff', 0)
            if dims == 'nn':
                bi = add(p['B'], pl.BlockSpec((p['Ka'], bn), lambda i, j, ro=ro, co=co: (ro, j + co)))
            else:
                bi = add(p['B'], pl.BlockSpec((bn, p['Ka']), lambda i, j, ro=ro, co=co: (j + ro, co)))
            g.append((ai, bi))
        plan.append(g)
    tile_idx = [add(a, pl.BlockSpec((bm, bn), lambda i, j, o=o: (i, j + o))) for a, o in tiles]
    row_idx = [add(a, pl.BlockSpec((1, bn), lambda i, j, o=o: (0, j + o))) for a, o in rows]
    n_in = len(arrays)

    out_shape = [jax.ShapeDtypeStruct((M, N), dt) for dt in outs]
    out_specs = [pl.BlockSpec((bm, bn), lambda i, j: (i, j)) for _ in outs]
    if h_out:
        out_shape.append(jax.ShapeDtypeStruct((M, x.shape[1]), BF))
        out_specs.append(pl.BlockSpec((bm, x.shape[1]), lambda i, j: (i, 0)))
    for shp in reds:
        out_shape.append(jax.ShapeDtypeStruct(shp, F32))
        out_specs.append(pl.BlockSpec(shp, lambda i, j: (0, 0)))
    n_main = len(outs)
    scratch = [pltpu.VMEM((bm, x.shape[1]), BF)] if x is not None else []

    def body(*refs):
        ins, out_refs, scr = refs[:n_in], refs[n_in:n_in + len(out_shape)], refs[n_in + len(out_shape):]
        i, j = pl.program_id(0), pl.program_id(1)
        if x is not None:
            hn_ref = scr[0]

            @pl.when(j == 0)
            def _():
                xv = ins[0][...]
                hn = (xv * _rms_stats(xv) * ins[1][...]).astype(BF)
                hn_ref[...] = hn
                if h_out:
                    out_refs[n_main][...] = hn

        accs = []
        for g in plan:
            acc = None
            for ai, bi in g:
                a = hn_ref[...] if ai is None else ins[ai][...]
                if a.dtype != BF:
                    a = a.astype(BF)
                d = _dot(a, ins[bi][...], dims)
                acc = d if acc is None else acc + d
            accs.append(acc)
        out_vals, red_vals = epi(accs, [ins[t][...] for t in tile_idx], [ins[r][...] for r in row_idx])
        for r, v in zip(out_refs[:n_main], out_vals):
            r[...] = v.astype(r.dtype)
        if reds:
            red_refs = out_refs[n_main + (1 if h_out else 0):]

            @pl.when(i == 0)
            def _():
                for r in red_refs:
                    r[...] = jnp.zeros(r.shape, F32)

            for r, v in zip(red_refs, red_vals):
                r[...] += v

    res, arrived = _carried_call(body, name=name, grid=(nI, nJ), in_specs=specs, out_specs=out_specs,
                                 out_shape=out_shape, scratch_shapes=scratch, operands=arrays, carry=carry)
    return res if carry is None else (res, arrived)


def _epi_plain(accs, tiles, rows):
    return [accs[0]], []


def _epi_resid(accs, tiles, rows):
    y = tiles[0] + accs[0]
    if rows:
        y = y + rows[0]
    return [y], []


def _epi_swiglu(accs, tiles, rows):
    g, u = accs
    return [g, u, g * _sigmoid(g) * u], []


def _epi_glu(accs, tiles, rows):
    a, g = accs[0] + rows[0], accs[1] + rows[1]
    return [a, g, a * _sigmoid(g)], []


def _epi_swiglu_bwd(accs, tiles, rows):
    da = accs[0]
    g, u = tiles[0].astype(F32), tiles[1].astype(F32)
    sg = _sigmoid(g)
    return [da * u * (sg * (1.0 + g * (1.0 - sg))), da * (g * sg)], []


def _epi_rms_bwd(accs, tiles, rows):
    dx, dgr = _rms_bwd(tiles[0], rows[0], accs[0])
    return [tiles[1] + dx], [jnp.sum(dgr, axis=0, keepdims=True)]


def _epi_rms_gain_only(accs, tiles, rows):
    _, dgr = _rms_bwd(tiles[0], rows[0], accs[0])
    return [], [jnp.sum(dgr, axis=0, keepdims=True)]


def _norm_mm(name, x, gain, W, *, N, coff=0, bn, out_dtype, bm=512, h_out=False):
    return _fused_mm(name, dims='nn', M=x.shape[0], N=N, bm=bm, bn=bn, x=x, gain=gain,
                     groups=[[dict(A=None, Ka=x.shape[1], B=W, coff=coff)]], epi=_epi_plain, outs=[out_dtype],
                     h_out=h_out)


def _mm_resid(name, pairs, resid, bias=None, bm=512):
    M = resid.shape[0]
    return _fused_mm(name, dims='nn', M=M, N=D, bm=bm, bn=D, groups=[pairs], epi=_epi_resid, outs=[F32],
                     tiles=[(resid, 0)], rows=[(bias, 0)] if bias is not None else [])[0]


def _mm_nt_plain(name, dy, W, bm=512):
    return _fused_mm(name, dims='nt', M=dy.shape[0], N=W.shape[0], bm=bm, bn=W.shape[0],
                     groups=[[dict(A=dy, Ka=dy.shape[1], B=W)]], epi=_epi_plain, outs=[BF])[0]


def _mm_nt_rms_bwd(name, pairs, x, gain, dx_in, bm=256, carry=None):
    out = _fused_mm(name, dims='nt', M=x.shape[0], N=D, bm=bm, bn=D, groups=[pairs], epi=_epi_rms_bwd,
                    outs=[F32], tiles=[(x, 0), (dx_in, 0)], rows=[(gain, 0)], reds=[(1, D)], carry=carry)
    if carry is None:
        return out[0], out[1]
    return out[0][0], out[0][1], out[1]


def _mm_tn(name, A, G, bk=1024, parts=1):
    T, Ka, Kg = A.shape[0], A.shape[1], G.shape[1]
    w = Kg // parts
    bm = Ka if Ka <= 1024 else Ka // 2
    bn = w if w <= 1408 else w // 2
    bk = min(bk, T)
    per = w // bn
    nI, nJ, nK = Ka // bm, Kg // bn, T // bk

    def body(a_ref, g_ref, o_ref, acc):
        k = pl.program_id(2)

        @pl.when(k == 0)
        def _():
            acc[...] = jnp.zeros(acc.shape, F32)

        acc[...] += _dot(a_ref[...].astype(BF), g_ref[...].astype(BF), 'tn')

        @pl.when(k == nK - 1)
        def _():
            o_ref[...] = acc[...].astype(BF)

    return _pcall(body, name=name, grid=(nI, nJ, nK),
                  in_specs=[pl.BlockSpec((bk, bm), lambda i, j, k: (k, i)),
                            pl.BlockSpec((bk, bn), lambda i, j, k: (k, j))],
                  out_specs=pl.BlockSpec((None, bm, bn), lambda i, j, k: (j // per, i, j % per)),
                  out_shape=jax.ShapeDtypeStruct((parts, Ka, w), BF),
                  scratch_shapes=[pltpu.VMEM((bm, bn), F32)],
                  compiler_params=_params(("arbitrary", "arbitrary", "arbitrary")))(A, G)


def _colsum(name, a, bt=512):
    M, N = a.shape
    bt = min(bt, M)

    def body(a_ref, o_ref):
        @pl.when(pl.program_id(0) == 0)
        def _():
            o_ref[...] = jnp.zeros(o_ref.shape, F32)

        o_ref[...] += jnp.sum(a_ref[...].astype(F32), axis=0, keepdims=True)

    return _pcall(body, name=name, grid=(M // bt,), in_specs=[pl.BlockSpec((bt, N), lambda i: (i, 0))],
                  out_specs=pl.BlockSpec((1, N), lambda i: (0, 0)), out_shape=jax.ShapeDtypeStruct((1, N), F32),
                  compiler_params=_params(("arbitrary",)))(a)


def _cumsum_rows(v):
    T = v.shape[0]
    row = lax.broadcasted_iota(jnp.int32, v.shape, 0)
    s = 1
    while s < T:
        v = v + jnp.where(row >= s, pltpu.roll(v, s, 0), 0.0)
        s *= 2
    return v


def _log_sigmoid(z):
    return jnp.minimum(z, 0.0) - jnp.log(1.0 + jnp.exp(-jnp.abs(z)))


def _fox_gate_fwd(fl, fbias, B, T):
    def body(fl_ref, b_ref, o_ref):
        o_ref[...] = _cumsum_rows(_log_sigmoid(fl_ref[...] + b_ref[...]))

    return _pcall(body, name="fox_gate_fwd", grid=(B,),
                  in_specs=[pl.BlockSpec((T, 128), lambda b: (b, 0)), pl.BlockSpec((1, 128), lambda b: (0, 0))],
                  out_specs=pl.BlockSpec((T, 128), lambda b: (b, 0)),
                  out_shape=jax.ShapeDtypeStruct((B * T, 128), F32), compiler_params=_params(("arbitrary",)))(fl, fbias)


def _fox_gate_bwd(fl, fbias, dcq, dck, B, T):
    def body(fl_ref, b_ref, dcq_ref, dck_ref, dfl_ref, db_ref):
        dc = dcq_ref[...] + dck_ref[...]
        rev = jnp.sum(dc, axis=0, keepdims=True) - _cumsum_rows(dc) + dc
        dfl = rev * _sigmoid(-(fl_ref[...] + b_ref[...]))
        dfl_ref[...] = dfl

        @pl.when(pl.program_id(0) == 0)
        def _():
            db_ref[...] = jnp.zeros(db_ref.shape, F32)

        db_ref[...] += jnp.sum(dfl, axis=0, keepdims=True)

    return _pcall(body, name="fox_gate_bwd", grid=(B,),
                  in_specs=[pl.BlockSpec((T, 128), lambda b: (b, 0)), pl.BlockSpec((1, 128), lambda b: (0, 0)),
                            pl.BlockSpec((T, 128), lambda b: (b, 0)), pl.BlockSpec((T, 128), lambda b: (b, 0))],
                  out_specs=[pl.BlockSpec((T, 128), lambda b: (b, 0)), pl.BlockSpec((1, 128), lambda b: (0, 0))],
                  out_shape=[jax.ShapeDtypeStruct((B * T, 128), F32), jax.ShapeDtypeStruct((1, 128), F32)],
                  compiler_params=_params(("arbitrary",)))(fl, fbias, dcq, dck)


def _carried_call(body, *, name, grid, in_specs, out_specs, out_shape, scratch_shapes, operands, carry):
    if carry is None:
        return _pcall(body, name=name, grid=grid, in_specs=in_specs, out_specs=out_specs, out_shape=out_shape,
                      scratch_shapes=scratch_shapes, compiler_params=_params(("arbitrary",) * len(grid)))(*operands), []
    n, n_in, n_out, n_scr = len(carry['inputs']), len(in_specs), len(out_specs), len(scratch_shapes)

    def wrapped(*refs):
        ins, cin = refs[:n_in], refs[n_in:n_in + n]
        outs, cout = refs[n_in + n:n_in + n + n_out], refs[n_in + n + n_out:n_in + 2 * n + n_out]
        scr = refs[n_in + 2 * n + n_out:]
        send_sems, recv_sems = scr[n_scr:]
        ids = [pl.program_id(d) for d in range(len(grid))]
        first = functools.reduce(jnp.logical_and, [i == 0 for i in ids])
        last = functools.reduce(jnp.logical_and, [i == g - 1 for i, g in zip(ids, grid)])

        @pl.when(first)
        def _():
            for cp in carry['copies'](cin, cout, send_sems, recv_sems):
                cp.start()

        body(*ins, *outs, *scr[:n_scr])

        @pl.when(last)
        def _():
            for cp in carry['copies'](cin, cout, send_sems, recv_sems):
                cp.wait()

    aliases = {n_in + a: n_out + a for a in range(n)} if carry['in_place'] else {}
    res = _pcall(wrapped, name=name, grid=grid, in_specs=list(in_specs) + [ANY] * n,
                 out_specs=list(out_specs) + [ANY] * n, out_shape=list(out_shape) + carry['out_shape'],
                 scratch_shapes=list(scratch_shapes) + _sem_pairs(carry['nsem']), input_output_aliases=aliases,
                 compiler_params=_params(("arbitrary",) * len(grid)))(*operands, *carry['inputs'])
    return res[:n_out], res[n_out:]


def _carry_gather(bufs):
    n = len(bufs)

    def copies(in_refs, out_refs, send_sems, recv_sems):
        x, y, c = _coords()
        cps = []
        for a in range(n):
            blk = out_refs[a].at[2 * x + y, c]
            cps += [pltpu.make_async_remote_copy(src_ref=blk, dst_ref=blk, send_sem=send_sems.at[3 * a + j],
                                                 recv_sem=recv_sems.at[3 * a + j], device_id=(cx, cy, c),
                                                 device_id_type=MESH) for j, (cx, cy) in enumerate(_other_chips(x, y))]
        return cps

    return dict(inputs=bufs, out_shape=[jax.ShapeDtypeStruct(b.shape, b.dtype) for b in bufs], in_place=True,
                nsem=3 * n, copies=copies)


def _carry_exchange(qs):
    n = len(qs)

    def copies(in_refs, out_refs, send_sems, recv_sems):
        x, y, c = _coords()
        return [pltpu.make_async_remote_copy(src_ref=in_refs[a].at[2 * cx + cy], dst_ref=out_refs[a].at[j],
                                             send_sem=send_sems.at[3 * a + j], recv_sem=recv_sems.at[3 * a + j],
                                             device_id=(cx, cy, c), device_id_type=MESH)
                for a in range(n) for j, (cx, cy) in enumerate(_other_chips(x, y))]

    return dict(inputs=qs, out_shape=[jax.ShapeDtypeStruct((3,) + q.shape[1:], q.dtype) for q in qs], in_place=False,
                nsem=3 * n, copies=copies)


NEG = -1e30
QSUB = 1


def _fox_fwd(qkv, cum4, B, T, qoff, bq, bk, carry=None):
    nq, nkb = T // bq, T // bk
    N = B * T

    def body(q_ref, k_ref, v_ref, cum_ref, o_ref, lse_ref):
        hp, i = pl.program_id(1), pl.program_id(2)
        sq = bq // QSUB
        lane = lax.broadcasted_iota(jnp.int32, (sq, 128), 1)
        heads = [slice(e * HD, (e + 1) * HD) for e in range(2)]
        chains = [(e, sl, slice(r * sq, (r + 1) * sq)) for e, sl in enumerate(heads) for r in range(QSUB)]
        qs = [q_ref[rows, sl] * 0.125 for _, sl, rows in chains]

        def block(j, carry, diagonal):
            ks = pl.multiple_of(j * bk, bk)
            out = []
            for n, (e, sl, rows) in enumerate(chains):
                m, l, acc = carry[n]
                s = _dot(qs[n], k_ref[pl.ds(ks, bk), sl], 'nt') - cum_ref[0, 2 * hp + e, pl.ds(j, 1), :]
                if diagonal:
                    keep = (lax.broadcasted_iota(jnp.int32, (sq, bk), 0) + rows.start
                            >= lax.broadcasted_iota(jnp.int32, (sq, bk), 1))
                    s = jnp.where(keep, s, NEG)
                m_new = jnp.maximum(m, jnp.max(s, axis=1, keepdims=True))
                p = jnp.exp(s - m_new)
                alpha = jnp.exp(m - m_new)
                l = alpha * l + jnp.sum(p, axis=1, keepdims=True)
                acc = alpha * acc + _dot(p.astype(BF), v_ref[pl.ds(ks, bk), sl], 'nn')
                out.append((m_new, l, acc))
            return tuple(out)

        init = tuple((jnp.full((sq, 1), NEG, F32), jnp.zeros((sq, 1), F32), jnp.zeros((sq, HD), F32)) for _ in chains)
        carry = lax.fori_loop(0, i, lambda j, c: block(j, c, False), init)
        carry = block(i, carry, True)
        for r in range(QSUB):
            lse_tile = jnp.zeros((sq, 128), F32)
            for n, (e, sl, rows) in enumerate(chains):
                if rows.start == r * sq:
                    m, l, acc = carry[n]
                    o_ref[rows, sl] = (acc / l).astype(BF)
                    lse_tile = jnp.where(lane == e, m + jnp.log(l), lse_tile)
            lse_ref[r * sq:(r + 1) * sq, :] = lse_tile

    return _carried_call(body, name="fox_fwd", grid=(B, 4, nq),
                         in_specs=[pl.BlockSpec((bq, 128), lambda b, h, i: (b * nq + i, qoff + h)),
                                   pl.BlockSpec((T, 128), lambda b, h, i: (b, qoff + 4 + h)),
                                   pl.BlockSpec((T, 128), lambda b, h, i: (b, qoff + 8 + h)),
                                   pl.BlockSpec((1, NG, nkb, bk), lambda b, h, i: (b, 0, 0, 0))],
                         out_specs=[pl.BlockSpec((bq, 128), lambda b, h, i: (b * nq + i, h)),
                                    pl.BlockSpec((bq, 128), lambda b, h, i: (b * nq + i, h))],
                         out_shape=[jax.ShapeDtypeStruct((N, FOXW), BF), jax.ShapeDtypeStruct((N, FOXW), F32)],
                         scratch_shapes=[], operands=(qkv, qkv, qkv, cum4), carry=carry)


def _fox_bwd(qkv, cum4, o, lse, dcat, B, T, qoff, bq, bk, carry=None):
    nq, nkb = T // bq, T // bk
    N = B * T

    def body(q_ref, k_ref, v_ref, cum_ref, o_ref, lse_ref, do_ref, dq_ref, dk_ref, dv_ref, dcum_ref, dcq_ref,
             dq_acc, dl_ref, rs_ref):
        hp = pl.program_id(1)
        heads = [slice(e * HD, (e + 1) * HD) for e in range(2)]
        keep = lax.broadcasted_iota(jnp.int32, (bq, bk), 0) >= lax.broadcasted_iota(jnp.int32, (bq, bk), 1)
        dcq_ref[...] = jnp.zeros(dcq_ref.shape, F32)
        dq_acc[...] = jnp.zeros(dq_acc.shape, F32)
        rs_ref[...] = jnp.zeros(rs_ref.shape, F32)
        for e, sl in enumerate(heads):
            dl_ref[e] = jnp.sum(do_ref[:, sl].astype(F32) * o_ref[:, sl].astype(F32), axis=1, keepdims=True)
        for j in range(nkb):
            krows = slice(j * bk, (j + 1) * bk)

            def tile(i, carry, diagonal):
                qs = i * bq if diagonal else pl.multiple_of(i * bq, bq)
                out = []
                for e, sl in enumerate(heads):
                    dk_a, dv_a, cs = carry[e]
                    q, k = q_ref[pl.ds(qs, bq), sl], k_ref[krows, sl]
                    do = do_ref[pl.ds(qs, bq), sl]
                    s = _dot(q, k, 'nt') * 0.125 - cum_ref[0, 2 * hp + e, j:j + 1, :]
                    p = jnp.exp(s - lse_ref[pl.ds(qs, bq), e:e + 1])
                    if diagonal:
                        p = jnp.where(keep, p, 0.0)
                    dv_a = dv_a + _dot(p.astype(BF), do, 'tn')
                    ds = p * (_dot(do, v_ref[krows, sl], 'nt') - dl_ref[e, pl.ds(qs, bq), :])
                    cs = cs + jnp.sum(ds, axis=0, keepdims=True)
                    rs_ref[e, pl.ds(qs, bq), :] += jnp.sum(ds, axis=1, keepdims=True)
                    dsb = ds.astype(BF)
                    dk_a = dk_a + _dot(dsb, q, 'tn')
                    dq_acc[e, pl.ds(qs, bq), :] += _dot(dsb, k, 'nn')
                    out.append((dk_a, dv_a, cs))
                return tuple(out)

            init = tuple((jnp.zeros((bk, HD), F32), jnp.zeros((bk, HD), F32), jnp.zeros((1, bk), F32)) for _ in heads)
            carry = lax.fori_loop(j + 1, nq, lambda i, c: tile(i, c, False), tile(j, init, True))
            for e, sl in enumerate(heads):
                dk_a, dv_a, cs = carry[e]
                dk_ref[krows, sl] = (dk_a * 0.125).astype(BF)
                dv_ref[krows, sl] = dv_a.astype(BF)
                dcum_ref[0, e, j:j + 1, :] = -cs
        for e, sl in enumerate(heads):
            dq_ref[:, sl] = (dq_acc[e] * 0.125).astype(BF)
            dcq_ref[:, e:e + 1] = rs_ref[e]

    seq = lambda off: pl.BlockSpec((T, 128), lambda b, h, off=off: (b, off + h))
    return _carried_call(body, name="fox_bwd", grid=(B, 4),
                         in_specs=[seq(qoff), seq(qoff + 4), seq(qoff + 8),
                                   pl.BlockSpec((1, NG, nkb, bk), lambda b, h: (b, 0, 0, 0)),
                                   seq(0), seq(0), seq(0)],
                         out_specs=[seq(0), seq(0), seq(0),
                                    pl.BlockSpec((1, 2, nkb, bk), lambda b, h: (b, h, 0, 0)), seq(0)],
                         out_shape=[jax.ShapeDtypeStruct((N, FOXW), BF)] * 3
                         + [jax.ShapeDtypeStruct((B, NG, nkb, bk), F32), jax.ShapeDtypeStruct((N, FOXW), F32)],
                         scratch_shapes=[pltpu.VMEM((2, T, HD), F32), pltpu.VMEM((2, T, 1), F32),
                                         pltpu.VMEM((2, T, 1), F32)],
                         operands=(qkv, qkv, qkv, cum4, o, lse, dcat), carry=carry)


_GC = math.sqrt(2.0 / math.pi)
_GA = 0.044715


def _gelu(z):
    return 0.5 * z * (1.0 + jnp.tanh(_GC * (z + _GA * z * z * z)))


def _gelu_grad(z):
    t = jnp.tanh(_GC * (z + _GA * z * z * z))
    return 0.5 * (1.0 + t) + 0.5 * z * (1.0 - t * t) * (_GC * (1.0 + 3.0 * _GA * z * z))


def _gmlp_common(z, lng, lnb):
    zg = _gelu(z)
    u, vg = zg[:, :GW], zg[:, GW:]
    mu = jnp.mean(vg, axis=-1, keepdims=True)
    xc = vg - mu
    rstd = lax.rsqrt(jnp.mean(xc * xc, axis=-1, keepdims=True) + EPS)
    xhat = xc * rstd
    return u, xhat, rstd, xhat * lng + lnb


def _tril_w(ws_ref):
    tri = lax.broadcasted_iota(jnp.int32, (CH, CH), 0) >= lax.broadcasted_iota(jnp.int32, (CH, CH), 1)
    return [jnp.where(tri, ws_ref[g], 0.0).astype(BF) for g in range(NG)], tri


def _split_pair(vp):
    lane = lax.broadcasted_iota(jnp.int32, vp.shape, 1)
    zero = jnp.zeros(vp.shape, vp.dtype)
    return jnp.concatenate([jnp.where(lane < HD, vp, zero), jnp.where(lane >= HD, vp, zero)], axis=0)


def _gmlp_mix(wt, vgn_b):
    outs = []
    for p in range(NG // 2):
        wcat = jnp.concatenate([wt[2 * p], wt[2 * p + 1]], axis=1)
        outs.append(_dot(wcat, _split_pair(vgn_b[:, 128 * p:128 * (p + 1)]), 'nn'))
    return jnp.concatenate(outs, axis=1)


def _gmlp_fwd(z, lng, lnb, ws, bfull, bt):
    N = z.shape[0]

    def body(z_ref, lng_ref, lnb_ref, ws_ref, bf_ref, o_ref):
        wt, _ = _tril_w(ws_ref)
        for c in range(bt // CH):
            rows = slice(c * CH, (c + 1) * CH)
            u, _, _, vgn = _gmlp_common(z_ref[rows, :], lng_ref[...], lnb_ref[...])
            mixed = _gmlp_mix(wt, vgn.astype(BF)) + bf_ref[...]
            o_ref[rows, :] = (u * mixed).astype(BF)

    full = lambda shp: pl.BlockSpec(shp, lambda i: (0,) * len(shp))
    return _pcall(body, name="gmlp_fwd", grid=(N // bt,),
                  in_specs=[pl.BlockSpec((bt, D), lambda i: (i, 0)), full((1, GW)), full((1, GW)),
                            full((NG, CH, CH)), full((CH, GW))],
                  out_specs=pl.BlockSpec((bt, GW), lambda i: (i, 0)), out_shape=jax.ShapeDtypeStruct((N, GW), BF),
                  compiler_params=_params(("arbitrary",)))(z, lng, lnb, ws, bfull)


def _gmlp_bwd(z, dcat, lng, lnb, ws, bfull, bt):
    N = z.shape[0]

    def body(z_ref, da_ref, lng_ref, lnb_ref, ws_ref, bf_ref, dz_ref, dg_ref, db_ref, dws_ref, dbf_ref):
        @pl.when(pl.program_id(0) == 0)
        def _():
            for r in (dg_ref, db_ref, dws_ref, dbf_ref):
                r[...] = jnp.zeros(r.shape, F32)

        wt, tri = _tril_w(ws_ref)
        lane = lax.broadcasted_iota(jnp.int32, (CH, 128), 1)
        for c in range(bt // CH):
            rows = slice(c * CH, (c + 1) * CH)
            zc = z_ref[rows, :]
            u, xhat, rstd, vgn = _gmlp_common(zc, lng_ref[...], lnb_ref[...])
            vgn_b = vgn.astype(BF)
            mixed = _gmlp_mix(wt, vgn_b) + bf_ref[...]
            da = da_ref[rows, :].astype(F32)
            dmix = da * u
            du = da * mixed
            dbf_ref[...] += dmix
            dvs = []
            for p in range(NG // 2):
                cols = slice(128 * p, 128 * (p + 1))
                dmp = dmix[:, cols].astype(BF)
                dwp = _dot(_split_pair(dmp), vgn_b[:, cols], 'nt')
                dws_ref[2 * p] += jnp.where(tri, dwp[:CH], 0.0)
                dws_ref[2 * p + 1] += jnp.where(tri, dwp[CH:], 0.0)
                dvs.append(jnp.where(lane < HD, _dot(wt[2 * p], dmp, 'tn'), _dot(wt[2 * p + 1], dmp, 'tn')))
            dvgn = jnp.concatenate(dvs, axis=1)
            dg_ref[...] += jnp.sum(dvgn * xhat, axis=0, keepdims=True)
            db_ref[...] += jnp.sum(dvgn, axis=0, keepdims=True)
            dxh = dvgn * lng_ref[...]
            dvg = rstd * (dxh - jnp.mean(dxh, axis=-1, keepdims=True)
                          - xhat * jnp.mean(dxh * xhat, axis=-1, keepdims=True))
            dz_ref[rows, :] = (jnp.concatenate([du, dvg], axis=1) * _gelu_grad(zc)).astype(BF)

    full = lambda shp: pl.BlockSpec(shp, lambda i: (0,) * len(shp))
    return _pcall(body, name="gmlp_bwd", grid=(N // bt,),
                  in_specs=[pl.BlockSpec((bt, D), lambda i: (i, 0)), pl.BlockSpec((bt, GW), lambda i: (i, 1)),
                            full((1, GW)), full((1, GW)), full((NG, CH, CH)), full((CH, GW))],
                  out_specs=[pl.BlockSpec((bt, D), lambda i: (i, 0)), full((1, GW)), full((1, GW)),
                             full((NG, CH, CH)), full((CH, GW))],
                  out_shape=[jax.ShapeDtypeStruct((N, D), BF), jax.ShapeDtypeStruct((1, GW), F32),
                             jax.ShapeDtypeStruct((1, GW), F32), jax.ShapeDtypeStruct((NG, CH, CH), F32),
                             jax.ShapeDtypeStruct((CH, GW), F32)],
                  compiler_params=_params(("arbitrary",)))(z, dcat, lng, lnb, ws, bfull)


def _group_sum(name, a):
    def body(a_ref, o_ref):
        lane = lax.broadcasted_iota(jnp.int32, (CH, 128), 1)
        out = jnp.zeros((CH, 128), F32)
        for g in range(NG):
            out = jnp.where(lane == g, jnp.sum(a_ref[:, g * HD:(g + 1) * HD], axis=1, keepdims=True), out)
        o_ref[...] = out

    return _pcall(body, name=name, out_shape=jax.ShapeDtypeStruct((CH, 128), F32))(a)


def _xattn_softmax(q_h, k_h):
    s = _dot(q_h, k_h, 'nt') * (XD ** -0.5)
    p = jnp.exp(s - jnp.max(s, axis=1, keepdims=True))
    return p / jnp.sum(p, axis=1, keepdims=True)


def _xattn_fwd(name, q, kv, B, T, bq):
    nq = T // bq

    def body(q_ref, kv_ref, o_ref):
        for h in range(XH):
            cols = slice(h * XD, (h + 1) * XD)
            p = _xattn_softmax(q_ref[:, cols], kv_ref[:, cols])
            o_ref[:, cols] = _dot(p.astype(BF), kv_ref[:, D + h * XD:D + (h + 1) * XD], 'nn').astype(BF)

    return _pcall(body, name=name, grid=(B, nq),
                  in_specs=[pl.BlockSpec((bq, D), lambda b, i: (b * nq + i, 0)),
                            pl.BlockSpec((NMEM, 2 * D), lambda b, i: (b, 0))],
                  out_specs=pl.BlockSpec((bq, D), lambda b, i: (b * nq + i, 0)),
                  out_shape=jax.ShapeDtypeStruct((B * T, D), BF), compiler_params=_params(("arbitrary", "arbitrary")))(q, kv)


def _xattn_bwd(name, q, kv, do, B, T, bq):
    nq = T // bq
    sc = XD ** -0.5

    def body(q_ref, kv_ref, do_ref, dq_ref, dkv_ref):
        @pl.when(pl.program_id(1) == 0)
        def _():
            dkv_ref[...] = jnp.zeros(dkv_ref.shape, F32)

        for h in range(XH):
            cols = slice(h * XD, (h + 1) * XD)
            vcols = slice(D + h * XD, D + (h + 1) * XD)
            qh, kh, doh = q_ref[:, cols], kv_ref[:, cols], do_ref[:, cols]
            p = _xattn_softmax(qh, kh)
            dp = _dot(doh, kv_ref[:, vcols], 'nt')
            ds = p * (dp - jnp.sum(p * dp, axis=1, keepdims=True))
            dsb = ds.astype(BF)
            dq_ref[:, cols] = (_dot(dsb, kh, 'nn') * sc).astype(BF)
            dkv_ref[:, cols] += _dot(dsb, qh, 'tn') * sc
            dkv_ref[:, vcols] += _dot(p.astype(BF), doh, 'tn')

    blk = pl.BlockSpec((bq, D), lambda b, i: (b * nq + i, 0))
    return _pcall(body, name=name, grid=(B, nq),
                  in_specs=[blk, pl.BlockSpec((NMEM, 2 * D), lambda b, i: (b, 0)), blk],
                  out_specs=[blk, pl.BlockSpec((NMEM, 2 * D), lambda b, i: (b, 0))],
                  out_shape=[jax.ShapeDtypeStruct((B * T, D), BF), jax.ShapeDtypeStruct((B * NMEM, 2 * D), F32)],
                  compiler_params=_params(("arbitrary", "arbitrary")))(q, kv, do)


def _ln_stats(v):
    mu = jnp.mean(v, axis=-1, keepdims=True)
    xc = v - mu
    rstd = lax.rsqrt(jnp.mean(xc * xc, axis=-1, keepdims=True) + EPS)
    return xc * rstd, rstd


SUB = 8


LANES = 128
NSTRIP = D // LANES


def _fill_window(win, parts):
    for s in range(NSTRIP):
        for r0, val in parts:
            win[s, r0:r0 + val.shape[0], :] = val[:, s * LANES:(s + 1) * LANES]


def _fill_phases(win, sh, rows):
    for b in range(1, SUB):
        for s in range(NSTRIP):
            sh[b - 1, s] = win[s, b:b + rows, :]


def _fill_taps(w8, w_ref):
    for s in range(NSTRIP):
        for j in range(CK):
            w8[s, SUB * j:SUB * (j + 1), :] = jnp.broadcast_to(w_ref[j:j + 1, s * LANES:(s + 1) * LANES], (SUB, LANES))


def _row_groups(win, sh, s):
    cache = {}

    def get(o, t):
        a, b = divmod(o, SUB)
        key = (b, t + a)
        if key not in cache:
            rows = slice(SUB * (t + a), SUB * (t + a + 1))
            cache[key] = win[s, rows, :] if b == 0 else sh[b - 1, s, rows, :]
        return cache[key]

    return get


def _from_strips(ref):
    return jnp.concatenate([ref[s] for s in range(NSTRIP)], axis=1)


def _sum_groups(name, a):
    R, C = a.shape[0] // SUB, a.shape[1]

    def body(a_ref, o_ref):
        o_ref[...] = jnp.sum(a_ref[...].reshape(R, SUB, C), axis=1)

    return _pcall(body, name=name, out_shape=jax.ShapeDtypeStruct((R, C), F32))(a)


def _conv_fwd(y, w32, wb, lng, lnb, B, T, bt):
    nt = T // bt
    hb = bt // HALO
    prows = bt + HALO - SUB

    def body(y_ref, yp_ref, w_ref, wb_ref, lng_ref, lnb_ref, s_ref, yc_ref, win, sh, w8, out):
        i = pl.program_id(1)
        _fill_window(win, [(0, jnp.where(i > 0, yp_ref[...], 0.0)), (HALO, y_ref[...])])
        _fill_phases(win, sh, prows)

        @pl.when((pl.program_id(0) == 0) & (i == 0))
        def _():
            _fill_taps(w8, w_ref)

        def strip(s, carry):
            get = _row_groups(win, sh, s)
            for t in range(bt // SUB):
                accs = [jnp.zeros((SUB, LANES), F32), jnp.zeros((SUB, LANES), F32)]
                for j in range(CK):
                    accs[j % 2] = accs[j % 2] + w8[s, SUB * j:SUB * (j + 1), :] * get(HALO - (CK - 1) + j, t)
                out[s, SUB * t:SUB * (t + 1), :] = accs[0] + accs[1]
            return carry

        lax.fori_loop(0, NSTRIP, strip, 0)
        acc = _from_strips(out) + wb_ref[...]
        yc_ref[...] = acc
        xhat, _ = _ln_stats(acc)
        ln = xhat * lng_ref[...] + lnb_ref[...]
        s_ref[...] = (ln * _sigmoid(ln)).astype(BF)

    row = lambda n: pl.BlockSpec((n, D), lambda b, i: (0, 0))
    cur = pl.BlockSpec((bt, D), lambda b, i: (b * nt + i, 0))
    return _pcall(body, name="conv_fwd", grid=(B, nt),
                  in_specs=[cur, pl.BlockSpec((HALO, D), lambda b, i: (jnp.maximum((b * nt + i) * hb - 1, 0), 0)),
                            row(HALO), row(1), row(1), row(1)],
                  out_specs=[cur, cur],
                  out_shape=[jax.ShapeDtypeStruct((B * T, D), BF), jax.ShapeDtypeStruct((B * T, D), F32)],
                  scratch_shapes=[pltpu.VMEM((NSTRIP, bt + HALO, LANES), F32),
                                  pltpu.VMEM((SUB - 1, NSTRIP, prows, LANES), F32),
                                  pltpu.VMEM((NSTRIP, HALO * SUB, LANES), F32), pltpu.VMEM((NSTRIP, bt, LANES), F32)],
                  compiler_params=_params(("arbitrary", "arbitrary")))(y, y, w32, wb, lng, lnb)


def _conv_bwd(ds, yc, y, pa, pg, w32, lng, lnb, B, T, bt):
    nt = T // bt
    hb = bt // HALO
    nblk32 = B * T // HALO

    def ln_bwd(dsv, ycv, lng, lnb):
        xhat, rstd = _ln_stats(ycv)
        ln = xhat * lng + lnb
        sg = _sigmoid(ln)
        dln = dsv * (sg * (1.0 + ln * (1.0 - sg)))
        dxh = dln * lng
        dyc = rstd * (dxh - jnp.mean(dxh, axis=-1, keepdims=True)
                      - xhat * jnp.mean(dxh * xhat, axis=-1, keepdims=True))
        return dyc, dln, xhat

    prows = bt + HALO - SUB

    def body(ds_ref, dsn_ref, yc_ref, ycn_ref, y_ref, yp_ref, pa_ref, pg_ref, w_ref, lng_ref, lnb_ref,
             dpa_ref, dpg_ref, dw_ref, dwb_ref, dlng_ref, dlnb_ref, dba_ref, dbg_ref,
             dwin, ywin, dsh, ysh, w8, dy_out, dw_out):
        i = pl.program_id(1)

        @pl.when((pl.program_id(0) == 0) & (i == 0))
        def _():
            for r in (dw_ref, dwb_ref, dlng_ref, dlnb_ref, dba_ref, dbg_ref):
                r[...] = jnp.zeros(r.shape, F32)
            _fill_taps(w8, w_ref)

        lng, lnb = lng_ref[...], lnb_ref[...]
        dyc, dln, xhat = ln_bwd(ds_ref[...].astype(F32), yc_ref[...], lng, lnb)
        dycn, _, _ = ln_bwd(dsn_ref[...].astype(F32), ycn_ref[...], lng, lnb)
        _fill_window(dwin, [(0, dyc), (bt, jnp.where(i < nt - 1, dycn, 0.0))])
        _fill_window(ywin, [(0, jnp.where(i > 0, yp_ref[...], 0.0)), (HALO, y_ref[...])])
        dlng_ref[...] += jnp.sum(dln * xhat, axis=0, keepdims=True)
        dlnb_ref[...] += jnp.sum(dln, axis=0, keepdims=True)
        dwb_ref[...] += jnp.sum(dyc, axis=0, keepdims=True)
        _fill_phases(dwin, dsh, prows)
        _fill_phases(ywin, ysh, prows)

        def strip(s, carry):
            get_d, get_y = _row_groups(dwin, dsh, s), _row_groups(ywin, ysh, s)
            dw_acc = [jnp.zeros((SUB, LANES), F32) for _ in range(CK)]
            for t in range(bt // SUB):
                dyc_g = get_d(0, t)
                dys = [jnp.zeros((SUB, LANES), F32), jnp.zeros((SUB, LANES), F32)]
                for j in range(CK):
                    dys[j % 2] = dys[j % 2] + w8[s, SUB * j:SUB * (j + 1), :] * get_d(CK - 1 - j, t)
                    dw_acc[j] = dw_acc[j] + dyc_g * get_y(HALO - (CK - 1) + j, t)
                dy_out[s, SUB * t:SUB * (t + 1), :] = dys[0] + dys[1]
            for j in range(CK):
                dw_out[s, SUB * j:SUB * (j + 1), :] = dw_acc[j]
            return carry

        lax.fori_loop(0, NSTRIP, strip, 0)
        dw_ref[0:CK * SUB, :] += _from_strips(dw_out)
        dy = _from_strips(dy_out)
        a, g = pa_ref[...].astype(F32), pg_ref[...].astype(F32)
        sg = _sigmoid(g)
        da = dy * sg
        dg = dy * a * sg * (1.0 - sg)
        dpa_ref[...] = da.astype(BF)
        dpg_ref[...] = dg.astype(BF)
        dba_ref[...] += jnp.sum(da, axis=0, keepdims=True)
        dbg_ref[...] += jnp.sum(dg, axis=0, keepdims=True)

    cur = pl.BlockSpec((bt, D), lambda b, i: (b * nt + i, 0))
    nxt = pl.BlockSpec((HALO, D), lambda b, i: (jnp.minimum((b * nt + i + 1) * hb, nblk32 - 1), 0))
    prv = pl.BlockSpec((HALO, D), lambda b, i: (jnp.maximum((b * nt + i) * hb - 1, 0), 0))
    row = lambda n: pl.BlockSpec((n, D), lambda b, i: (0, 0))
    N = B * T
    return _pcall(body, name="conv_bwd", grid=(B, nt),
                  in_specs=[cur, nxt, cur, nxt, cur, prv, cur, cur, row(HALO), row(1), row(1)],
                  out_specs=[cur, cur, row(HALO * SUB), row(1), row(1), row(1), row(1), row(1)],
                  out_shape=[jax.ShapeDtypeStruct((N, D), BF)] * 2 + [jax.ShapeDtypeStruct((HALO * SUB, D), F32)]
                  + [jax.ShapeDtypeStruct((1, D), F32)] * 5,
                  scratch_shapes=[pltpu.VMEM((NSTRIP, bt + HALO, LANES), F32), pltpu.VMEM((NSTRIP, bt + HALO, LANES), F32),
                                  pltpu.VMEM((SUB - 1, NSTRIP, prows, LANES), F32),
                                  pltpu.VMEM((SUB - 1, NSTRIP, prows, LANES), F32),
                                  pltpu.VMEM((NSTRIP, HALO * SUB, LANES), F32), pltpu.VMEM((NSTRIP, bt, LANES), F32),
                                  pltpu.VMEM((NSTRIP, CK * SUB, LANES), F32)],
                  compiler_params=_params(("arbitrary", "arbitrary")))(ds, ds, yc, yc, y, y, pa, pg, w32, lng, lnb)


def _head(x, tgt, gain, bt=512):
    N = x.shape[0]
    bt = min(bt, N)

    def body(x_ref, t_ref, g_ref, dx_ref, loss_ref, dg_ref):
        @pl.when(pl.program_id(0) == 0)
        def _():
            loss_ref[...] = jnp.zeros(loss_ref.shape, F32)
            dg_ref[...] = jnp.zeros(dg_ref.shape, F32)

        xv = x_ref[...]
        gain = g_ref[...]
        err = xv * _rms_stats(xv) * gain - t_ref[...]
        loss_ref[...] += 0.5 * jnp.sum(jnp.mean(err * err, axis=-1, keepdims=True), axis=0, keepdims=True)
        dx, dgr = _rms_bwd(xv, gain, err * (1.0 / D))
        dx_ref[...] = dx
        dg_ref[...] += jnp.sum(dgr, axis=0, keepdims=True)

    blk = pl.BlockSpec((bt, D), lambda i: (i, 0))
    return _pcall(body, name="loss_head", grid=(N // bt,),
                  in_specs=[blk, blk, pl.BlockSpec((1, D), lambda i: (0, 0))],
                  out_specs=[blk, pl.BlockSpec((1, 128), lambda i: (0, 0)), pl.BlockSpec((1, D), lambda i: (0, 0))],
                  out_shape=[jax.ShapeDtypeStruct((N, D), F32), jax.ShapeDtypeStruct((1, 128), F32),
                             jax.ShapeDtypeStruct((1, D), F32)],
                  compiler_params=_params(("arbitrary",)))(x, tgt, gain)


def _local_step(x, mem, tgt, Wb, P, hooks=None):
    B, T, _ = x.shape
    N = B * T
    bq = bk = min(512, T)
    bt = min(512, T)
    x0 = x.reshape(N, D)
    mem2 = mem.reshape(B * NMEM, D)
    tgt2 = tgt.reshape(N, D)
    row = lambda v: v.reshape(1, -1)
    G = {}

    w_in = Wb['w_in_e'][0]
    w_inp = jnp.concatenate([w_in[:, 3 * FOXW + NG:], w_in[:, :3 * FOXW], w_in[:, 3 * FOXW:3 * FOXW + NG],
                             jnp.zeros((D, 128 - NG), BF)], axis=1)
    g_e = row(P['mix_norm_e'])
    z, h0 = _norm_mm("proj_z", x0, g_e, w_inp, N=D, coff=0, bn=D, out_dtype=F32, h_out=True)
    qkv = _norm_mm("proj_qkv", x0, g_e, w_inp, N=3 * FOXW, coff=2, bn=FOXW, out_dtype=BF)[0]
    fl = _norm_mm("proj_f", x0, g_e, w_inp, N=128, coff=20, bn=128, out_dtype=F32)[0]
    fbias = jnp.concatenate([P['fox_f_bias'].reshape(1, NG), jnp.zeros((1, 128 - NG), F32)], axis=1)
    cum = _fox_gate_fwd(fl, fbias, B, T)
    cum4 = cum[:, :NG].reshape(B, T, NG).transpose(0, 2, 1).reshape(B, NG, T // bk, bk)
    (b_out, lse), arrived = _fox_fwd(qkv, cum4, B, T, 0, bq, bk, carry=hooks['fwd_carry']() if hooks else None)
    if hooks:
        Wb = {**Wb, **hooks['fwd_done'](arrived)}
    lng, lnb = row(P['gmlp_ln_g']), row(P['gmlp_ln_b'])
    ws = P['gmlp_w_s'][0]
    bfull = jnp.repeat(P['gmlp_b_s'][0].T, HD, axis=1)
    a_out = _gmlp_fwd(z, lng, lnb, ws, bfull, bt)
    w_out = Wb['w_out_e'][0]
    x1 = _mm_resid("mix_out", [dict(A=b_out, Ka=FOXW, B=w_out, roff=0), dict(A=a_out, Ka=GW, B=w_out, roff=1)], x0)

    def xa_ffn_fwd(l, xin):
        qx, hq = _norm_mm(f"xa_q{l}", xin, row(P['xa_norm'][l]), Wb['xa_wq'][l], N=D, bn=D, out_dtype=BF, h_out=True)
        kv, hm = _norm_mm(f"xa_kv{l}", mem2, row(P['mem_norm'][l]), Wb['xa_wkv'][l], N=2 * D, bn=D, out_dtype=BF,
                          h_out=True)
        o = _xattn_fwd(f"xattn_fwd{l}", qx, kv, B, T, bt)
        xm = _mm_resid(f"xa_o{l}", [dict(A=o, Ka=D, B=Wb['xa_wo'][l])], xin)
        wgu = Wb['ffn_w_gu'][l]
        carry = hooks['fwd2_carry']() if hooks and l == 0 else None
        res = _fused_mm(f"ffn_gu{l}", dims='nn', M=N, N=FF, bm=min(512, N), bn=FF // 2, x=xm,
                        gain=row(P['ffn_norm'][l]),
                        groups=[[dict(A=None, Ka=D, B=wgu, coff=0)], [dict(A=None, Ka=D, B=wgu, coff=2)]],
                        epi=_epi_swiglu, outs=[BF, BF, BF], h_out=True, carry=carry)
        if carry is not None:
            res, arrived = res
            Wb.update(hooks['fwd2_done'](arrived))
        g, u, a, hf = res
        xo = _mm_resid(f"ffn_down{l}", [dict(A=a, Ka=FF, B=Wb['ffn_w_down'][l])], xm)
        return xo, dict(xin=xin, qx=qx, hq=hq, kv=kv, hm=hm, o=o, xm=xm, g=g, u=u, a=a, hf=hf)

    x3, S0 = xa_ffn_fwd(0, x1)
    w_cin = Wb['conv_w_in'][0]
    b_cin = row(P['conv_b_in'])
    pa, pg, y, hc = _fused_mm("conv_in", dims='nn', M=N, N=D, bm=min(512, N), bn=D, x=x3, gain=row(P['mix_norm_o']),
                              groups=[[dict(A=None, Ka=D, B=w_cin, coff=0)], [dict(A=None, Ka=D, B=w_cin, coff=1)]],
                              epi=_epi_glu, outs=[BF, BF, F32], rows=[(b_cin, 0), (b_cin, 1)], h_out=True)
    w32 = jnp.concatenate([P['conv_dw_w'][0], jnp.zeros((HALO - CK, D), F32)], axis=0)
    cbt = min(256, T)
    s, yc = _conv_fwd(y, w32, row(P['conv_dw_b']), row(P['conv_ln_g']), row(P['conv_ln_b']), B, T, cbt)
    x4 = _mm_resid("conv_out", [dict(A=s, Ka=D, B=Wb['conv_w_out'][0])], x3, bias=row(P['conv_b_out']))
    x6, S1 = xa_ffn_fwd(1, x4)
    dx, loss_t, dgf = _head(x6, tgt2, row(P['final_norm']))
    G['final_norm'] = dgf.reshape(D)

    def by_rows(dw):
        return dw.reshape(NCHIP, dw.shape[1] // NCHIP, dw.shape[2])

    def xa_ffn_bwd(l, S, dx):
        wgu, wdown = Wb['ffn_w_gu'][l], Wb['ffn_w_down'][l]
        dwdown = by_rows(_mm_tn(f"dw_down{l}", S['a'], dx))
        dg, du = _fused_mm(f"ffn_dgu{l}", dims='nt', M=N, N=FF, bm=min(512, N), bn=FF // 2,
                           groups=[[dict(A=dx, Ka=D, B=wdown)]], epi=_epi_swiglu_bwd, outs=[BF, BF],
                           tiles=[(S['g'], 0), (S['u'], 0)])
        dwgu = jnp.concatenate([_mm_tn(f"dw_g{l}", S['hf'], dg, parts=2), _mm_tn(f"dw_u{l}", S['hf'], du, parts=2)])
        dx, dgn = _mm_nt_rms_bwd(f"ffn_dx{l}", [dict(A=dg, Ka=FF, B=wgu, coff=0), dict(A=du, Ka=FF, B=wgu, coff=1)],
                                 S['xm'], row(P['ffn_norm'][l]), dx)
        dwo = by_rows(_mm_tn(f"dw_o{l}", S['o'], dx))
        do = _mm_nt_plain(f"xa_do{l}", dx, Wb['xa_wo'][l])
        dq, dkv = _xattn_bwd(f"xattn_bwd{l}", S['qx'], S['kv'], do, B, T, bt)
        dwq = by_rows(_mm_tn(f"dw_q{l}", S['hq'], dq))
        dwkv = _mm_tn(f"dw_kv{l}", S['hm'], dkv, parts=NCHIP)
        dmn = _fused_mm(f"xa_dmem{l}", dims='nt', M=B * NMEM, N=D, bm=min(256, B * NMEM), bn=D,
                        groups=[[dict(A=dkv, Ka=2 * D, B=Wb['xa_wkv'][l])]], epi=_epi_rms_gain_only, outs=[],
                        tiles=[(mem2, 0)], rows=[(row(P['mem_norm'][l]), 0)], reds=[(1, D)])[0]
        dx, dxn = _mm_nt_rms_bwd(f"xa_dx{l}", [dict(A=dq, Ka=D, B=Wb['xa_wq'][l])], S['xin'],
                                 row(P['xa_norm'][l]), dx)
        return dx, dict(ffn_w_down=dwdown, ffn_w_gu=dwgu, ffn_norm=dgn.reshape(D), xa_wo=dwo, xa_wq=dwq,
                        xa_wkv=dwkv, mem_norm=dmn.reshape(D), xa_norm=dxn.reshape(D))

    dx, G1 = xa_ffn_bwd(1, S1, dx)
    G['conv_w_out'] = [by_rows(_mm_tn("dw_cout", s, dx))]
    G['conv_b_out'] = _colsum("db_cout", dx)
    dsv = _mm_nt_plain("conv_ds", dx, Wb['conv_w_out'][0])
    dpa, dpg, dw32, dwb, dlng, dlnb, dba, dbg = _conv_bwd(dsv, yc, y, pa, pg, w32, row(P['conv_ln_g']),
                                                          row(P['conv_ln_b']), B, T, cbt)
    G['conv_dw_w'] = _sum_groups("conv_dw_sum", dw32)[:CK][None]
    G['conv_dw_b'], G['conv_ln_g'], G['conv_ln_b'] = dwb, dlng, dlnb
    G['conv_b_in'] = jnp.concatenate([dba, dbg], axis=1)
    G['conv_w_in'] = [jnp.concatenate([_mm_tn("dw_cin_a", hc, dpa, parts=2), _mm_tn("dw_cin_g", hc, dpg, parts=2)])]
    dx, dgo = _mm_nt_rms_bwd("conv_dx", [dict(A=dpa, Ka=D, B=w_cin, coff=0), dict(A=dpg, Ka=D, B=w_cin, coff=1)],
                             x3, row(P['mix_norm_o']), dx)
    G['mix_norm_o'] = dgo
    dx, G0 = xa_ffn_bwd(0, S0, dx)
    for k in G0:
        G[k] = [G0[k], G1[k]]
    G['w_out_e'] = [by_rows(jnp.concatenate([_mm_tn("dw_out_b", b_out, dx), _mm_tn("dw_out_a", a_out, dx)], axis=1))]
    dcat = _mm_nt_plain("mix_dcat", dx, w_out)
    (dq, dk, dv, dcum4, dcq4), arrived = _fox_bwd(qkv, cum4, b_out, lse, dcat, B, T, 0, bq, bk,
                                                  carry=hooks['bwd_carry'](G) if hooks else None)
    if hooks:
        hooks['bwd_done'](arrived)
    dz, dlg, dlb, dws, dbf = _gmlp_bwd(z, dcat, lng, lnb, ws, bfull, bt)
    G['gmlp_ln_g'], G['gmlp_ln_b'], G['gmlp_w_s'] = dlg, dlb, dws[None]
    G['gmlp_b_s'] = _group_sum("gmlp_db", dbf)[:, :NG].T[None]
    pad = jnp.zeros((N, 128 - NG), F32)
    dck = jnp.concatenate([dcum4.reshape(B, NG, T).transpose(0, 2, 1).reshape(N, NG), pad], axis=1)
    dcq = jnp.concatenate([dcq4.reshape(N, NG // 2, 128)[:, :, :2].reshape(N, NG), pad], axis=1)
    dfl, dfb = _fox_gate_bwd(fl, fbias, dcq, dck, B, T)
    G['fox_f_bias'] = dfb[:, :NG]
    dw_in = jnp.concatenate([_mm_tn("dw_in_q", h0, dq)[0], _mm_tn("dw_in_k", h0, dk)[0], _mm_tn("dw_in_v", h0, dv)[0],
                             _mm_tn("dw_in_f", h0, dfl)[0][:, :NG], _mm_tn("dw_in_z", h0, dz)[0]], axis=1)
    G['w_in_e'] = [dw_in.reshape(D, NCHIP, IN_W // NCHIP).transpose(1, 0, 2)]
    pairs = [dict(A=dz, Ka=D, B=w_inp, coff=0), dict(A=dq, Ka=FOXW, B=w_inp, coff=2),
             dict(A=dk, Ka=FOXW, B=w_inp, coff=3), dict(A=dv, Ka=FOXW, B=w_inp, coff=4),
             dict(A=dfl, Ka=128, B=w_inp, coff=20)]
    if hooks:
        dx, dge, arrived = _mm_nt_rms_bwd("mix_dx", pairs, x0, g_e, dx, carry=hooks['last_carry'](G))
        hooks['last_done'](arrived)
    else:
        dx, dge = _mm_nt_rms_bwd("mix_dx", pairs, x0, g_e, dx)
    G['mix_norm_e'] = dge
    return loss_t[0, 0], dx.reshape(B, T, D), G


COLS = 1024
ANY = pl.BlockSpec(memory_space=pl.ANY)


def _coords():
    return lax.axis_index("x"), lax.axis_index("y"), lax.axis_index("c")


def _other_chips(x, y):
    return [(1 - x, y), (x, 1 - y), (1 - x, 1 - y)]


def _own_slot(v, me):
    return lax.dynamic_update_slice(lax.empty((NCHIP,) + v.shape, v.dtype), v[None], (me,) + (0,) * v.ndim)


def _all_gather(name, shards, me):
    n = len(shards)
    bufs = [_own_slot(v, me) for v in shards]

    def body(*refs):
        out_refs, (send_sems, recv_sems) = refs[n:2 * n], refs[2 * n:]
        x, y, c = _coords()
        mine = 2 * x + y
        sib = (x, y, 1 - c)
        chips = _other_chips(x, y)

        def rcopy(a, k, chip_idx, half, to):
            blk = out_refs[a].at[chip_idx, half]
            return pltpu.make_async_remote_copy(src_ref=blk, dst_ref=blk, send_sem=send_sems.at[6 * a + k],
                                                recv_sem=recv_sems.at[6 * a + k], device_id=to, device_id_type=MESH)

        first = [rcopy(a, j, mine, c, (cx, cy, c)) for a in range(n) for j, (cx, cy) in enumerate(chips)]
        for cp in first:
            cp.start()
        passed = []
        for a in range(n):
            for j, (cx, cy) in enumerate(chips):
                kj = 2 * cx + cy
                rcopy(a, j, kj, c, sib).wait_recv()
                fwd = rcopy(a, 3 + j, kj, c, sib)
                fwd.start()
                passed.append(fwd)
        for a in range(n):
            for j, (cx, cy) in enumerate(chips):
                rcopy(a, 3 + j, 2 * cx + cy, 1 - c, sib).wait_recv()
        for cp in first + passed:
            cp.wait_send()

    return _pcall(body, name=name, in_specs=[ANY] * n, out_specs=[ANY] * n,
                  out_shape=[jax.ShapeDtypeStruct(b.shape, b.dtype) for b in bufs],
                  input_output_aliases={a: a for a in range(n)}, scratch_shapes=_sem_pairs(6 * n))(*bufs)


def _sem_pairs(n):
    return [pltpu.SemaphoreType.DMA((n,)), pltpu.SemaphoreType.DMA((n,))]


def _gather_forward(name, bufs):
    n = len(bufs)

    def body(*refs):
        out_refs, (send_sems, recv_sems) = refs[n:2 * n], refs[2 * n:]
        x, y, c = _coords()

        def cp(a, j, kj, half):
            blk = out_refs[a].at[kj, half]
            return pltpu.make_async_remote_copy(src_ref=blk, dst_ref=blk, send_sem=send_sems.at[3 * a + j],
                                                recv_sem=recv_sems.at[3 * a + j], device_id=(x, y, 1 - c),
                                                device_id_type=MESH)

        chips = [2 * cx + cy for cx, cy in _other_chips(x, y)]
        sends = [cp(a, j, kj, c) for a in range(n) for j, kj in enumerate(chips)]
        for s in sends:
            s.start()
        for a in range(n):
            for j, kj in enumerate(chips):
                cp(a, j, kj, 1 - c).wait_recv()
        for s in sends:
            s.wait_send()

    return _pcall(body, name=name, in_specs=[ANY] * n, out_specs=[ANY] * n,
                  out_shape=[jax.ShapeDtypeStruct(b.shape, b.dtype) for b in bufs],
                  input_output_aliases={a: a for a in range(n)}, scratch_shapes=_sem_pairs(3 * n))(*bufs)


def _sibling_halves(name, ps):
    n = len(ps)

    def body(*refs):
        p_refs, out_refs, (send_sems, recv_sems) = refs[:n], refs[n:2 * n], refs[2 * n:]
        x, y, c = _coords()
        cps = [pltpu.make_async_remote_copy(src_ref=p_refs[a].at[k, 1 - c], dst_ref=out_refs[a].at[k],
                                            send_sem=send_sems.at[4 * a + k], recv_sem=recv_sems.at[4 * a + k],
                                            device_id=(x, y, 1 - c), device_id_type=MESH)
               for a in range(n) for k in range(NCHIP)]
        for cp in cps:
            cp.start()
        for cp in cps:
            cp.wait()

    return _pcall(body, name=name, in_specs=[ANY] * n, out_specs=[ANY] * n,
                  out_shape=[jax.ShapeDtypeStruct((NCHIP,) + p.shape[2:], p.dtype) for p in ps],
                  scratch_shapes=_sem_pairs(NCHIP * n))(*ps)


def _chip_exchange(name, qs):
    n = len(qs)

    def body(*refs):
        q_refs, out_refs, (send_sems, recv_sems) = refs[:n], refs[n:2 * n], refs[2 * n:]
        x, y, c = _coords()
        cps = [pltpu.make_async_remote_copy(src_ref=q_refs[a].at[2 * cx + cy], dst_ref=out_refs[a].at[j],
                                            send_sem=send_sems.at[3 * a + j], recv_sem=recv_sems.at[3 * a + j],
                                            device_id=(cx, cy, c), device_id_type=MESH)
               for a in range(n) for j, (cx, cy) in enumerate(_other_chips(x, y))]
        for cp in cps:
            cp.start()
        for cp in cps:
            cp.wait()

    return _pcall(body, name=name, in_specs=[ANY] * n, out_specs=[ANY] * n,
                  out_shape=[jax.ShapeDtypeStruct((3,) + q.shape[1:], q.dtype) for q in qs],
                  scratch_shapes=_sem_pairs(3 * n))(*qs)


def _sibling_swap(name, hs):
    n = len(hs)

    def body(*refs):
        out_refs, (send_sems, recv_sems) = refs[n:2 * n], refs[2 * n:]
        x, y, c = _coords()
        sib = (x, y, 1 - c)
        sends = [pltpu.make_async_remote_copy(src_ref=out_refs[a].at[c], dst_ref=out_refs[a].at[c],
                                              send_sem=send_sems.at[a], recv_sem=recv_sems.at[a], device_id=sib,
                                              device_id_type=MESH) for a in range(n)]
        for cp in sends:
            cp.start()
        for a in range(n):
            theirs = out_refs[a].at[1 - c]
            pltpu.make_async_remote_copy(src_ref=theirs, dst_ref=theirs, send_sem=send_sems.at[a],
                                         recv_sem=recv_sems.at[a], device_id=sib, device_id_type=MESH).wait_recv()
        for cp in sends:
            cp.wait_send()

    return _pcall(body, name=name, in_specs=[ANY] * n, out_specs=[ANY] * n,
                  out_shape=[jax.ShapeDtypeStruct(h.shape, h.dtype) for h in hs],
                  input_output_aliases={a: a for a in range(n)}, scratch_shapes=_sem_pairs(n))(*hs)


ADD_BLOCK_BYTES = 2 * 1024 * 1024


def _row_block(R, C):
    if R * C * 4 <= ADD_BLOCK_BYTES:
        return R
    for br in (512, 256, 128, 64, 32, 16, 8):
        if R % br == 0 and br * C * 4 <= ADD_BLOCK_BYTES:
            return br
    return R


def _add_own_half(name, p, recv, c_arr, out_dtype):
    _, _, R, C = p.shape
    br = _row_block(R, C)

    def body(c_ref, p_ref, r_ref, o_ref):
        o_ref[...] = (p_ref[...].astype(F32) + r_ref[...].astype(F32)).astype(o_ref.dtype)

    spec = pltpu.PrefetchScalarGridSpec(
        num_scalar_prefetch=1, grid=(NCHIP, R // br),
        in_specs=[pl.BlockSpec((None, None, br, C), lambda k, r, c_ref: (k, c_ref[0], r, 0)),
                  pl.BlockSpec((None, br, C), lambda k, r, c_ref: (k, r, 0))],
        out_specs=pl.BlockSpec((None, br, C), lambda k, r, c_ref: (k, r, 0)))
    return _pcall(body, name=name, grid_spec=spec, out_shape=jax.ShapeDtypeStruct((NCHIP, R, C), out_dtype),
                  compiler_params=_params(("arbitrary", "arbitrary")))(c_arr, p, recv)


def _add_chips(name, q, recv, idx_arr):
    _, R, C = q.shape
    br = _row_block(R, C)

    def body(idx_ref, q_ref, r_ref, o_ref):
        o_ref[...] = ((q_ref[...].astype(F32) + r_ref[0].astype(F32)) + r_ref[1].astype(F32)) + r_ref[2].astype(F32)

    spec = pltpu.PrefetchScalarGridSpec(
        num_scalar_prefetch=1, grid=(R // br,),
        in_specs=[pl.BlockSpec((None, br, C), lambda r, idx: (idx[0], r, 0)),
                  pl.BlockSpec((3, br, C), lambda r, idx: (0, r, 0))],
        out_specs=pl.BlockSpec((None, br, C), lambda r, idx: (idx[1], r, 0)))
    return _pcall(body, name=name, grid_spec=spec, out_shape=jax.ShapeDtypeStruct((2, R, C), F32),
                  compiler_params=_params(("arbitrary",)))(idx_arr, q, recv)


EARLY = ['w_in_e', 'w_out_e']
LATE = [n for n in BIG if n not in EARLY]


def _adamw(name, w, g, m, v):
    shape = w.shape
    cols = shape[-1]
    rows = w.size // cols
    w2, g2, m2, v2 = (a.reshape(rows, cols) for a in (w, g, m, v))
    bt = next((b for b in (256, 128) if rows % b == 0), rows)

    def body(w_ref, g_ref, m_ref, v_ref, d_ref, nm_ref, nv_ref):
        gv = g_ref[...]
        nm = ADAM_B1 * m_ref[...] + (1.0 - ADAM_B1) * gv
        nv = ADAM_B2 * v_ref[...] + (1.0 - ADAM_B2) * (gv * gv)
        m_hat = nm / (1.0 - ADAM_B1 ** ADAM_STEP)
        v_hat = nv / (1.0 - ADAM_B2 ** ADAM_STEP)
        d_ref[...] = -ADAM_LR * (m_hat / (jnp.sqrt(v_hat) + ADAM_EPS) + ADAM_WD * w_ref[...])
        nm_ref[...] = nm
        nv_ref[...] = nv

    blk = pl.BlockSpec((bt, cols), lambda i: (i, 0))
    outs = _pcall(body, name=name, grid=(rows // bt,), in_specs=[blk] * 4, out_specs=[blk] * 3,
                  out_shape=[jax.ShapeDtypeStruct((rows, cols), F32)] * 3, compiler_params=_params(("arbitrary",)))(
        w2, g2, m2, v2)
    return [o.reshape(shape) for o in outs]


SMALL_SHARDED = ['mix_norm_o', 'conv_b_in', 'conv_dw_w', 'conv_dw_b', 'conv_ln_g', 'conv_ln_b', 'conv_b_out']
REPLICATED = [n for n in WEIGHTS if SHARD_AXIS[n] is None]
NCHIP = 4


def _halves(flat, tile_rows):
    unit = 2 * tile_rows * COLS
    total = -(-flat.size // unit) * unit
    return jnp.pad(flat, (0, total - flat.size)).reshape(2, total // (2 * COLS), COLS)


def _flat(arrays):
    return jnp.concatenate([a.reshape(-1) for a in arrays])


def _chip_block(a, axis, k):
    n = a.shape[axis] // NCHIP
    return lax.slice_in_dim(a, k * n, (k + 1) * n, axis=axis)


def _full_shape(n, shard_shape):
    s = list(shard_shape[n])
    s[SHARD_AXIS[n]] *= NCHIP
    return tuple(s)


def _unpack(flat, names, shapes):
    out, off = {}, 0
    for n in names:
        size = math.prod(shapes[n])
        out[n] = flat[off:off + size].reshape(shapes[n])
        off += size
    return out


def kernel(x, mem, mix_norm_e, w_in_e, fox_f_bias, gmlp_ln_g, gmlp_ln_b, gmlp_w_s, gmlp_b_s, w_out_e, mix_norm_o, conv_w_in, conv_b_in, conv_dw_w, conv_dw_b, conv_ln_g, conv_ln_b, conv_w_out, conv_b_out, xa_norm, mem_norm, xa_wq, xa_wkv, xa_wo, ffn_norm, ffn_w_gu, ffn_w_down, final_norm, loss_target, m_mix_norm_e, m_w_in_e, m_fox_f_bias, m_gmlp_ln_g, m_gmlp_ln_b, m_gmlp_w_s, m_gmlp_b_s, m_w_out_e, m_mix_norm_o, m_conv_w_in, m_conv_b_in, m_conv_dw_w, m_conv_dw_b, m_conv_ln_g, m_conv_ln_b, m_conv_w_out, m_conv_b_out, m_xa_norm, m_mem_norm, m_xa_wq, m_xa_wkv, m_xa_wo, m_ffn_norm, m_ffn_w_gu, m_ffn_w_down, m_final_norm, v_mix_norm_e, v_w_in_e, v_fox_f_bias, v_gmlp_ln_g, v_gmlp_ln_b, v_gmlp_w_s, v_gmlp_b_s, v_w_out_e, v_mix_norm_o, v_conv_w_in, v_conv_b_in, v_conv_dw_w, v_conv_dw_b, v_conv_ln_g, v_conv_ln_b, v_conv_w_out, v_conv_b_out, v_xa_norm, v_mem_norm, v_xa_wq, v_xa_wkv, v_xa_wo, v_ffn_norm, v_ffn_w_gu, v_ffn_w_down, v_final_norm):
    env = locals()
    w = {n: env[n] for n in WEIGHTS}
    m = {n: env["m_" + n] for n in WEIGHTS}
    v = {n: env["v_" + n] for n in WEIGHTS}
    shard_shape = {n: w[n].shape for n in WEIGHTS}
    xi, yi, ci = _coords()
    me = 2 * xi + yi
    c_arr = jnp.reshape(ci, (1,)).astype(jnp.int32)
    idx_arr = jnp.stack([me, ci]).astype(jnp.int32)

    def two_halves(a):
        return a.reshape(2, a.shape[0] // 2, a.shape[1])

    def shard(n, l):
        return two_halves(w[n][l].astype(BF))

    def matrix(n, gathered):
        rows, cols = w[n].shape[1:]
        g = gathered.reshape(NCHIP, rows, cols)
        return g.reshape(NCHIP * rows, cols) if SHARD_AXIS[n] == 1 else g.transpose(1, 0, 2).reshape(rows, NCHIP * cols)

    Wb = {n: [matrix(n, g)] for n, g in zip(EARLY, _all_gather("gather_mixer", [shard(n, 0) for n in EARLY], me))}
    vec = _all_gather("gather_vectors", [_halves(_flat([w[n] for n in SMALL_SHARDED]), 8)], me)[0].reshape(NCHIP, -1)
    parts = [_unpack(vec[k], SMALL_SHARDED, shard_shape) for k in range(NCHIP)]
    P = {n: jnp.concatenate([parts[k][n] for k in range(NCHIP)], axis=SHARD_AXIS[n]) for n in SMALL_SHARDED}
    P.update({n: w[n] for n in REPLICATED})
    first = [(n, 0) for n in LATE]
    second = [(n, 1) for n in LATE if w[n].shape[0] > 1]
    bufs = {key: _own_slot(shard(*key), me) for key in first + second}
    state = {}

    def fwd_done(arrived):
        state['layer0'] = {n: matrix(n, g) for (n, _), g in zip(first, _gather_forward("gather_forward0", arrived))}
        return {n: [g] for n, g in state['layer0'].items()}

    def fwd2_done(arrived):
        return {n: [state['layer0'][n], matrix(n, g)]
                for (n, _), g in zip(second, _gather_forward("gather_forward1", arrived))}

    def by_halves(G, names):
        return [g.reshape(NCHIP, 2, g.shape[1] // 2, g.shape[2]) for n in names for g in G[n]]

    def pair_sums(tag, ps):
        got = _sibling_halves(f"rs_sibling_halves_{tag}", ps)
        return [_add_own_half(f"rs_add_pair_{tag}{a}", p, g, c_arr, p.dtype) for a, (p, g) in enumerate(zip(ps, got))]

    def bwd_carry(G):
        state['qs'] = pair_sums("late", by_halves(G, LATE + ['w_out_e']))
        return _carry_exchange(state['qs'])

    def last_carry(G):
        state['qs_in'] = pair_sums("in", by_halves(G, ['w_in_e']))
        return _carry_exchange(state['qs_in'])

    hooks = dict(fwd_carry=lambda: _carry_gather([bufs[key] for key in first]), fwd_done=fwd_done,
                 fwd2_carry=lambda: _carry_gather([bufs[key] for key in second]), fwd2_done=fwd2_done,
                 bwd_carry=bwd_carry, bwd_done=lambda arrived: state.update(got=arrived),
                 last_carry=last_carry, last_done=lambda arrived: state.update(got_in=arrived))
    loss_part, grad_x, G = _local_step(x, mem, loss_target, Wb, P, hooks)
    loss = lax.psum(loss_part, ("x", "y", "c"))

    def layers(n):
        return G[n] if isinstance(G[n], list) else ([G[n]] if G[n].ndim == 1 else [G[n][l] for l in range(G[n].shape[0])])

    rep = _flat([a for n in REPLICATED for a in layers(n)])
    quarter = -(-rep.size // (NCHIP * 2 * 8 * COLS)) * (2 * 8 * COLS)
    rep = jnp.pad(rep, (0, NCHIP * quarter - rep.size)).reshape(NCHIP, quarter)
    segs = [[_chip_block(a, SHARD_AXIS[n] - 1, k).reshape(-1) for n in SMALL_SHARDED for a in layers(n)] + [rep[k]]
            for k in range(NCHIP)]
    size = sum(piece.size for piece in segs[0])
    total = -(-size // (2 * 8 * COLS)) * (2 * 8 * COLS)
    p_small = jnp.concatenate([piece for seg in segs for piece in seg + [jnp.zeros((total - size,), F32)]])
    p_small = p_small.reshape(NCHIP, 2, total // (2 * COLS), COLS)
    qs_small = pair_sums("vectors", [p_small])
    pending = [("late", state['qs'], state['got']), ("in", state['qs_in'], state['got_in']),
               ("vectors", qs_small, _chip_exchange("rs_chip_exchange_vectors", qs_small))]
    hs = [_add_chips(f"rs_add_chips_{tag}{a}", q, g, idx_arr)
          for tag, qs, got in pending for a, (q, g) in enumerate(zip(qs, got))]
    red = _sibling_swap("rs_sibling_swap", hs)
    mine, at = {}, 0
    for n in LATE + ['w_out_e', 'w_in_e']:
        nl = shard_shape[n][0]
        mine[n] = jnp.stack([r.reshape(shard_shape[n][1:]) for r in red[at:at + nl]])
        at += nl
    red_small = red[-1].reshape(-1)
    mine.update(_unpack(red_small, SMALL_SHARDED, shard_shape))
    off = sum(math.prod(shard_shape[n]) for n in SMALL_SHARDED)
    rep_all = _all_gather("gather_replicated_grads",
                          [red_small[off:off + quarter].reshape(2, quarter // (2 * COLS), COLS)], me)[0]
    mine.update(_unpack(rep_all.reshape(-1), REPLICATED, shard_shape))

    grads, deltas, new_m, new_v = [], [], [], []
    for n in WEIGHTS:
        d, nm, nv = _adamw("adamw_" + n, w[n], mine[n], m[n], v[n])
        grads.append(mine[n])
        deltas.append(d)
        new_m.append(nm)
        new_v.append(nv)
    return (loss, grad_x, *grads, *deltas, *new_m, *new_v)
```

```python
import functools
import math

import jax
import jax.numpy as jnp
from jax import lax
from jax.experimental import pallas as pl
from jax.experimental.pallas import tpu as pltpu

F32 = jnp.float32
BF = jnp.bfloat16
MESH = pl.DeviceIdType.MESH

D = 1024
FOXW = 512
HD = 64
GW = 512
CH = 128
NG = 8
FF = 2816
NMEM = 256
XH = 4
XD = 256
CK = 31
HALO = 32
EPS = 1e-6
IN_W = 2568
IN_WP = 2688
VMEM_LIMIT = 56 * 1024 * 1024

ADAM_LR, ADAM_B1, ADAM_B2, ADAM_EPS, ADAM_WD, ADAM_STEP = 0.001, 0.9, 0.999, 1e-08, 0.01, 10

WEIGHTS = ['mix_norm_e', 'w_in_e', 'fox_f_bias', 'gmlp_ln_g', 'gmlp_ln_b', 'gmlp_w_s', 'gmlp_b_s', 'w_out_e',
           'mix_norm_o', 'conv_w_in', 'conv_b_in', 'conv_dw_w', 'conv_dw_b', 'conv_ln_g', 'conv_ln_b',
           'conv_w_out', 'conv_b_out', 'xa_norm', 'mem_norm', 'xa_wq', 'xa_wkv', 'xa_wo', 'ffn_norm',
           'ffn_w_gu', 'ffn_w_down', 'final_norm']
SHARD_AXIS = {'mix_norm_e': None, 'w_in_e': 2, 'fox_f_bias': None, 'gmlp_ln_g': None, 'gmlp_ln_b': None,
              'gmlp_w_s': None, 'gmlp_b_s': None, 'w_out_e': 1, 'mix_norm_o': 1, 'conv_w_in': 2, 'conv_b_in': 1,
              'conv_dw_w': 2, 'conv_dw_b': 1, 'conv_ln_g': 1, 'conv_ln_b': 1, 'conv_w_out': 1, 'conv_b_out': 1,
              'xa_norm': None, 'mem_norm': None, 'xa_wq': 1, 'xa_wkv': 2, 'xa_wo': 1, 'ffn_norm': None,
              'ffn_w_gu': 2, 'ffn_w_down': 1, 'final_norm': None}
BIG = ['w_in_e', 'w_out_e', 'conv_w_in', 'conv_w_out', 'xa_wq', 'xa_wkv', 'xa_wo', 'ffn_w_gu', 'ffn_w_down']


def _pcall(body, **kw):
    return pl.pallas_call(body, **kw)


def _params(sem=None, **kw):
    return pltpu.CompilerParams(dimension_semantics=sem, vmem_limit_bytes=VMEM_LIMIT, **kw)


def _dot(a, b, dims):
    dn = {'nn': (((1,), (0,)), ((), ())), 'nt': (((1,), (1,)), ((), ())), 'tn': (((0,), (0,)), ((), ()))}[dims]
    return lax.dot_general(a, b, dn, preferred_element_type=F32)


def _sigmoid(x):
    return 1.0 / (1.0 + jnp.exp(-x))


def _rms_stats(xv):
    return lax.rsqrt(jnp.mean(xv * xv, axis=-1, keepdims=True) + EPS)


def _rms_bwd(xv, gain, dh):
    r = _rms_stats(xv)
    t = dh * gain
    dx = r * t - xv * (r * r * r * jnp.mean(t * xv, axis=-1, keepdims=True))
    return dx, dh * xv * r


def _fused_mm(name, *, dims, M, N, bm, bn, groups, epi, outs, x=None, gain=None, tiles=(), rows=(),
              h_out=False, reds=(), carry=None, cols_outer=False):
    bm = min(bm, M)
    nI, nJ = M // bm, N // bn
    assert nI * bm == M and nJ * bn == N
    assert not reds or nJ == 1
    arrays, specs = [], []

    def spec(shape, index):
        return pl.BlockSpec(shape, (lambda jj, ii: index(ii, jj)) if cols_outer else index)

    def add(arr, shape, index):
        arrays.append(arr)
        specs.append(spec(shape, index))
        return len(arrays) - 1

    def first_pass(i, j):
        return (jnp.where(j == 0, i, nI - 1), 0) if cols_outer else (i, 0)

    if x is not None:
        K0 = x.shape[1]
        add(x, (bm, K0), first_pass)
        add(gain, (1, K0), lambda i, j: (0, 0))
    plan = []
    for grp in groups:
        g = []
        for p in grp:
            ai = None
            if p['A'] is not None:
                ai = add(p['A'], (bm, p['Ka']), lambda i, j, o=p.get('acoff', 0): (i, o))
            ro, co = p.get('roff', 0), p.get('coff', 0)
            if dims == 'nn':
                bi = add(p['B'], (p['Ka'], bn), lambda i, j, ro=ro, co=co: (ro, j + co))
            else:
                bi = add(p['B'], (bn, p['Ka']), lambda i, j, ro=ro, co=co: (j + ro, co))
            g.append((ai, bi))
        plan.append(g)
    tile_idx = [add(a, (bm, bn), lambda i, j, o=o: (i, j + o)) for a, o in tiles]
    row_idx = [add(a, (1, bn), lambda i, j, o=o: (0, j + o)) for a, o in rows]
    n_in = len(arrays)

    out_shape = [jax.ShapeDtypeStruct((M, N), dt) for dt in outs]
    out_specs = [spec((bm, bn), lambda i, j: (i, j)) for _ in outs]
    if h_out:
        out_shape.append(jax.ShapeDtypeStruct((M, x.shape[1]), BF))
        out_specs.append(spec((bm, x.shape[1]), first_pass))
    for shp in reds:
        out_shape.append(jax.ShapeDtypeStruct(shp, F32))
        out_specs.append(spec(shp, lambda i, j: (0, 0)))
    n_main = len(outs)
    scratch = [pltpu.VMEM((M if cols_outer else bm, x.shape[1]), BF)] if x is not None else []

    def body(*refs):
        ins, out_refs, scr = refs[:n_in], refs[n_in:n_in + len(out_shape)], refs[n_in + len(out_shape):]
        i, j = (pl.program_id(1), pl.program_id(0)) if cols_outer else (pl.program_id(0), pl.program_id(1))
        if x is not None:
            hn_rows = pl.ds(pl.multiple_of(i * bm, bm), bm) if cols_outer else slice(None)
            hn_ref = scr[0]

            @pl.when(j == 0)
            def _():
                xv = ins[0][...]
                hn = (xv * _rms_stats(xv) * ins[1][...]).astype(BF)
                hn_ref[hn_rows, :] = hn
                if h_out:
                    out_refs[n_main][...] = hn

        accs = []
        for g in plan:
            acc = None
            for ai, bi in g:
                a = hn_ref[hn_rows, :] if ai is None else ins[ai][...]
                if a.dtype != BF:
                    a = a.astype(BF)
                d = _dot(a, ins[bi][...], dims)
                acc = d if acc is None else acc + d
            accs.append(acc)
        out_vals, red_vals = epi(accs, [ins[t][...] for t in tile_idx], [ins[r][...] for r in row_idx])
        for r, v in zip(out_refs[:n_main], out_vals):
            r[...] = v.astype(r.dtype)
        if reds:
            red_refs = out_refs[n_main + (1 if h_out else 0):]

            @pl.when(i == 0)
            def _():
                for r in red_refs:
                    r[...] = jnp.zeros(r.shape, F32)

            for r, v in zip(red_refs, red_vals):
                r[...] += v

    res, arrived = _carried_call(body, name=name, grid=(nJ, nI) if cols_outer else (nI, nJ), in_specs=specs,
                                 out_specs=out_specs,
                                 out_shape=out_shape, scratch_shapes=scratch, operands=arrays, carry=carry)
    return res if carry is None else (res, arrived)


def _epi_plain(accs, tiles, rows):
    return [accs[0]], []


def _epi_resid(accs, tiles, rows):
    y = tiles[0] + accs[0]
    if rows:
        y = y + rows[0]
    return [y], []


def _epi_swiglu(accs, tiles, rows):
    g, u = accs
    return [g, u, g * _sigmoid(g) * u], []


def _epi_glu(accs, tiles, rows):
    a, g = accs[0] + rows[0], accs[1] + rows[1]
    return [a, g, a * _sigmoid(g)], []


def _epi_swiglu_bwd(accs, tiles, rows):
    da = accs[0]
    g, u = tiles[0].astype(F32), tiles[1].astype(F32)
    sg = _sigmoid(g)
    return [da * u * (sg * (1.0 + g * (1.0 - sg))), da * (g * sg)], []


def _epi_rms_bwd(accs, tiles, rows):
    dx, dgr = _rms_bwd(tiles[0], rows[0], accs[0])
    return [tiles[1] + dx], [jnp.sum(dgr, axis=0, keepdims=True)]


def _epi_rms_gain_only(accs, tiles, rows):
    _, dgr = _rms_bwd(tiles[0], rows[0], accs[0])
    return [], [jnp.sum(dgr, axis=0, keepdims=True)]


def _norm_mm(name, x, gain, W, *, N, coff=0, bn, out_dtype, bm=512, h_out=False):
    return _fused_mm(name, dims='nn', M=x.shape[0], N=N, bm=bm, bn=bn, x=x, gain=gain,
                     groups=[[dict(A=None, Ka=x.shape[1], B=W, coff=coff)]], epi=_epi_plain, outs=[out_dtype],
                     h_out=h_out)


def _mm_resid(name, pairs, resid, bias=None, bm=512):
    M = resid.shape[0]
    return _fused_mm(name, dims='nn', M=M, N=D, bm=bm, bn=D, groups=[pairs], epi=_epi_resid, outs=[F32],
                     tiles=[(resid, 0)], rows=[(bias, 0)] if bias is not None else [])[0]


def _mm_nt_plain(name, dy, W, bm=512):
    return _fused_mm(name, dims='nt', M=dy.shape[0], N=W.shape[0], bm=bm, bn=W.shape[0],
                     groups=[[dict(A=dy, Ka=dy.shape[1], B=W)]], epi=_epi_plain, outs=[BF])[0]


def _mm_nt_rms_bwd(name, pairs, x, gain, dx_in, bm=256, carry=None):
    out = _fused_mm(name, dims='nt', M=x.shape[0], N=D, bm=bm, bn=D, groups=[pairs], epi=_epi_rms_bwd,
                    outs=[F32], tiles=[(x, 0), (dx_in, 0)], rows=[(gain, 0)], reds=[(1, D)], carry=carry)
    if carry is None:
        return out[0], out[1]
    return out[0][0], out[0][1], out[1]


def _mm_tn(name, A, G, bk=1024, parts=1):
    T, Ka, Kg = A.shape[0], A.shape[1], G.shape[1]
    w = Kg // parts
    bm = Ka if Ka <= 1024 else Ka // 2
    bn = w if w <= 1408 else w // 2
    bk = min(bk, T)
    per = w // bn
    nI, nJ, nK = Ka // bm, Kg // bn, T // bk

    def body(a_ref, g_ref, o_ref, acc):
        k = pl.program_id(2)

        @pl.when(k == 0)
        def _():
            acc[...] = jnp.zeros(acc.shape, F32)

        acc[...] += _dot(a_ref[...].astype(BF), g_ref[...].astype(BF), 'tn')

        @pl.when(k == nK - 1)
        def _():
            o_ref[...] = acc[...].astype(BF)

    return _pcall(body, name=name, grid=(nI, nJ, nK),
                  in_specs=[pl.BlockSpec((bk, bm), lambda i, j, k: (k, i)),
                            pl.BlockSpec((bk, bn), lambda i, j, k: (k, j))],
                  out_specs=pl.BlockSpec((None, bm, bn), lambda i, j, k: (j // per, i, j % per)),
                  out_shape=jax.ShapeDtypeStruct((parts, Ka, w), BF),
                  scratch_shapes=[pltpu.VMEM((bm, bn), F32)],
                  compiler_params=_params(("arbitrary", "arbitrary", "arbitrary")))(A, G)


def _colsum(name, a, bt=512):
    M, N = a.shape
    bt = min(bt, M)

    def body(a_ref, o_ref):
        @pl.when(pl.program_id(0) == 0)
        def _():
            o_ref[...] = jnp.zeros(o_ref.shape, F32)

        o_ref[...] += jnp.sum(a_ref[...].astype(F32), axis=0, keepdims=True)

    return _pcall(body, name=name, grid=(M // bt,), in_specs=[pl.BlockSpec((bt, N), lambda i: (i, 0))],
                  out_specs=pl.BlockSpec((1, N), lambda i: (0, 0)), out_shape=jax.ShapeDtypeStruct((1, N), F32),
                  compiler_params=_params(("arbitrary",)))(a)


def _cumsum_rows(v):
    T = v.shape[0]
    row = lax.broadcasted_iota(jnp.int32, v.shape, 0)
    s = 1
    while s < T:
        v = v + jnp.where(row >= s, pltpu.roll(v, s, 0), 0.0)
        s *= 2
    return v


def _log_sigmoid(z):
    return jnp.minimum(z, 0.0) - jnp.log(1.0 + jnp.exp(-jnp.abs(z)))


def _fox_gate_fwd(fl, fbias, B, T):
    def body(fl_ref, b_ref, o_ref):
        o_ref[...] = _cumsum_rows(_log_sigmoid(fl_ref[...] + b_ref[...]))

    return _pcall(body, name="fox_gate_fwd", grid=(B,),
                  in_specs=[pl.BlockSpec((T, 128), lambda b: (b, 0)), pl.BlockSpec((1, 128), lambda b: (0, 0))],
                  out_specs=pl.BlockSpec((T, 128), lambda b: (b, 0)),
                  out_shape=jax.ShapeDtypeStruct((B * T, 128), F32), compiler_params=_params(("arbitrary",)))(fl, fbias)


def _fox_gate_bwd(fl, fbias, dcq, dck, B, T):
    def body(fl_ref, b_ref, dcq_ref, dck_ref, dfl_ref, db_ref):
        dc = dcq_ref[...] + dck_ref[...]
        rev = jnp.sum(dc, axis=0, keepdims=True) - _cumsum_rows(dc) + dc
        dfl = rev * _sigmoid(-(fl_ref[...] + b_ref[...]))
        dfl_ref[...] = dfl

        @pl.when(pl.program_id(0) == 0)
        def _():
            db_ref[...] = jnp.zeros(db_ref.shape, F32)

        db_ref[...] += jnp.sum(dfl, axis=0, keepdims=True)

    return _pcall(body, name="fox_gate_bwd", grid=(B,),
                  in_specs=[pl.BlockSpec((T, 128), lambda b: (b, 0)), pl.BlockSpec((1, 128), lambda b: (0, 0)),
                            pl.BlockSpec((T, 128), lambda b: (b, 0)), pl.BlockSpec((T, 128), lambda b: (b, 0))],
                  out_specs=[pl.BlockSpec((T, 128), lambda b: (b, 0)), pl.BlockSpec((1, 128), lambda b: (0, 0))],
                  out_shape=[jax.ShapeDtypeStruct((B * T, 128), F32), jax.ShapeDtypeStruct((1, 128), F32)],
                  compiler_params=_params(("arbitrary",)))(fl, fbias, dcq, dck)


def _carried_call(body, *, name, grid, in_specs, out_specs, out_shape, scratch_shapes, operands, carry):
    if carry is None:
        return _pcall(body, name=name, grid=grid, in_specs=in_specs, out_specs=out_specs, out_shape=out_shape,
                      scratch_shapes=scratch_shapes, compiler_params=_params(("arbitrary",) * len(grid)))(*operands), []
    n, n_in, n_out, n_scr = len(carry['inputs']), len(in_specs), len(out_specs), len(scratch_shapes)

    def wrapped(*refs):
        ins, cin = refs[:n_in], refs[n_in:n_in + n]
        outs, cout = refs[n_in + n:n_in + n + n_out], refs[n_in + n + n_out:n_in + 2 * n + n_out]
        scr = refs[n_in + 2 * n + n_out:]
        send_sems, recv_sems = scr[n_scr:]
        ids = [pl.program_id(d) for d in range(len(grid))]
        first = functools.reduce(jnp.logical_and, [i == 0 for i in ids])
        last = functools.reduce(jnp.logical_and, [i == g - 1 for i, g in zip(ids, grid)])

        @pl.when(first)
        def _():
            for cp in carry['copies'](cin, cout, send_sems, recv_sems):
                cp.start()

        body(*ins, *outs, *scr[:n_scr])

        @pl.when(last)
        def _():
            for cp in carry['copies'](cin, cout, send_sems, recv_sems):
                cp.wait()

    aliases = {n_in + a: n_out + a for a in range(n)} if carry['in_place'] else {}
    res = _pcall(wrapped, name=name, grid=grid, in_specs=list(in_specs) + [ANY] * n,
                 out_specs=list(out_specs) + [ANY] * n, out_shape=list(out_shape) + carry['out_shape'],
                 scratch_shapes=list(scratch_shapes) + _sem_pairs(carry['nsem']), input_output_aliases=aliases,
                 compiler_params=_params(("arbitrary",) * len(grid)))(*operands, *carry['inputs'])
    return res[:n_out], res[n_out:]


def _carry_gather(bufs):
    n = len(bufs)

    def copies(in_refs, out_refs, send_sems, recv_sems):
        x, y, c = _coords()
        cps = []
        for a in range(n):
            blk = out_refs[a].at[2 * x + y, c]
            cps += [pltpu.make_async_remote_copy(src_ref=blk, dst_ref=blk, send_sem=send_sems.at[3 * a + j],
                                                 recv_sem=recv_sems.at[3 * a + j], device_id=(cx, cy, c),
                                                 device_id_type=MESH) for j, (cx, cy) in enumerate(_other_chips(x, y))]
        return cps

    return dict(inputs=bufs, out_shape=[jax.ShapeDtypeStruct(b.shape, b.dtype) for b in bufs], in_place=True,
                nsem=3 * n, copies=copies)


def _carry_exchange(qs):
    n = len(qs)

    def copies(in_refs, out_refs, send_sems, recv_sems):
        x, y, c = _coords()
        return [pltpu.make_async_remote_copy(src_ref=in_refs[a].at[2 * cx + cy], dst_ref=out_refs[a].at[j],
                                             send_sem=send_sems.at[3 * a + j], recv_sem=recv_sems.at[3 * a + j],
                                             device_id=(cx, cy, c), device_id_type=MESH)
                for a in range(n) for j, (cx, cy) in enumerate(_other_chips(x, y))]

    return dict(inputs=qs, out_shape=[jax.ShapeDtypeStruct((3,) + q.shape[1:], q.dtype) for q in qs], in_place=False,
                nsem=3 * n, copies=copies)


NEG = -1e30
QSUB = 1


def _fox_fwd(qkv, cum4, B, T, qoff, bq, bk, carry=None):
    nq, nkb = T // bq, T // bk
    N = B * T

    def body(q_ref, k_ref, v_ref, cum_ref, o_ref, lse_ref):
        hp, i = pl.program_id(1), pl.program_id(2)
        sq = bq // QSUB
        lane = lax.broadcasted_iota(jnp.int32, (sq, 128), 1)
        heads = [slice(e * HD, (e + 1) * HD) for e in range(2)]
        chains = [(e, sl, slice(r * sq, (r + 1) * sq)) for e, sl in enumerate(heads) for r in range(QSUB)]
        qs = [q_ref[rows, sl] * 0.125 for _, sl, rows in chains]

        def block(j, carry, diagonal):
            ks = pl.multiple_of(j * bk, bk)
            out = []
            for n, (e, sl, rows) in enumerate(chains):
                m, l, acc = carry[n]
                s = _dot(qs[n], k_ref[pl.ds(ks, bk), sl], 'nt') - cum_ref[0, 2 * hp + e, pl.ds(j, 1), :]
                if diagonal:
                    keep = (lax.broadcasted_iota(jnp.int32, (sq, bk), 0) + rows.start
                            >= lax.broadcasted_iota(jnp.int32, (sq, bk), 1))
                    s = jnp.where(keep, s, NEG)
                m_new = jnp.maximum(m, jnp.max(s, axis=1, keepdims=True))
                p = jnp.exp(s - m_new)
                alpha = jnp.exp(m - m_new)
                l = alpha * l + jnp.sum(p, axis=1, keepdims=True)
                acc = alpha * acc + _dot(p.astype(BF), v_ref[pl.ds(ks, bk), sl], 'nn')
                out.append((m_new, l, acc))
            return tuple(out)

        init = tuple((jnp.full((sq, 1), NEG, F32), jnp.zeros((sq, 1), F32), jnp.zeros((sq, HD), F32)) for _ in chains)
        carry = lax.fori_loop(0, i, lambda j, c: block(j, c, False), init)
        carry = block(i, carry, True)
        for r in range(QSUB):
            lse_tile = jnp.zeros((sq, 128), F32)
            for n, (e, sl, rows) in enumerate(chains):
                if rows.start == r * sq:
                    m, l, acc = carry[n]
                    o_ref[rows, sl] = (acc / l).astype(BF)
                    lse_tile = jnp.where(lane == e, m + jnp.log(l), lse_tile)
            lse_ref[r * sq:(r + 1) * sq, :] = lse_tile

    return _carried_call(body, name="fox_fwd", grid=(B, 4, nq),
                         in_specs=[pl.BlockSpec((bq, 128), lambda b, h, i: (b * nq + i, qoff + h)),
                                   pl.BlockSpec((T, 128), lambda b, h, i: (b, qoff + 4 + h)),
                                   pl.BlockSpec((T, 128), lambda b, h, i: (b, qoff + 8 + h)),
                                   pl.BlockSpec((1, NG, nkb, bk), lambda b, h, i: (b, 0, 0, 0))],
                         out_specs=[pl.BlockSpec((bq, 128), lambda b, h, i: (b * nq + i, h)),
                                    pl.BlockSpec((bq, 128), lambda b, h, i: (b * nq + i, h))],
                         out_shape=[jax.ShapeDtypeStruct((N, FOXW), BF), jax.ShapeDtypeStruct((N, FOXW), F32)],
                         scratch_shapes=[], operands=(qkv, qkv, qkv, cum4), carry=carry)


def _fox_bwd(qkv, cum4, o, lse, dcat, B, T, qoff, bq, bk, carry=None):
    nq, nkb = T // bq, T // bk
    N = B * T

    def body(q_ref, k_ref, v_ref, cum_ref, o_ref, lse_ref, do_ref, dq_ref, dk_ref, dv_ref, dcum_ref, dcq_ref,
             dq_acc, dl_ref, rs_ref):
        hp = pl.program_id(1)
        heads = [slice(e * HD, (e + 1) * HD) for e in range(2)]
        keep = lax.broadcasted_iota(jnp.int32, (bq, bk), 0) >= lax.broadcasted_iota(jnp.int32, (bq, bk), 1)
        dcq_ref[...] = jnp.zeros(dcq_ref.shape, F32)
        dq_acc[...] = jnp.zeros(dq_acc.shape, F32)
        rs_ref[...] = jnp.zeros(rs_ref.shape, F32)
        for e, sl in enumerate(heads):
            dl_ref[e] = jnp.sum(do_ref[:, sl].astype(F32) * o_ref[:, sl].astype(F32), axis=1, keepdims=True)
        for j in range(nkb):
            krows = slice(j * bk, (j + 1) * bk)

            def tile(i, carry, diagonal):
                qs = i * bq if diagonal else pl.multiple_of(i * bq, bq)
                out = []
                for e, sl in enumerate(heads):
                    dk_a, dv_a, cs = carry[e]
                    q, k = q_ref[pl.ds(qs, bq), sl], k_ref[krows, sl]
                    do = do_ref[pl.ds(qs, bq), sl]
                    s = _dot(q, k, 'nt') * 0.125 - cum_ref[0, 2 * hp + e, j:j + 1, :]
                    p = jnp.exp(s - lse_ref[pl.ds(qs, bq), e:e + 1])
                    if diagonal:
                        p = jnp.where(keep, p, 0.0)
                    dv_a = dv_a + _dot(p.astype(BF), do, 'tn')
                    ds = p * (_dot(do, v_ref[krows, sl], 'nt') - dl_ref[e, pl.ds(qs, bq), :])
                    cs = cs + jnp.sum(ds, axis=0, keepdims=True)
                    rs_ref[e, pl.ds(qs, bq), :] += jnp.sum(ds, axis=1, keepdims=True)
                    dsb = ds.astype(BF)
                    dk_a = dk_a + _dot(dsb, q, 'tn')
                    dq_acc[e, pl.ds(qs, bq), :] += _dot(dsb, k, 'nn')
                    out.append((dk_a, dv_a, cs))
                return tuple(out)

            init = tuple((jnp.zeros((bk, HD), F32), jnp.zeros((bk, HD), F32), jnp.zeros((1, bk), F32)) for _ in heads)
            carry = lax.fori_loop(j + 1, nq, lambda i, c: tile(i, c, False), tile(j, init, True))
            for e, sl in enumerate(heads):
                dk_a, dv_a, cs = carry[e]
                dk_ref[krows, sl] = (dk_a * 0.125).astype(BF)
                dv_ref[krows, sl] = dv_a.astype(BF)
                dcum_ref[0, e, j:j + 1, :] = -cs
        for e, sl in enumerate(heads):
            dq_ref[:, sl] = (dq_acc[e] * 0.125).astype(BF)
            dcq_ref[:, e:e + 1] = rs_ref[e]

    seq = lambda off: pl.BlockSpec((T, 128), lambda b, h, off=off: (b, off + h))
    return _carried_call(body, name="fox_bwd", grid=(B, 4),
                         in_specs=[seq(qoff), seq(qoff + 4), seq(qoff + 8),
                                   pl.BlockSpec((1, NG, nkb, bk), lambda b, h: (b, 0, 0, 0)),
                                   seq(0), seq(0), seq(0)],
                         out_specs=[seq(0), seq(0), seq(0),
                                    pl.BlockSpec((1, 2, nkb, bk), lambda b, h: (b, h, 0, 0)), seq(0)],
                         out_shape=[jax.ShapeDtypeStruct((N, FOXW), BF)] * 3
                         + [jax.ShapeDtypeStruct((B, NG, nkb, bk), F32), jax.ShapeDtypeStruct((N, FOXW), F32)],
                         scratch_shapes=[pltpu.VMEM((2, T, HD), F32), pltpu.VMEM((2, T, 1), F32),
                                         pltpu.VMEM((2, T, 1), F32)],
                         operands=(qkv, qkv, qkv, cum4, o, lse, dcat), carry=carry)


_GC = math.sqrt(2.0 / math.pi)
_GA = 0.044715


def _gelu(z):
    return 0.5 * z * (1.0 + jnp.tanh(_GC * (z + _GA * z * z * z)))


def _gelu_grad(z):
    t = jnp.tanh(_GC * (z + _GA * z * z * z))
    return 0.5 * (1.0 + t) + 0.5 * z * (1.0 - t * t) * (_GC * (1.0 + 3.0 * _GA * z * z))


def _gmlp_common(z, lng, lnb):
    zg = _gelu(z)
    u, vg = zg[:, :GW], zg[:, GW:]
    mu = jnp.mean(vg, axis=-1, keepdims=True)
    xc = vg - mu
    rstd = lax.rsqrt(jnp.mean(xc * xc, axis=-1, keepdims=True) + EPS)
    xhat = xc * rstd
    return u, xhat, rstd, xhat * lng + lnb


def _tril_w(ws_ref):
    tri = lax.broadcasted_iota(jnp.int32, (CH, CH), 0) >= lax.broadcasted_iota(jnp.int32, (CH, CH), 1)
    return [jnp.where(tri, ws_ref[g], 0.0).astype(BF) for g in range(NG)], tri


def _split_pair(vp):
    lane = lax.broadcasted_iota(jnp.int32, vp.shape, 1)
    zero = jnp.zeros(vp.shape, vp.dtype)
    return jnp.concatenate([jnp.where(lane < HD, vp, zero), jnp.where(lane >= HD, vp, zero)], axis=0)


def _gmlp_mix(wt, vgn_b):
    outs = []
    for p in range(NG // 2):
        wcat = jnp.concatenate([wt[2 * p], wt[2 * p + 1]], axis=1)
        outs.append(_dot(wcat, _split_pair(vgn_b[:, 128 * p:128 * (p + 1)]), 'nn'))
    return jnp.concatenate(outs, axis=1)


def _gmlp_fwd(z, lng, lnb, ws, bfull, bt):
    N = z.shape[0]

    def body(z_ref, lng_ref, lnb_ref, ws_ref, bf_ref, o_ref):
        wt, _ = _tril_w(ws_ref)
        for c in range(bt // CH):
            rows = slice(c * CH, (c + 1) * CH)
            u, _, _, vgn = _gmlp_common(z_ref[rows, :], lng_ref[...], lnb_ref[...])
            mixed = _gmlp_mix(wt, vgn.astype(BF)) + bf_ref[...]
            o_ref[rows, :] = (u * mixed).astype(BF)

    full = lambda shp: pl.BlockSpec(shp, lambda i: (0,) * len(shp))
    return _pcall(body, name="gmlp_fwd", grid=(N // bt,),
                  in_specs=[pl.BlockSpec((bt, D), lambda i: (i, 0)), full((1, GW)), full((1, GW)),
                            full((NG, CH, CH)), full((CH, GW))],
                  out_specs=pl.BlockSpec((bt, GW), lambda i: (i, 0)), out_shape=jax.ShapeDtypeStruct((N, GW), BF),
                  compiler_params=_params(("arbitrary",)))(z, lng, lnb, ws, bfull)


def _gmlp_bwd(z, dcat, lng, lnb, ws, bfull, bt):
    N = z.shape[0]

    def body(z_ref, da_ref, lng_ref, lnb_ref, ws_ref, bf_ref, dz_ref, dg_ref, db_ref, dws_ref, dbf_ref):
        @pl.when(pl.program_id(0) == 0)
        def _():
            for r in (dg_ref, db_ref, dws_ref, dbf_ref):
                r[...] = jnp.zeros(r.shape, F32)

        wt, tri = _tril_w(ws_ref)
        lane = lax.broadcasted_iota(jnp.int32, (CH, 128), 1)
        for c in range(bt // CH):
            rows = slice(c * CH, (c + 1) * CH)
            zc = z_ref[rows, :]
            u, xhat, rstd, vgn = _gmlp_common(zc, lng_ref[...], lnb_ref[...])
            vgn_b = vgn.astype(BF)
            mixed = _gmlp_mix(wt, vgn_b) + bf_ref[...]
            da = da_ref[rows, :].astype(F32)
            dmix = da * u
            du = da * mixed
            dbf_ref[...] += dmix
            dvs = []
            for p in range(NG // 2):
                cols = slice(128 * p, 128 * (p + 1))
                dmp = dmix[:, cols].astype(BF)
                dwp = _dot(_split_pair(dmp), vgn_b[:, cols], 'nt')
                dws_ref[2 * p] += jnp.where(tri, dwp[:CH], 0.0)
                dws_ref[2 * p + 1] += jnp.where(tri, dwp[CH:], 0.0)
                dvs.append(jnp.where(lane < HD, _dot(wt[2 * p], dmp, 'tn'), _dot(wt[2 * p + 1], dmp, 'tn')))
            dvgn = jnp.concatenate(dvs, axis=1)
            dg_ref[...] += jnp.sum(dvgn * xhat, axis=0, keepdims=True)
            db_ref[...] += jnp.sum(dvgn, axis=0, keepdims=True)
            dxh = dvgn * lng_ref[...]
            dvg = rstd * (dxh - jnp.mean(dxh, axis=-1, keepdims=True)
                          - xhat * jnp.mean(dxh * xhat, axis=-1, keepdims=True))
            dz_ref[rows, :] = (jnp.concatenate([du, dvg], axis=1) * _gelu_grad(zc)).astype(BF)

    full = lambda shp: pl.BlockSpec(shp, lambda i: (0,) * len(shp))
    return _pcall(body, name="gmlp_bwd", grid=(N // bt,),
                  in_specs=[pl.BlockSpec((bt, D), lambda i: (i, 0)), pl.BlockSpec((bt, GW), lambda i: (i, 1)),
                            full((1, GW)), full((1, GW)), full((NG, CH, CH)), full((CH, GW))],
                  out_specs=[pl.BlockSpec((bt, D), lambda i: (i, 0)), full((1, GW)), full((1, GW)),
                             full((NG, CH, CH)), full((CH, GW))],
                  out_shape=[jax.ShapeDtypeStruct((N, D), BF), jax.ShapeDtypeStruct((1, GW), F32),
                             jax.ShapeDtypeStruct((1, GW), F32), jax.ShapeDtypeStruct((NG, CH, CH), F32),
                             jax.ShapeDtypeStruct((CH, GW), F32)],
                  compiler_params=_params(("arbitrary",)))(z, dcat, lng, lnb, ws, bfull)


def _group_sum(name, a):
    def body(a_ref, o_ref):
        lane = lax.broadcasted_iota(jnp.int32, (CH, 128), 1)
        out = jnp.zeros((CH, 128), F32)
        for g in range(NG):
            out = jnp.where(lane == g, jnp.sum(a_ref[:, g * HD:(g + 1) * HD], axis=1, keepdims=True), out)
        o_ref[...] = out

    return _pcall(body, name=name, out_shape=jax.ShapeDtypeStruct((CH, 128), F32))(a)


def _xattn_softmax(q_h, k_h):
    s = _dot(q_h, k_h, 'nt') * (XD ** -0.5)
    p = jnp.exp(s - jnp.max(s, axis=1, keepdims=True))
    return p / jnp.sum(p, axis=1, keepdims=True)


def _xattn_fwd(name, q, kv, B, T, bq):
    nq = T // bq

    def body(q_ref, kv_ref, o_ref):
        for h in range(XH):
            cols = slice(h * XD, (h + 1) * XD)
            p = _xattn_softmax(q_ref[:, cols], kv_ref[:, cols])
            o_ref[:, cols] = _dot(p.astype(BF), kv_ref[:, D + h * XD:D + (h + 1) * XD], 'nn').astype(BF)

    return _pcall(body, name=name, grid=(B, nq),
                  in_specs=[pl.BlockSpec((bq, D), lambda b, i: (b * nq + i, 0)),
                            pl.BlockSpec((NMEM, 2 * D), lambda b, i: (b, 0))],
                  out_specs=pl.BlockSpec((bq, D), lambda b, i: (b * nq + i, 0)),
                  out_shape=jax.ShapeDtypeStruct((B * T, D), BF), compiler_params=_params(("arbitrary", "arbitrary")))(q, kv)


def _xattn_bwd(name, q, kv, do, B, T, bq):
    nq = T // bq
    sc = XD ** -0.5

    def body(q_ref, kv_ref, do_ref, dq_ref, dkv_ref):
        @pl.when(pl.program_id(1) == 0)
        def _():
            dkv_ref[...] = jnp.zeros(dkv_ref.shape, F32)

        for h in range(XH):
            cols = slice(h * XD, (h + 1) * XD)
            vcols = slice(D + h * XD, D + (h + 1) * XD)
            qh, kh, doh = q_ref[:, cols], kv_ref[:, cols], do_ref[:, cols]
            p = _xattn_softmax(qh, kh)
            dp = _dot(doh, kv_ref[:, vcols], 'nt')
            ds = p * (dp - jnp.sum(p * dp, axis=1, keepdims=True))
            dsb = ds.astype(BF)
            dq_ref[:, cols] = (_dot(dsb, kh, 'nn') * sc).astype(BF)
            dkv_ref[:, cols] += _dot(dsb, qh, 'tn') * sc
            dkv_ref[:, vcols] += _dot(p.astype(BF), doh, 'tn')

    blk = pl.BlockSpec((bq, D), lambda b, i: (b * nq + i, 0))
    return _pcall(body, name=name, grid=(B, nq),
                  in_specs=[blk, pl.BlockSpec((NMEM, 2 * D), lambda b, i: (b, 0)), blk],
                  out_specs=[blk, pl.BlockSpec((NMEM, 2 * D), lambda b, i: (b, 0))],
                  out_shape=[jax.ShapeDtypeStruct((B * T, D), BF), jax.ShapeDtypeStruct((B * NMEM, 2 * D), F32)],
                  compiler_params=_params(("arbitrary", "arbitrary")))(q, kv, do)


def _ln_stats(v):
    mu = jnp.mean(v, axis=-1, keepdims=True)
    xc = v - mu
    rstd = lax.rsqrt(jnp.mean(xc * xc, axis=-1, keepdims=True) + EPS)
    return xc * rstd, rstd


SUB = 8


LANES = 128
NSTRIP = D // LANES


def _fill_window(win, parts):
    for s in range(NSTRIP):
        for r0, val in parts:
            win[s, r0:r0 + val.shape[0], :] = val[:, s * LANES:(s + 1) * LANES]


def _fill_phases(win, sh, rows):
    for b in range(1, SUB):
        for s in range(NSTRIP):
            sh[b - 1, s] = win[s, b:b + rows, :]


def _fill_taps(w8, w_ref):
    for s in range(NSTRIP):
        for j in range(CK):
            w8[s, SUB * j:SUB * (j + 1), :] = jnp.broadcast_to(w_ref[j:j + 1, s * LANES:(s + 1) * LANES], (SUB, LANES))


def _row_groups(win, sh, s):
    cache = {}

    def get(o, t):
        a, b = divmod(o, SUB)
        key = (b, t + a)
        if key not in cache:
            rows = slice(SUB * (t + a), SUB * (t + a + 1))
            cache[key] = win[s, rows, :] if b == 0 else sh[b - 1, s, rows, :]
        return cache[key]

    return get


def _from_strips(ref):
    return jnp.concatenate([ref[s] for s in range(NSTRIP)], axis=1)


def _sum_groups(name, a):
    R, C = a.shape[0] // SUB, a.shape[1]

    def body(a_ref, o_ref):
        o_ref[...] = jnp.sum(a_ref[...].reshape(R, SUB, C), axis=1)

    return _pcall(body, name=name, out_shape=jax.ShapeDtypeStruct((R, C), F32))(a)


def _conv_fwd(y, w32, wb, lng, lnb, B, T, bt):
    nt = T // bt
    hb = bt // HALO
    prows = bt + HALO - SUB

    def body(y_ref, yp_ref, w_ref, wb_ref, lng_ref, lnb_ref, s_ref, yc_ref, win, sh, w8, out):
        i = pl.program_id(1)
        _fill_window(win, [(0, jnp.where(i > 0, yp_ref[...], 0.0)), (HALO, y_ref[...])])
        _fill_phases(win, sh, prows)

        @pl.when((pl.program_id(0) == 0) & (i == 0))
        def _():
            _fill_taps(w8, w_ref)

        def strip(s, carry):
            get = _row_groups(win, sh, s)
            for t in range(bt // SUB):
                accs = [jnp.zeros((SUB, LANES), F32), jnp.zeros((SUB, LANES), F32)]
                for j in range(CK):
                    accs[j % 2] = accs[j % 2] + w8[s, SUB * j:SUB * (j + 1), :] * get(HALO - (CK - 1) + j, t)
                out[s, SUB * t:SUB * (t + 1), :] = accs[0] + accs[1]
            return carry

        lax.fori_loop(0, NSTRIP, strip, 0)
        acc = _from_strips(out) + wb_ref[...]
        yc_ref[...] = acc
        xhat, _ = _ln_stats(acc)
        ln = xhat * lng_ref[...] + lnb_ref[...]
        s_ref[...] = (ln * _sigmoid(ln)).astype(BF)

    row = lambda n: pl.BlockSpec((n, D), lambda b, i: (0, 0))
    cur = pl.BlockSpec((bt, D), lambda b, i: (b * nt + i, 0))
    return _pcall(body, name="conv_fwd", grid=(B, nt),
                  in_specs=[cur, pl.BlockSpec((HALO, D), lambda b, i: (jnp.maximum((b * nt + i) * hb - 1, 0), 0)),
                            row(HALO), row(1), row(1), row(1)],
                  out_specs=[cur, cur],
                  out_shape=[jax.ShapeDtypeStruct((B * T, D), BF), jax.ShapeDtypeStruct((B * T, D), F32)],
                  scratch_shapes=[pltpu.VMEM((NSTRIP, bt + HALO, LANES), F32),
                                  pltpu.VMEM((SUB - 1, NSTRIP, prows, LANES), F32),
                                  pltpu.VMEM((NSTRIP, HALO * SUB, LANES), F32), pltpu.VMEM((NSTRIP, bt, LANES), F32)],
                  compiler_params=_params(("arbitrary", "arbitrary")))(y, y, w32, wb, lng, lnb)


def _conv_bwd(ds, yc, y, pa, pg, w32, lng, lnb, B, T, bt):
    nt = T // bt
    hb = bt // HALO
    nblk32 = B * T // HALO

    def ln_bwd(dsv, ycv, lng, lnb):
        xhat, rstd = _ln_stats(ycv)
        ln = xhat * lng + lnb
        sg = _sigmoid(ln)
        dln = dsv * (sg * (1.0 + ln * (1.0 - sg)))
        dxh = dln * lng
        dyc = rstd * (dxh - jnp.mean(dxh, axis=-1, keepdims=True)
                      - xhat * jnp.mean(dxh * xhat, axis=-1, keepdims=True))
        return dyc, dln, xhat

    prows = bt + HALO - SUB

    def body(ds_ref, dsn_ref, yc_ref, ycn_ref, y_ref, yp_ref, pa_ref, pg_ref, w_ref, lng_ref, lnb_ref,
             dpa_ref, dpg_ref, dw_ref, dwb_ref, dlng_ref, dlnb_ref, dba_ref, dbg_ref,
             dwin, ywin, dsh, ysh, w8, dy_out, dw_out):
        i = pl.program_id(1)

        @pl.when((pl.program_id(0) == 0) & (i == 0))
        def _():
            for r in (dw_ref, dwb_ref, dlng_ref, dlnb_ref, dba_ref, dbg_ref):
                r[...] = jnp.zeros(r.shape, F32)
            _fill_taps(w8, w_ref)

        lng, lnb = lng_ref[...], lnb_ref[...]
        dyc, dln, xhat = ln_bwd(ds_ref[...].astype(F32), yc_ref[...], lng, lnb)
        dycn, _, _ = ln_bwd(dsn_ref[...].astype(F32), ycn_ref[...], lng, lnb)
        _fill_window(dwin, [(0, dyc), (bt, jnp.where(i < nt - 1, dycn, 0.0))])
        _fill_window(ywin, [(0, jnp.where(i > 0, yp_ref[...], 0.0)), (HALO, y_ref[...])])
        dlng_ref[...] += jnp.sum(dln * xhat, axis=0, keepdims=True)
        dlnb_ref[...] += jnp.sum(dln, axis=0, keepdims=True)
        dwb_ref[...] += jnp.sum(dyc, axis=0, keepdims=True)
        _fill_phases(dwin, dsh, prows)
        _fill_phases(ywin, ysh, prows)

        def strip(s, carry):
            get_d, get_y = _row_groups(dwin, dsh, s), _row_groups(ywin, ysh, s)
            dw_acc = [jnp.zeros((SUB, LANES), F32) for _ in range(CK)]
            for t in range(bt // SUB):
                dyc_g = get_d(0, t)
                dys = [jnp.zeros((SUB, LANES), F32), jnp.zeros((SUB, LANES), F32)]
                for j in range(CK):
                    dys[j % 2] = dys[j % 2] + w8[s, SUB * j:SUB * (j + 1), :] * get_d(CK - 1 - j, t)
                    dw_acc[j] = dw_acc[j] + dyc_g * get_y(HALO - (CK - 1) + j, t)
                dy_out[s, SUB * t:SUB * (t + 1), :] = dys[0] + dys[1]
            for j in range(CK):
                dw_out[s, SUB * j:SUB * (j + 1), :] = dw_acc[j]
            return carry

        lax.fori_loop(0, NSTRIP, strip, 0)
        dw_ref[0:CK * SUB, :] += _from_strips(dw_out)
        dy = _from_strips(dy_out)
        a, g = pa_ref[...].astype(F32), pg_ref[...].astype(F32)
        sg = _sigmoid(g)
        da = dy * sg
        dg = dy * a * sg * (1.0 - sg)
        dpa_ref[...] = da.astype(BF)
        dpg_ref[...] = dg.astype(BF)
        dba_ref[...] += jnp.sum(da, axis=0, keepdims=True)
        dbg_ref[...] += jnp.sum(dg, axis=0, keepdims=True)

    cur = pl.BlockSpec((bt, D), lambda b, i: (b * nt + i, 0))
    nxt = pl.BlockSpec((HALO, D), lambda b, i: (jnp.minimum((b * nt + i + 1) * hb, nblk32 - 1), 0))
    prv = pl.BlockSpec((HALO, D), lambda b, i: (jnp.maximum((b * nt + i) * hb - 1, 0), 0))
    row = lambda n: pl.BlockSpec((n, D), lambda b, i: (0, 0))
    N = B * T
    return _pcall(body, name="conv_bwd", grid=(B, nt),
                  in_specs=[cur, nxt, cur, nxt, cur, prv, cur, cur, row(HALO), row(1), row(1)],
                  out_specs=[cur, cur, row(HALO * SUB), row(1), row(1), row(1), row(1), row(1)],
                  out_shape=[jax.ShapeDtypeStruct((N, D), BF)] * 2 + [jax.ShapeDtypeStruct((HALO * SUB, D), F32)]
                  + [jax.ShapeDtypeStruct((1, D), F32)] * 5,
                  scratch_shapes=[pltpu.VMEM((NSTRIP, bt + HALO, LANES), F32), pltpu.VMEM((NSTRIP, bt + HALO, LANES), F32),
                                  pltpu.VMEM((SUB - 1, NSTRIP, prows, LANES), F32),
                                  pltpu.VMEM((SUB - 1, NSTRIP, prows, LANES), F32),
                                  pltpu.VMEM((NSTRIP, HALO * SUB, LANES), F32), pltpu.VMEM((NSTRIP, bt, LANES), F32),
                                  pltpu.VMEM((NSTRIP, CK * SUB, LANES), F32)],
                  compiler_params=_params(("arbitrary", "arbitrary")))(ds, ds, yc, yc, y, y, pa, pg, w32, lng, lnb)


def _head(x, tgt, gain, bt=512):
    N = x.shape[0]
    bt = min(bt, N)

    def body(x_ref, t_ref, g_ref, dx_ref, loss_ref, dg_ref):
        @pl.when(pl.program_id(0) == 0)
        def _():
            loss_ref[...] = jnp.zeros(loss_ref.shape, F32)
            dg_ref[...] = jnp.zeros(dg_ref.shape, F32)

        xv = x_ref[...]
        gain = g_ref[...]
        err = xv * _rms_stats(xv) * gain - t_ref[...]
        loss_ref[...] += 0.5 * jnp.sum(jnp.mean(err * err, axis=-1, keepdims=True), axis=0, keepdims=True)
        dx, dgr = _rms_bwd(xv, gain, err * (1.0 / D))
        dx_ref[...] = dx
        dg_ref[...] += jnp.sum(dgr, axis=0, keepdims=True)

    blk = pl.BlockSpec((bt, D), lambda i: (i, 0))
    return _pcall(body, name="loss_head", grid=(N // bt,),
                  in_specs=[blk, blk, pl.BlockSpec((1, D), lambda i: (0, 0))],
                  out_specs=[blk, pl.BlockSpec((1, 128), lambda i: (0, 0)), pl.BlockSpec((1, D), lambda i: (0, 0))],
                  out_shape=[jax.ShapeDtypeStruct((N, D), F32), jax.ShapeDtypeStruct((1, 128), F32),
                             jax.ShapeDtypeStruct((1, D), F32)],
                  compiler_params=_params(("arbitrary",)))(x, tgt, gain)


def _local_step(x, mem, tgt, Wb, P, hooks=None):
    B, T, _ = x.shape
    N = B * T
    bq = bk = min(512, T)
    bt = min(512, T)
    x0 = x.reshape(N, D)
    mem2 = mem.reshape(B * NMEM, D)
    tgt2 = tgt.reshape(N, D)
    row = lambda v: v.reshape(1, -1)
    G = {}

    w_in = Wb['w_in_e'][0]
    w_inp = jnp.concatenate([w_in[:, 3 * FOXW + NG:], w_in[:, :3 * FOXW], w_in[:, 3 * FOXW:3 * FOXW + NG],
                             jnp.zeros((D, 128 - NG), BF)], axis=1)
    g_e = row(P['mix_norm_e'])
    z, h0 = _norm_mm("proj_z", x0, g_e, w_inp, N=D, coff=0, bn=D, out_dtype=F32, h_out=True)
    qkv = _norm_mm("proj_qkv", x0, g_e, w_inp, N=3 * FOXW, coff=2, bn=FOXW, out_dtype=BF)[0]
    fl = _norm_mm("proj_f", x0, g_e, w_inp, N=128, coff=20, bn=128, out_dtype=F32)[0]
    fbias = jnp.concatenate([P['fox_f_bias'].reshape(1, NG), jnp.zeros((1, 128 - NG), F32)], axis=1)
    cum = _fox_gate_fwd(fl, fbias, B, T)
    cum4 = cum[:, :NG].reshape(B, T, NG).transpose(0, 2, 1).reshape(B, NG, T // bk, bk)
    (b_out, lse), arrived = _fox_fwd(qkv, cum4, B, T, 0, bq, bk, carry=hooks['fwd_carry']() if hooks else None)
    if hooks:
        Wb = {**Wb, **hooks['fwd_done'](arrived)}
    lng, lnb = row(P['gmlp_ln_g']), row(P['gmlp_ln_b'])
    ws = P['gmlp_w_s'][0]
    bfull = jnp.repeat(P['gmlp_b_s'][0].T, HD, axis=1)
    a_out = _gmlp_fwd(z, lng, lnb, ws, bfull, bt)
    w_out = Wb['w_out_e'][0]
    x1 = _mm_resid("mix_out", [dict(A=b_out, Ka=FOXW, B=w_out, roff=0), dict(A=a_out, Ka=GW, B=w_out, roff=1)], x0)

    def xa_ffn_fwd(l, xin):
        qx, hq = _norm_mm(f"xa_q{l}", xin, row(P['xa_norm'][l]), Wb['xa_wq'][l], N=D, bn=D, out_dtype=BF, h_out=True)
        kv, hm = _norm_mm(f"xa_kv{l}", mem2, row(P['mem_norm'][l]), Wb['xa_wkv'][l], N=2 * D, bn=D, out_dtype=BF,
                          h_out=True)
        o = _xattn_fwd(f"xattn_fwd{l}", qx, kv, B, T, bt)
        xm = _mm_resid(f"xa_o{l}", [dict(A=o, Ka=D, B=Wb['xa_wo'][l])], xin)
        wgu = Wb['ffn_w_gu'][l]
        carry = hooks['fwd2_carry']() if hooks and l == 0 else None
        res = _fused_mm(f"ffn_gu{l}", dims='nn', M=N, N=FF, bm=min(512, N), bn=FF // 2, x=xm,
                        gain=row(P['ffn_norm'][l]),
                        groups=[[dict(A=None, Ka=D, B=wgu, coff=0)], [dict(A=None, Ka=D, B=wgu, coff=2)]],
                        epi=_epi_swiglu, outs=[BF, BF, BF], h_out=True, carry=carry, cols_outer=True)
        if carry is not None:
            res, arrived = res
            Wb.update(hooks['fwd2_done'](arrived))
        g, u, a, hf = res
        xo = _mm_resid(f"ffn_down{l}", [dict(A=a, Ka=FF, B=Wb['ffn_w_down'][l])], xm)
        return xo, dict(xin=xin, qx=qx, hq=hq, kv=kv, hm=hm, o=o, xm=xm, g=g, u=u, a=a, hf=hf)

    x3, S0 = xa_ffn_fwd(0, x1)
    w_cin = Wb['conv_w_in'][0]
    b_cin = row(P['conv_b_in'])
    pa, pg, y, hc = _fused_mm("conv_in", dims='nn', M=N, N=D, bm=min(512, N), bn=D, x=x3, gain=row(P['mix_norm_o']),
                              groups=[[dict(A=None, Ka=D, B=w_cin, coff=0)], [dict(A=None, Ka=D, B=w_cin, coff=1)]],
                              epi=_epi_glu, outs=[BF, BF, F32], rows=[(b_cin, 0), (b_cin, 1)], h_out=True)
    w32 = jnp.concatenate([P['conv_dw_w'][0], jnp.zeros((HALO - CK, D), F32)], axis=0)
    cbt = min(256, T)
    s, yc = _conv_fwd(y, w32, row(P['conv_dw_b']), row(P['conv_ln_g']), row(P['conv_ln_b']), B, T, cbt)
    x4 = _mm_resid("conv_out", [dict(A=s, Ka=D, B=Wb['conv_w_out'][0])], x3, bias=row(P['conv_b_out']))
    x6, S1 = xa_ffn_fwd(1, x4)
    dx, loss_t, dgf = _head(x6, tgt2, row(P['final_norm']))
    G['final_norm'] = dgf.reshape(D)

    def by_rows(dw):
        return dw.reshape(NCHIP, dw.shape[1] // NCHIP, dw.shape[2])

    def xa_ffn_bwd(l, S, dx):
        wgu, wdown = Wb['ffn_w_gu'][l], Wb['ffn_w_down'][l]
        dwdown = by_rows(_mm_tn(f"dw_down{l}", S['a'], dx))
        dg, du = _fused_mm(f"ffn_dgu{l}", dims='nt', M=N, N=FF, bm=min(512, N), bn=FF // 2,
                           groups=[[dict(A=dx, Ka=D, B=wdown)]], epi=_epi_swiglu_bwd, outs=[BF, BF],
                           tiles=[(S['g'], 0), (S['u'], 0)], cols_outer=True)
        dwgu = jnp.concatenate([_mm_tn(f"dw_g{l}", S['hf'], dg, parts=2), _mm_tn(f"dw_u{l}", S['hf'], du, parts=2)])
        dx, dgn = _mm_nt_rms_bwd(f"ffn_dx{l}", [dict(A=dg, Ka=FF, B=wgu, coff=0), dict(A=du, Ka=FF, B=wgu, coff=1)],
                                 S['xm'], row(P['ffn_norm'][l]), dx)
        dwo = by_rows(_mm_tn(f"dw_o{l}", S['o'], dx))
        do = _mm_nt_plain(f"xa_do{l}", dx, Wb['xa_wo'][l])
        dq, dkv = _xattn_bwd(f"xattn_bwd{l}", S['qx'], S['kv'], do, B, T, bt)
        dwq = by_rows(_mm_tn(f"dw_q{l}", S['hq'], dq))
        dwkv = _mm_tn(f"dw_kv{l}", S['hm'], dkv, parts=NCHIP)
        dmn = _fused_mm(f"xa_dmem{l}", dims='nt', M=B * NMEM, N=D, bm=min(256, B * NMEM), bn=D,
                        groups=[[dict(A=dkv, Ka=2 * D, B=Wb['xa_wkv'][l])]], epi=_epi_rms_gain_only, outs=[],
                        tiles=[(mem2, 0)], rows=[(row(P['mem_norm'][l]), 0)], reds=[(1, D)])[0]
        dx, dxn = _mm_nt_rms_bwd(f"xa_dx{l}", [dict(A=dq, Ka=D, B=Wb['xa_wq'][l])], S['xin'],
                                 row(P['xa_norm'][l]), dx)
        return dx, dict(ffn_w_down=dwdown, ffn_w_gu=dwgu, ffn_norm=dgn.reshape(D), xa_wo=dwo, xa_wq=dwq,
                        xa_wkv=dwkv, mem_norm=dmn.reshape(D), xa_norm=dxn.reshape(D))

    dx, G1 = xa_ffn_bwd(1, S1, dx)
    G['conv_w_out'] = [by_rows(_mm_tn("dw_cout", s, dx))]
    G['conv_b_out'] = _colsum("db_cout", dx)
    dsv = _mm_nt_plain("conv_ds", dx, Wb['conv_w_out'][0])
    dpa, dpg, dw32, dwb, dlng, dlnb, dba, dbg = _conv_bwd(dsv, yc, y, pa, pg, w32, row(P['conv_ln_g']),
                                                          row(P['conv_ln_b']), B, T, cbt)
    G['conv_dw_w'] = _sum_groups("conv_dw_sum", dw32)[:CK][None]
    G['conv_dw_b'], G['conv_ln_g'], G['conv_ln_b'] = dwb, dlng, dlnb
    G['conv_b_in'] = jnp.concatenate([dba, dbg], axis=1)
    G['conv_w_in'] = [jnp.concatenate([_mm_tn("dw_cin_a", hc, dpa, parts=2), _mm_tn("dw_cin_g", hc, dpg, parts=2)])]
    dx, dgo = _mm_nt_rms_bwd("conv_dx", [dict(A=dpa, Ka=D, B=w_cin, coff=0), dict(A=dpg, Ka=D, B=w_cin, coff=1)],
                             x3, row(P['mix_norm_o']), dx)
    G['mix_norm_o'] = dgo
    dx, G0 = xa_ffn_bwd(0, S0, dx)
    for k in G0:
        G[k] = [G0[k], G1[k]]
    G['w_out_e'] = [by_rows(jnp.concatenate([_mm_tn("dw_out_b", b_out, dx), _mm_tn("dw_out_a", a_out, dx)], axis=1))]
    dcat = _mm_nt_plain("mix_dcat", dx, w_out)
    (dq, dk, dv, dcum4, dcq4), arrived = _fox_bwd(qkv, cum4, b_out, lse, dcat, B, T, 0, bq, bk,
                                                  carry=hooks['bwd_carry'](G) if hooks else None)
    if hooks:
        hooks['bwd_done'](arrived)
    dz, dlg, dlb, dws, dbf = _gmlp_bwd(z, dcat, lng, lnb, ws, bfull, bt)
    G['gmlp_ln_g'], G['gmlp_ln_b'], G['gmlp_w_s'] = dlg, dlb, dws[None]
    G['gmlp_b_s'] = _group_sum("gmlp_db", dbf)[:, :NG].T[None]
    pad = jnp.zeros((N, 128 - NG), F32)
    dck = jnp.concatenate([dcum4.reshape(B, NG, T).transpose(0, 2, 1).reshape(N, NG), pad], axis=1)
    dcq = jnp.concatenate([dcq4.reshape(N, NG // 2, 128)[:, :, :2].reshape(N, NG), pad], axis=1)
    dfl, dfb = _fox_gate_bwd(fl, fbias, dcq, dck, B, T)
    G['fox_f_bias'] = dfb[:, :NG]
    dw_in = jnp.concatenate([_mm_tn("dw_in_q", h0, dq)[0], _mm_tn("dw_in_k", h0, dk)[0], _mm_tn("dw_in_v", h0, dv)[0],
                             _mm_tn("dw_in_f", h0, dfl)[0][:, :NG], _mm_tn("dw_in_z", h0, dz)[0]], axis=1)
    G['w_in_e'] = [dw_in.reshape(D, NCHIP, IN_W // NCHIP).transpose(1, 0, 2)]
    pairs = [dict(A=dz, Ka=D, B=w_inp, coff=0), dict(A=dq, Ka=FOXW, B=w_inp, coff=2),
             dict(A=dk, Ka=FOXW, B=w_inp, coff=3), dict(A=dv, Ka=FOXW, B=w_inp, coff=4),
             dict(A=dfl, Ka=128, B=w_inp, coff=20)]
    if hooks:
        dx, dge, arrived = _mm_nt_rms_bwd("mix_dx", pairs, x0, g_e, dx, carry=hooks['last_carry'](G))
        hooks['last_done'](arrived)
    else:
        dx, dge = _mm_nt_rms_bwd("mix_dx", pairs, x0, g_e, dx)
    G['mix_norm_e'] = dge
    return loss_t[0, 0], dx.reshape(B, T, D), G


COLS = 1024
ANY = pl.BlockSpec(memory_space=pl.ANY)


def _coords():
    return lax.axis_index("x"), lax.axis_index("y"), lax.axis_index("c")


def _other_chips(x, y):
    return [(1 - x, y), (x, 1 - y), (1 - x, 1 - y)]


def _own_slot(v, me):
    return lax.dynamic_update_slice(lax.empty((NCHIP,) + v.shape, v.dtype), v[None], (me,) + (0,) * v.ndim)


def _all_gather(name, shards, me):
    n = len(shards)
    bufs = [_own_slot(v, me) for v in shards]

    def body(*refs):
        out_refs, (send_sems, recv_sems) = refs[n:2 * n], refs[2 * n:]
        x, y, c = _coords()
        mine = 2 * x + y
        sib = (x, y, 1 - c)
        chips = _other_chips(x, y)

        def rcopy(a, k, chip_idx, half, to):
            blk = out_refs[a].at[chip_idx, half]
            return pltpu.make_async_remote_copy(src_ref=blk, dst_ref=blk, send_sem=send_sems.at[6 * a + k],
                                                recv_sem=recv_sems.at[6 * a + k], device_id=to, device_id_type=MESH)

        first = [rcopy(a, j, mine, c, (cx, cy, c)) for a in range(n) for j, (cx, cy) in enumerate(chips)]
        for cp in first:
            cp.start()
        passed = []
        for a in range(n):
            for j, (cx, cy) in enumerate(chips):
                kj = 2 * cx + cy
                rcopy(a, j, kj, c, sib).wait_recv()
                fwd = rcopy(a, 3 + j, kj, c, sib)
                fwd.start()
                passed.append(fwd)
        for a in range(n):
            for j, (cx, cy) in enumerate(chips):
                rcopy(a, 3 + j, 2 * cx + cy, 1 - c, sib).wait_recv()
        for cp in first + passed:
            cp.wait_send()

    return _pcall(body, name=name, in_specs=[ANY] * n, out_specs=[ANY] * n,
                  out_shape=[jax.ShapeDtypeStruct(b.shape, b.dtype) for b in bufs],
                  input_output_aliases={a: a for a in range(n)}, scratch_shapes=_sem_pairs(6 * n))(*bufs)


def _sem_pairs(n):
    return [pltpu.SemaphoreType.DMA((n,)), pltpu.SemaphoreType.DMA((n,))]


def _gather_forward(name, bufs):
    n = len(bufs)

    def body(*refs):
        out_refs, (send_sems, recv_sems) = refs[n:2 * n], refs[2 * n:]
        x, y, c = _coords()

        def cp(a, j, kj, half):
            blk = out_refs[a].at[kj, half]
            return pltpu.make_async_remote_copy(src_ref=blk, dst_ref=blk, send_sem=send_sems.at[3 * a + j],
                                                recv_sem=recv_sems.at[3 * a + j], device_id=(x, y, 1 - c),
                                                device_id_type=MESH)

        chips = [2 * cx + cy for cx, cy in _other_chips(x, y)]
        sends = [cp(a, j, kj, c) for a in range(n) for j, kj in enumerate(chips)]
        for s in sends:
            s.start()
        for a in range(n):
            for j, kj in enumerate(chips):
                cp(a, j, kj, 1 - c).wait_recv()
        for s in sends:
            s.wait_send()

    return _pcall(body, name=name, in_specs=[ANY] * n, out_specs=[ANY] * n,
                  out_shape=[jax.ShapeDtypeStruct(b.shape, b.dtype) for b in bufs],
                  input_output_aliases={a: a for a in range(n)}, scratch_shapes=_sem_pairs(3 * n))(*bufs)


def _sibling_halves(name, ps):
    n = len(ps)

    def body(*refs):
        p_refs, out_refs, (send_sems, recv_sems) = refs[:n], refs[n:2 * n], refs[2 * n:]
        x, y, c = _coords()
        cps = [pltpu.make_async_remote_copy(src_ref=p_refs[a].at[k, 1 - c], dst_ref=out_refs[a].at[k],
                                            send_sem=send_sems.at[4 * a + k], recv_sem=recv_sems.at[4 * a + k],
                                            device_id=(x, y, 1 - c), device_id_type=MESH)
               for a in range(n) for k in range(NCHIP)]
        for cp in cps:
            cp.start()
        for cp in cps:
            cp.wait()

    return _pcall(body, name=name, in_specs=[ANY] * n, out_specs=[ANY] * n,
                  out_shape=[jax.ShapeDtypeStruct((NCHIP,) + p.shape[2:], p.dtype) for p in ps],
                  scratch_shapes=_sem_pairs(NCHIP * n))(*ps)


def _chip_exchange(name, qs):
    n = len(qs)

    def body(*refs):
        q_refs, out_refs, (send_sems, recv_sems) = refs[:n], refs[n:2 * n], refs[2 * n:]
        x, y, c = _coords()
        cps = [pltpu.make_async_remote_copy(src_ref=q_refs[a].at[2 * cx + cy], dst_ref=out_refs[a].at[j],
                                            send_sem=send_sems.at[3 * a + j], recv_sem=recv_sems.at[3 * a + j],
                                            device_id=(cx, cy, c), device_id_type=MESH)
               for a in range(n) for j, (cx, cy) in enumerate(_other_chips(x, y))]
        for cp in cps:
            cp.start()
        for cp in cps:
            cp.wait()

    return _pcall(body, name=name, in_specs=[ANY] * n, out_specs=[ANY] * n,
                  out_shape=[jax.ShapeDtypeStruct((3,) + q.shape[1:], q.dtype) for q in qs],
                  scratch_shapes=_sem_pairs(3 * n))(*qs)


def _sibling_swap(name, hs):
    n = len(hs)

    def body(*refs):
        out_refs, (send_sems, recv_sems) = refs[n:2 * n], refs[2 * n:]
        x, y, c = _coords()
        sib = (x, y, 1 - c)
        sends = [pltpu.make_async_remote_copy(src_ref=out_refs[a].at[c], dst_ref=out_refs[a].at[c],
                                              send_sem=send_sems.at[a], recv_sem=recv_sems.at[a], device_id=sib,
                                              device_id_type=MESH) for a in range(n)]
        for cp in sends:
            cp.start()
        for a in range(n):
            theirs = out_refs[a].at[1 - c]
            pltpu.make_async_remote_copy(src_ref=theirs, dst_ref=theirs, send_sem=send_sems.at[a],
                                         recv_sem=recv_sems.at[a], device_id=sib, device_id_type=MESH).wait_recv()
        for cp in sends:
            cp.wait_send()

    return _pcall(body, name=name, in_specs=[ANY] * n, out_specs=[ANY] * n,
                  out_shape=[jax.ShapeDtypeStruct(h.shape, h.dtype) for h in hs],
                  input_output_aliases={a: a for a in range(n)}, scratch_shapes=_sem_pairs(n))(*hs)


ADD_BLOCK_BYTES = 2 * 1024 * 1024


def _row_block(R, C):
    if R * C * 4 <= ADD_BLOCK_BYTES:
        return R
    for br in (512, 256, 128, 64, 32, 16, 8):
        if R % br == 0 and br * C * 4 <= ADD_BLOCK_BYTES:
            return br
    return R


def _add_own_half(name, p, recv, c_arr, out_dtype):
    _, _, R, C = p.shape
    br = _row_block(R, C)

    def body(c_ref, p_ref, r_ref, o_ref):
        o_ref[...] = (p_ref[...].astype(F32) + r_ref[...].astype(F32)).astype(o_ref.dtype)

    spec = pltpu.PrefetchScalarGridSpec(
        num_scalar_prefetch=1, grid=(NCHIP, R // br),
        in_specs=[pl.BlockSpec((None, None, br, C), lambda k, r, c_ref: (k, c_ref[0], r, 0)),
                  pl.BlockSpec((None, br, C), lambda k, r, c_ref: (k, r, 0))],
        out_specs=pl.BlockSpec((None, br, C), lambda k, r, c_ref: (k, r, 0)))
    return _pcall(body, name=name, grid_spec=spec, out_shape=jax.ShapeDtypeStruct((NCHIP, R, C), out_dtype),
                  compiler_params=_params(("arbitrary", "arbitrary")))(c_arr, p, recv)


def _add_chips(name, q, recv, idx_arr):
    _, R, C = q.shape
    br = _row_block(R, C)

    def body(idx_ref, q_ref, r_ref, o_ref):
        o_ref[...] = ((q_ref[...].astype(F32) + r_ref[0].astype(F32)) + r_ref[1].astype(F32)) + r_ref[2].astype(F32)

    spec = pltpu.PrefetchScalarGridSpec(
        num_scalar_prefetch=1, grid=(R // br,),
        in_specs=[pl.BlockSpec((None, br, C), lambda r, idx: (idx[0], r, 0)),
                  pl.BlockSpec((3, br, C), lambda r, idx: (0, r, 0))],
        out_specs=pl.BlockSpec((None, br, C), lambda r, idx: (idx[1], r, 0)))
    return _pcall(body, name=name, grid_spec=spec, out_shape=jax.ShapeDtypeStruct((2, R, C), F32),
                  compiler_params=_params(("arbitrary",)))(idx_arr, q, recv)


EARLY = ['w_in_e']
LATE = [n for n in BIG if n not in EARLY]


def _adamw(name, w, g, m, v):
    shape = w.shape
    cols = shape[-1]
    rows = w.size // cols
    w2, g2, m2, v2 = (a.reshape(rows, cols) for a in (w, g, m, v))
    bt = next((b for b in (256, 128) if rows % b == 0), rows)

    def body(w_ref, g_ref, m_ref, v_ref, d_ref, nm_ref, nv_ref):
        gv = g_ref[...]
        nm = ADAM_B1 * m_ref[...] + (1.0 - ADAM_B1) * gv
        nv = ADAM_B2 * v_ref[...] + (1.0 - ADAM_B2) * (gv * gv)
        m_hat = nm / (1.0 - ADAM_B1 ** ADAM_STEP)
        v_hat = nv / (1.0 - ADAM_B2 ** ADAM_STEP)
        d_ref[...] = -ADAM_LR * (m_hat / (jnp.sqrt(v_hat) + ADAM_EPS) + ADAM_WD * w_ref[...])
        nm_ref[...] = nm
        nv_ref[...] = nv

    blk = pl.BlockSpec((bt, cols), lambda i: (i, 0))
    outs = _pcall(body, name=name, grid=(rows // bt,), in_specs=[blk] * 4, out_specs=[blk] * 3,
                  out_shape=[jax.ShapeDtypeStruct((rows, cols), F32)] * 3, compiler_params=_params(("arbitrary",)))(
        w2, g2, m2, v2)
    return [o.reshape(shape) for o in outs]


SMALL_SHARDED = ['mix_norm_o', 'conv_b_in', 'conv_dw_w', 'conv_dw_b', 'conv_ln_g', 'conv_ln_b', 'conv_b_out']
REPLICATED = [n for n in WEIGHTS if SHARD_AXIS[n] is None]
NCHIP = 4


def _halves(flat, tile_rows):
    unit = 2 * tile_rows * COLS
    total = -(-flat.size // unit) * unit
    return jnp.pad(flat, (0, total - flat.size)).reshape(2, total // (2 * COLS), COLS)


def _flat(arrays):
    return jnp.concatenate([a.reshape(-1) for a in arrays])


def _chip_block(a, axis, k):
    n = a.shape[axis] // NCHIP
    return lax.slice_in_dim(a, k * n, (k + 1) * n, axis=axis)


def _full_shape(n, shard_shape):
    s = list(shard_shape[n])
    s[SHARD_AXIS[n]] *= NCHIP
    return tuple(s)


def _unpack(flat, names, shapes):
    out, off = {}, 0
    for n in names:
        size = math.prod(shapes[n])
        out[n] = flat[off:off + size].reshape(shapes[n])
        off += size
    return out


def kernel(x, mem, mix_norm_e, w_in_e, fox_f_bias, gmlp_ln_g, gmlp_ln_b, gmlp_w_s, gmlp_b_s, w_out_e, mix_norm_o, conv_w_in, conv_b_in, conv_dw_w, conv_dw_b, conv_ln_g, conv_ln_b, conv_w_out, conv_b_out, xa_norm, mem_norm, xa_wq, xa_wkv, xa_wo, ffn_norm, ffn_w_gu, ffn_w_down, final_norm, loss_target, m_mix_norm_e, m_w_in_e, m_fox_f_bias, m_gmlp_ln_g, m_gmlp_ln_b, m_gmlp_w_s, m_gmlp_b_s, m_w_out_e, m_mix_norm_o, m_conv_w_in, m_conv_b_in, m_conv_dw_w, m_conv_dw_b, m_conv_ln_g, m_conv_ln_b, m_conv_w_out, m_conv_b_out, m_xa_norm, m_mem_norm, m_xa_wq, m_xa_wkv, m_xa_wo, m_ffn_norm, m_ffn_w_gu, m_ffn_w_down, m_final_norm, v_mix_norm_e, v_w_in_e, v_fox_f_bias, v_gmlp_ln_g, v_gmlp_ln_b, v_gmlp_w_s, v_gmlp_b_s, v_w_out_e, v_mix_norm_o, v_conv_w_in, v_conv_b_in, v_conv_dw_w, v_conv_dw_b, v_conv_ln_g, v_conv_ln_b, v_conv_w_out, v_conv_b_out, v_xa_norm, v_mem_norm, v_xa_wq, v_xa_wkv, v_xa_wo, v_ffn_norm, v_ffn_w_gu, v_ffn_w_down, v_final_norm):
    env = locals()
    w = {n: env[n] for n in WEIGHTS}
    m = {n: env["m_" + n] for n in WEIGHTS}
    v = {n: env["v_" + n] for n in WEIGHTS}
    shard_shape = {n: w[n].shape for n in WEIGHTS}
    xi, yi, ci = _coords()
    me = 2 * xi + yi
    c_arr = jnp.reshape(ci, (1,)).astype(jnp.int32)
    idx_arr = jnp.stack([me, ci]).astype(jnp.int32)

    def two_halves(a):
        return a.reshape(2, a.shape[0] // 2, a.shape[1])

    def shard(n, l):
        return two_halves(w[n][l].astype(BF))

    def matrix(n, gathered):
        rows, cols = w[n].shape[1:]
        g = gathered.reshape(NCHIP, rows, cols)
        return g.reshape(NCHIP * rows, cols) if SHARD_AXIS[n] == 1 else g.transpose(1, 0, 2).reshape(rows, NCHIP * cols)

    Wb = {n: [matrix(n, g)] for n, g in zip(EARLY, _all_gather("gather_mixer", [shard(n, 0) for n in EARLY], me))}
    vec = _all_gather("gather_vectors", [_halves(_flat([w[n] for n in SMALL_SHARDED]), 8)], me)[0].reshape(NCHIP, -1)
    parts = [_unpack(vec[k], SMALL_SHARDED, shard_shape) for k in range(NCHIP)]
    P = {n: jnp.concatenate([parts[k][n] for k in range(NCHIP)], axis=SHARD_AXIS[n]) for n in SMALL_SHARDED}
    P.update({n: w[n] for n in REPLICATED})
    first = [(n, 0) for n in LATE]
    second = [(n, 1) for n in LATE if w[n].shape[0] > 1]
    bufs = {key: _own_slot(shard(*key), me) for key in first + second}
    state = {}

    def fwd_done(arrived):
        state['layer0'] = {n: matrix(n, g) for (n, _), g in zip(first, _gather_forward("gather_forward0", arrived))}
        return {n: [g] for n, g in state['layer0'].items()}

    def fwd2_done(arrived):
        return {n: [state['layer0'][n], matrix(n, g)]
                for (n, _), g in zip(second, _gather_forward("gather_forward1", arrived))}

    def by_halves(G, names):
        return [g.reshape(NCHIP, 2, g.shape[1] // 2, g.shape[2]) for n in names for g in G[n]]

    def pair_sums(tag, ps):
        got = _sibling_halves(f"rs_sibling_halves_{tag}", ps)
        return [_add_own_half(f"rs_add_pair_{tag}{a}", p, g, c_arr, p.dtype) for a, (p, g) in enumerate(zip(ps, got))]

    def bwd_carry(G):
        state['qs'] = pair_sums("late", by_halves(G, LATE))
        return _carry_exchange(state['qs'])

    def last_carry(G):
        state['qs_in'] = pair_sums("in", by_halves(G, ['w_in_e']))
        return _carry_exchange(state['qs_in'])

    hooks = dict(fwd_carry=lambda: _carry_gather([bufs[key] for key in first]), fwd_done=fwd_done,
                 fwd2_carry=lambda: _carry_gather([bufs[key] for key in second]), fwd2_done=fwd2_done,
                 bwd_carry=bwd_carry, bwd_done=lambda arrived: state.update(got=arrived),
                 last_carry=last_carry, last_done=lambda arrived: state.update(got_in=arrived))
    loss_part, grad_x, G = _local_step(x, mem, loss_target, Wb, P, hooks)
    loss = lax.psum(loss_part, ("x", "y", "c"))

    def layers(n):
        return G[n] if isinstance(G[n], list) else ([G[n]] if G[n].ndim == 1 else [G[n][l] for l in range(G[n].shape[0])])

    rep = _flat([a for n in REPLICATED for a in layers(n)])
    quarter = -(-rep.size // (NCHIP * 2 * 8 * COLS)) * (2 * 8 * COLS)
    rep = jnp.pad(rep, (0, NCHIP * quarter - rep.size)).reshape(NCHIP, quarter)
    segs = [[_chip_block(a, SHARD_AXIS[n] - 1, k).reshape(-1) for n in SMALL_SHARDED for a in layers(n)] + [rep[k]]
            for k in range(NCHIP)]
    size = sum(piece.size for piece in segs[0])
    total = -(-size // (2 * 8 * COLS)) * (2 * 8 * COLS)
    p_small = jnp.concatenate([piece for seg in segs for piece in seg + [jnp.zeros((total - size,), F32)]])
    p_small = p_small.reshape(NCHIP, 2, total // (2 * COLS), COLS)
    qs_small = pair_sums("vectors", [p_small])
    pending = [("late", state['qs'], state['got']), ("in", state['qs_in'], state['got_in']),
               ("vectors", qs_small, _chip_exchange("rs_chip_exchange_vectors", qs_small))]
    hs = [_add_chips(f"rs_add_chips_{tag}{a}", q, g, idx_arr)
          for tag, qs, got in pending for a, (q, g) in enumerate(zip(qs, got))]
    red = _sibling_swap("rs_sibling_swap", hs)
    mine, at = {}, 0
    for n in LATE + EARLY:
        nl = shard_shape[n][0]
        mine[n] = jnp.stack([r.reshape(shard_shape[n][1:]) for r in red[at:at + nl]])
        at += nl
    red_small = red[-1].reshape(-1)
    mine.update(_unpack(red_small, SMALL_SHARDED, shard_shape))
    off = sum(math.prod(shard_shape[n]) for n in SMALL_SHARDED)
    rep_all = _all_gather("gather_replicated_grads",
                          [red_small[off:off + quarter].reshape(2, quarter // (2 * COLS), COLS)], me)[0]
    mine.update(_unpack(rep_all.reshape(-1), REPLICATED, shard_shape))

    grads, deltas, new_m, new_v = [], [], [], []
    for n in WEIGHTS:
        d, nm, nv = _adamw("adamw_" + n, w[n], mine[n], m[n], v[n])
        grads.append(mine[n])
        deltas.append(d)
        new_m.append(nm)
        new_v.append(nv)
    return (loss, grad_x, *grads, *deltas, *new_m, *new_v)
```

```python
import functools
import math

import jax
import jax.numpy as jnp
from jax import lax
from jax.experimental import pallas as pl
from jax.experimental.pallas import tpu as pltpu

F32 = jnp.float32
BF = jnp.bfloat16
MESH = pl.DeviceIdType.MESH

D = 1024
FOXW = 512
HD = 64
GW = 512
CH = 128
NG = 8
FF = 2816
NMEM = 256
XH = 4
XD = 256
CK = 31
HALO = 32
EPS = 1e-6
IN_W = 2568
IN_WP = 2688
VMEM_LIMIT = 56 * 1024 * 1024

ADAM_LR, ADAM_B1, ADAM_B2, ADAM_EPS, ADAM_WD, ADAM_STEP = 0.001, 0.9, 0.999, 1e-08, 0.01, 10

WEIGHTS = ['mix_norm_e', 'w_in_e', 'fox_f_bias', 'gmlp_ln_g', 'gmlp_ln_b', 'gmlp_w_s', 'gmlp_b_s', 'w_out_e',
           'mix_norm_o', 'conv_w_in', 'conv_b_in', 'conv_dw_w', 'conv_dw_b', 'conv_ln_g', 'conv_ln_b',
           'conv_w_out', 'conv_b_out', 'xa_norm', 'mem_norm', 'xa_wq', 'xa_wkv', 'xa_wo', 'ffn_norm',
           'ffn_w_gu', 'ffn_w_down', 'final_norm']
SHARD_AXIS = {'mix_norm_e': None, 'w_in_e': 2, 'fox_f_bias': None, 'gmlp_ln_g': None, 'gmlp_ln_b': None,
              'gmlp_w_s': None, 'gmlp_b_s': None, 'w_out_e': 1, 'mix_norm_o': 1, 'conv_w_in': 2, 'conv_b_in': 1,
              'conv_dw_w': 2, 'conv_dw_b': 1, 'conv_ln_g': 1, 'conv_ln_b': 1, 'conv_w_out': 1, 'conv_b_out': 1,
              'xa_norm': None, 'mem_norm': None, 'xa_wq': 1, 'xa_wkv': 2, 'xa_wo': 1, 'ffn_norm': None,
              'ffn_w_gu': 2, 'ffn_w_down': 1, 'final_norm': None}
BIG = ['w_in_e', 'w_out_e', 'conv_w_in', 'conv_w_out', 'xa_wq', 'xa_wkv', 'xa_wo', 'ffn_w_gu', 'ffn_w_down']


def _pcall(body, **kw):
    return pl.pallas_call(body, **kw)


def _params(sem=None, **kw):
    return pltpu.CompilerParams(dimension_semantics=sem, vmem_limit_bytes=VMEM_LIMIT, **kw)


def _dot(a, b, dims):
    dn = {'nn': (((1,), (0,)), ((), ())), 'nt': (((1,), (1,)), ((), ())), 'tn': (((0,), (0,)), ((), ()))}[dims]
    return lax.dot_general(a, b, dn, preferred_element_type=F32)


def _sigmoid(x):
    return 1.0 / (1.0 + jnp.exp(-x))


def _rms_stats(xv):
    return lax.rsqrt(jnp.mean(xv * xv, axis=-1, keepdims=True) + EPS)


def _rms_bwd(xv, gain, dh):
    r = _rms_stats(xv)
    t = dh * gain
    dx = r * t - xv * (r * r * r * jnp.mean(t * xv, axis=-1, keepdims=True))
    return dx, dh * xv * r


def _fused_mm(name, *, dims, M, N, bm, bn, groups, epi, outs, x=None, gain=None, tiles=(), rows=(),
              h_out=False, reds=(), carry=None, cols_outer=False):
    bm = min(bm, M)
    nI, nJ = M // bm, N // bn
    assert nI * bm == M and nJ * bn == N
    assert not reds or nJ == 1
    arrays, specs = [], []

    def spec(shape, index):
        return pl.BlockSpec(shape, (lambda jj, ii: index(ii, jj)) if cols_outer else index)

    def add(arr, shape, index):
        arrays.append(arr)
        specs.append(spec(shape, index))
        return len(arrays) - 1

    def first_pass(i, j):
        return (jnp.where(j == 0, i, nI - 1), 0) if cols_outer else (i, 0)

    if x is not None:
        K0 = x.shape[1]
        add(x, (bm, K0), first_pass)
        add(gain, (1, K0), lambda i, j: (0, 0))
    plan = []
    for grp in groups:
        g = []
        for p in grp:
            ai = None
            if p['A'] is not None:
                ai = add(p['A'], (bm, p['Ka']), lambda i, j, o=p.get('acoff', 0): (i, o))
            ro, co = p.get('roff', 0), p.get('coff', 0)
            if dims == 'nn':
                bi = add(p['B'], (p['Ka'], bn), lambda i, j, ro=ro, co=co: (ro, j + co))
            else:
                bi = add(p['B'], (bn, p['Ka']), lambda i, j, ro=ro, co=co: (j + ro, co))
            g.append((ai, bi))
        plan.append(g)
    tile_idx = [add(a, (bm, bn), lambda i, j, o=o: (i, j + o)) for a, o in tiles]
    row_idx = [add(a, (1, bn), lambda i, j, o=o: (0, j + o)) for a, o in rows]
    n_in = len(arrays)

    out_shape = [jax.ShapeDtypeStruct((M, N), dt) for dt in outs]
    out_specs = [spec((bm, bn), lambda i, j: (i, j)) for _ in outs]
    if h_out:
        out_shape.append(jax.ShapeDtypeStruct((M, x.shape[1]), BF))
        out_specs.append(spec((bm, x.shape[1]), first_pass))
    for shp in reds:
        out_shape.append(jax.ShapeDtypeStruct(shp, F32))
        out_specs.append(spec(shp, lambda i, j: (0, 0)))
    n_main = len(outs)
    scratch = [pltpu.VMEM((M if cols_outer else bm, x.shape[1]), BF)] if x is not None else []

    def body(*refs):
        ins, out_refs, scr = refs[:n_in], refs[n_in:n_in + len(out_shape)], refs[n_in + len(out_shape):]
        i, j = (pl.program_id(1), pl.program_id(0)) if cols_outer else (pl.program_id(0), pl.program_id(1))
        if x is not None:
            hn_rows = pl.ds(pl.multiple_of(i * bm, bm), bm) if cols_outer else slice(None)
            hn_ref = scr[0]

            @pl.when(j == 0)
            def _():
                xv = ins[0][...]
                hn = (xv * _rms_stats(xv) * ins[1][...]).astype(BF)
                hn_ref[hn_rows, :] = hn
                if h_out:
                    out_refs[n_main][...] = hn

        accs = []
        for g in plan:
            acc = None
            for ai, bi in g:
                a = hn_ref[hn_rows, :] if ai is None else ins[ai][...]
                if a.dtype != BF:
                    a = a.astype(BF)
                d = _dot(a, ins[bi][...], dims)
                acc = d if acc is None else acc + d
            accs.append(acc)
        out_vals, red_vals = epi(accs, [ins[t][...] for t in tile_idx], [ins[r][...] for r in row_idx])
        for r, v in zip(out_refs[:n_main], out_vals):
            r[...] = v.astype(r.dtype)
        if reds:
            red_refs = out_refs[n_main + (1 if h_out else 0):]

            @pl.when(i == 0)
            def _():
                for r in red_refs:
                    r[...] = jnp.zeros(r.shape, F32)

            for r, v in zip(red_refs, red_vals):
                r[...] += v

    res, arrived = _carried_call(body, name=name, grid=(nJ, nI) if cols_outer else (nI, nJ), in_specs=specs,
                                 out_specs=out_specs,
                                 out_shape=out_shape, scratch_shapes=scratch, operands=arrays, carry=carry)
    return res if carry is None else (res, arrived)


def _epi_plain(accs, tiles, rows):
    return [accs[0]], []


def _epi_resid(accs, tiles, rows):
    y = tiles[0] + accs[0]
    if rows:
        y = y + rows[0]
    return [y], []


def _epi_swiglu(accs, tiles, rows):
    g, u = accs
    return [g, u, g * _sigmoid(g) * u], []


def _epi_glu(accs, tiles, rows):
    a, g = accs[0] + rows[0], accs[1] + rows[1]
    return [a, g, a * _sigmoid(g)], []


def _epi_swiglu_bwd(accs, tiles, rows):
    da = accs[0]
    g, u = tiles[0].astype(F32), tiles[1].astype(F32)
    sg = _sigmoid(g)
    return [da * u * (sg * (1.0 + g * (1.0 - sg))), da * (g * sg)], []


def _epi_rms_bwd(accs, tiles, rows):
    dx, dgr = _rms_bwd(tiles[0], rows[0], accs[0])
    return [tiles[1] + dx], [jnp.sum(dgr, axis=0, keepdims=True)]


def _epi_rms_gain_only(accs, tiles, rows):
    _, dgr = _rms_bwd(tiles[0], rows[0], accs[0])
    return [], [jnp.sum(dgr, axis=0, keepdims=True)]


def _norm_mm(name, x, gain, W, *, N, coff=0, bn, out_dtype, bm=512, h_out=False):
    return _fused_mm(name, dims='nn', M=x.shape[0], N=N, bm=bm, bn=bn, x=x, gain=gain,
                     groups=[[dict(A=None, Ka=x.shape[1], B=W, coff=coff)]], epi=_epi_plain, outs=[out_dtype],
                     h_out=h_out)


def _mm_resid(name, pairs, resid, bias=None, bm=512):
    M = resid.shape[0]
    return _fused_mm(name, dims='nn', M=M, N=D, bm=bm, bn=D, groups=[pairs], epi=_epi_resid, outs=[F32],
                     tiles=[(resid, 0)], rows=[(bias, 0)] if bias is not None else [])[0]


def _mm_nt_plain(name, dy, W, bm=512):
    return _fused_mm(name, dims='nt', M=dy.shape[0], N=W.shape[0], bm=bm, bn=W.shape[0],
                     groups=[[dict(A=dy, Ka=dy.shape[1], B=W)]], epi=_epi_plain, outs=[BF])[0]


def _mm_nt_rms_bwd(name, pairs, x, gain, dx_in, bm=256, carry=None):
    out = _fused_mm(name, dims='nt', M=x.shape[0], N=D, bm=bm, bn=D, groups=[pairs], epi=_epi_rms_bwd,
                    outs=[F32], tiles=[(x, 0), (dx_in, 0)], rows=[(gain, 0)], reds=[(1, D)], carry=carry)
    if carry is None:
        return out[0], out[1]
    return out[0][0], out[0][1], out[1]


def _mm_tn(name, A, G, bk=2048, parts=1):
    T, Ka, Kg = A.shape[0], A.shape[1], G.shape[1]
    w = Kg // parts
    bm = Ka if Ka <= 1024 else Ka // 2
    bn = w if w <= 1408 else w // 2
    bk = min(bk, T)
    per = w // bn
    nI, nJ, nK = Ka // bm, Kg // bn, T // bk

    def body(a_ref, g_ref, o_ref, acc):
        k = pl.program_id(2)

        @pl.when(k == 0)
        def _():
            acc[...] = jnp.zeros(acc.shape, F32)

        acc[...] += _dot(a_ref[...].astype(BF), g_ref[...].astype(BF), 'tn')

        @pl.when(k == nK - 1)
        def _():
            o_ref[...] = acc[...].astype(BF)

    return _pcall(body, name=name, grid=(nI, nJ, nK),
                  in_specs=[pl.BlockSpec((bk, bm), lambda i, j, k: (k, i)),
                            pl.BlockSpec((bk, bn), lambda i, j, k: (k, j))],
                  out_specs=pl.BlockSpec((None, bm, bn), lambda i, j, k: (j // per, i, j % per)),
                  out_shape=jax.ShapeDtypeStruct((parts, Ka, w), BF),
                  scratch_shapes=[pltpu.VMEM((bm, bn), F32)],
                  compiler_params=_params(("arbitrary", "arbitrary", "arbitrary")))(A, G)


def _colsum(name, a, bt=512):
    M, N = a.shape
    bt = min(bt, M)

    def body(a_ref, o_ref):
        @pl.when(pl.program_id(0) == 0)
        def _():
            o_ref[...] = jnp.zeros(o_ref.shape, F32)

        o_ref[...] += jnp.sum(a_ref[...].astype(F32), axis=0, keepdims=True)

    return _pcall(body, name=name, grid=(M // bt,), in_specs=[pl.BlockSpec((bt, N), lambda i: (i, 0))],
                  out_specs=pl.BlockSpec((1, N), lambda i: (0, 0)), out_shape=jax.ShapeDtypeStruct((1, N), F32),
                  compiler_params=_params(("arbitrary",)))(a)


def _cumsum_rows(v):
    T = v.shape[0]
    row = lax.broadcasted_iota(jnp.int32, v.shape, 0)
    s = 1
    while s < T:
        v = v + jnp.where(row >= s, pltpu.roll(v, s, 0), 0.0)
        s *= 2
    return v


def _log_sigmoid(z):
    return jnp.minimum(z, 0.0) - jnp.log(1.0 + jnp.exp(-jnp.abs(z)))


def _fox_gate_fwd(fl, fbias, B, T):
    def body(fl_ref, b_ref, o_ref):
        o_ref[...] = _cumsum_rows(_log_sigmoid(fl_ref[...] + b_ref[...]))

    return _pcall(body, name="fox_gate_fwd", grid=(B,),
                  in_specs=[pl.BlockSpec((T, 128), lambda b: (b, 0)), pl.BlockSpec((1, 128), lambda b: (0, 0))],
                  out_specs=pl.BlockSpec((T, 128), lambda b: (b, 0)),
                  out_shape=jax.ShapeDtypeStruct((B * T, 128), F32), compiler_params=_params(("arbitrary",)))(fl, fbias)


def _fox_gate_bwd(fl, fbias, dcq, dck, B, T):
    def body(fl_ref, b_ref, dcq_ref, dck_ref, dfl_ref, db_ref):
        dc = dcq_ref[...] + dck_ref[...]
        rev = jnp.sum(dc, axis=0, keepdims=True) - _cumsum_rows(dc) + dc
        dfl = rev * _sigmoid(-(fl_ref[...] + b_ref[...]))
        dfl_ref[...] = dfl

        @pl.when(pl.program_id(0) == 0)
        def _():
            db_ref[...] = jnp.zeros(db_ref.shape, F32)

        db_ref[...] += jnp.sum(dfl, axis=0, keepdims=True)

    return _pcall(body, name="fox_gate_bwd", grid=(B,),
                  in_specs=[pl.BlockSpec((T, 128), lambda b: (b, 0)), pl.BlockSpec((1, 128), lambda b: (0, 0)),
                            pl.BlockSpec((T, 128), lambda b: (b, 0)), pl.BlockSpec((T, 128), lambda b: (b, 0))],
                  out_specs=[pl.BlockSpec((T, 128), lambda b: (b, 0)), pl.BlockSpec((1, 128), lambda b: (0, 0))],
                  out_shape=[jax.ShapeDtypeStruct((B * T, 128), F32), jax.ShapeDtypeStruct((1, 128), F32)],
                  compiler_params=_params(("arbitrary",)))(fl, fbias, dcq, dck)


def _carried_call(body, *, name, grid, in_specs, out_specs, out_shape, scratch_shapes, operands, carry):
    if carry is None:
        return _pcall(body, name=name, grid=grid, in_specs=in_specs, out_specs=out_specs, out_shape=out_shape,
                      scratch_shapes=scratch_shapes, compiler_params=_params(("arbitrary",) * len(grid)))(*operands), []
    n, n_in, n_out, n_scr = len(carry['inputs']), len(in_specs), len(out_specs), len(scratch_shapes)

    def wrapped(*refs):
        ins, cin = refs[:n_in], refs[n_in:n_in + n]
        outs, cout = refs[n_in + n:n_in + n + n_out], refs[n_in + n + n_out:n_in + 2 * n + n_out]
        scr = refs[n_in + 2 * n + n_out:]
        send_sems, recv_sems = scr[n_scr:]
        ids = [pl.program_id(d) for d in range(len(grid))]
        first = functools.reduce(jnp.logical_and, [i == 0 for i in ids])
        last = functools.reduce(jnp.logical_and, [i == g - 1 for i, g in zip(ids, grid)])

        @pl.when(first)
        def _():
            for cp in carry['copies'](cin, cout, send_sems, recv_sems):
                cp.start()

        body(*ins, *outs, *scr[:n_scr])

        @pl.when(last)
        def _():
            for cp in carry['copies'](cin, cout, send_sems, recv_sems):
                cp.wait()

    aliases = {n_in + a: n_out + a for a in range(n)} if carry['in_place'] else {}
    res = _pcall(wrapped, name=name, grid=grid, in_specs=list(in_specs) + [ANY] * n,
                 out_specs=list(out_specs) + [ANY] * n, out_shape=list(out_shape) + carry['out_shape'],
                 scratch_shapes=list(scratch_shapes) + _sem_pairs(carry['nsem']), input_output_aliases=aliases,
                 compiler_params=_params(("arbitrary",) * len(grid)))(*operands, *carry['inputs'])
    return res[:n_out], res[n_out:]


def _carry_gather(bufs):
    n = len(bufs)

    def copies(in_refs, out_refs, send_sems, recv_sems):
        x, y, c = _coords()
        cps = []
        for a in range(n):
            blk = out_refs[a].at[2 * x + y, c]
            cps += [pltpu.make_async_remote_copy(src_ref=blk, dst_ref=blk, send_sem=send_sems.at[3 * a + j],
                                                 recv_sem=recv_sems.at[3 * a + j], device_id=(cx, cy, c),
                                                 device_id_type=MESH) for j, (cx, cy) in enumerate(_other_chips(x, y))]
        return cps

    return dict(inputs=bufs, out_shape=[jax.ShapeDtypeStruct(b.shape, b.dtype) for b in bufs], in_place=True,
                nsem=3 * n, copies=copies)


def _carry_exchange(qs):
    n = len(qs)

    def copies(in_refs, out_refs, send_sems, recv_sems):
        x, y, c = _coords()
        return [pltpu.make_async_remote_copy(src_ref=in_refs[a].at[2 * cx + cy], dst_ref=out_refs[a].at[j],
                                             send_sem=send_sems.at[3 * a + j], recv_sem=recv_sems.at[3 * a + j],
                                             device_id=(cx, cy, c), device_id_type=MESH)
                for a in range(n) for j, (cx, cy) in enumerate(_other_chips(x, y))]

    return dict(inputs=qs, out_shape=[jax.ShapeDtypeStruct((3,) + q.shape[1:], q.dtype) for q in qs], in_place=False,
                nsem=3 * n, copies=copies)


NEG = -1e30
QSUB = 1


def _fox_fwd(qkv, cum4, B, T, qoff, bq, bk, carry=None):
    nq, nkb = T // bq, T // bk
    N = B * T

    def body(q_ref, k_ref, v_ref, cum_ref, o_ref, lse_ref):
        hp, i = pl.program_id(1), pl.program_id(2)
        sq = bq // QSUB
        lane = lax.broadcasted_iota(jnp.int32, (sq, 128), 1)
        heads = [slice(e * HD, (e + 1) * HD) for e in range(2)]
        chains = [(e, sl, slice(r * sq, (r + 1) * sq)) for e, sl in enumerate(heads) for r in range(QSUB)]
        qs = [q_ref[rows, sl] * 0.125 for _, sl, rows in chains]

        def block(j, carry, diagonal):
            ks = pl.multiple_of(j * bk, bk)
            out = []
            for n, (e, sl, rows) in enumerate(chains):
                m, l, acc = carry[n]
                s = _dot(qs[n], k_ref[pl.ds(ks, bk), sl], 'nt') - cum_ref[0, 2 * hp + e, pl.ds(j, 1), :]
                if diagonal:
                    keep = (lax.broadcasted_iota(jnp.int32, (sq, bk), 0) + rows.start
                            >= lax.broadcasted_iota(jnp.int32, (sq, bk), 1))
                    s = jnp.where(keep, s, NEG)
                m_new = jnp.maximum(m, jnp.max(s, axis=1, keepdims=True))
                p = jnp.exp(s - m_new)
                alpha = jnp.exp(m - m_new)
                l = alpha * l + jnp.sum(p, axis=1, keepdims=True)
                acc = alpha * acc + _dot(p.astype(BF), v_ref[pl.ds(ks, bk), sl], 'nn')
                out.append((m_new, l, acc))
            return tuple(out)

        init = tuple((jnp.full((sq, 1), NEG, F32), jnp.zeros((sq, 1), F32), jnp.zeros((sq, HD), F32)) for _ in chains)
        carry = lax.fori_loop(0, i, lambda j, c: block(j, c, False), init)
        carry = block(i, carry, True)
        for r in range(QSUB):
            lse_tile = jnp.zeros((sq, 128), F32)
            for n, (e, sl, rows) in enumerate(chains):
                if rows.start == r * sq:
                    m, l, acc = carry[n]
                    o_ref[rows, sl] = (acc / l).astype(BF)
                    lse_tile = jnp.where(lane == e, m + jnp.log(l), lse_tile)
            lse_ref[r * sq:(r + 1) * sq, :] = lse_tile

    return _carried_call(body, name="fox_fwd", grid=(B, 4, nq),
                         in_specs=[pl.BlockSpec((bq, 128), lambda b, h, i: (b * nq + i, qoff + h)),
                                   pl.BlockSpec((T, 128), lambda b, h, i: (b, qoff + 4 + h)),
                                   pl.BlockSpec((T, 128), lambda b, h, i: (b, qoff + 8 + h)),
                                   pl.BlockSpec((1, NG, nkb, bk), lambda b, h, i: (b, 0, 0, 0))],
                         out_specs=[pl.BlockSpec((bq, 128), lambda b, h, i: (b * nq + i, h)),
                                    pl.BlockSpec((bq, 128), lambda b, h, i: (b * nq + i, h))],
                         out_shape=[jax.ShapeDtypeStruct((N, FOXW), BF), jax.ShapeDtypeStruct((N, FOXW), F32)],
                         scratch_shapes=[], operands=(qkv, qkv, qkv, cum4), carry=carry)


def _fox_bwd(qkv, cum4, o, lse, dcat, B, T, qoff, bq, bk, carry=None):
    nq, nkb = T // bq, T // bk
    N = B * T

    def body(q_ref, k_ref, v_ref, cum_ref, o_ref, lse_ref, do_ref, dq_ref, dk_ref, dv_ref, dcum_ref, dcq_ref,
             dq_acc, dl_ref, rs_ref):
        hp = pl.program_id(1)
        heads = [slice(e * HD, (e + 1) * HD) for e in range(2)]
        keep = lax.broadcasted_iota(jnp.int32, (bq, bk), 0) >= lax.broadcasted_iota(jnp.int32, (bq, bk), 1)
        dcq_ref[...] = jnp.zeros(dcq_ref.shape, F32)
        dq_acc[...] = jnp.zeros(dq_acc.shape, F32)
        rs_ref[...] = jnp.zeros(rs_ref.shape, F32)
        for e, sl in enumerate(heads):
            dl_ref[e] = jnp.sum(do_ref[:, sl].astype(F32) * o_ref[:, sl].astype(F32), axis=1, keepdims=True)
        for j in range(nkb):
            krows = slice(j * bk, (j + 1) * bk)

            def tile(i, carry, diagonal):
                qs = i * bq if diagonal else pl.multiple_of(i * bq, bq)
                out = []
                for e, sl in enumerate(heads):
                    dk_a, dv_a, cs = carry[e]
                    q, k = q_ref[pl.ds(qs, bq), sl], k_ref[krows, sl]
                    do = do_ref[pl.ds(qs, bq), sl]
                    s = _dot(q, k, 'nt') * 0.125 - cum_ref[0, 2 * hp + e, j:j + 1, :]
                    p = jnp.exp(s - lse_ref[pl.ds(qs, bq), e:e + 1])
                    if diagonal:
                        p = jnp.where(keep, p, 0.0)
                    dv_a = dv_a + _dot(p.astype(BF), do, 'tn')
                    ds = p * (_dot(do, v_ref[krows, sl], 'nt') - dl_ref[e, pl.ds(qs, bq), :])
                    cs = cs + jnp.sum(ds, axis=0, keepdims=True)
                    rs_ref[e, pl.ds(qs, bq), :] += jnp.sum(ds, axis=1, keepdims=True)
                    dsb = ds.astype(BF)
                    dk_a = dk_a + _dot(dsb, q, 'tn')
                    dq_acc[e, pl.ds(qs, bq), :] += _dot(dsb, k, 'nn')
                    out.append((dk_a, dv_a, cs))
                return tuple(out)

            init = tuple((jnp.zeros((bk, HD), F32), jnp.zeros((bk, HD), F32), jnp.zeros((1, bk), F32)) for _ in heads)
            carry = lax.fori_loop(j + 1, nq, lambda i, c: tile(i, c, False), tile(j, init, True))
            for e, sl in enumerate(heads):
                dk_a, dv_a, cs = carry[e]
                dk_ref[krows, sl] = (dk_a * 0.125).astype(BF)
                dv_ref[krows, sl] = dv_a.astype(BF)
                dcum_ref[0, e, j:j + 1, :] = -cs
        for e, sl in enumerate(heads):
            dq_ref[:, sl] = (dq_acc[e] * 0.125).astype(BF)
            dcq_ref[:, e:e + 1] = rs_ref[e]

    seq = lambda off: pl.BlockSpec((T, 128), lambda b, h, off=off: (b, off + h))
    return _carried_call(body, name="fox_bwd", grid=(B, 4),
                         in_specs=[seq(qoff), seq(qoff + 4), seq(qoff + 8),
                                   pl.BlockSpec((1, NG, nkb, bk), lambda b, h: (b, 0, 0, 0)),
                                   seq(0), seq(0), seq(0)],
                         out_specs=[seq(0), seq(0), seq(0),
                                    pl.BlockSpec((1, 2, nkb, bk), lambda b, h: (b, h, 0, 0)), seq(0)],
                         out_shape=[jax.ShapeDtypeStruct((N, FOXW), BF)] * 3
                         + [jax.ShapeDtypeStruct((B, NG, nkb, bk), F32), jax.ShapeDtypeStruct((N, FOXW), F32)],
                         scratch_shapes=[pltpu.VMEM((2, T, HD), F32), pltpu.VMEM((2, T, 1), F32),
                                         pltpu.VMEM((2, T, 1), F32)],
                         operands=(qkv, qkv, qkv, cum4, o, lse, dcat), carry=carry)


_GC = math.sqrt(2.0 / math.pi)
_GA = 0.044715


def _gelu(z):
    return 0.5 * z * (1.0 + jnp.tanh(_GC * (z + _GA * z * z * z)))


def _gelu_grad(z):
    t = jnp.tanh(_GC * (z + _GA * z * z * z))
    return 0.5 * (1.0 + t) + 0.5 * z * (1.0 - t * t) * (_GC * (1.0 + 3.0 * _GA * z * z))


def _gmlp_common(z, lng, lnb):
    zg = _gelu(z)
    u, vg = zg[:, :GW], zg[:, GW:]
    mu = jnp.mean(vg, axis=-1, keepdims=True)
    xc = vg - mu
    rstd = lax.rsqrt(jnp.mean(xc * xc, axis=-1, keepdims=True) + EPS)
    xhat = xc * rstd
    return u, xhat, rstd, xhat * lng + lnb


def _tril_w(ws_ref):
    tri = lax.broadcasted_iota(jnp.int32, (CH, CH), 0) >= lax.broadcasted_iota(jnp.int32, (CH, CH), 1)
    return [jnp.where(tri, ws_ref[g], 0.0).astype(BF) for g in range(NG)], tri


def _split_pair(vp):
    lane = lax.broadcasted_iota(jnp.int32, vp.shape, 1)
    zero = jnp.zeros(vp.shape, vp.dtype)
    return jnp.concatenate([jnp.where(lane < HD, vp, zero), jnp.where(lane >= HD, vp, zero)], axis=0)


def _gmlp_mix(wt, vgn_b):
    outs = []
    for p in range(NG // 2):
        wcat = jnp.concatenate([wt[2 * p], wt[2 * p + 1]], axis=1)
        outs.append(_dot(wcat, _split_pair(vgn_b[:, 128 * p:128 * (p + 1)]), 'nn'))
    return jnp.concatenate(outs, axis=1)


def _gmlp_fwd(z, lng, lnb, ws, bfull, bt):
    N = z.shape[0]

    def body(z_ref, lng_ref, lnb_ref, ws_ref, bf_ref, o_ref):
        wt, _ = _tril_w(ws_ref)
        for c in range(bt // CH):
            rows = slice(c * CH, (c + 1) * CH)
            u, _, _, vgn = _gmlp_common(z_ref[rows, :], lng_ref[...], lnb_ref[...])
            mixed = _gmlp_mix(wt, vgn.astype(BF)) + bf_ref[...]
            o_ref[rows, :] = (u * mixed).astype(BF)

    full = lambda shp: pl.BlockSpec(shp, lambda i: (0,) * len(shp))
    return _pcall(body, name="gmlp_fwd", grid=(N // bt,),
                  in_specs=[pl.BlockSpec((bt, D), lambda i: (i, 0)), full((1, GW)), full((1, GW)),
                            full((NG, CH, CH)), full((CH, GW))],
                  out_specs=pl.BlockSpec((bt, GW), lambda i: (i, 0)), out_shape=jax.ShapeDtypeStruct((N, GW), BF),
                  compiler_params=_params(("arbitrary",)))(z, lng, lnb, ws, bfull)


def _gmlp_bwd(z, dcat, lng, lnb, ws, bfull, bt):
    N = z.shape[0]

    def body(z_ref, da_ref, lng_ref, lnb_ref, ws_ref, bf_ref, dz_ref, dg_ref, db_ref, dws_ref, dbf_ref):
        @pl.when(pl.program_id(0) == 0)
        def _():
            for r in (dg_ref, db_ref, dws_ref, dbf_ref):
                r[...] = jnp.zeros(r.shape, F32)

        wt, tri = _tril_w(ws_ref)
        lane = lax.broadcasted_iota(jnp.int32, (CH, 128), 1)
        for c in range(bt // CH):
            rows = slice(c * CH, (c + 1) * CH)
            zc = z_ref[rows, :]
            u, xhat, rstd, vgn = _gmlp_common(zc, lng_ref[...], lnb_ref[...])
            vgn_b = vgn.astype(BF)
            mixed = _gmlp_mix(wt, vgn_b) + bf_ref[...]
            da = da_ref[rows, :].astype(F32)
            dmix = da * u
            du = da * mixed
            dbf_ref[...] += dmix
            dvs = []
            for p in range(NG // 2):
                cols = slice(128 * p, 128 * (p + 1))
                dmp = dmix[:, cols].astype(BF)
                dwp = _dot(_split_pair(dmp), vgn_b[:, cols], 'nt')
                dws_ref[2 * p] += jnp.where(tri, dwp[:CH], 0.0)
                dws_ref[2 * p + 1] += jnp.where(tri, dwp[CH:], 0.0)
                dvs.append(jnp.where(lane < HD, _dot(wt[2 * p], dmp, 'tn'), _dot(wt[2 * p + 1], dmp, 'tn')))
            dvgn = jnp.concatenate(dvs, axis=1)
            dg_ref[...] += jnp.sum(dvgn * xhat, axis=0, keepdims=True)
            db_ref[...] += jnp.sum(dvgn, axis=0, keepdims=True)
            dxh = dvgn * lng_ref[...]
            dvg = rstd * (dxh - jnp.mean(dxh, axis=-1, keepdims=True)
                          - xhat * jnp.mean(dxh * xhat, axis=-1, keepdims=True))
            dz_ref[rows, :] = (jnp.concatenate([du, dvg], axis=1) * _gelu_grad(zc)).astype(BF)

    full = lambda shp: pl.BlockSpec(shp, lambda i: (0,) * len(shp))
    return _pcall(body, name="gmlp_bwd", grid=(N // bt,),
                  in_specs=[pl.BlockSpec((bt, D), lambda i: (i, 0)), pl.BlockSpec((bt, GW), lambda i: (i, 1)),
                            full((1, GW)), full((1, GW)), full((NG, CH, CH)), full((CH, GW))],
                  out_specs=[pl.BlockSpec((bt, D), lambda i: (i, 0)), full((1, GW)), full((1, GW)),
                             full((NG, CH, CH)), full((CH, GW))],
                  out_shape=[jax.ShapeDtypeStruct((N, D), BF), jax.ShapeDtypeStruct((1, GW), F32),
                             jax.ShapeDtypeStruct((1, GW), F32), jax.ShapeDtypeStruct((NG, CH, CH), F32),
                             jax.ShapeDtypeStruct((CH, GW), F32)],
                  compiler_params=_params(("arbitrary",)))(z, dcat, lng, lnb, ws, bfull)


def _group_sum(name, a):
    def body(a_ref, o_ref):
        lane = lax.broadcasted_iota(jnp.int32, (CH, 128), 1)
        out = jnp.zeros((CH, 128), F32)
        for g in range(NG):
            out = jnp.where(lane == g, jnp.sum(a_ref[:, g * HD:(g + 1) * HD], axis=1, keepdims=True), out)
        o_ref[...] = out

    return _pcall(body, name=name, out_shape=jax.ShapeDtypeStruct((CH, 128), F32))(a)


def _xattn_softmax(q_h, k_h):
    s = _dot(q_h, k_h, 'nt') * (XD ** -0.5)
    p = jnp.exp(s - jnp.max(s, axis=1, keepdims=True))
    return p / jnp.sum(p, axis=1, keepdims=True)


def _xattn_fwd(name, q, kv, B, T, bq):
    nq = T // bq

    def body(q_ref, kv_ref, o_ref):
        for h in range(XH):
            cols = slice(h * XD, (h + 1) * XD)
            p = _xattn_softmax(q_ref[:, cols], kv_ref[:, cols])
            o_ref[:, cols] = _dot(p.astype(BF), kv_ref[:, D + h * XD:D + (h + 1) * XD], 'nn').astype(BF)

    return _pcall(body, name=name, grid=(B, nq),
                  in_specs=[pl.BlockSpec((bq, D), lambda b, i: (b * nq + i, 0)),
                            pl.BlockSpec((NMEM, 2 * D), lambda b, i: (b, 0))],
                  out_specs=pl.BlockSpec((bq, D), lambda b, i: (b * nq + i, 0)),
                  out_shape=jax.ShapeDtypeStruct((B * T, D), BF), compiler_params=_params(("arbitrary", "arbitrary")))(q, kv)


def _xattn_bwd(name, q, kv, do, B, T, bq):
    nq = T // bq
    sc = XD ** -0.5

    def body(q_ref, kv_ref, do_ref, dq_ref, dkv_ref):
        @pl.when(pl.program_id(1) == 0)
        def _():
            dkv_ref[...] = jnp.zeros(dkv_ref.shape, F32)

        for h in range(XH):
            cols = slice(h * XD, (h + 1) * XD)
            vcols = slice(D + h * XD, D + (h + 1) * XD)
            qh, kh, doh = q_ref[:, cols], kv_ref[:, cols], do_ref[:, cols]
            p = _xattn_softmax(qh, kh)
            dp = _dot(doh, kv_ref[:, vcols], 'nt')
            ds = p * (dp - jnp.sum(p * dp, axis=1, keepdims=True))
            dsb = ds.astype(BF)
            dq_ref[:, cols] = (_dot(dsb, kh, 'nn') * sc).astype(BF)
            dkv_ref[:, cols] += _dot(dsb, qh, 'tn') * sc
            dkv_ref[:, vcols] += _dot(p.astype(BF), doh, 'tn')

    blk = pl.BlockSpec((bq, D), lambda b, i: (b * nq + i, 0))
    return _pcall(body, name=name, grid=(B, nq),
                  in_specs=[blk, pl.BlockSpec((NMEM, 2 * D), lambda b, i: (b, 0)), blk],
                  out_specs=[blk, pl.BlockSpec((NMEM, 2 * D), lambda b, i: (b, 0))],
                  out_shape=[jax.ShapeDtypeStruct((B * T, D), BF), jax.ShapeDtypeStruct((B * NMEM, 2 * D), F32)],
                  compiler_params=_params(("arbitrary", "arbitrary")))(q, kv, do)


def _ln_stats(v):
    mu = jnp.mean(v, axis=-1, keepdims=True)
    xc = v - mu
    rstd = lax.rsqrt(jnp.mean(xc * xc, axis=-1, keepdims=True) + EPS)
    return xc * rstd, rstd


SUB = 8


LANES = 128
NSTRIP = D // LANES


def _fill_window(win, parts):
    for s in range(NSTRIP):
        for r0, val in parts:
            win[s, r0:r0 + val.shape[0], :] = val[:, s * LANES:(s + 1) * LANES]


def _fill_phases(win, sh, rows):
    for b in range(1, SUB):
        for s in range(NSTRIP):
            sh[b - 1, s] = win[s, b:b + rows, :]


def _fill_taps(w8, w_ref):
    for s in range(NSTRIP):
        for j in range(CK):
            w8[s, SUB * j:SUB * (j + 1), :] = jnp.broadcast_to(w_ref[j:j + 1, s * LANES:(s + 1) * LANES], (SUB, LANES))


def _row_groups(win, sh, s):
    cache = {}

    def get(o, t):
        a, b = divmod(o, SUB)
        key = (b, t + a)
        if key not in cache:
            rows = slice(SUB * (t + a), SUB * (t + a + 1))
            cache[key] = win[s, rows, :] if b == 0 else sh[b - 1, s, rows, :]
        return cache[key]

    return get


def _from_strips(ref):
    return jnp.concatenate([ref[s] for s in range(NSTRIP)], axis=1)


def _sum_groups(name, a):
    R, C = a.shape[0] // SUB, a.shape[1]

    def body(a_ref, o_ref):
        o_ref[...] = jnp.sum(a_ref[...].reshape(R, SUB, C), axis=1)

    return _pcall(body, name=name, out_shape=jax.ShapeDtypeStruct((R, C), F32))(a)


def _conv_fwd(y, w32, wb, lng, lnb, B, T, bt):
    nt = T // bt
    hb = bt // HALO
    prows = bt + HALO - SUB

    def body(y_ref, yp_ref, w_ref, wb_ref, lng_ref, lnb_ref, s_ref, yc_ref, win, sh, w8, out):
        i = pl.program_id(1)
        _fill_window(win, [(0, jnp.where(i > 0, yp_ref[...], 0.0)), (HALO, y_ref[...])])
        _fill_phases(win, sh, prows)

        @pl.when((pl.program_id(0) == 0) & (i == 0))
        def _():
            _fill_taps(w8, w_ref)

        def strip(s, carry):
            get = _row_groups(win, sh, s)
            for t in range(bt // SUB):
                accs = [jnp.zeros((SUB, LANES), F32), jnp.zeros((SUB, LANES), F32)]
                for j in range(CK):
                    accs[j % 2] = accs[j % 2] + w8[s, SUB * j:SUB * (j + 1), :] * get(HALO - (CK - 1) + j, t)
                out[s, SUB * t:SUB * (t + 1), :] = accs[0] + accs[1]
            return carry

        lax.fori_loop(0, NSTRIP, strip, 0)
        acc = _from_strips(out) + wb_ref[...]
        yc_ref[...] = acc
        xhat, _ = _ln_stats(acc)
        ln = xhat * lng_ref[...] + lnb_ref[...]
        s_ref[...] = (ln * _sigmoid(ln)).astype(BF)

    row = lambda n: pl.BlockSpec((n, D), lambda b, i: (0, 0))
    cur = pl.BlockSpec((bt, D), lambda b, i: (b * nt + i, 0))
    return _pcall(body, name="conv_fwd", grid=(B, nt),
                  in_specs=[cur, pl.BlockSpec((HALO, D), lambda b, i: (jnp.maximum((b * nt + i) * hb - 1, 0), 0)),
                            row(HALO), row(1), row(1), row(1)],
                  out_specs=[cur, cur],
                  out_shape=[jax.ShapeDtypeStruct((B * T, D), BF), jax.ShapeDtypeStruct((B * T, D), F32)],
                  scratch_shapes=[pltpu.VMEM((NSTRIP, bt + HALO, LANES), F32),
                                  pltpu.VMEM((SUB - 1, NSTRIP, prows, LANES), F32),
                                  pltpu.VMEM((NSTRIP, HALO * SUB, LANES), F32), pltpu.VMEM((NSTRIP, bt, LANES), F32)],
                  compiler_params=_params(("arbitrary", "arbitrary")))(y, y, w32, wb, lng, lnb)


def _conv_bwd(ds, yc, y, pa, pg, w32, lng, lnb, B, T, bt):
    nt = T // bt
    hb = bt // HALO
    nblk32 = B * T // HALO

    def ln_bwd(dsv, ycv, lng, lnb):
        xhat, rstd = _ln_stats(ycv)
        ln = xhat * lng + lnb
        sg = _sigmoid(ln)
        dln = dsv * (sg * (1.0 + ln * (1.0 - sg)))
        dxh = dln * lng
        dyc = rstd * (dxh - jnp.mean(dxh, axis=-1, keepdims=True)
                      - xhat * jnp.mean(dxh * xhat, axis=-1, keepdims=True))
        return dyc, dln, xhat

    prows = bt + HALO - SUB

    def body(ds_ref, dsn_ref, yc_ref, ycn_ref, y_ref, yp_ref, pa_ref, pg_ref, w_ref, lng_ref, lnb_ref,
             dpa_ref, dpg_ref, dw_ref, dwb_ref, dlng_ref, dlnb_ref, dba_ref, dbg_ref,
             dwin, ywin, dsh, ysh, w8, dy_out, dw_out):
        i = pl.program_id(1)

        @pl.when((pl.program_id(0) == 0) & (i == 0))
        def _():
            for r in (dw_ref, dwb_ref, dlng_ref, dlnb_ref, dba_ref, dbg_ref):
                r[...] = jnp.zeros(r.shape, F32)
            _fill_taps(w8, w_ref)

        lng, lnb = lng_ref[...], lnb_ref[...]
        dyc, dln, xhat = ln_bwd(ds_ref[...].astype(F32), yc_ref[...], lng, lnb)
        dycn, _, _ = ln_bwd(dsn_ref[...].astype(F32), ycn_ref[...], lng, lnb)
        _fill_window(dwin, [(0, dyc), (bt, jnp.where(i < nt - 1, dycn, 0.0))])
        _fill_window(ywin, [(0, jnp.where(i > 0, yp_ref[...], 0.0)), (HALO, y_ref[...])])
        dlng_ref[...] += jnp.sum(dln * xhat, axis=0, keepdims=True)
        dlnb_ref[...] += jnp.sum(dln, axis=0, keepdims=True)
        dwb_ref[...] += jnp.sum(dyc, axis=0, keepdims=True)
        _fill_phases(dwin, dsh, prows)
        _fill_phases(ywin, ysh, prows)

        def strip(s, carry):
            get_d, get_y = _row_groups(dwin, dsh, s), _row_groups(ywin, ysh, s)
            dw_acc = [jnp.zeros((SUB, LANES), F32) for _ in range(CK)]
            for t in range(bt // SUB):
                dyc_g = get_d(0, t)
                dys = [jnp.zeros((SUB, LANES), F32), jnp.zeros((SUB, LANES), F32)]
                for j in range(CK):
                    dys[j % 2] = dys[j % 2] + w8[s, SUB * j:SUB * (j + 1), :] * get_d(CK - 1 - j, t)
                    dw_acc[j] = dw_acc[j] + dyc_g * get_y(HALO - (CK - 1) + j, t)
                dy_out[s, SUB * t:SUB * (t + 1), :] = dys[0] + dys[1]
            for j in range(CK):
                dw_out[s, SUB * j:SUB * (j + 1), :] = dw_acc[j]
            return carry

        lax.fori_loop(0, NSTRIP, strip, 0)
        dw_ref[0:CK * SUB, :] += _from_strips(dw_out)
        dy = _from_strips(dy_out)
        a, g = pa_ref[...].astype(F32), pg_ref[...].astype(F32)
        sg = _sigmoid(g)
        da = dy * sg
        dg = dy * a * sg * (1.0 - sg)
        dpa_ref[...] = da.astype(BF)
        dpg_ref[...] = dg.astype(BF)
        dba_ref[...] += jnp.sum(da, axis=0, keepdims=True)
        dbg_ref[...] += jnp.sum(dg, axis=0, keepdims=True)

    cur = pl.BlockSpec((bt, D), lambda b, i: (b * nt + i, 0))
    nxt = pl.BlockSpec((HALO, D), lambda b, i: (jnp.minimum((b * nt + i + 1) * hb, nblk32 - 1), 0))
    prv = pl.BlockSpec((HALO, D), lambda b, i: (jnp.maximum((b * nt + i) * hb - 1, 0), 0))
    row = lambda n: pl.BlockSpec((n, D), lambda b, i: (0, 0))
    N = B * T
    return _pcall(body, name="conv_bwd", grid=(B, nt),
                  in_specs=[cur, nxt, cur, nxt, cur, prv, cur, cur, row(HALO), row(1), row(1)],
                  out_specs=[cur, cur, row(HALO * SUB), row(1), row(1), row(1), row(1), row(1)],
                  out_shape=[jax.ShapeDtypeStruct((N, D), BF)] * 2 + [jax.ShapeDtypeStruct((HALO * SUB, D), F32)]
                  + [jax.ShapeDtypeStruct((1, D), F32)] * 5,
                  scratch_shapes=[pltpu.VMEM((NSTRIP, bt + HALO, LANES), F32), pltpu.VMEM((NSTRIP, bt + HALO, LANES), F32),
                                  pltpu.VMEM((SUB - 1, NSTRIP, prows, LANES), F32),
                                  pltpu.VMEM((SUB - 1, NSTRIP, prows, LANES), F32),
                                  pltpu.VMEM((NSTRIP, HALO * SUB, LANES), F32), pltpu.VMEM((NSTRIP, bt, LANES), F32),
                                  pltpu.VMEM((NSTRIP, CK * SUB, LANES), F32)],
                  compiler_params=_params(("arbitrary", "arbitrary")))(ds, ds, yc, yc, y, y, pa, pg, w32, lng, lnb)


def _head(x, tgt, gain, bt=512):
    N = x.shape[0]
    bt = min(bt, N)

    def body(x_ref, t_ref, g_ref, dx_ref, loss_ref, dg_ref):
        @pl.when(pl.program_id(0) == 0)
        def _():
            loss_ref[...] = jnp.zeros(loss_ref.shape, F32)
            dg_ref[...] = jnp.zeros(dg_ref.shape, F32)

        xv = x_ref[...]
        gain = g_ref[...]
        err = xv * _rms_stats(xv) * gain - t_ref[...]
        loss_ref[...] += 0.5 * jnp.sum(jnp.mean(err * err, axis=-1, keepdims=True), axis=0, keepdims=True)
        dx, dgr = _rms_bwd(xv, gain, err * (1.0 / D))
        dx_ref[...] = dx
        dg_ref[...] += jnp.sum(dgr, axis=0, keepdims=True)

    blk = pl.BlockSpec((bt, D), lambda i: (i, 0))
    return _pcall(body, name="loss_head", grid=(N // bt,),
                  in_specs=[blk, blk, pl.BlockSpec((1, D), lambda i: (0, 0))],
                  out_specs=[blk, pl.BlockSpec((1, 128), lambda i: (0, 0)), pl.BlockSpec((1, D), lambda i: (0, 0))],
                  out_shape=[jax.ShapeDtypeStruct((N, D), F32), jax.ShapeDtypeStruct((1, 128), F32),
                             jax.ShapeDtypeStruct((1, D), F32)],
                  compiler_params=_params(("arbitrary",)))(x, tgt, gain)


def _local_step(x, mem, tgt, Wb, P, hooks=None):
    B, T, _ = x.shape
    N = B * T
    bq = bk = min(512, T)
    bt = min(512, T)
    x0 = x.reshape(N, D)
    mem2 = mem.reshape(B * NMEM, D)
    tgt2 = tgt.reshape(N, D)
    row = lambda v: v.reshape(1, -1)
    G = {}

    w_in = Wb['w_in_e'][0]
    w_inp = jnp.concatenate([w_in[:, 3 * FOXW + NG:], w_in[:, :3 * FOXW], w_in[:, 3 * FOXW:3 * FOXW + NG],
                             jnp.zeros((D, 128 - NG), BF)], axis=1)
    g_e = row(P['mix_norm_e'])
    z, h0 = _norm_mm("proj_z", x0, g_e, w_inp, N=D, coff=0, bn=D, out_dtype=F32, h_out=True)
    qkv = _norm_mm("proj_qkv", x0, g_e, w_inp, N=3 * FOXW, coff=2, bn=FOXW, out_dtype=BF)[0]
    fl = _norm_mm("proj_f", x0, g_e, w_inp, N=128, coff=20, bn=128, out_dtype=F32)[0]
    fbias = jnp.concatenate([P['fox_f_bias'].reshape(1, NG), jnp.zeros((1, 128 - NG), F32)], axis=1)
    cum = _fox_gate_fwd(fl, fbias, B, T)
    cum4 = cum[:, :NG].reshape(B, T, NG).transpose(0, 2, 1).reshape(B, NG, T // bk, bk)
    (b_out, lse), arrived = _fox_fwd(qkv, cum4, B, T, 0, bq, bk, carry=hooks['fwd_carry']() if hooks else None)
    if hooks:
        Wb = {**Wb, **hooks['fwd_done'](arrived)}
    lng, lnb = row(P['gmlp_ln_g']), row(P['gmlp_ln_b'])
    ws = P['gmlp_w_s'][0]
    bfull = jnp.repeat(P['gmlp_b_s'][0].T, HD, axis=1)
    a_out = _gmlp_fwd(z, lng, lnb, ws, bfull, bt)
    w_out = Wb['w_out_e'][0]
    x1 = _mm_resid("mix_out", [dict(A=b_out, Ka=FOXW, B=w_out, roff=0), dict(A=a_out, Ka=GW, B=w_out, roff=1)], x0)

    def xa_ffn_fwd(l, xin):
        qx, hq = _norm_mm(f"xa_q{l}", xin, row(P['xa_norm'][l]), Wb['xa_wq'][l], N=D, bn=D, out_dtype=BF, h_out=True)
        kv, hm = _norm_mm(f"xa_kv{l}", mem2, row(P['mem_norm'][l]), Wb['xa_wkv'][l], N=2 * D, bn=D, out_dtype=BF,
                          h_out=True)
        o = _xattn_fwd(f"xattn_fwd{l}", qx, kv, B, T, bt)
        xm = _mm_resid(f"xa_o{l}", [dict(A=o, Ka=D, B=Wb['xa_wo'][l])], xin)
        wgu = Wb['ffn_w_gu'][l]
        carry = hooks['fwd2_carry']() if hooks and l == 0 else None
        res = _fused_mm(f"ffn_gu{l}", dims='nn', M=N, N=FF, bm=min(512, N), bn=FF // 2, x=xm,
                        gain=row(P['ffn_norm'][l]),
                        groups=[[dict(A=None, Ka=D, B=wgu, coff=0)], [dict(A=None, Ka=D, B=wgu, coff=2)]],
                        epi=_epi_swiglu, outs=[BF, BF, BF], h_out=True, carry=carry, cols_outer=True)
        if carry is not None:
            res, arrived = res
            Wb.update(hooks['fwd2_done'](arrived))
        g, u, a, hf = res
        xo = _mm_resid(f"ffn_down{l}", [dict(A=a, Ka=FF, B=Wb['ffn_w_down'][l])], xm)
        return xo, dict(xin=xin, qx=qx, hq=hq, kv=kv, hm=hm, o=o, xm=xm, g=g, u=u, a=a, hf=hf)

    x3, S0 = xa_ffn_fwd(0, x1)
    w_cin = Wb['conv_w_in'][0]
    b_cin = row(P['conv_b_in'])
    pa, pg, y, hc = _fused_mm("conv_in", dims='nn', M=N, N=D, bm=min(512, N), bn=D, x=x3, gain=row(P['mix_norm_o']),
                              groups=[[dict(A=None, Ka=D, B=w_cin, coff=0)], [dict(A=None, Ka=D, B=w_cin, coff=1)]],
                              epi=_epi_glu, outs=[BF, BF, F32], rows=[(b_cin, 0), (b_cin, 1)], h_out=True)
    w32 = jnp.concatenate([P['conv_dw_w'][0], jnp.zeros((HALO - CK, D), F32)], axis=0)
    cbt = min(256, T)
    s, yc = _conv_fwd(y, w32, row(P['conv_dw_b']), row(P['conv_ln_g']), row(P['conv_ln_b']), B, T, cbt)
    x4 = _mm_resid("conv_out", [dict(A=s, Ka=D, B=Wb['conv_w_out'][0])], x3, bias=row(P['conv_b_out']))
    x6, S1 = xa_ffn_fwd(1, x4)
    dx, loss_t, dgf = _head(x6, tgt2, row(P['final_norm']))
    G['final_norm'] = dgf.reshape(D)

    def by_rows(dw):
        return dw.reshape(NCHIP, dw.shape[1] // NCHIP, dw.shape[2])

    def xa_ffn_bwd(l, S, dx):
        wgu, wdown = Wb['ffn_w_gu'][l], Wb['ffn_w_down'][l]
        dwdown = by_rows(_mm_tn(f"dw_down{l}", S['a'], dx))
        dg, du = _fused_mm(f"ffn_dgu{l}", dims='nt', M=N, N=FF, bm=min(512, N), bn=FF // 2,
                           groups=[[dict(A=dx, Ka=D, B=wdown)]], epi=_epi_swiglu_bwd, outs=[BF, BF],
                           tiles=[(S['g'], 0), (S['u'], 0)], cols_outer=True)
        dwgu = jnp.concatenate([_mm_tn(f"dw_g{l}", S['hf'], dg, parts=2), _mm_tn(f"dw_u{l}", S['hf'], du, parts=2)])
        dx, dgn = _mm_nt_rms_bwd(f"ffn_dx{l}", [dict(A=dg, Ka=FF, B=wgu, coff=0), dict(A=du, Ka=FF, B=wgu, coff=1)],
                                 S['xm'], row(P['ffn_norm'][l]), dx)
        dwo = by_rows(_mm_tn(f"dw_o{l}", S['o'], dx))
        do = _mm_nt_plain(f"xa_do{l}", dx, Wb['xa_wo'][l])
        dq, dkv = _xattn_bwd(f"xattn_bwd{l}", S['qx'], S['kv'], do, B, T, bt)
        dwq = by_rows(_mm_tn(f"dw_q{l}", S['hq'], dq))
        dwkv = _mm_tn(f"dw_kv{l}", S['hm'], dkv, parts=NCHIP)
        dmn = _fused_mm(f"xa_dmem{l}", dims='nt', M=B * NMEM, N=D, bm=min(256, B * NMEM), bn=D,
                        groups=[[dict(A=dkv, Ka=2 * D, B=Wb['xa_wkv'][l])]], epi=_epi_rms_gain_only, outs=[],
                        tiles=[(mem2, 0)], rows=[(row(P['mem_norm'][l]), 0)], reds=[(1, D)])[0]
        dx, dxn = _mm_nt_rms_bwd(f"xa_dx{l}", [dict(A=dq, Ka=D, B=Wb['xa_wq'][l])], S['xin'],
                                 row(P['xa_norm'][l]), dx)
        return dx, dict(ffn_w_down=dwdown, ffn_w_gu=dwgu, ffn_norm=dgn.reshape(D), xa_wo=dwo, xa_wq=dwq,
                        xa_wkv=dwkv, mem_norm=dmn.reshape(D), xa_norm=dxn.reshape(D))

    dx, G1 = xa_ffn_bwd(1, S1, dx)
    G['conv_w_out'] = [by_rows(_mm_tn("dw_cout", s, dx))]
    G['conv_b_out'] = _colsum("db_cout", dx)
    dsv = _mm_nt_plain("conv_ds", dx, Wb['conv_w_out'][0])
    dpa, dpg, dw32, dwb, dlng, dlnb, dba, dbg = _conv_bwd(dsv, yc, y, pa, pg, w32, row(P['conv_ln_g']),
                                                          row(P['conv_ln_b']), B, T, cbt)
    G['conv_dw_w'] = _sum_groups("conv_dw_sum", dw32)[:CK][None]
    G['conv_dw_b'], G['conv_ln_g'], G['conv_ln_b'] = dwb, dlng, dlnb
    G['conv_b_in'] = jnp.concatenate([dba, dbg], axis=1)
    G['conv_w_in'] = [jnp.concatenate([_mm_tn("dw_cin_a", hc, dpa, parts=2), _mm_tn("dw_cin_g", hc, dpg, parts=2)])]
    dx, dgo = _mm_nt_rms_bwd("conv_dx", [dict(A=dpa, Ka=D, B=w_cin, coff=0), dict(A=dpg, Ka=D, B=w_cin, coff=1)],
                             x3, row(P['mix_norm_o']), dx)
    G['mix_norm_o'] = dgo
    dx, G0 = xa_ffn_bwd(0, S0, dx)
    for k in G0:
        G[k] = [G0[k], G1[k]]
    G['w_out_e'] = [by_rows(jnp.concatenate([_mm_tn("dw_out_b", b_out, dx), _mm_tn("dw_out_a", a_out, dx)], axis=1))]
    dcat = _mm_nt_plain("mix_dcat", dx, w_out)
    (dq, dk, dv, dcum4, dcq4), arrived = _fox_bwd(qkv, cum4, b_out, lse, dcat, B, T, 0, bq, bk,
                                                  carry=hooks['bwd_carry'](G) if hooks else None)
    if hooks:
        hooks['bwd_done'](arrived)
    dz, dlg, dlb, dws, dbf = _gmlp_bwd(z, dcat, lng, lnb, ws, bfull, bt)
    G['gmlp_ln_g'], G['gmlp_ln_b'], G['gmlp_w_s'] = dlg, dlb, dws[None]
    G['gmlp_b_s'] = _group_sum("gmlp_db", dbf)[:, :NG].T[None]
    pad = jnp.zeros((N, 128 - NG), F32)
    dck = jnp.concatenate([dcum4.reshape(B, NG, T).transpose(0, 2, 1).reshape(N, NG), pad], axis=1)
    dcq = jnp.concatenate([dcq4.reshape(N, NG // 2, 128)[:, :, :2].reshape(N, NG), pad], axis=1)
    dfl, dfb = _fox_gate_bwd(fl, fbias, dcq, dck, B, T)
    G['fox_f_bias'] = dfb[:, :NG]
    dw_in = jnp.concatenate([_mm_tn("dw_in_q", h0, dq)[0], _mm_tn("dw_in_k", h0, dk)[0], _mm_tn("dw_in_v", h0, dv)[0],
                             _mm_tn("dw_in_f", h0, dfl)[0][:, :NG], _mm_tn("dw_in_z", h0, dz)[0]], axis=1)
    G['w_in_e'] = [dw_in.reshape(D, NCHIP, IN_W // NCHIP).transpose(1, 0, 2)]
    pairs = [dict(A=dz, Ka=D, B=w_inp, coff=0), dict(A=dq, Ka=FOXW, B=w_inp, coff=2),
             dict(A=dk, Ka=FOXW, B=w_inp, coff=3), dict(A=dv, Ka=FOXW, B=w_inp, coff=4),
             dict(A=dfl, Ka=128, B=w_inp, coff=20)]
    if hooks:
        dx, dge, arrived = _mm_nt_rms_bwd("mix_dx", pairs, x0, g_e, dx, carry=hooks['last_carry'](G))
        hooks['last_done'](arrived)
    else:
        dx, dge = _mm_nt_rms_bwd("mix_dx", pairs, x0, g_e, dx)
    G['mix_norm_e'] = dge
    return loss_t[0, 0], dx.reshape(B, T, D), G


COLS = 1024
ANY = pl.BlockSpec(memory_space=pl.ANY)


def _coords():
    return lax.axis_index("x"), lax.axis_index("y"), lax.axis_index("c")


def _other_chips(x, y):
    return [(1 - x, y), (x, 1 - y), (1 - x, 1 - y)]


def _own_slot(v, me):
    return lax.dynamic_update_slice(lax.empty((NCHIP,) + v.shape, v.dtype), v[None], (me,) + (0,) * v.ndim)


def _all_gather(name, shards, me):
    n = len(shards)
    bufs = [_own_slot(v, me) for v in shards]

    def body(*refs):
        out_refs, (send_sems, recv_sems) = refs[n:2 * n], refs[2 * n:]
        x, y, c = _coords()
        mine = 2 * x + y
        sib = (x, y, 1 - c)
        chips = _other_chips(x, y)

        def rcopy(a, k, chip_idx, half, to):
            blk = out_refs[a].at[chip_idx, half]
            return pltpu.make_async_remote_copy(src_ref=blk, dst_ref=blk, send_sem=send_sems.at[6 * a + k],
                                                recv_sem=recv_sems.at[6 * a + k], device_id=to, device_id_type=MESH)

        first = [rcopy(a, j, mine, c, (cx, cy, c)) for a in range(n) for j, (cx, cy) in enumerate(chips)]
        for cp in first:
            cp.start()
        passed = []
        for a in range(n):
            for j, (cx, cy) in enumerate(chips):
                kj = 2 * cx + cy
                rcopy(a, j, kj, c, sib).wait_recv()
                fwd = rcopy(a, 3 + j, kj, c, sib)
                fwd.start()
                passed.append(fwd)
        for a in range(n):
            for j, (cx, cy) in enumerate(chips):
                rcopy(a, 3 + j, 2 * cx + cy, 1 - c, sib).wait_recv()
        for cp in first + passed:
            cp.wait_send()

    return _pcall(body, name=name, in_specs=[ANY] * n, out_specs=[ANY] * n,
                  out_shape=[jax.ShapeDtypeStruct(b.shape, b.dtype) for b in bufs],
                  input_output_aliases={a: a for a in range(n)}, scratch_shapes=_sem_pairs(6 * n))(*bufs)


def _sem_pairs(n):
    return [pltpu.SemaphoreType.DMA((n,)), pltpu.SemaphoreType.DMA((n,))]


def _gather_forward(name, bufs):
    n = len(bufs)

    def body(*refs):
        out_refs, (send_sems, recv_sems) = refs[n:2 * n], refs[2 * n:]
        x, y, c = _coords()

        def cp(a, j, kj, half):
            blk = out_refs[a].at[kj, half]
            return pltpu.make_async_remote_copy(src_ref=blk, dst_ref=blk, send_sem=send_sems.at[3 * a + j],
                                                recv_sem=recv_sems.at[3 * a + j], device_id=(x, y, 1 - c),
                                                device_id_type=MESH)

        chips = [2 * cx + cy for cx, cy in _other_chips(x, y)]
        sends = [cp(a, j, kj, c) for a in range(n) for j, kj in enumerate(chips)]
        for s in sends:
            s.start()
        for a in range(n):
            for j, kj in enumerate(chips):
                cp(a, j, kj, 1 - c).wait_recv()
        for s in sends:
            s.wait_send()

    return _pcall(body, name=name, in_specs=[ANY] * n, out_specs=[ANY] * n,
                  out_shape=[jax.ShapeDtypeStruct(b.shape, b.dtype) for b in bufs],
                  input_output_aliases={a: a for a in range(n)}, scratch_shapes=_sem_pairs(3 * n))(*bufs)


def _sibling_halves(name, ps):
    n = len(ps)

    def body(*refs):
        p_refs, out_refs, (send_sems, recv_sems) = refs[:n], refs[n:2 * n], refs[2 * n:]
        x, y, c = _coords()
        cps = [pltpu.make_async_remote_copy(src_ref=p_refs[a].at[k, 1 - c], dst_ref=out_refs[a].at[k],
                                            send_sem=send_sems.at[4 * a + k], recv_sem=recv_sems.at[4 * a + k],
                                            device_id=(x, y, 1 - c), device_id_type=MESH)
               for a in range(n) for k in range(NCHIP)]
        for cp in cps:
            cp.start()
        for cp in cps:
            cp.wait()

    return _pcall(body, name=name, in_specs=[ANY] * n, out_specs=[ANY] * n,
                  out_shape=[jax.ShapeDtypeStruct((NCHIP,) + p.shape[2:], p.dtype) for p in ps],
                  scratch_shapes=_sem_pairs(NCHIP * n))(*ps)


def _chip_exchange(name, qs):
    n = len(qs)

    def body(*refs):
        q_refs, out_refs, (send_sems, recv_sems) = refs[:n], refs[n:2 * n], refs[2 * n:]
        x, y, c = _coords()
        cps = [pltpu.make_async_remote_copy(src_ref=q_refs[a].at[2 * cx + cy], dst_ref=out_refs[a].at[j],
                                            send_sem=send_sems.at[3 * a + j], recv_sem=recv_sems.at[3 * a + j],
                                            device_id=(cx, cy, c), device_id_type=MESH)
               for a in range(n) for j, (cx, cy) in enumerate(_other_chips(x, y))]
        for cp in cps:
            cp.start()
        for cp in cps:
            cp.wait()

    return _pcall(body, name=name, in_specs=[ANY] * n, out_specs=[ANY] * n,
                  out_shape=[jax.ShapeDtypeStruct((3,) + q.shape[1:], q.dtype) for q in qs],
                  scratch_shapes=_sem_pairs(3 * n))(*qs)


def _sibling_swap(name, hs):
    n = len(hs)

    def body(*refs):
        out_refs, (send_sems, recv_sems) = refs[n:2 * n], refs[2 * n:]
        x, y, c = _coords()
        sib = (x, y, 1 - c)
        sends = [pltpu.make_async_remote_copy(src_ref=out_refs[a].at[c], dst_ref=out_refs[a].at[c],
                                              send_sem=send_sems.at[a], recv_sem=recv_sems.at[a], device_id=sib,
                                              device_id_type=MESH) for a in range(n)]
        for cp in sends:
            cp.start()
        for a in range(n):
            theirs = out_refs[a].at[1 - c]
            pltpu.make_async_remote_copy(src_ref=theirs, dst_ref=theirs, send_sem=send_sems.at[a],
                                         recv_sem=recv_sems.at[a], device_id=sib, device_id_type=MESH).wait_recv()
        for cp in sends:
            cp.wait_send()

    return _pcall(body, name=name, in_specs=[ANY] * n, out_specs=[ANY] * n,
                  out_shape=[jax.ShapeDtypeStruct(h.shape, h.dtype) for h in hs],
                  input_output_aliases={a: a for a in range(n)}, scratch_shapes=_sem_pairs(n))(*hs)


ADD_BLOCK_BYTES = 2 * 1024 * 1024


def _row_block(R, C):
    if R * C * 4 <= ADD_BLOCK_BYTES:
        return R
    for br in (512, 256, 128, 64, 32, 16, 8):
        if R % br == 0 and br * C * 4 <= ADD_BLOCK_BYTES:
            return br
    return R


def _add_own_half(name, p, recv, c_arr, out_dtype):
    _, _, R, C = p.shape
    br = _row_block(R, C)

    def body(c_ref, p_ref, r_ref, o_ref):
        o_ref[...] = (p_ref[...].astype(F32) + r_ref[...].astype(F32)).astype(o_ref.dtype)

    spec = pltpu.PrefetchScalarGridSpec(
        num_scalar_prefetch=1, grid=(NCHIP, R // br),
        in_specs=[pl.BlockSpec((None, None, br, C), lambda k, r, c_ref: (k, c_ref[0], r, 0)),
                  pl.BlockSpec((None, br, C), lambda k, r, c_ref: (k, r, 0))],
        out_specs=pl.BlockSpec((None, br, C), lambda k, r, c_ref: (k, r, 0)))
    return _pcall(body, name=name, grid_spec=spec, out_shape=jax.ShapeDtypeStruct((NCHIP, R, C), out_dtype),
                  compiler_params=_params(("arbitrary", "arbitrary")))(c_arr, p, recv)


def _add_chips(name, q, recv, idx_arr):
    _, R, C = q.shape
    br = _row_block(R, C)

    def body(idx_ref, q_ref, r_ref, o_ref):
        o_ref[...] = ((q_ref[...].astype(F32) + r_ref[0].astype(F32)) + r_ref[1].astype(F32)) + r_ref[2].astype(F32)

    spec = pltpu.PrefetchScalarGridSpec(
        num_scalar_prefetch=1, grid=(R // br,),
        in_specs=[pl.BlockSpec((None, br, C), lambda r, idx: (idx[0], r, 0)),
                  pl.BlockSpec((3, br, C), lambda r, idx: (0, r, 0))],
        out_specs=pl.BlockSpec((None, br, C), lambda r, idx: (idx[1], r, 0)))
    return _pcall(body, name=name, grid_spec=spec, out_shape=jax.ShapeDtypeStruct((2, R, C), F32),
                  compiler_params=_params(("arbitrary",)))(idx_arr, q, recv)


EARLY = ['w_in_e']
LATE = [n for n in BIG if n not in EARLY]


def _adamw(name, w, g, m, v):
    shape = w.shape
    cols = shape[-1]
    rows = w.size // cols
    w2, g2, m2, v2 = (a.reshape(rows, cols) for a in (w, g, m, v))
    bt = next((b for b in (256, 128) if rows % b == 0), rows)

    def body(w_ref, g_ref, m_ref, v_ref, d_ref, nm_ref, nv_ref):
        gv = g_ref[...]
        nm = ADAM_B1 * m_ref[...] + (1.0 - ADAM_B1) * gv
        nv = ADAM_B2 * v_ref[...] + (1.0 - ADAM_B2) * (gv * gv)
        m_hat = nm / (1.0 - ADAM_B1 ** ADAM_STEP)
        v_hat = nv / (1.0 - ADAM_B2 ** ADAM_STEP)
        d_ref[...] = -ADAM_LR * (m_hat / (jnp.sqrt(v_hat) + ADAM_EPS) + ADAM_WD * w_ref[...])
        nm_ref[...] = nm
        nv_ref[...] = nv

    blk = pl.BlockSpec((bt, cols), lambda i: (i, 0))
    outs = _pcall(body, name=name, grid=(rows // bt,), in_specs=[blk] * 4, out_specs=[blk] * 3,
                  out_shape=[jax.ShapeDtypeStruct((rows, cols), F32)] * 3, compiler_params=_params(("arbitrary",)))(
        w2, g2, m2, v2)
    return [o.reshape(shape) for o in outs]


SMALL_SHARDED = ['mix_norm_o', 'conv_b_in', 'conv_dw_w', 'conv_dw_b', 'conv_ln_g', 'conv_ln_b', 'conv_b_out']
REPLICATED = [n for n in WEIGHTS if SHARD_AXIS[n] is None]
NCHIP = 4


def _halves(flat, tile_rows):
    unit = 2 * tile_rows * COLS
    total = -(-flat.size // unit) * unit
    return jnp.pad(flat, (0, total - flat.size)).reshape(2, total // (2 * COLS), COLS)


def _flat(arrays):
    return jnp.concatenate([a.reshape(-1) for a in arrays])


def _chip_block(a, axis, k):
    n = a.shape[axis] // NCHIP
    return lax.slice_in_dim(a, k * n, (k + 1) * n, axis=axis)


def _full_shape(n, shard_shape):
    s = list(shard_shape[n])
    s[SHARD_AXIS[n]] *= NCHIP
    return tuple(s)


def _unpack(flat, names, shapes):
    out, off = {}, 0
    for n in names:
        size = math.prod(shapes[n])
        out[n] = flat[off:off + size].reshape(shapes[n])
        off += size
    return out


def kernel(x, mem, mix_norm_e, w_in_e, fox_f_bias, gmlp_ln_g, gmlp_ln_b, gmlp_w_s, gmlp_b_s, w_out_e, mix_norm_o, conv_w_in, conv_b_in, conv_dw_w, conv_dw_b, conv_ln_g, conv_ln_b, conv_w_out, conv_b_out, xa_norm, mem_norm, xa_wq, xa_wkv, xa_wo, ffn_norm, ffn_w_gu, ffn_w_down, final_norm, loss_target, m_mix_norm_e, m_w_in_e, m_fox_f_bias, m_gmlp_ln_g, m_gmlp_ln_b, m_gmlp_w_s, m_gmlp_b_s, m_w_out_e, m_mix_norm_o, m_conv_w_in, m_conv_b_in, m_conv_dw_w, m_conv_dw_b, m_conv_ln_g, m_conv_ln_b, m_conv_w_out, m_conv_b_out, m_xa_norm, m_mem_norm, m_xa_wq, m_xa_wkv, m_xa_wo, m_ffn_norm, m_ffn_w_gu, m_ffn_w_down, m_final_norm, v_mix_norm_e, v_w_in_e, v_fox_f_bias, v_gmlp_ln_g, v_gmlp_ln_b, v_gmlp_w_s, v_gmlp_b_s, v_w_out_e, v_mix_norm_o, v_conv_w_in, v_conv_b_in, v_conv_dw_w, v_conv_dw_b, v_conv_ln_g, v_conv_ln_b, v_conv_w_out, v_conv_b_out, v_xa_norm, v_mem_norm, v_xa_wq, v_xa_wkv, v_xa_wo, v_ffn_norm, v_ffn_w_gu, v_ffn_w_down, v_final_norm):
    env = locals()
    w = {n: env[n] for n in WEIGHTS}
    m = {n: env["m_" + n] for n in WEIGHTS}
    v = {n: env["v_" + n] for n in WEIGHTS}
    shard_shape = {n: w[n].shape for n in WEIGHTS}
    xi, yi, ci = _coords()
    me = 2 * xi + yi
    c_arr = jnp.reshape(ci, (1,)).astype(jnp.int32)
    idx_arr = jnp.stack([me, ci]).astype(jnp.int32)

    def two_halves(a):
        return a.reshape(2, a.shape[0] // 2, a.shape[1])

    def shard(n, l):
        return two_halves(w[n][l].astype(BF))

    def matrix(n, gathered):
        rows, cols = w[n].shape[1:]
        g = gathered.reshape(NCHIP, rows, cols)
        return g.reshape(NCHIP * rows, cols) if SHARD_AXIS[n] == 1 else g.transpose(1, 0, 2).reshape(rows, NCHIP * cols)

    Wb = {n: [matrix(n, g)] for n, g in zip(EARLY, _all_gather("gather_mixer", [shard(n, 0) for n in EARLY], me))}
    vec = _all_gather("gather_vectors", [_halves(_flat([w[n] for n in SMALL_SHARDED]), 8)], me)[0].reshape(NCHIP, -1)
    parts = [_unpack(vec[k], SMALL_SHARDED, shard_shape) for k in range(NCHIP)]
    P = {n: jnp.concatenate([parts[k][n] for k in range(NCHIP)], axis=SHARD_AXIS[n]) for n in SMALL_SHARDED}
    P.update({n: w[n] for n in REPLICATED})
    first = [(n, 0) for n in LATE]
    second = [(n, 1) for n in LATE if w[n].shape[0] > 1]
    bufs = {key: _own_slot(shard(*key), me) for key in first + second}
    state = {}

    def fwd_done(arrived):
        state['layer0'] = {n: matrix(n, g) for (n, _), g in zip(first, _gather_forward("gather_forward0", arrived))}
        return {n: [g] for n, g in state['layer0'].items()}

    def fwd2_done(arrived):
        return {n: [state['layer0'][n], matrix(n, g)]
                for (n, _), g in zip(second, _gather_forward("gather_forward1", arrived))}

    def by_halves(G, names):
        return [g.reshape(NCHIP, 2, g.shape[1] // 2, g.shape[2]) for n in names for g in G[n]]

    def pair_sums(tag, ps):
        got = _sibling_halves(f"rs_sibling_halves_{tag}", ps)
        return [_add_own_half(f"rs_add_pair_{tag}{a}", p, g, c_arr, p.dtype) for a, (p, g) in enumerate(zip(ps, got))]

    def bwd_carry(G):
        state['qs'] = pair_sums("late", by_halves(G, LATE))
        return _carry_exchange(state['qs'])

    def last_carry(G):
        state['qs_in'] = pair_sums("in", by_halves(G, ['w_in_e']))
        return _carry_exchange(state['qs_in'])

    hooks = dict(fwd_carry=lambda: _carry_gather([bufs[key] for key in first]), fwd_done=fwd_done,
                 fwd2_carry=lambda: _carry_gather([bufs[key] for key in second]), fwd2_done=fwd2_done,
                 bwd_carry=bwd_carry, bwd_done=lambda arrived: state.update(got=arrived),
                 last_carry=last_carry, last_done=lambda arrived: state.update(got_in=arrived))
    loss_part, grad_x, G = _local_step(x, mem, loss_target, Wb, P, hooks)
    loss = lax.psum(loss_part, ("x", "y", "c"))

    def layers(n):
        return G[n] if isinstance(G[n], list) else ([G[n]] if G[n].ndim == 1 else [G[n][l] for l in range(G[n].shape[0])])

    rep = _flat([a for n in REPLICATED for a in layers(n)])
    quarter = -(-rep.size // (NCHIP * 2 * 8 * COLS)) * (2 * 8 * COLS)
    rep = jnp.pad(rep, (0, NCHIP * quarter - rep.size)).reshape(NCHIP, quarter)
    segs = [[_chip_block(a, SHARD_AXIS[n] - 1, k).reshape(-1) for n in SMALL_SHARDED for a in layers(n)] + [rep[k]]
            for k in range(NCHIP)]
    size = sum(piece.size for piece in segs[0])
    total = -(-size // (2 * 8 * COLS)) * (2 * 8 * COLS)
    p_small = jnp.concatenate([piece for seg in segs for piece in seg + [jnp.zeros((total - size,), F32)]])
    p_small = p_small.reshape(NCHIP, 2, total // (2 * COLS), COLS)
    qs_small = pair_sums("vectors", [p_small])
    pending = [("late", state['qs'], state['got']), ("in", state['qs_in'], state['got_in']),
               ("vectors", qs_small, _chip_exchange("rs_chip_exchange_vectors", qs_small))]
    hs = [_add_chips(f"rs_add_chips_{tag}{a}", q, g, idx_arr)
          for tag, qs, got in pending for a, (q, g) in enumerate(zip(qs, got))]
    red = _sibling_swap("rs_sibling_swap", hs)
    mine, at = {}, 0
    for n in LATE + EARLY:
        nl = shard_shape[n][0]
        mine[n] = jnp.stack([r.reshape(shard_shape[n][1:]) for r in red[at:at + nl]])
        at += nl
    red_small = red[-1].reshape(-1)
    mine.update(_unpack(red_small, SMALL_SHARDED, shard_shape))
    off = sum(math.prod(shard_shape[n]) for n in SMALL_SHARDED)
    rep_all = _all_gather("gather_replicated_grads",
                          [red_small[off:off + quarter].reshape(2, quarter // (2 * COLS), COLS)], me)[0]
    mine.update(_unpack(rep_all.reshape(-1), REPLICATED, shard_shape))

    grads, deltas, new_m, new_v = [], [], [], []
    for n in WEIGHTS:
        d, nm, nv = _adamw("adamw_" + n, w[n], mine[n], m[n], v[n])
        grads.append(mine[n])
        deltas.append(d)
        new_m.append(nm)
        new_v.append(nv)
    return (loss, grad_x, *grads, *deltas, *new_m, *new_v)
```

```python
import functools
import math

import jax
import jax.numpy as jnp
from jax import lax
from jax.experimental import pallas as pl
from jax.experimental.pallas import tpu as pltpu

F32 = jnp.float32
BF = jnp.bfloat16
MESH = pl.DeviceIdType.MESH

D = 1024
FOXW = 512
HD = 64
GW = 512
CH = 128
NG = 8
FF = 2816
NMEM = 256
XH = 4
XD = 256
CK = 31
HALO = 32
EPS = 1e-6
IN_W = 2568
IN_WP = 2688
VMEM_LIMIT = 56 * 1024 * 1024

ADAM_LR, ADAM_B1, ADAM_B2, ADAM_EPS, ADAM_WD, ADAM_STEP = 0.001, 0.9, 0.999, 1e-08, 0.01, 10

WEIGHTS = ['mix_norm_e', 'w_in_e', 'fox_f_bias', 'gmlp_ln_g', 'gmlp_ln_b', 'gmlp_w_s', 'gmlp_b_s', 'w_out_e',
           'mix_norm_o', 'conv_w_in', 'conv_b_in', 'conv_dw_w', 'conv_dw_b', 'conv_ln_g', 'conv_ln_b',
           'conv_w_out', 'conv_b_out', 'xa_norm', 'mem_norm', 'xa_wq', 'xa_wkv', 'xa_wo', 'ffn_norm',
           'ffn_w_gu', 'ffn_w_down', 'final_norm']
SHARD_AXIS = {'mix_norm_e': None, 'w_in_e': 2, 'fox_f_bias': None, 'gmlp_ln_g': None, 'gmlp_ln_b': None,
              'gmlp_w_s': None, 'gmlp_b_s': None, 'w_out_e': 1, 'mix_norm_o': 1, 'conv_w_in': 2, 'conv_b_in': 1,
              'conv_dw_w': 2, 'conv_dw_b': 1, 'conv_ln_g': 1, 'conv_ln_b': 1, 'conv_w_out': 1, 'conv_b_out': 1,
              'xa_norm': None, 'mem_norm': None, 'xa_wq': 1, 'xa_wkv': 2, 'xa_wo': 1, 'ffn_norm': None,
              'ffn_w_gu': 2, 'ffn_w_down': 1, 'final_norm': None}
BIG = ['w_in_e', 'w_out_e', 'conv_w_in', 'conv_w_out', 'xa_wq', 'xa_wkv', 'xa_wo', 'ffn_w_gu', 'ffn_w_down']


def _pcall(body, **kw):
    return pl.pallas_call(body, **kw)


def _params(sem=None, **kw):
    return pltpu.CompilerParams(dimension_semantics=sem, vmem_limit_bytes=VMEM_LIMIT, **kw)


def _dot(a, b, dims):
    dn = {'nn': (((1,), (0,)), ((), ())), 'nt': (((1,), (1,)), ((), ())), 'tn': (((0,), (0,)), ((), ()))}[dims]
    return lax.dot_general(a, b, dn, preferred_element_type=F32)


def _sigmoid(x):
    return 1.0 / (1.0 + jnp.exp(-x))


def _rms_stats(xv):
    return lax.rsqrt(jnp.mean(xv * xv, axis=-1, keepdims=True) + EPS)


def _rms_bwd(xv, gain, dh):
    r = _rms_stats(xv)
    t = dh * gain
    dx = r * t - xv * (r * r * r * jnp.mean(t * xv, axis=-1, keepdims=True))
    return dx, dh * xv * r


def _fused_mm(name, *, dims, M, N, bm, bn, groups, epi, outs, x=None, gain=None, tiles=(), rows=(),
              h_out=False, reds=(), carry=None, cols_outer=False):
    bm = min(bm, M)
    nI, nJ = M // bm, N // bn
    assert nI * bm == M and nJ * bn == N
    assert not reds or nJ == 1
    arrays, specs = [], []

    def spec(shape, index):
        return pl.BlockSpec(shape, (lambda jj, ii: index(ii, jj)) if cols_outer else index)

    def add(arr, shape, index):
        arrays.append(arr)
        specs.append(spec(shape, index))
        return len(arrays) - 1

    def first_pass(i, j):
        return (jnp.where(j == 0, i, nI - 1), 0) if cols_outer else (i, 0)

    if x is not None:
        K0 = x.shape[1]
        add(x, (bm, K0), first_pass)
        add(gain, (1, K0), lambda i, j: (0, 0))
    plan = []
    for grp in groups:
        g = []
        for p in grp:
            ai = None
            if p['A'] is not None:
                ai = add(p['A'], (bm, p['Ka']), lambda i, j, o=p.get('acoff', 0): (i, o))
            ro, co = p.get('roff', 0), p.get('coff', 0)
            if dims == 'nn':
                bi = add(p['B'], (p['Ka'], bn), lambda i, j, ro=ro, co=co: (ro, j + co))
            else:
                bi = add(p['B'], (bn, p['Ka']), lambda i, j, ro=ro, co=co: (j + ro, co))
            g.append((ai, bi))
        plan.append(g)
    tile_idx = [add(a, (bm, bn), lambda i, j, o=o: (i, j + o)) for a, o in tiles]
    row_idx = [add(a, (1, bn), lambda i, j, o=o: (0, j + o)) for a, o in rows]
    n_in = len(arrays)

    out_shape = [jax.ShapeDtypeStruct((M, N), dt) for dt in outs]
    out_specs = [spec((bm, bn), lambda i, j: (i, j)) for _ in outs]
    if h_out:
        out_shape.append(jax.ShapeDtypeStruct((M, x.shape[1]), BF))
        out_specs.append(spec((bm, x.shape[1]), first_pass))
    for shp in reds:
        out_shape.append(jax.ShapeDtypeStruct(shp, F32))
        out_specs.append(spec(shp, lambda i, j: (0, 0)))
    n_main = len(outs)
    scratch = [pltpu.VMEM((M if cols_outer else bm, x.shape[1]), BF)] if x is not None else []

    def body(*refs):
        ins, out_refs, scr = refs[:n_in], refs[n_in:n_in + len(out_shape)], refs[n_in + len(out_shape):]
        i, j = (pl.program_id(1), pl.program_id(0)) if cols_outer else (pl.program_id(0), pl.program_id(1))
        if x is not None:
            hn_rows = pl.ds(pl.multiple_of(i * bm, bm), bm) if cols_outer else slice(None)
            hn_ref = scr[0]

            @pl.when(j == 0)
            def _():
                xv = ins[0][...]
                hn = (xv * _rms_stats(xv) * ins[1][...]).astype(BF)
                hn_ref[hn_rows, :] = hn
                if h_out:
                    out_refs[n_main][...] = hn

        accs = []
        for g in plan:
            acc = None
            for ai, bi in g:
                a = hn_ref[hn_rows, :] if ai is None else ins[ai][...]
                if a.dtype != BF:
                    a = a.astype(BF)
                d = _dot(a, ins[bi][...], dims)
                acc = d if acc is None else acc + d
            accs.append(acc)
        out_vals, red_vals = epi(accs, [ins[t][...] for t in tile_idx], [ins[r][...] for r in row_idx])
        for r, v in zip(out_refs[:n_main], out_vals):
            r[...] = v.astype(r.dtype)
        if reds:
            red_refs = out_refs[n_main + (1 if h_out else 0):]

            @pl.when(i == 0)
            def _():
                for r in red_refs:
                    r[...] = jnp.zeros(r.shape, F32)

            for r, v in zip(red_refs, red_vals):
                r[...] += v

    res, arrived = _carried_call(body, name=name, grid=(nJ, nI) if cols_outer else (nI, nJ), in_specs=specs,
                                 out_specs=out_specs,
                                 out_shape=out_shape, scratch_shapes=scratch, operands=arrays, carry=carry)
    return res if carry is None else (res, arrived)


def _epi_plain(accs, tiles, rows):
    return [accs[0]], []


def _epi_resid(accs, tiles, rows):
    y = tiles[0] + accs[0]
    if rows:
        y = y + rows[0]
    return [y], []


def _epi_swiglu(accs, tiles, rows):
    g, u = accs
    return [g, u, g * _sigmoid(g) * u], []


def _epi_glu(accs, tiles, rows):
    a, g = accs[0] + rows[0], accs[1] + rows[1]
    return [a, g, a * _sigmoid(g)], []


def _epi_swiglu_bwd(accs, tiles, rows):
    da = accs[0]
    g, u = tiles[0].astype(F32), tiles[1].astype(F32)
    sg = _sigmoid(g)
    return [da * u * (sg * (1.0 + g * (1.0 - sg))), da * (g * sg)], []


def _epi_rms_bwd(accs, tiles, rows):
    dx, dgr = _rms_bwd(tiles[0], rows[0], accs[0])
    return [tiles[1] + dx], [jnp.sum(dgr, axis=0, keepdims=True)]


def _epi_rms_gain_only(accs, tiles, rows):
    _, dgr = _rms_bwd(tiles[0], rows[0], accs[0])
    return [], [jnp.sum(dgr, axis=0, keepdims=True)]


def _norm_mm(name, x, gain, W, *, N, coff=0, bn, out_dtype, bm=1024, h_out=False):
    return _fused_mm(name, dims='nn', M=x.shape[0], N=N, bm=bm, bn=bn, x=x, gain=gain,
                     groups=[[dict(A=None, Ka=x.shape[1], B=W, coff=coff)]], epi=_epi_plain, outs=[out_dtype],
                     h_out=h_out)


def _mm_resid(name, pairs, resid, bias=None, bm=1024):
    M = resid.shape[0]
    return _fused_mm(name, dims='nn', M=M, N=D, bm=bm, bn=D, groups=[pairs], epi=_epi_resid, outs=[F32],
                     tiles=[(resid, 0)], rows=[(bias, 0)] if bias is not None else [])[0]


def _mm_nt_plain(name, dy, W, bm=1024):
    return _fused_mm(name, dims='nt', M=dy.shape[0], N=W.shape[0], bm=bm, bn=W.shape[0],
                     groups=[[dict(A=dy, Ka=dy.shape[1], B=W)]], epi=_epi_plain, outs=[BF])[0]


def _mm_nt_rms_bwd(name, pairs, x, gain, dx_in, bm=256, carry=None):
    out = _fused_mm(name, dims='nt', M=x.shape[0], N=D, bm=bm, bn=D, groups=[pairs], epi=_epi_rms_bwd,
                    outs=[F32], tiles=[(x, 0), (dx_in, 0)], rows=[(gain, 0)], reds=[(1, D)], carry=carry)
    if carry is None:
        return out[0], out[1]
    return out[0][0], out[0][1], out[1]


def _mm_tn(name, A, G, bk=2048, parts=1):
    T, Ka, Kg = A.shape[0], A.shape[1], G.shape[1]
    w = Kg // parts
    bm = Ka if Ka <= 1024 else Ka // 2
    bn = w if w <= 1408 else w // 2
    bk = min(bk, T)
    per = w // bn
    nI, nJ, nK = Ka // bm, Kg // bn, T // bk

    def body(a_ref, g_ref, o_ref, acc):
        k = pl.program_id(2)

        @pl.when(k == 0)
        def _():
            acc[...] = jnp.zeros(acc.shape, F32)

        acc[...] += _dot(a_ref[...].astype(BF), g_ref[...].astype(BF), 'tn')

        @pl.when(k == nK - 1)
        def _():
            o_ref[...] = acc[...].astype(BF)

    return _pcall(body, name=name, grid=(nI, nJ, nK),
                  in_specs=[pl.BlockSpec((bk, bm), lambda i, j, k: (k, i)),
                            pl.BlockSpec((bk, bn), lambda i, j, k: (k, j))],
                  out_specs=pl.BlockSpec((None, bm, bn), lambda i, j, k: (j // per, i, j % per)),
                  out_shape=jax.ShapeDtypeStruct((parts, Ka, w), BF),
                  scratch_shapes=[pltpu.VMEM((bm, bn), F32)],
                  compiler_params=_params(("arbitrary", "arbitrary", "arbitrary")))(A, G)


def _colsum(name, a, bt=512):
    M, N = a.shape
    bt = min(bt, M)

    def body(a_ref, o_ref):
        @pl.when(pl.program_id(0) == 0)
        def _():
            o_ref[...] = jnp.zeros(o_ref.shape, F32)

        o_ref[...] += jnp.sum(a_ref[...].astype(F32), axis=0, keepdims=True)

    return _pcall(body, name=name, grid=(M // bt,), in_specs=[pl.BlockSpec((bt, N), lambda i: (i, 0))],
                  out_specs=pl.BlockSpec((1, N), lambda i: (0, 0)), out_shape=jax.ShapeDtypeStruct((1, N), F32),
                  compiler_params=_params(("arbitrary",)))(a)


def _cumsum_rows(v):
    T = v.shape[0]
    row = lax.broadcasted_iota(jnp.int32, v.shape, 0)
    s = 1
    while s < T:
        v = v + jnp.where(row >= s, pltpu.roll(v, s, 0), 0.0)
        s *= 2
    return v


def _log_sigmoid(z):
    return jnp.minimum(z, 0.0) - jnp.log(1.0 + jnp.exp(-jnp.abs(z)))


def _fox_gate_fwd(fl, fbias, B, T):
    def body(fl_ref, b_ref, o_ref):
        o_ref[...] = _cumsum_rows(_log_sigmoid(fl_ref[...] + b_ref[...]))

    return _pcall(body, name="fox_gate_fwd", grid=(B,),
                  in_specs=[pl.BlockSpec((T, 128), lambda b: (b, 0)), pl.BlockSpec((1, 128), lambda b: (0, 0))],
                  out_specs=pl.BlockSpec((T, 128), lambda b: (b, 0)),
                  out_shape=jax.ShapeDtypeStruct((B * T, 128), F32), compiler_params=_params(("arbitrary",)))(fl, fbias)


def _fox_gate_bwd(fl, fbias, dcq, dck, B, T):
    def body(fl_ref, b_ref, dcq_ref, dck_ref, dfl_ref, db_ref):
        dc = dcq_ref[...] + dck_ref[...]
        rev = jnp.sum(dc, axis=0, keepdims=True) - _cumsum_rows(dc) + dc
        dfl = rev * _sigmoid(-(fl_ref[...] + b_ref[...]))
        dfl_ref[...] = dfl

        @pl.when(pl.program_id(0) == 0)
        def _():
            db_ref[...] = jnp.zeros(db_ref.shape, F32)

        db_ref[...] += jnp.sum(dfl, axis=0, keepdims=True)

    return _pcall(body, name="fox_gate_bwd", grid=(B,),
                  in_specs=[pl.BlockSpec((T, 128), lambda b: (b, 0)), pl.BlockSpec((1, 128), lambda b: (0, 0)),
                            pl.BlockSpec((T, 128), lambda b: (b, 0)), pl.BlockSpec((T, 128), lambda b: (b, 0))],
                  out_specs=[pl.BlockSpec((T, 128), lambda b: (b, 0)), pl.BlockSpec((1, 128), lambda b: (0, 0))],
                  out_shape=[jax.ShapeDtypeStruct((B * T, 128), F32), jax.ShapeDtypeStruct((1, 128), F32)],
                  compiler_params=_params(("arbitrary",)))(fl, fbias, dcq, dck)


def _carried_call(body, *, name, grid, in_specs, out_specs, out_shape, scratch_shapes, operands, carry):
    if carry is None:
        return _pcall(body, name=name, grid=grid, in_specs=in_specs, out_specs=out_specs, out_shape=out_shape,
                      scratch_shapes=scratch_shapes, compiler_params=_params(("arbitrary",) * len(grid)))(*operands), []
    n, n_in, n_out, n_scr = len(carry['inputs']), len(in_specs), len(out_specs), len(scratch_shapes)

    def wrapped(*refs):
        ins, cin = refs[:n_in], refs[n_in:n_in + n]
        outs, cout = refs[n_in + n:n_in + n + n_out], refs[n_in + n + n_out:n_in + 2 * n + n_out]
        scr = refs[n_in + 2 * n + n_out:]
        send_sems, recv_sems = scr[n_scr:]
        ids = [pl.program_id(d) for d in range(len(grid))]
        first = functools.reduce(jnp.logical_and, [i == 0 for i in ids])
        last = functools.reduce(jnp.logical_and, [i == g - 1 for i, g in zip(ids, grid)])

        @pl.when(first)
        def _():
            for cp in carry['copies'](cin, cout, send_sems, recv_sems):
                cp.start()

        body(*ins, *outs, *scr[:n_scr])

        @pl.when(last)
        def _():
            for cp in carry['copies'](cin, cout, send_sems, recv_sems):
                cp.wait()

    aliases = {n_in + a: n_out + a for a in range(n)} if carry['in_place'] else {}
    res = _pcall(wrapped, name=name, grid=grid, in_specs=list(in_specs) + [ANY] * n,
                 out_specs=list(out_specs) + [ANY] * n, out_shape=list(out_shape) + carry['out_shape'],
                 scratch_shapes=list(scratch_shapes) + _sem_pairs(carry['nsem']), input_output_aliases=aliases,
                 compiler_params=_params(("arbitrary",) * len(grid)))(*operands, *carry['inputs'])
    return res[:n_out], res[n_out:]


def _carry_gather(bufs):
    n = len(bufs)

    def copies(in_refs, out_refs, send_sems, recv_sems):
        x, y, c = _coords()
        cps = []
        for a in range(n):
            blk = out_refs[a].at[2 * x + y, c]
            cps += [pltpu.make_async_remote_copy(src_ref=blk, dst_ref=blk, send_sem=send_sems.at[3 * a + j],
                                                 recv_sem=recv_sems.at[3 * a + j], device_id=(cx, cy, c),
                                                 device_id_type=MESH) for j, (cx, cy) in enumerate(_other_chips(x, y))]
        return cps

    return dict(inputs=bufs, out_shape=[jax.ShapeDtypeStruct(b.shape, b.dtype) for b in bufs], in_place=True,
                nsem=3 * n, copies=copies)


def _carry_exchange(qs):
    n = len(qs)

    def copies(in_refs, out_refs, send_sems, recv_sems):
        x, y, c = _coords()
        return [pltpu.make_async_remote_copy(src_ref=in_refs[a].at[2 * cx + cy], dst_ref=out_refs[a].at[j],
                                             send_sem=send_sems.at[3 * a + j], recv_sem=recv_sems.at[3 * a + j],
                                             device_id=(cx, cy, c), device_id_type=MESH)
                for a in range(n) for j, (cx, cy) in enumerate(_other_chips(x, y))]

    return dict(inputs=qs, out_shape=[jax.ShapeDtypeStruct((3,) + q.shape[1:], q.dtype) for q in qs], in_place=False,
                nsem=3 * n, copies=copies)


NEG = -1e30
QSUB = 1


def _fox_fwd(qkv, cum4, B, T, qoff, bq, bk, carry=None):
    nq, nkb = T // bq, T // bk
    N = B * T

    def body(q_ref, k_ref, v_ref, cum_ref, o_ref, lse_ref):
        hp, i = pl.program_id(1), pl.program_id(2)
        sq = bq // QSUB
        lane = lax.broadcasted_iota(jnp.int32, (sq, 128), 1)
        heads = [slice(e * HD, (e + 1) * HD) for e in range(2)]
        chains = [(e, sl, slice(r * sq, (r + 1) * sq)) for e, sl in enumerate(heads) for r in range(QSUB)]
        qs = [q_ref[rows, sl] * 0.125 for _, sl, rows in chains]

        def block(j, carry, diagonal):
            ks = pl.multiple_of(j * bk, bk)
            out = []
            for n, (e, sl, rows) in enumerate(chains):
                m, l, acc = carry[n]
                s = _dot(qs[n], k_ref[pl.ds(ks, bk), sl], 'nt') - cum_ref[0, 2 * hp + e, pl.ds(j, 1), :]
                if diagonal:
                    keep = (lax.broadcasted_iota(jnp.int32, (sq, bk), 0) + rows.start
                            >= lax.broadcasted_iota(jnp.int32, (sq, bk), 1))
                    s = jnp.where(keep, s, NEG)
                m_new = jnp.maximum(m, jnp.max(s, axis=1, keepdims=True))
                p = jnp.exp(s - m_new)
                alpha = jnp.exp(m - m_new)
                l = alpha * l + jnp.sum(p, axis=1, keepdims=True)
                acc = alpha * acc + _dot(p.astype(BF), v_ref[pl.ds(ks, bk), sl], 'nn')
                out.append((m_new, l, acc))
            return tuple(out)

        init = tuple((jnp.full((sq, 1), NEG, F32), jnp.zeros((sq, 1), F32), jnp.zeros((sq, HD), F32)) for _ in chains)
        carry = lax.fori_loop(0, i, lambda j, c: block(j, c, False), init)
        carry = block(i, carry, True)
        for r in range(QSUB):
            lse_tile = jnp.zeros((sq, 128), F32)
            for n, (e, sl, rows) in enumerate(chains):
                if rows.start == r * sq:
                    m, l, acc = carry[n]
                    o_ref[rows, sl] = (acc / l).astype(BF)
                    lse_tile = jnp.where(lane == e, m + jnp.log(l), lse_tile)
            lse_ref[r * sq:(r + 1) * sq, :] = lse_tile

    return _carried_call(body, name="fox_fwd", grid=(B, 4, nq),
                         in_specs=[pl.BlockSpec((bq, 128), lambda b, h, i: (b * nq + i, qoff + h)),
                                   pl.BlockSpec((T, 128), lambda b, h, i: (b, qoff + 4 + h)),
                                   pl.BlockSpec((T, 128), lambda b, h, i: (b, qoff + 8 + h)),
                                   pl.BlockSpec((1, NG, nkb, bk), lambda b, h, i: (b, 0, 0, 0))],
                         out_specs=[pl.BlockSpec((bq, 128), lambda b, h, i: (b * nq + i, h)),
                                    pl.BlockSpec((bq, 128), lambda b, h, i: (b * nq + i, h))],
                         out_shape=[jax.ShapeDtypeStruct((N, FOXW), BF), jax.ShapeDtypeStruct((N, FOXW), F32)],
                         scratch_shapes=[], operands=(qkv, qkv, qkv, cum4), carry=carry)


def _fox_bwd(qkv, cum4, o, lse, dcat, B, T, qoff, bq, bk, carry=None):
    nq, nkb = T // bq, T // bk
    N = B * T

    def body(q_ref, k_ref, v_ref, cum_ref, o_ref, lse_ref, do_ref, dq_ref, dk_ref, dv_ref, dcum_ref, dcq_ref,
             dq_acc, dl_ref, rs_ref):
        hp = pl.program_id(1)
        heads = [slice(e * HD, (e + 1) * HD) for e in range(2)]
        keep = lax.broadcasted_iota(jnp.int32, (bq, bk), 0) >= lax.broadcasted_iota(jnp.int32, (bq, bk), 1)
        dcq_ref[...] = jnp.zeros(dcq_ref.shape, F32)
        dq_acc[...] = jnp.zeros(dq_acc.shape, F32)
        rs_ref[...] = jnp.zeros(rs_ref.shape, F32)
        for e, sl in enumerate(heads):
            dl_ref[e] = jnp.sum(do_ref[:, sl].astype(F32) * o_ref[:, sl].astype(F32), axis=1, keepdims=True)
        for j in range(nkb):
            krows = slice(j * bk, (j + 1) * bk)

            def tile(i, carry, diagonal):
                qs = i * bq if diagonal else pl.multiple_of(i * bq, bq)
                out = []
                for e, sl in enumerate(heads):
                    dk_a, dv_a, cs = carry[e]
                    q, k = q_ref[pl.ds(qs, bq), sl], k_ref[krows, sl]
                    do = do_ref[pl.ds(qs, bq), sl]
                    s = _dot(q, k, 'nt') * 0.125 - cum_ref[0, 2 * hp + e, j:j + 1, :]
                    p = jnp.exp(s - lse_ref[pl.ds(qs, bq), e:e + 1])
                    if diagonal:
                        p = jnp.where(keep, p, 0.0)
                    dv_a = dv_a + _dot(p.astype(BF), do, 'tn')
                    ds = p * (_dot(do, v_ref[krows, sl], 'nt') - dl_ref[e, pl.ds(qs, bq), :])
                    cs = cs + jnp.sum(ds, axis=0, keepdims=True)
                    rs_ref[e, pl.ds(qs, bq), :] += jnp.sum(ds, axis=1, keepdims=True)
                    dsb = ds.astype(BF)
                    dk_a = dk_a + _dot(dsb, q, 'tn')
                    dq_acc[e, pl.ds(qs, bq), :] += _dot(dsb, k, 'nn')
                    out.append((dk_a, dv_a, cs))
                return tuple(out)

            init = tuple((jnp.zeros((bk, HD), F32), jnp.zeros((bk, HD), F32), jnp.zeros((1, bk), F32)) for _ in heads)
            carry = lax.fori_loop(j + 1, nq, lambda i, c: tile(i, c, False), tile(j, init, True))
            for e, sl in enumerate(heads):
                dk_a, dv_a, cs = carry[e]
                dk_ref[krows, sl] = (dk_a * 0.125).astype(BF)
                dv_ref[krows, sl] = dv_a.astype(BF)
                dcum_ref[0, e, j:j + 1, :] = -cs
        for e, sl in enumerate(heads):
            dq_ref[:, sl] = (dq_acc[e] * 0.125).astype(BF)
            dcq_ref[:, e:e + 1] = rs_ref[e]

    seq = lambda off: pl.BlockSpec((T, 128), lambda b, h, off=off: (b, off + h))
    return _carried_call(body, name="fox_bwd", grid=(B, 4),
                         in_specs=[seq(qoff), seq(qoff + 4), seq(qoff + 8),
                                   pl.BlockSpec((1, NG, nkb, bk), lambda b, h: (b, 0, 0, 0)),
                                   seq(0), seq(0), seq(0)],
                         out_specs=[seq(0), seq(0), seq(0),
                                    pl.BlockSpec((1, 2, nkb, bk), lambda b, h: (b, h, 0, 0)), seq(0)],
                         out_shape=[jax.ShapeDtypeStruct((N, FOXW), BF)] * 3
                         + [jax.ShapeDtypeStruct((B, NG, nkb, bk), F32), jax.ShapeDtypeStruct((N, FOXW), F32)],
                         scratch_shapes=[pltpu.VMEM((2, T, HD), F32), pltpu.VMEM((2, T, 1), F32),
                                         pltpu.VMEM((2, T, 1), F32)],
                         operands=(qkv, qkv, qkv, cum4, o, lse, dcat), carry=carry)


_GC = math.sqrt(2.0 / math.pi)
_GA = 0.044715


def _gelu(z):
    return 0.5 * z * (1.0 + jnp.tanh(_GC * (z + _GA * z * z * z)))


def _gelu_grad(z):
    t = jnp.tanh(_GC * (z + _GA * z * z * z))
    return 0.5 * (1.0 + t) + 0.5 * z * (1.0 - t * t) * (_GC * (1.0 + 3.0 * _GA * z * z))


def _gmlp_common(z, lng, lnb):
    zg = _gelu(z)
    u, vg = zg[:, :GW], zg[:, GW:]
    mu = jnp.mean(vg, axis=-1, keepdims=True)
    xc = vg - mu
    rstd = lax.rsqrt(jnp.mean(xc * xc, axis=-1, keepdims=True) + EPS)
    xhat = xc * rstd
    return u, xhat, rstd, xhat * lng + lnb


def _tril_w(ws_ref):
    tri = lax.broadcasted_iota(jnp.int32, (CH, CH), 0) >= lax.broadcasted_iota(jnp.int32, (CH, CH), 1)
    return [jnp.where(tri, ws_ref[g], 0.0).astype(BF) for g in range(NG)], tri


def _split_pair(vp):
    lane = lax.broadcasted_iota(jnp.int32, vp.shape, 1)
    zero = jnp.zeros(vp.shape, vp.dtype)
    return jnp.concatenate([jnp.where(lane < HD, vp, zero), jnp.where(lane >= HD, vp, zero)], axis=0)


def _gmlp_mix(wt, vgn_b):
    outs = []
    for p in range(NG // 2):
        wcat = jnp.concatenate([wt[2 * p], wt[2 * p + 1]], axis=1)
        outs.append(_dot(wcat, _split_pair(vgn_b[:, 128 * p:128 * (p + 1)]), 'nn'))
    return jnp.concatenate(outs, axis=1)


def _gmlp_fwd(z, lng, lnb, ws, bfull, bt):
    N = z.shape[0]

    def body(z_ref, lng_ref, lnb_ref, ws_ref, bf_ref, o_ref):
        wt, _ = _tril_w(ws_ref)
        for c in range(bt // CH):
            rows = slice(c * CH, (c + 1) * CH)
            u, _, _, vgn = _gmlp_common(z_ref[rows, :], lng_ref[...], lnb_ref[...])
            mixed = _gmlp_mix(wt, vgn.astype(BF)) + bf_ref[...]
            o_ref[rows, :] = (u * mixed).astype(BF)

    full = lambda shp: pl.BlockSpec(shp, lambda i: (0,) * len(shp))
    return _pcall(body, name="gmlp_fwd", grid=(N // bt,),
                  in_specs=[pl.BlockSpec((bt, D), lambda i: (i, 0)), full((1, GW)), full((1, GW)),
                            full((NG, CH, CH)), full((CH, GW))],
                  out_specs=pl.BlockSpec((bt, GW), lambda i: (i, 0)), out_shape=jax.ShapeDtypeStruct((N, GW), BF),
                  compiler_params=_params(("arbitrary",)))(z, lng, lnb, ws, bfull)


def _gmlp_bwd(z, dcat, lng, lnb, ws, bfull, bt):
    N = z.shape[0]

    def body(z_ref, da_ref, lng_ref, lnb_ref, ws_ref, bf_ref, dz_ref, dg_ref, db_ref, dws_ref, dbf_ref):
        @pl.when(pl.program_id(0) == 0)
        def _():
            for r in (dg_ref, db_ref, dws_ref, dbf_ref):
                r[...] = jnp.zeros(r.shape, F32)

        wt, tri = _tril_w(ws_ref)
        lane = lax.broadcasted_iota(jnp.int32, (CH, 128), 1)
        for c in range(bt // CH):
            rows = slice(c * CH, (c + 1) * CH)
            zc = z_ref[rows, :]
            u, xhat, rstd, vgn = _gmlp_common(zc, lng_ref[...], lnb_ref[...])
            vgn_b = vgn.astype(BF)
            mixed = _gmlp_mix(wt, vgn_b) + bf_ref[...]
            da = da_ref[rows, :].astype(F32)
            dmix = da * u
            du = da * mixed
            dbf_ref[...] += dmix
            dvs = []
            for p in range(NG // 2):
                cols = slice(128 * p, 128 * (p + 1))
                dmp = dmix[:, cols].astype(BF)
                dwp = _dot(_split_pair(dmp), vgn_b[:, cols], 'nt')
                dws_ref[2 * p] += jnp.where(tri, dwp[:CH], 0.0)
                dws_ref[2 * p + 1] += jnp.where(tri, dwp[CH:], 0.0)
                dvs.append(jnp.where(lane < HD, _dot(wt[2 * p], dmp, 'tn'), _dot(wt[2 * p + 1], dmp, 'tn')))
            dvgn = jnp.concatenate(dvs, axis=1)
            dg_ref[...] += jnp.sum(dvgn * xhat, axis=0, keepdims=True)
            db_ref[...] += jnp.sum(dvgn, axis=0, keepdims=True)
            dxh = dvgn * lng_ref[...]
            dvg = rstd * (dxh - jnp.mean(dxh, axis=-1, keepdims=True)
                          - xhat * jnp.mean(dxh * xhat, axis=-1, keepdims=True))
            dz_ref[rows, :] = (jnp.concatenate([du, dvg], axis=1) * _gelu_grad(zc)).astype(BF)

    full = lambda shp: pl.BlockSpec(shp, lambda i: (0,) * len(shp))
    return _pcall(body, name="gmlp_bwd", grid=(N // bt,),
                  in_specs=[pl.BlockSpec((bt, D), lambda i: (i, 0)), pl.BlockSpec((bt, GW), lambda i: (i, 1)),
                            full((1, GW)), full((1, GW)), full((NG, CH, CH)), full((CH, GW))],
                  out_specs=[pl.BlockSpec((bt, D), lambda i: (i, 0)), full((1, GW)), full((1, GW)),
                             full((NG, CH, CH)), full((CH, GW))],
                  out_shape=[jax.ShapeDtypeStruct((N, D), BF), jax.ShapeDtypeStruct((1, GW), F32),
                             jax.ShapeDtypeStruct((1, GW), F32), jax.ShapeDtypeStruct((NG, CH, CH), F32),
                             jax.ShapeDtypeStruct((CH, GW), F32)],
                  compiler_params=_params(("arbitrary",)))(z, dcat, lng, lnb, ws, bfull)


def _group_sum(name, a):
    def body(a_ref, o_ref):
        lane = lax.broadcasted_iota(jnp.int32, (CH, 128), 1)
        out = jnp.zeros((CH, 128), F32)
        for g in range(NG):
            out = jnp.where(lane == g, jnp.sum(a_ref[:, g * HD:(g + 1) * HD], axis=1, keepdims=True), out)
        o_ref[...] = out

    return _pcall(body, name=name, out_shape=jax.ShapeDtypeStruct((CH, 128), F32))(a)


def _xattn_softmax(q_h, k_h):
    s = _dot(q_h, k_h, 'nt') * (XD ** -0.5)
    p = jnp.exp(s - jnp.max(s, axis=1, keepdims=True))
    return p / jnp.sum(p, axis=1, keepdims=True)


def _xattn_fwd(name, q, kv, B, T, bq):
    nq = T // bq

    def body(q_ref, kv_ref, o_ref):
        for h in range(XH):
            cols = slice(h * XD, (h + 1) * XD)
            p = _xattn_softmax(q_ref[:, cols], kv_ref[:, cols])
            o_ref[:, cols] = _dot(p.astype(BF), kv_ref[:, D + h * XD:D + (h + 1) * XD], 'nn').astype(BF)

    return _pcall(body, name=name, grid=(B, nq),
                  in_specs=[pl.BlockSpec((bq, D), lambda b, i: (b * nq + i, 0)),
                            pl.BlockSpec((NMEM, 2 * D), lambda b, i: (b, 0))],
                  out_specs=pl.BlockSpec((bq, D), lambda b, i: (b * nq + i, 0)),
                  out_shape=jax.ShapeDtypeStruct((B * T, D), BF), compiler_params=_params(("arbitrary", "arbitrary")))(q, kv)


def _xattn_bwd(name, q, kv, do, B, T, bq):
    nq = T // bq
    sc = XD ** -0.5

    def body(q_ref, kv_ref, do_ref, dq_ref, dkv_ref):
        @pl.when(pl.program_id(1) == 0)
        def _():
            dkv_ref[...] = jnp.zeros(dkv_ref.shape, F32)

        for h in range(XH):
            cols = slice(h * XD, (h + 1) * XD)
            vcols = slice(D + h * XD, D + (h + 1) * XD)
            qh, kh, doh = q_ref[:, cols], kv_ref[:, cols], do_ref[:, cols]
            p = _xattn_softmax(qh, kh)
            dp = _dot(doh, kv_ref[:, vcols], 'nt')
            ds = p * (dp - jnp.sum(p * dp, axis=1, keepdims=True))
            dsb = ds.astype(BF)
            dq_ref[:, cols] = (_dot(dsb, kh, 'nn') * sc).astype(BF)
            dkv_ref[:, cols] += _dot(dsb, qh, 'tn') * sc
            dkv_ref[:, vcols] += _dot(p.astype(BF), doh, 'tn')

    blk = pl.BlockSpec((bq, D), lambda b, i: (b * nq + i, 0))
    return _pcall(body, name=name, grid=(B, nq),
                  in_specs=[blk, pl.BlockSpec((NMEM, 2 * D), lambda b, i: (b, 0)), blk],
                  out_specs=[blk, pl.BlockSpec((NMEM, 2 * D), lambda b, i: (b, 0))],
                  out_shape=[jax.ShapeDtypeStruct((B * T, D), BF), jax.ShapeDtypeStruct((B * NMEM, 2 * D), F32)],
                  compiler_params=_params(("arbitrary", "arbitrary")))(q, kv, do)


def _ln_stats(v):
    mu = jnp.mean(v, axis=-1, keepdims=True)
    xc = v - mu
    rstd = lax.rsqrt(jnp.mean(xc * xc, axis=-1, keepdims=True) + EPS)
    return xc * rstd, rstd


SUB = 8


LANES = 128
NSTRIP = D // LANES


def _fill_window(win, parts):
    for s in range(NSTRIP):
        for r0, val in parts:
            win[s, r0:r0 + val.shape[0], :] = val[:, s * LANES:(s + 1) * LANES]


def _fill_phases(win, sh, rows):
    for b in range(1, SUB):
        for s in range(NSTRIP):
            sh[b - 1, s] = win[s, b:b + rows, :]


def _fill_taps(w8, w_ref):
    for s in range(NSTRIP):
        for j in range(CK):
            w8[s, SUB * j:SUB * (j + 1), :] = jnp.broadcast_to(w_ref[j:j + 1, s * LANES:(s + 1) * LANES], (SUB, LANES))


def _row_groups(win, sh, s):
    cache = {}

    def get(o, t):
        a, b = divmod(o, SUB)
        key = (b, t + a)
        if key not in cache:
            rows = slice(SUB * (t + a), SUB * (t + a + 1))
            cache[key] = win[s, rows, :] if b == 0 else sh[b - 1, s, rows, :]
        return cache[key]

    return get


def _from_strips(ref):
    return jnp.concatenate([ref[s] for s in range(NSTRIP)], axis=1)


def _sum_groups(name, a):
    R, C = a.shape[0] // SUB, a.shape[1]

    def body(a_ref, o_ref):
        o_ref[...] = jnp.sum(a_ref[...].reshape(R, SUB, C), axis=1)

    return _pcall(body, name=name, out_shape=jax.ShapeDtypeStruct((R, C), F32))(a)


def _conv_fwd(y, w32, wb, lng, lnb, B, T, bt):
    nt = T // bt
    hb = bt // HALO
    prows = bt + HALO - SUB

    def body(y_ref, yp_ref, w_ref, wb_ref, lng_ref, lnb_ref, s_ref, yc_ref, win, sh, w8, out):
        i = pl.program_id(1)
        _fill_window(win, [(0, jnp.where(i > 0, yp_ref[...], 0.0)), (HALO, y_ref[...])])
        _fill_phases(win, sh, prows)

        @pl.when((pl.program_id(0) == 0) & (i == 0))
        def _():
            _fill_taps(w8, w_ref)

        def strip(s, carry):
            get = _row_groups(win, sh, s)
            for t in range(bt // SUB):
                accs = [jnp.zeros((SUB, LANES), F32), jnp.zeros((SUB, LANES), F32)]
                for j in range(CK):
                    accs[j % 2] = accs[j % 2] + w8[s, SUB * j:SUB * (j + 1), :] * get(HALO - (CK - 1) + j, t)
                out[s, SUB * t:SUB * (t + 1), :] = accs[0] + accs[1]
            return carry

        lax.fori_loop(0, NSTRIP, strip, 0)
        acc = _from_strips(out) + wb_ref[...]
        yc_ref[...] = acc
        xhat, _ = _ln_stats(acc)
        ln = xhat * lng_ref[...] + lnb_ref[...]
        s_ref[...] = (ln * _sigmoid(ln)).astype(BF)

    row = lambda n: pl.BlockSpec((n, D), lambda b, i: (0, 0))
    cur = pl.BlockSpec((bt, D), lambda b, i: (b * nt + i, 0))
    return _pcall(body, name="conv_fwd", grid=(B, nt),
                  in_specs=[cur, pl.BlockSpec((HALO, D), lambda b, i: (jnp.maximum((b * nt + i) * hb - 1, 0), 0)),
                            row(HALO), row(1), row(1), row(1)],
                  out_specs=[cur, cur],
                  out_shape=[jax.ShapeDtypeStruct((B * T, D), BF), jax.ShapeDtypeStruct((B * T, D), F32)],
                  scratch_shapes=[pltpu.VMEM((NSTRIP, bt + HALO, LANES), F32),
                                  pltpu.VMEM((SUB - 1, NSTRIP, prows, LANES), F32),
                                  pltpu.VMEM((NSTRIP, HALO * SUB, LANES), F32), pltpu.VMEM((NSTRIP, bt, LANES), F32)],
                  compiler_params=_params(("arbitrary", "arbitrary")))(y, y, w32, wb, lng, lnb)


def _conv_bwd(ds, yc, y, pa, pg, w32, lng, lnb, B, T, bt):
    nt = T // bt
    hb = bt // HALO
    nblk32 = B * T // HALO

    def ln_bwd(dsv, ycv, lng, lnb):
        xhat, rstd = _ln_stats(ycv)
        ln = xhat * lng + lnb
        sg = _sigmoid(ln)
        dln = dsv * (sg * (1.0 + ln * (1.0 - sg)))
        dxh = dln * lng
        dyc = rstd * (dxh - jnp.mean(dxh, axis=-1, keepdims=True)
                      - xhat * jnp.mean(dxh * xhat, axis=-1, keepdims=True))
        return dyc, dln, xhat

    prows = bt + HALO - SUB

    def body(ds_ref, dsn_ref, yc_ref, ycn_ref, y_ref, yp_ref, pa_ref, pg_ref, w_ref, lng_ref, lnb_ref,
             dpa_ref, dpg_ref, dw_ref, dwb_ref, dlng_ref, dlnb_ref, dba_ref, dbg_ref,
             dwin, ywin, dsh, ysh, w8, dy_out, dw_out):
        i = pl.program_id(1)

        @pl.when((pl.program_id(0) == 0) & (i == 0))
        def _():
            for r in (dw_ref, dwb_ref, dlng_ref, dlnb_ref, dba_ref, dbg_ref):
                r[...] = jnp.zeros(r.shape, F32)
            _fill_taps(w8, w_ref)

        lng, lnb = lng_ref[...], lnb_ref[...]
        dyc, dln, xhat = ln_bwd(ds_ref[...].astype(F32), yc_ref[...], lng, lnb)
        dycn, _, _ = ln_bwd(dsn_ref[...].astype(F32), ycn_ref[...], lng, lnb)
        _fill_window(dwin, [(0, dyc), (bt, jnp.where(i < nt - 1, dycn, 0.0))])
        _fill_window(ywin, [(0, jnp.where(i > 0, yp_ref[...], 0.0)), (HALO, y_ref[...])])
        dlng_ref[...] += jnp.sum(dln * xhat, axis=0, keepdims=True)
        dlnb_ref[...] += jnp.sum(dln, axis=0, keepdims=True)
        dwb_ref[...] += jnp.sum(dyc, axis=0, keepdims=True)
        _fill_phases(dwin, dsh, prows)
        _fill_phases(ywin, ysh, prows)

        def strip(s, carry):
            get_d, get_y = _row_groups(dwin, dsh, s), _row_groups(ywin, ysh, s)
            dw_acc = [jnp.zeros((SUB, LANES), F32) for _ in range(CK)]
            for t in range(bt // SUB):
                dyc_g = get_d(0, t)
                dys = [jnp.zeros((SUB, LANES), F32), jnp.zeros((SUB, LANES), F32)]
                for j in range(CK):
                    dys[j % 2] = dys[j % 2] + w8[s, SUB * j:SUB * (j + 1), :] * get_d(CK - 1 - j, t)
                    dw_acc[j] = dw_acc[j] + dyc_g * get_y(HALO - (CK - 1) + j, t)
                dy_out[s, SUB * t:SUB * (t + 1), :] = dys[0] + dys[1]
            for j in range(CK):
                dw_out[s, SUB * j:SUB * (j + 1), :] = dw_acc[j]
            return carry

        lax.fori_loop(0, NSTRIP, strip, 0)
        dw_ref[0:CK * SUB, :] += _from_strips(dw_out)
        dy = _from_strips(dy_out)
        a, g = pa_ref[...].astype(F32), pg_ref[...].astype(F32)
        sg = _sigmoid(g)
        da = dy * sg
        dg = dy * a * sg * (1.0 - sg)
        dpa_ref[...] = da.astype(BF)
        dpg_ref[...] = dg.astype(BF)
        dba_ref[...] += jnp.sum(da, axis=0, keepdims=True)
        dbg_ref[...] += jnp.sum(dg, axis=0, keepdims=True)

    cur = pl.BlockSpec((bt, D), lambda b, i: (b * nt + i, 0))
    nxt = pl.BlockSpec((HALO, D), lambda b, i: (jnp.minimum((b * nt + i + 1) * hb, nblk32 - 1), 0))
    prv = pl.BlockSpec((HALO, D), lambda b, i: (jnp.maximum((b * nt + i) * hb - 1, 0), 0))
    row = lambda n: pl.BlockSpec((n, D), lambda b, i: (0, 0))
    N = B * T
    return _pcall(body, name="conv_bwd", grid=(B, nt),
                  in_specs=[cur, nxt, cur, nxt, cur, prv, cur, cur, row(HALO), row(1), row(1)],
                  out_specs=[cur, cur, row(HALO * SUB), row(1), row(1), row(1), row(1), row(1)],
                  out_shape=[jax.ShapeDtypeStruct((N, D), BF)] * 2 + [jax.ShapeDtypeStruct((HALO * SUB, D), F32)]
                  + [jax.ShapeDtypeStruct((1, D), F32)] * 5,
                  scratch_shapes=[pltpu.VMEM((NSTRIP, bt + HALO, LANES), F32), pltpu.VMEM((NSTRIP, bt + HALO, LANES), F32),
                                  pltpu.VMEM((SUB - 1, NSTRIP, prows, LANES), F32),
                                  pltpu.VMEM((SUB - 1, NSTRIP, prows, LANES), F32),
                                  pltpu.VMEM((NSTRIP, HALO * SUB, LANES), F32), pltpu.VMEM((NSTRIP, bt, LANES), F32),
                                  pltpu.VMEM((NSTRIP, CK * SUB, LANES), F32)],
                  compiler_params=_params(("arbitrary", "arbitrary")))(ds, ds, yc, yc, y, y, pa, pg, w32, lng, lnb)


def _head(x, tgt, gain, bt=512):
    N = x.shape[0]
    bt = min(bt, N)

    def body(x_ref, t_ref, g_ref, dx_ref, loss_ref, dg_ref):
        @pl.when(pl.program_id(0) == 0)
        def _():
            loss_ref[...] = jnp.zeros(loss_ref.shape, F32)
            dg_ref[...] = jnp.zeros(dg_ref.shape, F32)

        xv = x_ref[...]
        gain = g_ref[...]
        err = xv * _rms_stats(xv) * gain - t_ref[...]
        loss_ref[...] += 0.5 * jnp.sum(jnp.mean(err * err, axis=-1, keepdims=True), axis=0, keepdims=True)
        dx, dgr = _rms_bwd(xv, gain, err * (1.0 / D))
        dx_ref[...] = dx
        dg_ref[...] += jnp.sum(dgr, axis=0, keepdims=True)

    blk = pl.BlockSpec((bt, D), lambda i: (i, 0))
    return _pcall(body, name="loss_head", grid=(N // bt,),
                  in_specs=[blk, blk, pl.BlockSpec((1, D), lambda i: (0, 0))],
                  out_specs=[blk, pl.BlockSpec((1, 128), lambda i: (0, 0)), pl.BlockSpec((1, D), lambda i: (0, 0))],
                  out_shape=[jax.ShapeDtypeStruct((N, D), F32), jax.ShapeDtypeStruct((1, 128), F32),
                             jax.ShapeDtypeStruct((1, D), F32)],
                  compiler_params=_params(("arbitrary",)))(x, tgt, gain)


def _local_step(x, mem, tgt, Wb, P, hooks=None):
    B, T, _ = x.shape
    N = B * T
    bq = bk = min(512, T)
    bt = min(512, T)
    x0 = x.reshape(N, D)
    mem2 = mem.reshape(B * NMEM, D)
    tgt2 = tgt.reshape(N, D)
    row = lambda v: v.reshape(1, -1)
    G = {}

    w_in = Wb['w_in_e'][0]
    w_inp = jnp.concatenate([w_in[:, 3 * FOXW + NG:], w_in[:, :3 * FOXW], w_in[:, 3 * FOXW:3 * FOXW + NG],
                             jnp.zeros((D, 128 - NG), BF)], axis=1)
    g_e = row(P['mix_norm_e'])
    z, h0 = _norm_mm("proj_z", x0, g_e, w_inp, N=D, coff=0, bn=D, out_dtype=F32, h_out=True)
    qkv = _norm_mm("proj_qkv", x0, g_e, w_inp, N=3 * FOXW, coff=2, bn=FOXW, out_dtype=BF)[0]
    fl = _norm_mm("proj_f", x0, g_e, w_inp, N=128, coff=20, bn=128, out_dtype=F32)[0]
    fbias = jnp.concatenate([P['fox_f_bias'].reshape(1, NG), jnp.zeros((1, 128 - NG), F32)], axis=1)
    cum = _fox_gate_fwd(fl, fbias, B, T)
    cum4 = cum[:, :NG].reshape(B, T, NG).transpose(0, 2, 1).reshape(B, NG, T // bk, bk)
    (b_out, lse), arrived = _fox_fwd(qkv, cum4, B, T, 0, bq, bk, carry=hooks['fwd_carry']() if hooks else None)
    if hooks:
        Wb = {**Wb, **hooks['fwd_done'](arrived)}
    lng, lnb = row(P['gmlp_ln_g']), row(P['gmlp_ln_b'])
    ws = P['gmlp_w_s'][0]
    bfull = jnp.repeat(P['gmlp_b_s'][0].T, HD, axis=1)
    a_out = _gmlp_fwd(z, lng, lnb, ws, bfull, bt)
    w_out = Wb['w_out_e'][0]
    x1 = _mm_resid("mix_out", [dict(A=b_out, Ka=FOXW, B=w_out, roff=0), dict(A=a_out, Ka=GW, B=w_out, roff=1)], x0)

    def xa_ffn_fwd(l, xin):
        qx, hq = _norm_mm(f"xa_q{l}", xin, row(P['xa_norm'][l]), Wb['xa_wq'][l], N=D, bn=D, out_dtype=BF, h_out=True)
        kv, hm = _norm_mm(f"xa_kv{l}", mem2, row(P['mem_norm'][l]), Wb['xa_wkv'][l], N=2 * D, bn=D, out_dtype=BF,
                          h_out=True)
        o = _xattn_fwd(f"xattn_fwd{l}", qx, kv, B, T, bt)
        xm = _mm_resid(f"xa_o{l}", [dict(A=o, Ka=D, B=Wb['xa_wo'][l])], xin)
        wgu = Wb['ffn_w_gu'][l]
        carry = hooks['fwd2_carry']() if hooks and l == 0 else None
        res = _fused_mm(f"ffn_gu{l}", dims='nn', M=N, N=FF, bm=min(512, N), bn=FF // 2, x=xm,
                        gain=row(P['ffn_norm'][l]),
                        groups=[[dict(A=None, Ka=D, B=wgu, coff=0)], [dict(A=None, Ka=D, B=wgu, coff=2)]],
                        epi=_epi_swiglu, outs=[BF, BF, BF], h_out=True, carry=carry, cols_outer=True)
        if carry is not None:
            res, arrived = res
            Wb.update(hooks['fwd2_done'](arrived))
        g, u, a, hf = res
        xo = _mm_resid(f"ffn_down{l}", [dict(A=a, Ka=FF, B=Wb['ffn_w_down'][l])], xm)
        return xo, dict(xin=xin, qx=qx, hq=hq, kv=kv, hm=hm, o=o, xm=xm, g=g, u=u, a=a, hf=hf)

    x3, S0 = xa_ffn_fwd(0, x1)
    w_cin = Wb['conv_w_in'][0]
    b_cin = row(P['conv_b_in'])
    pa, pg, y, hc = _fused_mm("conv_in", dims='nn', M=N, N=D, bm=min(512, N), bn=D, x=x3, gain=row(P['mix_norm_o']),
                              groups=[[dict(A=None, Ka=D, B=w_cin, coff=0)], [dict(A=None, Ka=D, B=w_cin, coff=1)]],
                              epi=_epi_glu, outs=[BF, BF, F32], rows=[(b_cin, 0), (b_cin, 1)], h_out=True)
    w32 = jnp.concatenate([P['conv_dw_w'][0], jnp.zeros((HALO - CK, D), F32)], axis=0)
    cbt = min(256, T)
    s, yc = _conv_fwd(y, w32, row(P['conv_dw_b']), row(P['conv_ln_g']), row(P['conv_ln_b']), B, T, cbt)
    x4 = _mm_resid("conv_out", [dict(A=s, Ka=D, B=Wb['conv_w_out'][0])], x3, bias=row(P['conv_b_out']))
    x6, S1 = xa_ffn_fwd(1, x4)
    dx, loss_t, dgf = _head(x6, tgt2, row(P['final_norm']))
    G['final_norm'] = dgf.reshape(D)

    def by_rows(dw):
        return dw.reshape(NCHIP, dw.shape[1] // NCHIP, dw.shape[2])

    def xa_ffn_bwd(l, S, dx):
        wgu, wdown = Wb['ffn_w_gu'][l], Wb['ffn_w_down'][l]
        dwdown = by_rows(_mm_tn(f"dw_down{l}", S['a'], dx))
        dg, du = _fused_mm(f"ffn_dgu{l}", dims='nt', M=N, N=FF, bm=min(512, N), bn=FF // 2,
                           groups=[[dict(A=dx, Ka=D, B=wdown)]], epi=_epi_swiglu_bwd, outs=[BF, BF],
                           tiles=[(S['g'], 0), (S['u'], 0)], cols_outer=True)
        dwgu = jnp.concatenate([_mm_tn(f"dw_g{l}", S['hf'], dg, parts=2), _mm_tn(f"dw_u{l}", S['hf'], du, parts=2)])
        dx, dgn = _mm_nt_rms_bwd(f"ffn_dx{l}", [dict(A=dg, Ka=FF, B=wgu, coff=0), dict(A=du, Ka=FF, B=wgu, coff=1)],
                                 S['xm'], row(P['ffn_norm'][l]), dx)
        dwo = by_rows(_mm_tn(f"dw_o{l}", S['o'], dx))
        do = _mm_nt_plain(f"xa_do{l}", dx, Wb['xa_wo'][l])
        dq, dkv = _xattn_bwd(f"xattn_bwd{l}", S['qx'], S['kv'], do, B, T, bt)
        dwq = by_rows(_mm_tn(f"dw_q{l}", S['hq'], dq))
        dwkv = _mm_tn(f"dw_kv{l}", S['hm'], dkv, parts=NCHIP)
        dmn = _fused_mm(f"xa_dmem{l}", dims='nt', M=B * NMEM, N=D, bm=min(256, B * NMEM), bn=D,
                        groups=[[dict(A=dkv, Ka=2 * D, B=Wb['xa_wkv'][l])]], epi=_epi_rms_gain_only, outs=[],
                        tiles=[(mem2, 0)], rows=[(row(P['mem_norm'][l]), 0)], reds=[(1, D)])[0]
        dx, dxn = _mm_nt_rms_bwd(f"xa_dx{l}", [dict(A=dq, Ka=D, B=Wb['xa_wq'][l])], S['xin'],
                                 row(P['xa_norm'][l]), dx)
        return dx, dict(ffn_w_down=dwdown, ffn_w_gu=dwgu, ffn_norm=dgn.reshape(D), xa_wo=dwo, xa_wq=dwq,
                        xa_wkv=dwkv, mem_norm=dmn.reshape(D), xa_norm=dxn.reshape(D))

    dx, G1 = xa_ffn_bwd(1, S1, dx)
    G['conv_w_out'] = [by_rows(_mm_tn("dw_cout", s, dx))]
    G['conv_b_out'] = _colsum("db_cout", dx)
    dsv = _mm_nt_plain("conv_ds", dx, Wb['conv_w_out'][0])
    dpa, dpg, dw32, dwb, dlng, dlnb, dba, dbg = _conv_bwd(dsv, yc, y, pa, pg, w32, row(P['conv_ln_g']),
                                                          row(P['conv_ln_b']), B, T, cbt)
    G['conv_dw_w'] = _sum_groups("conv_dw_sum", dw32)[:CK][None]
    G['conv_dw_b'], G['conv_ln_g'], G['conv_ln_b'] = dwb, dlng, dlnb
    G['conv_b_in'] = jnp.concatenate([dba, dbg], axis=1)
    G['conv_w_in'] = [jnp.concatenate([_mm_tn("dw_cin_a", hc, dpa, parts=2), _mm_tn("dw_cin_g", hc, dpg, parts=2)])]
    dx, dgo = _mm_nt_rms_bwd("conv_dx", [dict(A=dpa, Ka=D, B=w_cin, coff=0), dict(A=dpg, Ka=D, B=w_cin, coff=1)],
                             x3, row(P['mix_norm_o']), dx)
    G['mix_norm_o'] = dgo
    dx, G0 = xa_ffn_bwd(0, S0, dx)
    for k in G0:
        G[k] = [G0[k], G1[k]]
    G['w_out_e'] = [by_rows(jnp.concatenate([_mm_tn("dw_out_b", b_out, dx), _mm_tn("dw_out_a", a_out, dx)], axis=1))]
    dcat = _mm_nt_plain("mix_dcat", dx, w_out)
    (dq, dk, dv, dcum4, dcq4), arrived = _fox_bwd(qkv, cum4, b_out, lse, dcat, B, T, 0, bq, bk,
                                                  carry=hooks['bwd_carry'](G) if hooks else None)
    if hooks:
        hooks['bwd_done'](arrived)
    dz, dlg, dlb, dws, dbf = _gmlp_bwd(z, dcat, lng, lnb, ws, bfull, bt)
    G['gmlp_ln_g'], G['gmlp_ln_b'], G['gmlp_w_s'] = dlg, dlb, dws[None]
    G['gmlp_b_s'] = _group_sum("gmlp_db", dbf)[:, :NG].T[None]
    pad = jnp.zeros((N, 128 - NG), F32)
    dck = jnp.concatenate([dcum4.reshape(B, NG, T).transpose(0, 2, 1).reshape(N, NG), pad], axis=1)
    dcq = jnp.concatenate([dcq4.reshape(N, NG // 2, 128)[:, :, :2].reshape(N, NG), pad], axis=1)
    dfl, dfb = _fox_gate_bwd(fl, fbias, dcq, dck, B, T)
    G['fox_f_bias'] = dfb[:, :NG]
    dw_in = jnp.concatenate([_mm_tn("dw_in_q", h0, dq)[0], _mm_tn("dw_in_k", h0, dk)[0], _mm_tn("dw_in_v", h0, dv)[0],
                             _mm_tn("dw_in_f", h0, dfl)[0][:, :NG], _mm_tn("dw_in_z", h0, dz)[0]], axis=1)
    G['w_in_e'] = [dw_in.reshape(D, NCHIP, IN_W // NCHIP).transpose(1, 0, 2)]
    pairs = [dict(A=dz, Ka=D, B=w_inp, coff=0), dict(A=dq, Ka=FOXW, B=w_inp, coff=2),
             dict(A=dk, Ka=FOXW, B=w_inp, coff=3), dict(A=dv, Ka=FOXW, B=w_inp, coff=4),
             dict(A=dfl, Ka=128, B=w_inp, coff=20)]
    if hooks:
        dx, dge, arrived = _mm_nt_rms_bwd("mix_dx", pairs, x0, g_e, dx, carry=hooks['last_carry'](G))
        hooks['last_done'](arrived)
    else:
        dx, dge = _mm_nt_rms_bwd("mix_dx", pairs, x0, g_e, dx)
    G['mix_norm_e'] = dge
    return loss_t[0, 0], dx.reshape(B, T, D), G


COLS = 1024
ANY = pl.BlockSpec(memory_space=pl.ANY)


def _coords():
    return lax.axis_index("x"), lax.axis_index("y"), lax.axis_index("c")


def _other_chips(x, y):
    return [(1 - x, y), (x, 1 - y), (1 - x, 1 - y)]


def _own_slot(v, me):
    return lax.dynamic_update_slice(lax.empty((NCHIP,) + v.shape, v.dtype), v[None], (me,) + (0,) * v.ndim)


def _all_gather(name, shards, me):
    n = len(shards)
    bufs = [_own_slot(v, me) for v in shards]

    def body(*refs):
        out_refs, (send_sems, recv_sems) = refs[n:2 * n], refs[2 * n:]
        x, y, c = _coords()
        mine = 2 * x + y
        sib = (x, y, 1 - c)
        chips = _other_chips(x, y)

        def rcopy(a, k, chip_idx, half, to):
            blk = out_refs[a].at[chip_idx, half]
            return pltpu.make_async_remote_copy(src_ref=blk, dst_ref=blk, send_sem=send_sems.at[6 * a + k],
                                                recv_sem=recv_sems.at[6 * a + k], device_id=to, device_id_type=MESH)

        first = [rcopy(a, j, mine, c, (cx, cy, c)) for a in range(n) for j, (cx, cy) in enumerate(chips)]
        for cp in first:
            cp.start()
        passed = []
        for a in range(n):
            for j, (cx, cy) in enumerate(chips):
                kj = 2 * cx + cy
                rcopy(a, j, kj, c, sib).wait_recv()
                fwd = rcopy(a, 3 + j, kj, c, sib)
                fwd.start()
                passed.append(fwd)
        for a in range(n):
            for j, (cx, cy) in enumerate(chips):
                rcopy(a, 3 + j, 2 * cx + cy, 1 - c, sib).wait_recv()
        for cp in first + passed:
            cp.wait_send()

    return _pcall(body, name=name, in_specs=[ANY] * n, out_specs=[ANY] * n,
                  out_shape=[jax.ShapeDtypeStruct(b.shape, b.dtype) for b in bufs],
                  input_output_aliases={a: a for a in range(n)}, scratch_shapes=_sem_pairs(6 * n))(*bufs)


def _sem_pairs(n):
    return [pltpu.SemaphoreType.DMA((n,)), pltpu.SemaphoreType.DMA((n,))]


def _gather_forward(name, bufs):
    n = len(bufs)

    def body(*refs):
        out_refs, (send_sems, recv_sems) = refs[n:2 * n], refs[2 * n:]
        x, y, c = _coords()

        def cp(a, j, kj, half):
            blk = out_refs[a].at[kj, half]
            return pltpu.make_async_remote_copy(src_ref=blk, dst_ref=blk, send_sem=send_sems.at[3 * a + j],
                                                recv_sem=recv_sems.at[3 * a + j], device_id=(x, y, 1 - c),
                                                device_id_type=MESH)

        chips = [2 * cx + cy for cx, cy in _other_chips(x, y)]
        sends = [cp(a, j, kj, c) for a in range(n) for j, kj in enumerate(chips)]
        for s in sends:
            s.start()
        for a in range(n):
            for j, kj in enumerate(chips):
                cp(a, j, kj, 1 - c).wait_recv()
        for s in sends:
            s.wait_send()

    return _pcall(body, name=name, in_specs=[ANY] * n, out_specs=[ANY] * n,
                  out_shape=[jax.ShapeDtypeStruct(b.shape, b.dtype) for b in bufs],
                  input_output_aliases={a: a for a in range(n)}, scratch_shapes=_sem_pairs(3 * n))(*bufs)


def _sibling_halves(name, ps):
    n = len(ps)

    def body(*refs):
        p_refs, out_refs, (send_sems, recv_sems) = refs[:n], refs[n:2 * n], refs[2 * n:]
        x, y, c = _coords()
        cps = [pltpu.make_async_remote_copy(src_ref=p_refs[a].at[k, 1 - c], dst_ref=out_refs[a].at[k],
                                            send_sem=send_sems.at[4 * a + k], recv_sem=recv_sems.at[4 * a + k],
                                            device_id=(x, y, 1 - c), device_id_type=MESH)
               for a in range(n) for k in range(NCHIP)]
        for cp in cps:
            cp.start()
        for cp in cps:
            cp.wait()

    return _pcall(body, name=name, in_specs=[ANY] * n, out_specs=[ANY] * n,
                  out_shape=[jax.ShapeDtypeStruct((NCHIP,) + p.shape[2:], p.dtype) for p in ps],
                  scratch_shapes=_sem_pairs(NCHIP * n))(*ps)


def _chip_exchange(name, qs):
    n = len(qs)

    def body(*refs):
        q_refs, out_refs, (send_sems, recv_sems) = refs[:n], refs[n:2 * n], refs[2 * n:]
        x, y, c = _coords()
        cps = [pltpu.make_async_remote_copy(src_ref=q_refs[a].at[2 * cx + cy], dst_ref=out_refs[a].at[j],
                                            send_sem=send_sems.at[3 * a + j], recv_sem=recv_sems.at[3 * a + j],
                                            device_id=(cx, cy, c), device_id_type=MESH)
               for a in range(n) for j, (cx, cy) in enumerate(_other_chips(x, y))]
        for cp in cps:
            cp.start()
        for cp in cps:
            cp.wait()

    return _pcall(body, name=name, in_specs=[ANY] * n, out_specs=[ANY] * n,
                  out_shape=[jax.ShapeDtypeStruct((3,) + q.shape[1:], q.dtype) for q in qs],
                  scratch_shapes=_sem_pairs(3 * n))(*qs)


def _sibling_swap(name, hs):
    n = len(hs)

    def body(*refs):
        out_refs, (send_sems, recv_sems) = refs[n:2 * n], refs[2 * n:]
        x, y, c = _coords()
        sib = (x, y, 1 - c)
        sends = [pltpu.make_async_remote_copy(src_ref=out_refs[a].at[c], dst_ref=out_refs[a].at[c],
                                              send_sem=send_sems.at[a], recv_sem=recv_sems.at[a], device_id=sib,
                                              device_id_type=MESH) for a in range(n)]
        for cp in sends:
            cp.start()
        for a in range(n):
            theirs = out_refs[a].at[1 - c]
            pltpu.make_async_remote_copy(src_ref=theirs, dst_ref=theirs, send_sem=send_sems.at[a],
                                         recv_sem=recv_sems.at[a], device_id=sib, device_id_type=MESH).wait_recv()
        for cp in sends:
            cp.wait_send()

    return _pcall(body, name=name, in_specs=[ANY] * n, out_specs=[ANY] * n,
                  out_shape=[jax.ShapeDtypeStruct(h.shape, h.dtype) for h in hs],
                  input_output_aliases={a: a for a in range(n)}, scratch_shapes=_sem_pairs(n))(*hs)


ADD_BLOCK_BYTES = 2 * 1024 * 1024


def _row_block(R, C):
    if R * C * 4 <= ADD_BLOCK_BYTES:
        return R
    for br in (512, 256, 128, 64, 32, 16, 8):
        if R % br == 0 and br * C * 4 <= ADD_BLOCK_BYTES:
            return br
    return R


def _add_own_half(name, p, recv, c_arr, out_dtype):
    _, _, R, C = p.shape
    br = _row_block(R, C)

    def body(c_ref, p_ref, r_ref, o_ref):
        o_ref[...] = (p_ref[...].astype(F32) + r_ref[...].astype(F32)).astype(o_ref.dtype)

    spec = pltpu.PrefetchScalarGridSpec(
        num_scalar_prefetch=1, grid=(NCHIP, R // br),
        in_specs=[pl.BlockSpec((None, None, br, C), lambda k, r, c_ref: (k, c_ref[0], r, 0)),
                  pl.BlockSpec((None, br, C), lambda k, r, c_ref: (k, r, 0))],
        out_specs=pl.BlockSpec((None, br, C), lambda k, r, c_ref: (k, r, 0)))
    return _pcall(body, name=name, grid_spec=spec, out_shape=jax.ShapeDtypeStruct((NCHIP, R, C), out_dtype),
                  compiler_params=_params(("arbitrary", "arbitrary")))(c_arr, p, recv)


def _add_chips(name, q, recv, idx_arr):
    _, R, C = q.shape
    br = _row_block(R, C)

    def body(idx_ref, q_ref, r_ref, o_ref):
        o_ref[...] = ((q_ref[...].astype(F32) + r_ref[0].astype(F32)) + r_ref[1].astype(F32)) + r_ref[2].astype(F32)

    spec = pltpu.PrefetchScalarGridSpec(
        num_scalar_prefetch=1, grid=(R // br,),
        in_specs=[pl.BlockSpec((None, br, C), lambda r, idx: (idx[0], r, 0)),
                  pl.BlockSpec((3, br, C), lambda r, idx: (0, r, 0))],
        out_specs=pl.BlockSpec((None, br, C), lambda r, idx: (idx[1], r, 0)))
    return _pcall(body, name=name, grid_spec=spec, out_shape=jax.ShapeDtypeStruct((2, R, C), F32),
                  compiler_params=_params(("arbitrary",)))(idx_arr, q, recv)


EARLY = ['w_in_e']
LATE = [n for n in BIG if n not in EARLY]


def _adamw(name, w, g, m, v):
    shape = w.shape
    cols = shape[-1]
    rows = w.size // cols
    w2, g2, m2, v2 = (a.reshape(rows, cols) for a in (w, g, m, v))
    bt = next((b for b in (256, 128) if rows % b == 0), rows)

    def body(w_ref, g_ref, m_ref, v_ref, d_ref, nm_ref, nv_ref):
        gv = g_ref[...]
        nm = ADAM_B1 * m_ref[...] + (1.0 - ADAM_B1) * gv
        nv = ADAM_B2 * v_ref[...] + (1.0 - ADAM_B2) * (gv * gv)
        m_hat = nm / (1.0 - ADAM_B1 ** ADAM_STEP)
        v_hat = nv / (1.0 - ADAM_B2 ** ADAM_STEP)
        d_ref[...] = -ADAM_LR * (m_hat / (jnp.sqrt(v_hat) + ADAM_EPS) + ADAM_WD * w_ref[...])
        nm_ref[...] = nm
        nv_ref[...] = nv

    blk = pl.BlockSpec((bt, cols), lambda i: (i, 0))
    outs = _pcall(body, name=name, grid=(rows // bt,), in_specs=[blk] * 4, out_specs=[blk] * 3,
                  out_shape=[jax.ShapeDtypeStruct((rows, cols), F32)] * 3, compiler_params=_params(("arbitrary",)))(
        w2, g2, m2, v2)
    return [o.reshape(shape) for o in outs]


SMALL_SHARDED = ['mix_norm_o', 'conv_b_in', 'conv_dw_w', 'conv_dw_b', 'conv_ln_g', 'conv_ln_b', 'conv_b_out']
REPLICATED = [n for n in WEIGHTS if SHARD_AXIS[n] is None]
NCHIP = 4


def _halves(flat, tile_rows):
    unit = 2 * tile_rows * COLS
    total = -(-flat.size // unit) * unit
    return jnp.pad(flat, (0, total - flat.size)).reshape(2, total // (2 * COLS), COLS)


def _flat(arrays):
    return jnp.concatenate([a.reshape(-1) for a in arrays])


def _chip_block(a, axis, k):
    n = a.shape[axis] // NCHIP
    return lax.slice_in_dim(a, k * n, (k + 1) * n, axis=axis)


def _full_shape(n, shard_shape):
    s = list(shard_shape[n])
    s[SHARD_AXIS[n]] *= NCHIP
    return tuple(s)


def _unpack(flat, names, shapes):
    out, off = {}, 0
    for n in names:
        size = math.prod(shapes[n])
        out[n] = flat[off:off + size].reshape(shapes[n])
        off += size
    return out


def kernel(x, mem, mix_norm_e, w_in_e, fox_f_bias, gmlp_ln_g, gmlp_ln_b, gmlp_w_s, gmlp_b_s, w_out_e, mix_norm_o, conv_w_in, conv_b_in, conv_dw_w, conv_dw_b, conv_ln_g, conv_ln_b, conv_w_out, conv_b_out, xa_norm, mem_norm, xa_wq, xa_wkv, xa_wo, ffn_norm, ffn_w_gu, ffn_w_down, final_norm, loss_target, m_mix_norm_e, m_w_in_e, m_fox_f_bias, m_gmlp_ln_g, m_gmlp_ln_b, m_gmlp_w_s, m_gmlp_b_s, m_w_out_e, m_mix_norm_o, m_conv_w_in, m_conv_b_in, m_conv_dw_w, m_conv_dw_b, m_conv_ln_g, m_conv_ln_b, m_conv_w_out, m_conv_b_out, m_xa_norm, m_mem_norm, m_xa_wq, m_xa_wkv, m_xa_wo, m_ffn_norm, m_ffn_w_gu, m_ffn_w_down, m_final_norm, v_mix_norm_e, v_w_in_e, v_fox_f_bias, v_gmlp_ln_g, v_gmlp_ln_b, v_gmlp_w_s, v_gmlp_b_s, v_w_out_e, v_mix_norm_o, v_conv_w_in, v_conv_b_in, v_conv_dw_w, v_conv_dw_b, v_conv_ln_g, v_conv_ln_b, v_conv_w_out, v_conv_b_out, v_xa_norm, v_mem_norm, v_xa_wq, v_xa_wkv, v_xa_wo, v_ffn_norm, v_ffn_w_gu, v_ffn_w_down, v_final_norm):
    env = locals()
    w = {n: env[n] for n in WEIGHTS}
    m = {n: env["m_" + n] for n in WEIGHTS}
    v = {n: env["v_" + n] for n in WEIGHTS}
    shard_shape = {n: w[n].shape for n in WEIGHTS}
    xi, yi, ci = _coords()
    me = 2 * xi + yi
    c_arr = jnp.reshape(ci, (1,)).astype(jnp.int32)
    idx_arr = jnp.stack([me, ci]).astype(jnp.int32)

    def two_halves(a):
        return a.reshape(2, a.shape[0] // 2, a.shape[1])

    def shard(n, l):
        return two_halves(w[n][l].astype(BF))

    def matrix(n, gathered):
        rows, cols = w[n].shape[1:]
        g = gathered.reshape(NCHIP, rows, cols)
        return g.reshape(NCHIP * rows, cols) if SHARD_AXIS[n] == 1 else g.transpose(1, 0, 2).reshape(rows, NCHIP * cols)

    got = _all_gather("gather_first", [shard(n, 0) for n in EARLY] + [_halves(_flat([w[n] for n in SMALL_SHARDED]), 8)], me)
    Wb = {n: [matrix(n, g)] for n, g in zip(EARLY, got)}
    vec = got[-1].reshape(NCHIP, -1)
    parts = [_unpack(vec[k], SMALL_SHARDED, shard_shape) for k in range(NCHIP)]
    P = {n: jnp.concatenate([parts[k][n] for k in range(NCHIP)], axis=SHARD_AXIS[n]) for n in SMALL_SHARDED}
    P.update({n: w[n] for n in REPLICATED})
    first = [(n, 0) for n in LATE]
    second = [(n, 1) for n in LATE if w[n].shape[0] > 1]
    bufs = {key: _own_slot(shard(*key), me) for key in first + second}
    state = {}

    def fwd_done(arrived):
        state['layer0'] = {n: matrix(n, g) for (n, _), g in zip(first, _gather_forward("gather_forward0", arrived))}
        return {n: [g] for n, g in state['layer0'].items()}

    def fwd2_done(arrived):
        return {n: [state['layer0'][n], matrix(n, g)]
                for (n, _), g in zip(second, _gather_forward("gather_forward1", arrived))}

    def by_halves(G, names):
        return [g.reshape(NCHIP, 2, g.shape[1] // 2, g.shape[2]) for n in names for g in G[n]]

    def pair_sums(tag, ps):
        got = _sibling_halves(f"rs_sibling_halves_{tag}", ps)
        return [_add_own_half(f"rs_add_pair_{tag}{a}", p, g, c_arr, p.dtype) for a, (p, g) in enumerate(zip(ps, got))]

    def bwd_carry(G):
        state['qs'] = pair_sums("late", by_halves(G, LATE))
        return _carry_exchange(state['qs'])

    def last_carry(G):
        state['qs_in'] = pair_sums("in", by_halves(G, ['w_in_e']))
        return _carry_exchange(state['qs_in'])

    hooks = dict(fwd_carry=lambda: _carry_gather([bufs[key] for key in first]), fwd_done=fwd_done,
                 fwd2_carry=lambda: _carry_gather([bufs[key] for key in second]), fwd2_done=fwd2_done,
                 bwd_carry=bwd_carry, bwd_done=lambda arrived: state.update(got=arrived),
                 last_carry=last_carry, last_done=lambda arrived: state.update(got_in=arrived))
    loss_part, grad_x, G = _local_step(x, mem, loss_target, Wb, P, hooks)
    loss = lax.psum(loss_part, ("x", "y", "c"))

    def layers(n):
        return G[n] if isinstance(G[n], list) else ([G[n]] if G[n].ndim == 1 else [G[n][l] for l in range(G[n].shape[0])])

    rep = _flat([a for n in REPLICATED for a in layers(n)])
    quarter = -(-rep.size // (NCHIP * 2 * 8 * COLS)) * (2 * 8 * COLS)
    rep = jnp.pad(rep, (0, NCHIP * quarter - rep.size)).reshape(NCHIP, quarter)
    segs = [[_chip_block(a, SHARD_AXIS[n] - 1, k).reshape(-1) for n in SMALL_SHARDED for a in layers(n)] + [rep[k]]
            for k in range(NCHIP)]
    size = sum(piece.size for piece in segs[0])
    total = -(-size // (2 * 8 * COLS)) * (2 * 8 * COLS)
    p_small = jnp.concatenate([piece for seg in segs for piece in seg + [jnp.zeros((total - size,), F32)]])
    p_small = p_small.reshape(NCHIP, 2, total // (2 * COLS), COLS)
    qs_small = pair_sums("vectors", [p_small])
    pending = [("late", state['qs'], state['got']), ("in", state['qs_in'], state['got_in']),
               ("vectors", qs_small, _chip_exchange("rs_chip_exchange_vectors", qs_small))]
    hs = [_add_chips(f"rs_add_chips_{tag}{a}", q, g, idx_arr)
          for tag, qs, got in pending for a, (q, g) in enumerate(zip(qs, got))]
    red = _sibling_swap("rs_sibling_swap", hs)
    mine, at = {}, 0
    for n in LATE + EARLY:
        nl = shard_shape[n][0]
        mine[n] = jnp.stack([r.reshape(shard_shape[n][1:]) for r in red[at:at + nl]])
        at += nl
    red_small = red[-1].reshape(-1)
    mine.update(_unpack(red_small, SMALL_SHARDED, shard_shape))
    off = sum(math.prod(shard_shape[n]) for n in SMALL_SHARDED)
    rep_all = _all_gather("gather_replicated_grads",
                          [red_small[off:off + quarter].reshape(2, quarter // (2 * COLS), COLS)], me)[0]
    mine.update(_unpack(rep_all.reshape(-1), REPLICATED, shard_shape))

    grads, deltas, new_m, new_v = [], [], [], []
    for n in WEIGHTS:
        d, nm, nv = _adamw("adamw_" + n, w[n], mine[n], m[n], v[n])
        grads.append(mine[n])
        deltas.append(d)
        new_m.append(nm)
        new_v.append(nv)
    return (loss, grad_x, *grads, *deltas, *new_m, *new_v)
```

```python
import functools
import math

import jax
import jax.numpy as jnp
from jax import lax
from jax.experimental import pallas as pl
from jax.experimental.pallas import tpu as pltpu

F32 = jnp.float32
BF = jnp.bfloat16
MESH = pl.DeviceIdType.MESH

D = 1024
FOXW = 512
HD = 64
GW = 512
CH = 128
NG = 8
FF = 2816
NMEM = 256
XH = 4
XD = 256
CK = 31
HALO = 32
EPS = 1e-6
IN_W = 2568
IN_WP = 2688
VMEM_LIMIT = 56 * 1024 * 1024

ADAM_LR, ADAM_B1, ADAM_B2, ADAM_EPS, ADAM_WD, ADAM_STEP = 0.001, 0.9, 0.999, 1e-08, 0.01, 10

WEIGHTS = ['mix_norm_e', 'w_in_e', 'fox_f_bias', 'gmlp_ln_g', 'gmlp_ln_b', 'gmlp_w_s', 'gmlp_b_s', 'w_out_e',
           'mix_norm_o', 'conv_w_in', 'conv_b_in', 'conv_dw_w', 'conv_dw_b', 'conv_ln_g', 'conv_ln_b',
           'conv_w_out', 'conv_b_out', 'xa_norm', 'mem_norm', 'xa_wq', 'xa_wkv', 'xa_wo', 'ffn_norm',
           'ffn_w_gu', 'ffn_w_down', 'final_norm']
SHARD_AXIS = {'mix_norm_e': None, 'w_in_e': 2, 'fox_f_bias': None, 'gmlp_ln_g': None, 'gmlp_ln_b': None,
              'gmlp_w_s': None, 'gmlp_b_s': None, 'w_out_e': 1, 'mix_norm_o': 1, 'conv_w_in': 2, 'conv_b_in': 1,
              'conv_dw_w': 2, 'conv_dw_b': 1, 'conv_ln_g': 1, 'conv_ln_b': 1, 'conv_w_out': 1, 'conv_b_out': 1,
              'xa_norm': None, 'mem_norm': None, 'xa_wq': 1, 'xa_wkv': 2, 'xa_wo': 1, 'ffn_norm': None,
              'ffn_w_gu': 2, 'ffn_w_down': 1, 'final_norm': None}
BIG = ['w_in_e', 'w_out_e', 'conv_w_in', 'conv_w_out', 'xa_wq', 'xa_wkv', 'xa_wo', 'ffn_w_gu', 'ffn_w_down']


def _pcall(body, **kw):
    return pl.pallas_call(body, **kw)


def _params(sem=None, **kw):
    return pltpu.CompilerParams(dimension_semantics=sem, vmem_limit_bytes=VMEM_LIMIT, **kw)


def _dot(a, b, dims):
    dn = {'nn': (((1,), (0,)), ((), ())), 'nt': (((1,), (1,)), ((), ())), 'tn': (((0,), (0,)), ((), ()))}[dims]
    return lax.dot_general(a, b, dn, preferred_element_type=F32)


def _sigmoid(x):
    return 1.0 / (1.0 + jnp.exp(-x))


def _rms_stats(xv):
    return lax.rsqrt(jnp.mean(xv * xv, axis=-1, keepdims=True) + EPS)


def _rms_bwd(xv, gain, dh):
    r = _rms_stats(xv)
    t = dh * gain
    dx = r * t - xv * (r * r * r * jnp.mean(t * xv, axis=-1, keepdims=True))
    return dx, dh * xv * r


def _fused_mm(name, *, dims, M, N, bm, bn, groups, epi, outs, x=None, gain=None, tiles=(), rows=(),
              h_out=False, reds=(), carry=None, cols_outer=False):
    bm = min(bm, M)
    nI, nJ = M // bm, N // bn
    assert nI * bm == M and nJ * bn == N
    assert not reds or nJ == 1
    arrays, specs = [], []

    def spec(shape, index):
        return pl.BlockSpec(shape, (lambda jj, ii: index(ii, jj)) if cols_outer else index)

    def add(arr, shape, index):
        arrays.append(arr)
        specs.append(spec(shape, index))
        return len(arrays) - 1

    def first_pass(i, j):
        return (jnp.where(j == 0, i, nI - 1), 0) if cols_outer else (i, 0)

    if x is not None:
        K0 = x.shape[1]
        add(x, (bm, K0), first_pass)
        add(gain, (1, K0), lambda i, j: (0, 0))
    plan = []
    for grp in groups:
        g = []
        for p in grp:
            ai = None
            if p['A'] is not None:
                ai = add(p['A'], (bm, p['Ka']), lambda i, j, o=p.get('acoff', 0): (i, o))
            ro, co = p.get('roff', 0), p.get('coff', 0)
            if dims == 'nn':
                bi = add(p['B'], (p['Ka'], bn), lambda i, j, ro=ro, co=co: (ro, j + co))
            else:
                bi = add(p['B'], (bn, p['Ka']), lambda i, j, ro=ro, co=co: (j + ro, co))
            g.append((ai, bi))
        plan.append(g)
    tile_idx = [add(a, (bm, bn), lambda i, j, o=o: (i, j + o)) for a, o in tiles]
    row_idx = [add(a, (1, bn), lambda i, j, o=o: (0, j + o)) for a, o in rows]
    n_in = len(arrays)

    out_shape = [jax.ShapeDtypeStruct((M, N), dt) for dt in outs]
    out_specs = [spec((bm, bn), lambda i, j: (i, j)) for _ in outs]
    if h_out:
        out_shape.append(jax.ShapeDtypeStruct((M, x.shape[1]), BF))
        out_specs.append(spec((bm, x.shape[1]), first_pass))
    for shp in reds:
        out_shape.append(jax.ShapeDtypeStruct(shp, F32))
        out_specs.append(spec(shp, lambda i, j: (0, 0)))
    n_main = len(outs)
    scratch = [pltpu.VMEM((M if cols_outer else bm, x.shape[1]), BF)] if x is not None else []

    def body(*refs):
        ins, out_refs, scr = refs[:n_in], refs[n_in:n_in + len(out_shape)], refs[n_in + len(out_shape):]
        i, j = (pl.program_id(1), pl.program_id(0)) if cols_outer else (pl.program_id(0), pl.program_id(1))
        if x is not None:
            hn_rows = pl.ds(pl.multiple_of(i * bm, bm), bm) if cols_outer else slice(None)
            hn_ref = scr[0]

            @pl.when(j == 0)
            def _():
                xv = ins[0][...]
                hn = (xv * _rms_stats(xv) * ins[1][...]).astype(BF)
                hn_ref[hn_rows, :] = hn
                if h_out:
                    out_refs[n_main][...] = hn

        accs = []
        for g in plan:
            acc = None
            for ai, bi in g:
                a = hn_ref[hn_rows, :] if ai is None else ins[ai][...]
                if a.dtype != BF:
                    a = a.astype(BF)
                d = _dot(a, ins[bi][...], dims)
                acc = d if acc is None else acc + d
            accs.append(acc)
        out_vals, red_vals = epi(accs, [ins[t][...] for t in tile_idx], [ins[r][...] for r in row_idx])
        for r, v in zip(out_refs[:n_main], out_vals):
            r[...] = v.astype(r.dtype)
        if reds:
            red_refs = out_refs[n_main + (1 if h_out else 0):]

            @pl.when(i == 0)
            def _():
                for r in red_refs:
                    r[...] = jnp.zeros(r.shape, F32)

            for r, v in zip(red_refs, red_vals):
                r[...] += v

    res, arrived = _carried_call(body, name=name, grid=(nJ, nI) if cols_outer else (nI, nJ), in_specs=specs,
                                 out_specs=out_specs,
                                 out_shape=out_shape, scratch_shapes=scratch, operands=arrays, carry=carry)
    return res if carry is None else (res, arrived)


def _epi_plain(accs, tiles, rows):
    return [accs[0]], []


def _epi_resid(accs, tiles, rows):
    y = tiles[0] + accs[0]
    if rows:
        y = y + rows[0]
    return [y], []


def _epi_swiglu(accs, tiles, rows):
    g, u = accs
    return [g, u, g * _sigmoid(g) * u], []


def _epi_glu(accs, tiles, rows):
    a, g = accs[0] + rows[0], accs[1] + rows[1]
    return [a, g, a * _sigmoid(g)], []


def _epi_swiglu_bwd(accs, tiles, rows):
    da = accs[0]
    g, u = tiles[0].astype(F32), tiles[1].astype(F32)
    sg = _sigmoid(g)
    return [da * u * (sg * (1.0 + g * (1.0 - sg))), da * (g * sg)], []


def _epi_rms_bwd(accs, tiles, rows):
    dx, dgr = _rms_bwd(tiles[0], rows[0], accs[0])
    return [tiles[1] + dx], [jnp.sum(dgr, axis=0, keepdims=True)]


def _epi_rms_gain_only(accs, tiles, rows):
    _, dgr = _rms_bwd(tiles[0], rows[0], accs[0])
    return [], [jnp.sum(dgr, axis=0, keepdims=True)]


def _norm_mm(name, x, gain, W, *, N, coff=0, bn, out_dtype, bm=1024, h_out=False):
    return _fused_mm(name, dims='nn', M=x.shape[0], N=N, bm=bm, bn=bn, x=x, gain=gain,
                     groups=[[dict(A=None, Ka=x.shape[1], B=W, coff=coff)]], epi=_epi_plain, outs=[out_dtype],
                     h_out=h_out)


def _mm_resid(name, pairs, resid, bias=None, bm=1024):
    M = resid.shape[0]
    return _fused_mm(name, dims='nn', M=M, N=D, bm=bm, bn=D, groups=[pairs], epi=_epi_resid, outs=[F32],
                     tiles=[(resid, 0)], rows=[(bias, 0)] if bias is not None else [])[0]


def _mm_nt_plain(name, dy, W, bm=1024):
    return _fused_mm(name, dims='nt', M=dy.shape[0], N=W.shape[0], bm=bm, bn=W.shape[0],
                     groups=[[dict(A=dy, Ka=dy.shape[1], B=W)]], epi=_epi_plain, outs=[BF])[0]


def _mm_nt_rms_bwd(name, pairs, x, gain, dx_in, bm=256, carry=None):
    out = _fused_mm(name, dims='nt', M=x.shape[0], N=D, bm=bm, bn=D, groups=[pairs], epi=_epi_rms_bwd,
                    outs=[F32], tiles=[(x, 0), (dx_in, 0)], rows=[(gain, 0)], reds=[(1, D)], carry=carry)
    if carry is None:
        return out[0], out[1]
    return out[0][0], out[0][1], out[1]


def _mm_tn(name, A, G, bk=2048, parts=1):
    T, Ka, Kg = A.shape[0], A.shape[1], G.shape[1]
    w = Kg // parts
    bm = Ka if Ka <= 1024 else Ka // 2
    bn = w if w <= 1408 else w // 2
    bk = min(bk, T)
    per = w // bn
    nI, nJ, nK = Ka // bm, Kg // bn, T // bk

    def body(a_ref, g_ref, o_ref, acc):
        k = pl.program_id(2)

        @pl.when(k == 0)
        def _():
            acc[...] = jnp.zeros(acc.shape, F32)

        acc[...] += _dot(a_ref[...].astype(BF), g_ref[...].astype(BF), 'tn')

        @pl.when(k == nK - 1)
        def _():
            o_ref[...] = acc[...].astype(BF)

    return _pcall(body, name=name, grid=(nI, nJ, nK),
                  in_specs=[pl.BlockSpec((bk, bm), lambda i, j, k: (k, i)),
                            pl.BlockSpec((bk, bn), lambda i, j, k: (k, j))],
                  out_specs=pl.BlockSpec((None, bm, bn), lambda i, j, k: (j // per, i, j % per)),
                  out_shape=jax.ShapeDtypeStruct((parts, Ka, w), BF),
                  scratch_shapes=[pltpu.VMEM((bm, bn), F32)],
                  compiler_params=_params(("arbitrary", "arbitrary", "arbitrary")))(A, G)


def _colsum(name, a, bt=512):
    M, N = a.shape
    bt = min(bt, M)

    def body(a_ref, o_ref):
        @pl.when(pl.program_id(0) == 0)
        def _():
            o_ref[...] = jnp.zeros(o_ref.shape, F32)

        o_ref[...] += jnp.sum(a_ref[...].astype(F32), axis=0, keepdims=True)

    return _pcall(body, name=name, grid=(M // bt,), in_specs=[pl.BlockSpec((bt, N), lambda i: (i, 0))],
                  out_specs=pl.BlockSpec((1, N), lambda i: (0, 0)), out_shape=jax.ShapeDtypeStruct((1, N), F32),
                  compiler_params=_params(("arbitrary",)))(a)


def _cumsum_rows(v):
    T = v.shape[0]
    row = lax.broadcasted_iota(jnp.int32, v.shape, 0)
    s = 1
    while s < T:
        v = v + jnp.where(row >= s, pltpu.roll(v, s, 0), 0.0)
        s *= 2
    return v


def _log_sigmoid(z):
    return jnp.minimum(z, 0.0) - jnp.log(1.0 + jnp.exp(-jnp.abs(z)))


def _fox_gate_fwd(fl, fbias, B, T):
    def body(fl_ref, b_ref, o_ref):
        o_ref[...] = _cumsum_rows(_log_sigmoid(fl_ref[...] + b_ref[...]))

    return _pcall(body, name="fox_gate_fwd", grid=(B,),
                  in_specs=[pl.BlockSpec((T, 128), lambda b: (b, 0)), pl.BlockSpec((1, 128), lambda b: (0, 0))],
                  out_specs=pl.BlockSpec((T, 128), lambda b: (b, 0)),
                  out_shape=jax.ShapeDtypeStruct((B * T, 128), F32), compiler_params=_params(("arbitrary",)))(fl, fbias)


def _fox_gate_bwd(fl, fbias, dcq, dck, B, T):
    def body(fl_ref, b_ref, dcq_ref, dck_ref, dfl_ref, db_ref):
        dc = dcq_ref[...] + dck_ref[...]
        rev = jnp.sum(dc, axis=0, keepdims=True) - _cumsum_rows(dc) + dc
        dfl = rev * _sigmoid(-(fl_ref[...] + b_ref[...]))
        dfl_ref[...] = dfl

        @pl.when(pl.program_id(0) == 0)
        def _():
            db_ref[...] = jnp.zeros(db_ref.shape, F32)

        db_ref[...] += jnp.sum(dfl, axis=0, keepdims=True)

    return _pcall(body, name="fox_gate_bwd", grid=(B,),
                  in_specs=[pl.BlockSpec((T, 128), lambda b: (b, 0)), pl.BlockSpec((1, 128), lambda b: (0, 0)),
                            pl.BlockSpec((T, 128), lambda b: (b, 0)), pl.BlockSpec((T, 128), lambda b: (b, 0))],
                  out_specs=[pl.BlockSpec((T, 128), lambda b: (b, 0)), pl.BlockSpec((1, 128), lambda b: (0, 0))],
                  out_shape=[jax.ShapeDtypeStruct((B * T, 128), F32), jax.ShapeDtypeStruct((1, 128), F32)],
                  compiler_params=_params(("arbitrary",)))(fl, fbias, dcq, dck)


def _carried_call(body, *, name, grid, in_specs, out_specs, out_shape, scratch_shapes, operands, carry):
    if carry is None:
        return _pcall(body, name=name, grid=grid, in_specs=in_specs, out_specs=out_specs, out_shape=out_shape,
                      scratch_shapes=scratch_shapes, compiler_params=_params(("arbitrary",) * len(grid)))(*operands), []
    n, n_in, n_out, n_scr = len(carry['inputs']), len(in_specs), len(out_specs), len(scratch_shapes)

    def wrapped(*refs):
        ins, cin = refs[:n_in], refs[n_in:n_in + n]
        outs, cout = refs[n_in + n:n_in + n + n_out], refs[n_in + n + n_out:n_in + 2 * n + n_out]
        scr = refs[n_in + 2 * n + n_out:]
        send_sems, recv_sems = scr[n_scr:]
        ids = [pl.program_id(d) for d in range(len(grid))]
        first = functools.reduce(jnp.logical_and, [i == 0 for i in ids])
        last = functools.reduce(jnp.logical_and, [i == g - 1 for i, g in zip(ids, grid)])

        @pl.when(first)
        def _():
            for cp in carry['copies'](cin, cout, send_sems, recv_sems):
                cp.start()

        body(*ins, *outs, *scr[:n_scr])

        @pl.when(last)
        def _():
            for cp in carry['copies'](cin, cout, send_sems, recv_sems):
                cp.wait()

    aliases = {n_in + a: n_out + a for a in range(n)} if carry['in_place'] else {}
    res = _pcall(wrapped, name=name, grid=grid, in_specs=list(in_specs) + [ANY] * n,
                 out_specs=list(out_specs) + [ANY] * n, out_shape=list(out_shape) + carry['out_shape'],
                 scratch_shapes=list(scratch_shapes) + _sem_pairs(carry['nsem']), input_output_aliases=aliases,
                 compiler_params=_params(("arbitrary",) * len(grid)))(*operands, *carry['inputs'])
    return res[:n_out], res[n_out:]


def _carry_gather(bufs):
    n = len(bufs)

    def copies(in_refs, out_refs, send_sems, recv_sems):
        x, y, c = _coords()
        cps = []
        for a in range(n):
            blk = out_refs[a].at[2 * x + y, c]
            cps += [pltpu.make_async_remote_copy(src_ref=blk, dst_ref=blk, send_sem=send_sems.at[3 * a + j],
                                                 recv_sem=recv_sems.at[3 * a + j], device_id=(cx, cy, c),
                                                 device_id_type=MESH) for j, (cx, cy) in enumerate(_other_chips(x, y))]
        return cps

    return dict(inputs=bufs, out_shape=[jax.ShapeDtypeStruct(b.shape, b.dtype) for b in bufs], in_place=True,
                nsem=3 * n, copies=copies)


def _carry_exchange(qs):
    n = len(qs)

    def copies(in_refs, out_refs, send_sems, recv_sems):
        x, y, c = _coords()
        return [pltpu.make_async_remote_copy(src_ref=in_refs[a].at[2 * cx + cy], dst_ref=out_refs[a].at[j],
                                             send_sem=send_sems.at[3 * a + j], recv_sem=recv_sems.at[3 * a + j],
                                             device_id=(cx, cy, c), device_id_type=MESH)
                for a in range(n) for j, (cx, cy) in enumerate(_other_chips(x, y))]

    return dict(inputs=qs, out_shape=[jax.ShapeDtypeStruct((3,) + q.shape[1:], q.dtype) for q in qs], in_place=False,
                nsem=3 * n, copies=copies)


NEG = -1e30
QSUB = 1


def _fox_fwd(qkv, cum4, B, T, qoff, bq, bk, carry=None):
    nq, nkb = T // bq, T // bk
    N = B * T

    def body(q_ref, k_ref, v_ref, cum_ref, o_ref, lse_ref):
        hp, i = pl.program_id(1), pl.program_id(2)
        sq = bq // QSUB
        lane = lax.broadcasted_iota(jnp.int32, (sq, 128), 1)
        heads = [slice(e * HD, (e + 1) * HD) for e in range(2)]
        chains = [(e, sl, slice(r * sq, (r + 1) * sq)) for e, sl in enumerate(heads) for r in range(QSUB)]
        qs = [q_ref[rows, sl] * 0.125 for _, sl, rows in chains]

        def block(j, carry, diagonal):
            ks = pl.multiple_of(j * bk, bk)
            out = []
            for n, (e, sl, rows) in enumerate(chains):
                m, l, acc = carry[n]
                s = _dot(qs[n], k_ref[pl.ds(ks, bk), sl], 'nt') - cum_ref[0, 2 * hp + e, pl.ds(j, 1), :]
                if diagonal:
                    keep = (lax.broadcasted_iota(jnp.int32, (sq, bk), 0) + rows.start
                            >= lax.broadcasted_iota(jnp.int32, (sq, bk), 1))
                    s = jnp.where(keep, s, NEG)
                m_new = jnp.maximum(m, jnp.max(s, axis=1, keepdims=True))
                p = jnp.exp(s - m_new)
                alpha = jnp.exp(m - m_new)
                l = alpha * l + jnp.sum(p, axis=1, keepdims=True)
                acc = alpha * acc + _dot(p.astype(BF), v_ref[pl.ds(ks, bk), sl], 'nn')
                out.append((m_new, l, acc))
            return tuple(out)

        init = tuple((jnp.full((sq, 1), NEG, F32), jnp.zeros((sq, 1), F32), jnp.zeros((sq, HD), F32)) for _ in chains)
        carry = lax.fori_loop(0, i, lambda j, c: block(j, c, False), init)
        carry = block(i, carry, True)
        for r in range(QSUB):
            lse_tile = jnp.zeros((sq, 128), F32)
            for n, (e, sl, rows) in enumerate(chains):
                if rows.start == r * sq:
                    m, l, acc = carry[n]
                    o_ref[rows, sl] = (acc / l).astype(BF)
                    lse_tile = jnp.where(lane == e, m + jnp.log(l), lse_tile)
            lse_ref[r * sq:(r + 1) * sq, :] = lse_tile

    return _carried_call(body, name="fox_fwd", grid=(B, 4, nq),
                         in_specs=[pl.BlockSpec((bq, 128), lambda b, h, i: (b * nq + i, qoff + h)),
                                   pl.BlockSpec((T, 128), lambda b, h, i: (b, qoff + 4 + h)),
                                   pl.BlockSpec((T, 128), lambda b, h, i: (b, qoff + 8 + h)),
                                   pl.BlockSpec((1, NG, nkb, bk), lambda b, h, i: (b, 0, 0, 0))],
                         out_specs=[pl.BlockSpec((bq, 128), lambda b, h, i: (b * nq + i, h)),
                                    pl.BlockSpec((bq, 128), lambda b, h, i: (b * nq + i, h))],
                         out_shape=[jax.ShapeDtypeStruct((N, FOXW), BF), jax.ShapeDtypeStruct((N, FOXW), F32)],
                         scratch_shapes=[], operands=(qkv, qkv, qkv, cum4), carry=carry)


def _fox_bwd(qkv, cum4, o, lse, dcat, B, T, qoff, bq, bk, carry=None):
    nq, nkb = T // bq, T // bk
    N = B * T

    def body(q_ref, k_ref, v_ref, cum_ref, o_ref, lse_ref, do_ref, dq_ref, dk_ref, dv_ref, dcum_ref, dcq_ref,
             dq_acc, dl_ref, rs_ref):
        hp = pl.program_id(1)
        heads = [slice(e * HD, (e + 1) * HD) for e in range(2)]
        keep = lax.broadcasted_iota(jnp.int32, (bq, bk), 0) >= lax.broadcasted_iota(jnp.int32, (bq, bk), 1)
        dcq_ref[...] = jnp.zeros(dcq_ref.shape, F32)
        dq_acc[...] = jnp.zeros(dq_acc.shape, F32)
        rs_ref[...] = jnp.zeros(rs_ref.shape, F32)
        for e, sl in enumerate(heads):
            dl_ref[e] = jnp.sum(do_ref[:, sl].astype(F32) * o_ref[:, sl].astype(F32), axis=1, keepdims=True)
        for j in range(nkb):
            krows = slice(j * bk, (j + 1) * bk)

            def tile(i, carry, diagonal):
                qs = i * bq if diagonal else pl.multiple_of(i * bq, bq)
                out = []
                for e, sl in enumerate(heads):
                    dk_a, dv_a, cs = carry[e]
                    q, k = q_ref[pl.ds(qs, bq), sl], k_ref[krows, sl]
                    do = do_ref[pl.ds(qs, bq), sl]
                    s = _dot(q, k, 'nt') * 0.125 - cum_ref[0, 2 * hp + e, j:j + 1, :]
                    p = jnp.exp(s - lse_ref[pl.ds(qs, bq), e:e + 1])
                    if diagonal:
                        p = jnp.where(keep, p, 0.0)
                    dv_a = dv_a + _dot(p.astype(BF), do, 'tn')
                    ds = p * (_dot(do, v_ref[krows, sl], 'nt') - dl_ref[e, pl.ds(qs, bq), :])
                    cs = cs + jnp.sum(ds, axis=0, keepdims=True)
                    rs_ref[e, pl.ds(qs, bq), :] += jnp.sum(ds, axis=1, keepdims=True)
                    dsb = ds.astype(BF)
                    dk_a = dk_a + _dot(dsb, q, 'tn')
                    dq_acc[e, pl.ds(qs, bq), :] += _dot(dsb, k, 'nn')
                    out.append((dk_a, dv_a, cs))
                return tuple(out)

            init = tuple((jnp.zeros((bk, HD), F32), jnp.zeros((bk, HD), F32), jnp.zeros((1, bk), F32)) for _ in heads)
            carry = lax.fori_loop(j + 1, nq, lambda i, c: tile(i, c, False), tile(j, init, True))
            for e, sl in enumerate(heads):
                dk_a, dv_a, cs = carry[e]
                dk_ref[krows, sl] = (dk_a * 0.125).astype(BF)
                dv_ref[krows, sl] = dv_a.astype(BF)
                dcum_ref[0, e, j:j + 1, :] = -cs
        for e, sl in enumerate(heads):
            dq_ref[:, sl] = (dq_acc[e] * 0.125).astype(BF)
            dcq_ref[:, e:e + 1] = rs_ref[e]

    seq = lambda off: pl.BlockSpec((T, 128), lambda b, h, off=off: (b, off + h))
    return _carried_call(body, name="fox_bwd", grid=(B, 4),
                         in_specs=[seq(qoff), seq(qoff + 4), seq(qoff + 8),
                                   pl.BlockSpec((1, NG, nkb, bk), lambda b, h: (b, 0, 0, 0)),
                                   seq(0), seq(0), seq(0)],
                         out_specs=[seq(0), seq(0), seq(0),
                                    pl.BlockSpec((1, 2, nkb, bk), lambda b, h: (b, h, 0, 0)), seq(0)],
                         out_shape=[jax.ShapeDtypeStruct((N, FOXW), BF)] * 3
                         + [jax.ShapeDtypeStruct((B, NG, nkb, bk), F32), jax.ShapeDtypeStruct((N, FOXW), F32)],
                         scratch_shapes=[pltpu.VMEM((2, T, HD), F32), pltpu.VMEM((2, T, 1), F32),
                                         pltpu.VMEM((2, T, 1), F32)],
                         operands=(qkv, qkv, qkv, cum4, o, lse, dcat), carry=carry)


_GC = math.sqrt(2.0 / math.pi)
_GA = 0.044715


def _gelu(z):
    return 0.5 * z * (1.0 + jnp.tanh(_GC * (z + _GA * z * z * z)))


def _gelu_grad(z):
    t = jnp.tanh(_GC * (z + _GA * z * z * z))
    return 0.5 * (1.0 + t) + 0.5 * z * (1.0 - t * t) * (_GC * (1.0 + 3.0 * _GA * z * z))


def _gmlp_common(z, lng, lnb):
    zg = _gelu(z)
    u, vg = zg[:, :GW], zg[:, GW:]
    mu = jnp.mean(vg, axis=-1, keepdims=True)
    xc = vg - mu
    rstd = lax.rsqrt(jnp.mean(xc * xc, axis=-1, keepdims=True) + EPS)
    xhat = xc * rstd
    return u, xhat, rstd, xhat * lng + lnb


def _tril_w(ws_ref):
    tri = lax.broadcasted_iota(jnp.int32, (CH, CH), 0) >= lax.broadcasted_iota(jnp.int32, (CH, CH), 1)
    return [jnp.where(tri, ws_ref[g], 0.0).astype(BF) for g in range(NG)], tri


def _split_pair(vp):
    lane = lax.broadcasted_iota(jnp.int32, vp.shape, 1)
    zero = jnp.zeros(vp.shape, vp.dtype)
    return jnp.concatenate([jnp.where(lane < HD, vp, zero), jnp.where(lane >= HD, vp, zero)], axis=0)


def _gmlp_mix(wt, vgn_b):
    outs = []
    for p in range(NG // 2):
        wcat = jnp.concatenate([wt[2 * p], wt[2 * p + 1]], axis=1)
        outs.append(_dot(wcat, _split_pair(vgn_b[:, 128 * p:128 * (p + 1)]), 'nn'))
    return jnp.concatenate(outs, axis=1)


def _gmlp_fwd(z, lng, lnb, ws, bfull, bt):
    N = z.shape[0]

    def body(z_ref, lng_ref, lnb_ref, ws_ref, bf_ref, o_ref):
        wt, _ = _tril_w(ws_ref)
        for c in range(bt // CH):
            rows = slice(c * CH, (c + 1) * CH)
            u, _, _, vgn = _gmlp_common(z_ref[rows, :], lng_ref[...], lnb_ref[...])
            mixed = _gmlp_mix(wt, vgn.astype(BF)) + bf_ref[...]
            o_ref[rows, :] = (u * mixed).astype(BF)

    full = lambda shp: pl.BlockSpec(shp, lambda i: (0,) * len(shp))
    return _pcall(body, name="gmlp_fwd", grid=(N // bt,),
                  in_specs=[pl.BlockSpec((bt, D), lambda i: (i, 0)), full((1, GW)), full((1, GW)),
                            full((NG, CH, CH)), full((CH, GW))],
                  out_specs=pl.BlockSpec((bt, GW), lambda i: (i, 0)), out_shape=jax.ShapeDtypeStruct((N, GW), BF),
                  compiler_params=_params(("arbitrary",)))(z, lng, lnb, ws, bfull)


def _gmlp_bwd(z, dcat, lng, lnb, ws, bfull, bt):
    N = z.shape[0]

    def body(z_ref, da_ref, lng_ref, lnb_ref, ws_ref, bf_ref, dz_ref, dg_ref, db_ref, dws_ref, dbf_ref):
        @pl.when(pl.program_id(0) == 0)
        def _():
            for r in (dg_ref, db_ref, dws_ref, dbf_ref):
                r[...] = jnp.zeros(r.shape, F32)

        wt, tri = _tril_w(ws_ref)
        lane = lax.broadcasted_iota(jnp.int32, (CH, 128), 1)
        for c in range(bt // CH):
            rows = slice(c * CH, (c + 1) * CH)
            zc = z_ref[rows, :]
            u, xhat, rstd, vgn = _gmlp_common(zc, lng_ref[...], lnb_ref[...])
            vgn_b = vgn.astype(BF)
            mixed = _gmlp_mix(wt, vgn_b) + bf_ref[...]
            da = da_ref[rows, :].astype(F32)
            dmix = da * u
            du = da * mixed
            dbf_ref[...] += dmix
            dvs = []
            for p in range(NG // 2):
                cols = slice(128 * p, 128 * (p + 1))
                dmp = dmix[:, cols].astype(BF)
                dwp = _dot(_split_pair(dmp), vgn_b[:, cols], 'nt')
                dws_ref[2 * p] += jnp.where(tri, dwp[:CH], 0.0)
                dws_ref[2 * p + 1] += jnp.where(tri, dwp[CH:], 0.0)
                dvs.append(jnp.where(lane < HD, _dot(wt[2 * p], dmp, 'tn'), _dot(wt[2 * p + 1], dmp, 'tn')))
            dvgn = jnp.concatenate(dvs, axis=1)
            dg_ref[...] += jnp.sum(dvgn * xhat, axis=0, keepdims=True)
            db_ref[...] += jnp.sum(dvgn, axis=0, keepdims=True)
            dxh = dvgn * lng_ref[...]
            dvg = rstd * (dxh - jnp.mean(dxh, axis=-1, keepdims=True)
                          - xhat * jnp.mean(dxh * xhat, axis=-1, keepdims=True))
            dz_ref[rows, :] = (jnp.concatenate([du, dvg], axis=1) * _gelu_grad(zc)).astype(BF)

    full = lambda shp: pl.BlockSpec(shp, lambda i: (0,) * len(shp))
    return _pcall(body, name="gmlp_bwd", grid=(N // bt,),
                  in_specs=[pl.BlockSpec((bt, D), lambda i: (i, 0)), pl.BlockSpec((bt, GW), lambda i: (i, 1)),
                            full((1, GW)), full((1, GW)), full((NG, CH, CH)), full((CH, GW))],
                  out_specs=[pl.BlockSpec((bt, D), lambda i: (i, 0)), full((1, GW)), full((1, GW)),
                             full((NG, CH, CH)), full((CH, GW))],
                  out_shape=[jax.ShapeDtypeStruct((N, D), BF), jax.ShapeDtypeStruct((1, GW), F32),
                             jax.ShapeDtypeStruct((1, GW), F32), jax.ShapeDtypeStruct((NG, CH, CH), F32),
                             jax.ShapeDtypeStruct((CH, GW), F32)],
                  compiler_params=_params(("arbitrary",)))(z, dcat, lng, lnb, ws, bfull)


def _group_sum(name, a):
    def body(a_ref, o_ref):
        lane = lax.broadcasted_iota(jnp.int32, (CH, 128), 1)
        out = jnp.zeros((CH, 128), F32)
        for g in range(NG):
            out = jnp.where(lane == g, jnp.sum(a_ref[:, g * HD:(g + 1) * HD], axis=1, keepdims=True), out)
        o_ref[...] = out

    return _pcall(body, name=name, out_shape=jax.ShapeDtypeStruct((CH, 128), F32))(a)


def _xattn_softmax(q_h, k_h):
    s = _dot(q_h, k_h, 'nt') * (XD ** -0.5)
    p = jnp.exp(s - jnp.max(s, axis=1, keepdims=True))
    return p / jnp.sum(p, axis=1, keepdims=True)


def _xattn_fwd(name, q, kv, B, T, bq):
    nq = T // bq

    def body(q_ref, kv_ref, o_ref):
        for h in range(XH):
            cols = slice(h * XD, (h + 1) * XD)
            p = _xattn_softmax(q_ref[:, cols], kv_ref[:, cols])
            o_ref[:, cols] = _dot(p.astype(BF), kv_ref[:, D + h * XD:D + (h + 1) * XD], 'nn').astype(BF)

    return _pcall(body, name=name, grid=(B, nq),
                  in_specs=[pl.BlockSpec((bq, D), lambda b, i: (b * nq + i, 0)),
                            pl.BlockSpec((NMEM, 2 * D), lambda b, i: (b, 0))],
                  out_specs=pl.BlockSpec((bq, D), lambda b, i: (b * nq + i, 0)),
                  out_shape=jax.ShapeDtypeStruct((B * T, D), BF), compiler_params=_params(("arbitrary", "arbitrary")))(q, kv)


def _xattn_bwd(name, q, kv, do, B, T, bq):
    nq = T // bq
    sc = XD ** -0.5

    def body(q_ref, kv_ref, do_ref, dq_ref, dkv_ref):
        @pl.when(pl.program_id(1) == 0)
        def _():
            dkv_ref[...] = jnp.zeros(dkv_ref.shape, F32)

        for h in range(XH):
            cols = slice(h * XD, (h + 1) * XD)
            vcols = slice(D + h * XD, D + (h + 1) * XD)
            qh, kh, doh = q_ref[:, cols], kv_ref[:, cols], do_ref[:, cols]
            p = _xattn_softmax(qh, kh)
            dp = _dot(doh, kv_ref[:, vcols], 'nt')
            ds = p * (dp - jnp.sum(p * dp, axis=1, keepdims=True))
            dsb = ds.astype(BF)
            dq_ref[:, cols] = (_dot(dsb, kh, 'nn') * sc).astype(BF)
            dkv_ref[:, cols] += _dot(dsb, qh, 'tn') * sc
            dkv_ref[:, vcols] += _dot(p.astype(BF), doh, 'tn')

    blk = pl.BlockSpec((bq, D), lambda b, i: (b * nq + i, 0))
    return _pcall(body, name=name, grid=(B, nq),
                  in_specs=[blk, pl.BlockSpec((NMEM, 2 * D), lambda b, i: (b, 0)), blk],
                  out_specs=[blk, pl.BlockSpec((NMEM, 2 * D), lambda b, i: (b, 0))],
                  out_shape=[jax.ShapeDtypeStruct((B * T, D), BF), jax.ShapeDtypeStruct((B * NMEM, 2 * D), F32)],
                  compiler_params=_params(("arbitrary", "arbitrary")))(q, kv, do)


def _ln_stats(v):
    mu = jnp.mean(v, axis=-1, keepdims=True)
    xc = v - mu
    rstd = lax.rsqrt(jnp.mean(xc * xc, axis=-1, keepdims=True) + EPS)
    return xc * rstd, rstd


SUB = 8


LANES = 128
NSTRIP = D // LANES


def _fill_window(win, parts):
    for s in range(NSTRIP):
        for r0, val in parts:
            win[s, r0:r0 + val.shape[0], :] = val[:, s * LANES:(s + 1) * LANES]


def _fill_phases(win, sh, rows):
    for b in range(1, SUB):
        for s in range(NSTRIP):
            sh[b - 1, s] = win[s, b:b + rows, :]


def _fill_taps(w8, w_ref):
    for s in range(NSTRIP):
        for j in range(CK):
            w8[s, SUB * j:SUB * (j + 1), :] = jnp.broadcast_to(w_ref[j:j + 1, s * LANES:(s + 1) * LANES], (SUB, LANES))


def _row_groups(win, sh, s):
    cache = {}

    def get(o, t):
        a, b = divmod(o, SUB)
        key = (b, t + a)
        if key not in cache:
            rows = slice(SUB * (t + a), SUB * (t + a + 1))
            cache[key] = win[s, rows, :] if b == 0 else sh[b - 1, s, rows, :]
        return cache[key]

    return get


def _from_strips(ref):
    return jnp.concatenate([ref[s] for s in range(NSTRIP)], axis=1)


def _sum_groups(name, a):
    R, C = a.shape[0] // SUB, a.shape[1]

    def body(a_ref, o_ref):
        o_ref[...] = jnp.sum(a_ref[...].reshape(R, SUB, C), axis=1)

    return _pcall(body, name=name, out_shape=jax.ShapeDtypeStruct((R, C), F32))(a)


def _conv_fwd(y, w32, wb, lng, lnb, B, T, bt):
    nt = T // bt
    hb = bt // HALO
    prows = bt + HALO - SUB

    def body(y_ref, yp_ref, w_ref, wb_ref, lng_ref, lnb_ref, s_ref, yc_ref, win, sh, w8, out):
        i = pl.program_id(1)
        _fill_window(win, [(0, jnp.where(i > 0, yp_ref[...], 0.0)), (HALO, y_ref[...])])
        _fill_phases(win, sh, prows)

        @pl.when((pl.program_id(0) == 0) & (i == 0))
        def _():
            _fill_taps(w8, w_ref)

        def strip(s, carry):
            get = _row_groups(win, sh, s)
            for t in range(bt // SUB):
                accs = [jnp.zeros((SUB, LANES), F32), jnp.zeros((SUB, LANES), F32)]
                for j in range(CK):
                    accs[j % 2] = accs[j % 2] + w8[s, SUB * j:SUB * (j + 1), :] * get(HALO - (CK - 1) + j, t)
                out[s, SUB * t:SUB * (t + 1), :] = accs[0] + accs[1]
            return carry

        lax.fori_loop(0, NSTRIP, strip, 0)
        acc = _from_strips(out) + wb_ref[...]
        yc_ref[...] = acc
        xhat, _ = _ln_stats(acc)
        ln = xhat * lng_ref[...] + lnb_ref[...]
        s_ref[...] = (ln * _sigmoid(ln)).astype(BF)

    row = lambda n: pl.BlockSpec((n, D), lambda b, i: (0, 0))
    cur = pl.BlockSpec((bt, D), lambda b, i: (b * nt + i, 0))
    return _pcall(body, name="conv_fwd", grid=(B, nt),
                  in_specs=[cur, pl.BlockSpec((HALO, D), lambda b, i: (jnp.maximum((b * nt + i) * hb - 1, 0), 0)),
                            row(HALO), row(1), row(1), row(1)],
                  out_specs=[cur, cur],
                  out_shape=[jax.ShapeDtypeStruct((B * T, D), BF), jax.ShapeDtypeStruct((B * T, D), F32)],
                  scratch_shapes=[pltpu.VMEM((NSTRIP, bt + HALO, LANES), F32),
                                  pltpu.VMEM((SUB - 1, NSTRIP, prows, LANES), F32),
                                  pltpu.VMEM((NSTRIP, HALO * SUB, LANES), F32), pltpu.VMEM((NSTRIP, bt, LANES), F32)],
                  compiler_params=_params(("arbitrary", "arbitrary")))(y, y, w32, wb, lng, lnb)


def _conv_bwd(ds, yc, y, pa, pg, w32, lng, lnb, B, T, bt):
    nt = T // bt
    hb = bt // HALO
    nblk32 = B * T // HALO

    def ln_bwd(dsv, ycv, lng, lnb):
        xhat, rstd = _ln_stats(ycv)
        ln = xhat * lng + lnb
        sg = _sigmoid(ln)
        dln = dsv * (sg * (1.0 + ln * (1.0 - sg)))
        dxh = dln * lng
        dyc = rstd * (dxh - jnp.mean(dxh, axis=-1, keepdims=True)
                      - xhat * jnp.mean(dxh * xhat, axis=-1, keepdims=True))
        return dyc, dln, xhat

    prows = bt + HALO - SUB

    def body(ds_ref, dsn_ref, yc_ref, ycn_ref, y_ref, yp_ref, pa_ref, pg_ref, w_ref, lng_ref, lnb_ref,
             dpa_ref, dpg_ref, dw_ref, dwb_ref, dlng_ref, dlnb_ref, dba_ref, dbg_ref,
             dwin, ywin, dsh, ysh, w8, dy_out, dw_out):
        i = pl.program_id(1)

        @pl.when((pl.program_id(0) == 0) & (i == 0))
        def _():
            for r in (dw_ref, dwb_ref, dlng_ref, dlnb_ref, dba_ref, dbg_ref):
                r[...] = jnp.zeros(r.shape, F32)
            _fill_taps(w8, w_ref)

        lng, lnb = lng_ref[...], lnb_ref[...]
        dyc, dln, xhat = ln_bwd(ds_ref[...].astype(F32), yc_ref[...], lng, lnb)
        dycn, _, _ = ln_bwd(dsn_ref[...].astype(F32), ycn_ref[...], lng, lnb)
        _fill_window(dwin, [(0, dyc), (bt, jnp.where(i < nt - 1, dycn, 0.0))])
        _fill_window(ywin, [(0, jnp.where(i > 0, yp_ref[...], 0.0)), (HALO, y_ref[...])])
        dlng_ref[...] += jnp.sum(dln * xhat, axis=0, keepdims=True)
        dlnb_ref[...] += jnp.sum(dln, axis=0, keepdims=True)
        dwb_ref[...] += jnp.sum(dyc, axis=0, keepdims=True)
        _fill_phases(dwin, dsh, prows)
        _fill_phases(ywin, ysh, prows)

        def strip(s, carry):
            get_d, get_y = _row_groups(dwin, dsh, s), _row_groups(ywin, ysh, s)
            dw_acc = [jnp.zeros((SUB, LANES), F32) for _ in range(CK)]
            for t in range(bt // SUB):
                dyc_g = get_d(0, t)
                dys = [jnp.zeros((SUB, LANES), F32), jnp.zeros((SUB, LANES), F32)]
                for j in range(CK):
                    dys[j % 2] = dys[j % 2] + w8[s, SUB * j:SUB * (j + 1), :] * get_d(CK - 1 - j, t)
                    dw_acc[j] = dw_acc[j] + dyc_g * get_y(HALO - (CK - 1) + j, t)
                dy_out[s, SUB * t:SUB * (t + 1), :] = dys[0] + dys[1]
            for j in range(CK):
                dw_out[s, SUB * j:SUB * (j + 1), :] = dw_acc[j]
            return carry

        lax.fori_loop(0, NSTRIP, strip, 0)
        dw_ref[0:CK * SUB, :] += _from_strips(dw_out)
        dy = _from_strips(dy_out)
        a, g = pa_ref[...].astype(F32), pg_ref[...].astype(F32)
        sg = _sigmoid(g)
        da = dy * sg
        dg = dy * a * sg * (1.0 - sg)
        dpa_ref[...] = da.astype(BF)
        dpg_ref[...] = dg.astype(BF)
        dba_ref[...] += jnp.sum(da, axis=0, keepdims=True)
        dbg_ref[...] += jnp.sum(dg, axis=0, keepdims=True)

    cur = pl.BlockSpec((bt, D), lambda b, i: (b * nt + i, 0))
    nxt = pl.BlockSpec((HALO, D), lambda b, i: (jnp.minimum((b * nt + i + 1) * hb, nblk32 - 1), 0))
    prv = pl.BlockSpec((HALO, D), lambda b, i: (jnp.maximum((b * nt + i) * hb - 1, 0), 0))
    row = lambda n: pl.BlockSpec((n, D), lambda b, i: (0, 0))
    N = B * T
    return _pcall(body, name="conv_bwd", grid=(B, nt),
                  in_specs=[cur, nxt, cur, nxt, cur, prv, cur, cur, row(HALO), row(1), row(1)],
                  out_specs=[cur, cur, row(HALO * SUB), row(1), row(1), row(1), row(1), row(1)],
                  out_shape=[jax.ShapeDtypeStruct((N, D), BF)] * 2 + [jax.ShapeDtypeStruct((HALO * SUB, D), F32)]
                  + [jax.ShapeDtypeStruct((1, D), F32)] * 5,
                  scratch_shapes=[pltpu.VMEM((NSTRIP, bt + HALO, LANES), F32), pltpu.VMEM((NSTRIP, bt + HALO, LANES), F32),
                                  pltpu.VMEM((SUB - 1, NSTRIP, prows, LANES), F32),
                                  pltpu.VMEM((SUB - 1, NSTRIP, prows, LANES), F32),
                                  pltpu.VMEM((NSTRIP, HALO * SUB, LANES), F32), pltpu.VMEM((NSTRIP, bt, LANES), F32),
                                  pltpu.VMEM((NSTRIP, CK * SUB, LANES), F32)],
                  compiler_params=_params(("arbitrary", "arbitrary")))(ds, ds, yc, yc, y, y, pa, pg, w32, lng, lnb)


def _head(x, tgt, gain, bt=512):
    N = x.shape[0]
    bt = min(bt, N)

    def body(x_ref, t_ref, g_ref, dx_ref, loss_ref, dg_ref):
        @pl.when(pl.program_id(0) == 0)
        def _():
            loss_ref[...] = jnp.zeros(loss_ref.shape, F32)
            dg_ref[...] = jnp.zeros(dg_ref.shape, F32)

        xv = x_ref[...]
        gain = g_ref[...]
        err = xv * _rms_stats(xv) * gain - t_ref[...]
        loss_ref[...] += 0.5 * jnp.sum(jnp.mean(err * err, axis=-1, keepdims=True), axis=0, keepdims=True)
        dx, dgr = _rms_bwd(xv, gain, err * (1.0 / D))
        dx_ref[...] = dx
        dg_ref[...] += jnp.sum(dgr, axis=0, keepdims=True)

    blk = pl.BlockSpec((bt, D), lambda i: (i, 0))
    return _pcall(body, name="loss_head", grid=(N // bt,),
                  in_specs=[blk, blk, pl.BlockSpec((1, D), lambda i: (0, 0))],
                  out_specs=[blk, pl.BlockSpec((1, 128), lambda i: (0, 0)), pl.BlockSpec((1, D), lambda i: (0, 0))],
                  out_shape=[jax.ShapeDtypeStruct((N, D), F32), jax.ShapeDtypeStruct((1, 128), F32),
                             jax.ShapeDtypeStruct((1, D), F32)],
                  compiler_params=_params(("arbitrary",)))(x, tgt, gain)


def _local_step(x, mem, tgt, Wb, P, hooks=None):
    B, T, _ = x.shape
    N = B * T
    bq = bk = min(512, T)
    bt = min(512, T)
    x0 = x.reshape(N, D)
    mem2 = mem.reshape(B * NMEM, D)
    tgt2 = tgt.reshape(N, D)
    row = lambda v: v.reshape(1, -1)
    G = {}

    w_in = Wb['w_in_e'][0]
    w_inp = jnp.concatenate([w_in[:, 3 * FOXW + NG:], w_in[:, :3 * FOXW], w_in[:, 3 * FOXW:3 * FOXW + NG],
                             jnp.zeros((D, 128 - NG), BF)], axis=1)
    g_e = row(P['mix_norm_e'])
    z, h0 = _norm_mm("proj_z", x0, g_e, w_inp, N=D, coff=0, bn=D, out_dtype=F32, h_out=True)
    qkv = _norm_mm("proj_qkv", x0, g_e, w_inp, N=3 * FOXW, coff=2, bn=FOXW, out_dtype=BF)[0]
    fl = _norm_mm("proj_f", x0, g_e, w_inp, N=128, coff=20, bn=128, out_dtype=F32)[0]
    fbias = jnp.concatenate([P['fox_f_bias'].reshape(1, NG), jnp.zeros((1, 128 - NG), F32)], axis=1)
    cum = _fox_gate_fwd(fl, fbias, B, T)
    cum4 = cum[:, :NG].reshape(B, T, NG).transpose(0, 2, 1).reshape(B, NG, T // bk, bk)
    (b_out, lse), arrived = _fox_fwd(qkv, cum4, B, T, 0, bq, bk, carry=hooks['fwd_carry']() if hooks else None)
    if hooks:
        Wb = {**Wb, **hooks['fwd_done'](arrived)}
    lng, lnb = row(P['gmlp_ln_g']), row(P['gmlp_ln_b'])
    ws = P['gmlp_w_s'][0]
    bfull = jnp.repeat(P['gmlp_b_s'][0].T, HD, axis=1)
    a_out = _gmlp_fwd(z, lng, lnb, ws, bfull, bt)
    w_out = Wb['w_out_e'][0]
    x1 = _mm_resid("mix_out", [dict(A=b_out, Ka=FOXW, B=w_out, roff=0), dict(A=a_out, Ka=GW, B=w_out, roff=1)], x0)

    def xa_ffn_fwd(l, xin):
        qx, hq = _norm_mm(f"xa_q{l}", xin, row(P['xa_norm'][l]), Wb['xa_wq'][l], N=D, bn=D, out_dtype=BF, h_out=True)
        kv, hm = _norm_mm(f"xa_kv{l}", mem2, row(P['mem_norm'][l]), Wb['xa_wkv'][l], N=2 * D, bn=D, out_dtype=BF,
                          h_out=True)
        o = _xattn_fwd(f"xattn_fwd{l}", qx, kv, B, T, bt)
        xm = _mm_resid(f"xa_o{l}", [dict(A=o, Ka=D, B=Wb['xa_wo'][l])], xin)
        wgu = Wb['ffn_w_gu'][l]
        carry = hooks['fwd2_carry']() if hooks and l == 0 else None
        res = _fused_mm(f"ffn_gu{l}", dims='nn', M=N, N=FF, bm=min(512, N), bn=FF // 2, x=xm,
                        gain=row(P['ffn_norm'][l]),
                        groups=[[dict(A=None, Ka=D, B=wgu, coff=0)], [dict(A=None, Ka=D, B=wgu, coff=2)]],
                        epi=_epi_swiglu, outs=[BF, BF, BF], h_out=True, carry=carry, cols_outer=True)
        if carry is not None:
            res, arrived = res
            Wb.update(hooks['fwd2_done'](arrived))
        g, u, a, hf = res
        xo = _mm_resid(f"ffn_down{l}", [dict(A=a, Ka=FF, B=Wb['ffn_w_down'][l])], xm)
        return xo, dict(xin=xin, qx=qx, hq=hq, kv=kv, hm=hm, o=o, xm=xm, g=g, u=u, a=a, hf=hf)

    x3, S0 = xa_ffn_fwd(0, x1)
    w_cin = Wb['conv_w_in'][0]
    b_cin = row(P['conv_b_in'])
    pa, pg, y, hc = _fused_mm("conv_in", dims='nn', M=N, N=D, bm=min(512, N), bn=D, x=x3, gain=row(P['mix_norm_o']),
                              groups=[[dict(A=None, Ka=D, B=w_cin, coff=0)], [dict(A=None, Ka=D, B=w_cin, coff=1)]],
                              epi=_epi_glu, outs=[BF, BF, F32], rows=[(b_cin, 0), (b_cin, 1)], h_out=True)
    w32 = jnp.concatenate([P['conv_dw_w'][0], jnp.zeros((HALO - CK, D), F32)], axis=0)
    cbt = min(256, T)
    s, yc = _conv_fwd(y, w32, row(P['conv_dw_b']), row(P['conv_ln_g']), row(P['conv_ln_b']), B, T, cbt)
    x4 = _mm_resid("conv_out", [dict(A=s, Ka=D, B=Wb['conv_w_out'][0])], x3, bias=row(P['conv_b_out']))
    x6, S1 = xa_ffn_fwd(1, x4)
    dx, loss_t, dgf = _head(x6, tgt2, row(P['final_norm']))
    G['final_norm'] = dgf.reshape(D)

    def by_rows(dw):
        return dw.reshape(NCHIP, dw.shape[1] // NCHIP, dw.shape[2])

    def xa_ffn_bwd(l, S, dx):
        wgu, wdown = Wb['ffn_w_gu'][l], Wb['ffn_w_down'][l]
        dwdown = by_rows(_mm_tn(f"dw_down{l}", S['a'], dx))
        dg, du = _fused_mm(f"ffn_dgu{l}", dims='nt', M=N, N=FF, bm=min(512, N), bn=FF // 2,
                           groups=[[dict(A=dx, Ka=D, B=wdown)]], epi=_epi_swiglu_bwd, outs=[BF, BF],
                           tiles=[(S['g'], 0), (S['u'], 0)], cols_outer=True)
        dwgu = jnp.concatenate([_mm_tn(f"dw_g{l}", S['hf'], dg, parts=2), _mm_tn(f"dw_u{l}", S['hf'], du, parts=2)])
        dx, dgn = _mm_nt_rms_bwd(f"ffn_dx{l}", [dict(A=dg, Ka=FF, B=wgu, coff=0), dict(A=du, Ka=FF, B=wgu, coff=1)],
                                 S['xm'], row(P['ffn_norm'][l]), dx)
        dwo = by_rows(_mm_tn(f"dw_o{l}", S['o'], dx))
        do = _mm_nt_plain(f"xa_do{l}", dx, Wb['xa_wo'][l])
        dq, dkv = _xattn_bwd(f"xattn_bwd{l}", S['qx'], S['kv'], do, B, T, bt)
        dwq = by_rows(_mm_tn(f"dw_q{l}", S['hq'], dq))
        dwkv = _mm_tn(f"dw_kv{l}", S['hm'], dkv, parts=NCHIP)
        dmn = _fused_mm(f"xa_dmem{l}", dims='nt', M=B * NMEM, N=D, bm=min(256, B * NMEM), bn=D,
                        groups=[[dict(A=dkv, Ka=2 * D, B=Wb['xa_wkv'][l])]], epi=_epi_rms_gain_only, outs=[],
                        tiles=[(mem2, 0)], rows=[(row(P['mem_norm'][l]), 0)], reds=[(1, D)])[0]
        dx, dxn = _mm_nt_rms_bwd(f"xa_dx{l}", [dict(A=dq, Ka=D, B=Wb['xa_wq'][l])], S['xin'],
                                 row(P['xa_norm'][l]), dx, bm=512)
        return dx, dict(ffn_w_down=dwdown, ffn_w_gu=dwgu, ffn_norm=dgn.reshape(D), xa_wo=dwo, xa_wq=dwq,
                        xa_wkv=dwkv, mem_norm=dmn.reshape(D), xa_norm=dxn.reshape(D))

    dx, G1 = xa_ffn_bwd(1, S1, dx)
    G['conv_w_out'] = [by_rows(_mm_tn("dw_cout", s, dx))]
    G['conv_b_out'] = _colsum("db_cout", dx)
    dsv = _mm_nt_plain("conv_ds", dx, Wb['conv_w_out'][0])
    dpa, dpg, dw32, dwb, dlng, dlnb, dba, dbg = _conv_bwd(dsv, yc, y, pa, pg, w32, row(P['conv_ln_g']),
                                                          row(P['conv_ln_b']), B, T, cbt)
    G['conv_dw_w'] = _sum_groups("conv_dw_sum", dw32)[:CK][None]
    G['conv_dw_b'], G['conv_ln_g'], G['conv_ln_b'] = dwb, dlng, dlnb
    G['conv_b_in'] = jnp.concatenate([dba, dbg], axis=1)
    G['conv_w_in'] = [jnp.concatenate([_mm_tn("dw_cin_a", hc, dpa, parts=2), _mm_tn("dw_cin_g", hc, dpg, parts=2)])]
    dx, dgo = _mm_nt_rms_bwd("conv_dx", [dict(A=dpa, Ka=D, B=w_cin, coff=0), dict(A=dpg, Ka=D, B=w_cin, coff=1)],
                             x3, row(P['mix_norm_o']), dx, bm=512)
    G['mix_norm_o'] = dgo
    dx, G0 = xa_ffn_bwd(0, S0, dx)
    for k in G0:
        G[k] = [G0[k], G1[k]]
    G['w_out_e'] = [by_rows(jnp.concatenate([_mm_tn("dw_out_b", b_out, dx), _mm_tn("dw_out_a", a_out, dx)], axis=1))]
    dcat = _mm_nt_plain("mix_dcat", dx, w_out)
    (dq, dk, dv, dcum4, dcq4), arrived = _fox_bwd(qkv, cum4, b_out, lse, dcat, B, T, 0, bq, bk,
                                                  carry=hooks['bwd_carry'](G) if hooks else None)
    if hooks:
        hooks['bwd_done'](arrived)
    dz, dlg, dlb, dws, dbf = _gmlp_bwd(z, dcat, lng, lnb, ws, bfull, bt)
    G['gmlp_ln_g'], G['gmlp_ln_b'], G['gmlp_w_s'] = dlg, dlb, dws[None]
    G['gmlp_b_s'] = _group_sum("gmlp_db", dbf)[:, :NG].T[None]
    pad = jnp.zeros((N, 128 - NG), F32)
    dck = jnp.concatenate([dcum4.reshape(B, NG, T).transpose(0, 2, 1).reshape(N, NG), pad], axis=1)
    dcq = jnp.concatenate([dcq4.reshape(N, NG // 2, 128)[:, :, :2].reshape(N, NG), pad], axis=1)
    dfl, dfb = _fox_gate_bwd(fl, fbias, dcq, dck, B, T)
    G['fox_f_bias'] = dfb[:, :NG]
    dw_in = jnp.concatenate([_mm_tn("dw_in_q", h0, dq)[0], _mm_tn("dw_in_k", h0, dk)[0], _mm_tn("dw_in_v", h0, dv)[0],
                             _mm_tn("dw_in_f", h0, dfl)[0][:, :NG], _mm_tn("dw_in_z", h0, dz)[0]], axis=1)
    G['w_in_e'] = [dw_in.reshape(D, NCHIP, IN_W // NCHIP).transpose(1, 0, 2)]
    pairs = [dict(A=dz, Ka=D, B=w_inp, coff=0), dict(A=dq, Ka=FOXW, B=w_inp, coff=2),
             dict(A=dk, Ka=FOXW, B=w_inp, coff=3), dict(A=dv, Ka=FOXW, B=w_inp, coff=4),
             dict(A=dfl, Ka=128, B=w_inp, coff=20)]
    if hooks:
        dx, dge, arrived = _mm_nt_rms_bwd("mix_dx", pairs, x0, g_e, dx, bm=512, carry=hooks['last_carry'](G))
        hooks['last_done'](arrived)
    else:
        dx, dge = _mm_nt_rms_bwd("mix_dx", pairs, x0, g_e, dx, bm=512)
    G['mix_norm_e'] = dge
    return loss_t[0, 0], dx.reshape(B, T, D), G


COLS = 1024
ANY = pl.BlockSpec(memory_space=pl.ANY)


def _coords():
    return lax.axis_index("x"), lax.axis_index("y"), lax.axis_index("c")


def _other_chips(x, y):
    return [(1 - x, y), (x, 1 - y), (1 - x, 1 - y)]


def _own_slot(v, me):
    return lax.dynamic_update_slice(lax.empty((NCHIP,) + v.shape, v.dtype), v[None], (me,) + (0,) * v.ndim)


def _all_gather(name, shards, me):
    n = len(shards)
    bufs = [_own_slot(v, me) for v in shards]

    def body(*refs):
        out_refs, (send_sems, recv_sems) = refs[n:2 * n], refs[2 * n:]
        x, y, c = _coords()
        mine = 2 * x + y
        sib = (x, y, 1 - c)
        chips = _other_chips(x, y)

        def rcopy(a, k, chip_idx, half, to):
            blk = out_refs[a].at[chip_idx, half]
            return pltpu.make_async_remote_copy(src_ref=blk, dst_ref=blk, send_sem=send_sems.at[6 * a + k],
                                                recv_sem=recv_sems.at[6 * a + k], device_id=to, device_id_type=MESH)

        first = [rcopy(a, j, mine, c, (cx, cy, c)) for a in range(n) for j, (cx, cy) in enumerate(chips)]
        for cp in first:
            cp.start()
        passed = []
        for a in range(n):
            for j, (cx, cy) in enumerate(chips):
                kj = 2 * cx + cy
                rcopy(a, j, kj, c, sib).wait_recv()
                fwd = rcopy(a, 3 + j, kj, c, sib)
                fwd.start()
                passed.append(fwd)
        for a in range(n):
            for j, (cx, cy) in enumerate(chips):
                rcopy(a, 3 + j, 2 * cx + cy, 1 - c, sib).wait_recv()
        for cp in first + passed:
            cp.wait_send()

    return _pcall(body, name=name, in_specs=[ANY] * n, out_specs=[ANY] * n,
                  out_shape=[jax.ShapeDtypeStruct(b.shape, b.dtype) for b in bufs],
                  input_output_aliases={a: a for a in range(n)}, scratch_shapes=_sem_pairs(6 * n))(*bufs)


def _sem_pairs(n):
    return [pltpu.SemaphoreType.DMA((n,)), pltpu.SemaphoreType.DMA((n,))]


def _gather_forward(name, bufs):
    n = len(bufs)

    def body(*refs):
        out_refs, (send_sems, recv_sems) = refs[n:2 * n], refs[2 * n:]
        x, y, c = _coords()

        def cp(a, j, kj, half):
            blk = out_refs[a].at[kj, half]
            return pltpu.make_async_remote_copy(src_ref=blk, dst_ref=blk, send_sem=send_sems.at[3 * a + j],
                                                recv_sem=recv_sems.at[3 * a + j], device_id=(x, y, 1 - c),
                                                device_id_type=MESH)

        chips = [2 * cx + cy for cx, cy in _other_chips(x, y)]
        sends = [cp(a, j, kj, c) for a in range(n) for j, kj in enumerate(chips)]
        for s in sends:
            s.start()
        for a in range(n):
            for j, kj in enumerate(chips):
                cp(a, j, kj, 1 - c).wait_recv()
        for s in sends:
            s.wait_send()

    return _pcall(body, name=name, in_specs=[ANY] * n, out_specs=[ANY] * n,
                  out_shape=[jax.ShapeDtypeStruct(b.shape, b.dtype) for b in bufs],
                  input_output_aliases={a: a for a in range(n)}, scratch_shapes=_sem_pairs(3 * n))(*bufs)


def _sibling_halves(name, ps):
    n = len(ps)

    def body(*refs):
        p_refs, out_refs, (send_sems, recv_sems) = refs[:n], refs[n:2 * n], refs[2 * n:]
        x, y, c = _coords()
        cps = [pltpu.make_async_remote_copy(src_ref=p_refs[a].at[k, 1 - c], dst_ref=out_refs[a].at[k],
                                            send_sem=send_sems.at[4 * a + k], recv_sem=recv_sems.at[4 * a + k],
                                            device_id=(x, y, 1 - c), device_id_type=MESH)
               for a in range(n) for k in range(NCHIP)]
        for cp in cps:
            cp.start()
        for cp in cps:
            cp.wait()

    return _pcall(body, name=name, in_specs=[ANY] * n, out_specs=[ANY] * n,
                  out_shape=[jax.ShapeDtypeStruct((NCHIP,) + p.shape[2:], p.dtype) for p in ps],
                  scratch_shapes=_sem_pairs(NCHIP * n))(*ps)


def _chip_exchange(name, qs):
    n = len(qs)

    def body(*refs):
        q_refs, out_refs, (send_sems, recv_sems) = refs[:n], refs[n:2 * n], refs[2 * n:]
        x, y, c = _coords()
        cps = [pltpu.make_async_remote_copy(src_ref=q_refs[a].at[2 * cx + cy], dst_ref=out_refs[a].at[j],
                                            send_sem=send_sems.at[3 * a + j], recv_sem=recv_sems.at[3 * a + j],
                                            device_id=(cx, cy, c), device_id_type=MESH)
               for a in range(n) for j, (cx, cy) in enumerate(_other_chips(x, y))]
        for cp in cps:
            cp.start()
        for cp in cps:
            cp.wait()

    return _pcall(body, name=name, in_specs=[ANY] * n, out_specs=[ANY] * n,
                  out_shape=[jax.ShapeDtypeStruct((3,) + q.shape[1:], q.dtype) for q in qs],
                  scratch_shapes=_sem_pairs(3 * n))(*qs)


def _sibling_swap(name, hs):
    n = len(hs)

    def body(*refs):
        out_refs, (send_sems, recv_sems) = refs[n:2 * n], refs[2 * n:]
        x, y, c = _coords()
        sib = (x, y, 1 - c)
        sends = [pltpu.make_async_remote_copy(src_ref=out_refs[a].at[c], dst_ref=out_refs[a].at[c],
                                              send_sem=send_sems.at[a], recv_sem=recv_sems.at[a], device_id=sib,
                                              device_id_type=MESH) for a in range(n)]
        for cp in sends:
            cp.start()
        for a in range(n):
            theirs = out_refs[a].at[1 - c]
            pltpu.make_async_remote_copy(src_ref=theirs, dst_ref=theirs, send_sem=send_sems.at[a],
                                         recv_sem=recv_sems.at[a], device_id=sib, device_id_type=MESH).wait_recv()
        for cp in sends:
            cp.wait_send()

    return _pcall(body, name=name, in_specs=[ANY] * n, out_specs=[ANY] * n,
                  out_shape=[jax.ShapeDtypeStruct(h.shape, h.dtype) for h in hs],
                  input_output_aliases={a: a for a in range(n)}, scratch_shapes=_sem_pairs(n))(*hs)


ADD_BLOCK_BYTES = 2 * 1024 * 1024


def _row_block(R, C):
    if R * C * 4 <= ADD_BLOCK_BYTES:
        return R
    for br in (512, 256, 128, 64, 32, 16, 8):
        if R % br == 0 and br * C * 4 <= ADD_BLOCK_BYTES:
            return br
    return R


def _add_own_half(name, p, recv, c_arr, out_dtype):
    _, _, R, C = p.shape
    br = _row_block(R, C)

    def body(c_ref, p_ref, r_ref, o_ref):
        o_ref[...] = (p_ref[...].astype(F32) + r_ref[...].astype(F32)).astype(o_ref.dtype)

    spec = pltpu.PrefetchScalarGridSpec(
        num_scalar_prefetch=1, grid=(NCHIP, R // br),
        in_specs=[pl.BlockSpec((None, None, br, C), lambda k, r, c_ref: (k, c_ref[0], r, 0)),
                  pl.BlockSpec((None, br, C), lambda k, r, c_ref: (k, r, 0))],
        out_specs=pl.BlockSpec((None, br, C), lambda k, r, c_ref: (k, r, 0)))
    return _pcall(body, name=name, grid_spec=spec, out_shape=jax.ShapeDtypeStruct((NCHIP, R, C), out_dtype),
                  compiler_params=_params(("arbitrary", "arbitrary")))(c_arr, p, recv)


def _add_chips(name, q, recv, idx_arr):
    _, R, C = q.shape
    br = _row_block(R, C)

    def body(idx_ref, q_ref, r_ref, o_ref):
        o_ref[...] = ((q_ref[...].astype(F32) + r_ref[0].astype(F32)) + r_ref[1].astype(F32)) + r_ref[2].astype(F32)

    spec = pltpu.PrefetchScalarGridSpec(
        num_scalar_prefetch=1, grid=(R // br,),
        in_specs=[pl.BlockSpec((None, br, C), lambda r, idx: (idx[0], r, 0)),
                  pl.BlockSpec((3, br, C), lambda r, idx: (0, r, 0))],
        out_specs=pl.BlockSpec((None, br, C), lambda r, idx: (idx[1], r, 0)))
    return _pcall(body, name=name, grid_spec=spec, out_shape=jax.ShapeDtypeStruct((2, R, C), F32),
                  compiler_params=_params(("arbitrary",)))(idx_arr, q, recv)


EARLY = ['w_in_e']
LATE = [n for n in BIG if n not in EARLY]


def _adamw(name, w, g, m, v):
    shape = w.shape
    cols = shape[-1]
    rows = w.size // cols
    w2, g2, m2, v2 = (a.reshape(rows, cols) for a in (w, g, m, v))
    bt = next((b for b in (256, 128) if rows % b == 0), rows)

    def body(w_ref, g_ref, m_ref, v_ref, d_ref, nm_ref, nv_ref):
        gv = g_ref[...]
        nm = ADAM_B1 * m_ref[...] + (1.0 - ADAM_B1) * gv
        nv = ADAM_B2 * v_ref[...] + (1.0 - ADAM_B2) * (gv * gv)
        m_hat = nm / (1.0 - ADAM_B1 ** ADAM_STEP)
        v_hat = nv / (1.0 - ADAM_B2 ** ADAM_STEP)
        d_ref[...] = -ADAM_LR * (m_hat / (jnp.sqrt(v_hat) + ADAM_EPS) + ADAM_WD * w_ref[...])
        nm_ref[...] = nm
        nv_ref[...] = nv

    blk = pl.BlockSpec((bt, cols), lambda i: (i, 0))
    outs = _pcall(body, name=name, grid=(rows // bt,), in_specs=[blk] * 4, out_specs=[blk] * 3,
                  out_shape=[jax.ShapeDtypeStruct((rows, cols), F32)] * 3, compiler_params=_params(("arbitrary",)))(
        w2, g2, m2, v2)
    return [o.reshape(shape) for o in outs]


SMALL_SHARDED = ['mix_norm_o', 'conv_b_in', 'conv_dw_w', 'conv_dw_b', 'conv_ln_g', 'conv_ln_b', 'conv_b_out']
REPLICATED = [n for n in WEIGHTS if SHARD_AXIS[n] is None]
NCHIP = 4


def _halves(flat, tile_rows):
    unit = 2 * tile_rows * COLS
    total = -(-flat.size // unit) * unit
    return jnp.pad(flat, (0, total - flat.size)).reshape(2, total // (2 * COLS), COLS)


def _flat(arrays):
    return jnp.concatenate([a.reshape(-1) for a in arrays])


def _chip_block(a, axis, k):
    n = a.shape[axis] // NCHIP
    return lax.slice_in_dim(a, k * n, (k + 1) * n, axis=axis)


def _full_shape(n, shard_shape):
    s = list(shard_shape[n])
    s[SHARD_AXIS[n]] *= NCHIP
    return tuple(s)


def _unpack(flat, names, shapes):
    out, off = {}, 0
    for n in names:
        size = math.prod(shapes[n])
        out[n] = flat[off:off + size].reshape(shapes[n])
        off += size
    return out


def kernel(x, mem, mix_norm_e, w_in_e, fox_f_bias, gmlp_ln_g, gmlp_ln_b, gmlp_w_s, gmlp_b_s, w_out_e, mix_norm_o, conv_w_in, conv_b_in, conv_dw_w, conv_dw_b, conv_ln_g, conv_ln_b, conv_w_out, conv_b_out, xa_norm, mem_norm, xa_wq, xa_wkv, xa_wo, ffn_norm, ffn_w_gu, ffn_w_down, final_norm, loss_target, m_mix_norm_e, m_w_in_e, m_fox_f_bias, m_gmlp_ln_g, m_gmlp_ln_b, m_gmlp_w_s, m_gmlp_b_s, m_w_out_e, m_mix_norm_o, m_conv_w_in, m_conv_b_in, m_conv_dw_w, m_conv_dw_b, m_conv_ln_g, m_conv_ln_b, m_conv_w_out, m_conv_b_out, m_xa_norm, m_mem_norm, m_xa_wq, m_xa_wkv, m_xa_wo, m_ffn_norm, m_ffn_w_gu, m_ffn_w_down, m_final_norm, v_mix_norm_e, v_w_in_e, v_fox_f_bias, v_gmlp_ln_g, v_gmlp_ln_b, v_gmlp_w_s, v_gmlp_b_s, v_w_out_e, v_mix_norm_o, v_conv_w_in, v_conv_b_in, v_conv_dw_w, v_conv_dw_b, v_conv_ln_g, v_conv_ln_b, v_conv_w_out, v_conv_b_out, v_xa_norm, v_mem_norm, v_xa_wq, v_xa_wkv, v_xa_wo, v_ffn_norm, v_ffn_w_gu, v_ffn_w_down, v_final_norm):
    env = locals()
    w = {n: env[n] for n in WEIGHTS}
    m = {n: env["m_" + n] for n in WEIGHTS}
    v = {n: env["v_" + n] for n in WEIGHTS}
    shard_shape = {n: w[n].shape for n in WEIGHTS}
    xi, yi, ci = _coords()
    me = 2 * xi + yi
    c_arr = jnp.reshape(ci, (1,)).astype(jnp.int32)
    idx_arr = jnp.stack([me, ci]).astype(jnp.int32)

    def two_halves(a):
        return a.reshape(2, a.shape[0] // 2, a.shape[1])

    def shard(n, l):
        return two_halves(w[n][l].astype(BF))

    def matrix(n, gathered):
        rows, cols = w[n].shape[1:]
        g = gathered.reshape(NCHIP, rows, cols)
        return g.reshape(NCHIP * rows, cols) if SHARD_AXIS[n] == 1 else g.transpose(1, 0, 2).reshape(rows, NCHIP * cols)

    got = _all_gather("gather_first", [shard(n, 0) for n in EARLY] + [_halves(_flat([w[n] for n in SMALL_SHARDED]), 8)], me)
    Wb = {n: [matrix(n, g)] for n, g in zip(EARLY, got)}
    vec = got[-1].reshape(NCHIP, -1)
    parts = [_unpack(vec[k], SMALL_SHARDED, shard_shape) for k in range(NCHIP)]
    P = {n: jnp.concatenate([parts[k][n] for k in range(NCHIP)], axis=SHARD_AXIS[n]) for n in SMALL_SHARDED}
    P.update({n: w[n] for n in REPLICATED})
    first = [(n, 0) for n in LATE]
    second = [(n, 1) for n in LATE if w[n].shape[0] > 1]
    bufs = {key: _own_slot(shard(*key), me) for key in first + second}
    state = {}

    def fwd_done(arrived):
        state['layer0'] = {n: matrix(n, g) for (n, _), g in zip(first, _gather_forward("gather_forward0", arrived))}
        return {n: [g] for n, g in state['layer0'].items()}

    def fwd2_done(arrived):
        return {n: [state['layer0'][n], matrix(n, g)]
                for (n, _), g in zip(second, _gather_forward("gather_forward1", arrived))}

    def by_halves(G, names):
        return [g.reshape(NCHIP, 2, g.shape[1] // 2, g.shape[2]) for n in names for g in G[n]]

    def pair_sums(tag, ps):
        got = _sibling_halves(f"rs_sibling_halves_{tag}", ps)
        return [_add_own_half(f"rs_add_pair_{tag}{a}", p, g, c_arr, p.dtype) for a, (p, g) in enumerate(zip(ps, got))]

    def bwd_carry(G):
        state['qs'] = pair_sums("late", by_halves(G, LATE))
        return _carry_exchange(state['qs'])

    def last_carry(G):
        state['qs_in'] = pair_sums("in", by_halves(G, ['w_in_e']))
        return _carry_exchange(state['qs_in'])

    hooks = dict(fwd_carry=lambda: _carry_gather([bufs[key] for key in first]), fwd_done=fwd_done,
                 fwd2_carry=lambda: _carry_gather([bufs[key] for key in second]), fwd2_done=fwd2_done,
                 bwd_carry=bwd_carry, bwd_done=lambda arrived: state.update(got=arrived),
                 last_carry=last_carry, last_done=lambda arrived: state.update(got_in=arrived))
    loss_part, grad_x, G = _local_step(x, mem, loss_target, Wb, P, hooks)
    loss = lax.psum(loss_part, ("x", "y", "c"))

    def layers(n):
        return G[n] if isinstance(G[n], list) else ([G[n]] if G[n].ndim == 1 else [G[n][l] for l in range(G[n].shape[0])])

    rep = _flat([a for n in REPLICATED for a in layers(n)])
    quarter = -(-rep.size // (NCHIP * 2 * 8 * COLS)) * (2 * 8 * COLS)
    rep = jnp.pad(rep, (0, NCHIP * quarter - rep.size)).reshape(NCHIP, quarter)
    segs = [[_chip_block(a, SHARD_AXIS[n] - 1, k).reshape(-1) for n in SMALL_SHARDED for a in layers(n)] + [rep[k]]
            for k in range(NCHIP)]
    size = sum(piece.size for piece in segs[0])
    total = -(-size // (2 * 8 * COLS)) * (2 * 8 * COLS)
    p_small = jnp.concatenate([piece for seg in segs for piece in seg + [jnp.zeros((total - size,), F32)]])
    p_small = p_small.reshape(NCHIP, 2, total // (2 * COLS), COLS)
    qs_small = pair_sums("vectors", [p_small])
    pending = [("late", state['qs'], state['got']), ("in", state['qs_in'], state['got_in']),
               ("vectors", qs_small, _chip_exchange("rs_chip_exchange_vectors", qs_small))]
    hs = [_add_chips(f"rs_add_chips_{tag}{a}", q, g, idx_arr)
          for tag, qs, got in pending for a, (q, g) in enumerate(zip(qs, got))]
    red = _sibling_swap("rs_sibling_swap", hs)
    mine, at = {}, 0
    for n in LATE + EARLY:
        nl = shard_shape[n][0]
        mine[n] = jnp.stack([r.reshape(shard_shape[n][1:]) for r in red[at:at + nl]])
        at += nl
    red_small = red[-1].reshape(-1)
    mine.update(_unpack(red_small, SMALL_SHARDED, shard_shape))
    off = sum(math.prod(shard_shape[n]) for n in SMALL_SHARDED)
    rep_all = _all_gather("gather_replicated_grads",
                          [red_small[off:off + quarter].reshape(2, quarter // (2 * COLS), COLS)], me)[0]
    mine.update(_unpack(rep_all.reshape(-1), REPLICATED, shard_shape))

    grads, deltas, new_m, new_v = [], [], [], []
    for n in WEIGHTS:
        d, nm, nv = _adamw("adamw_" + n, w[n], mine[n], m[n], v[n])
        grads.append(mine[n])
        deltas.append(d)
        new_m.append(nm)
        new_v.append(nv)
    return (loss, grad_x, *grads, *deltas, *new_m, *new_v)
```

```python
import functools
import math

import jax
import jax.numpy as jnp
from jax import lax
from jax.experimental import pallas as pl
from jax.experimental.pallas import tpu as pltpu

F32 = jnp.float32
BF = jnp.bfloat16
MESH = pl.DeviceIdType.MESH

D = 1024
FOXW = 512
HD = 64
GW = 512
CH = 128
NG = 8
FF = 2816
NMEM = 256
XH = 4
XD = 256
CK = 31
HALO = 32
EPS = 1e-6
IN_W = 2568
IN_WP = 2688
VMEM_LIMIT = 56 * 1024 * 1024

ADAM_LR, ADAM_B1, ADAM_B2, ADAM_EPS, ADAM_WD, ADAM_STEP = 0.001, 0.9, 0.999, 1e-08, 0.01, 10

WEIGHTS = ['mix_norm_e', 'w_in_e', 'fox_f_bias', 'gmlp_ln_g', 'gmlp_ln_b', 'gmlp_w_s', 'gmlp_b_s', 'w_out_e',
           'mix_norm_o', 'conv_w_in', 'conv_b_in', 'conv_dw_w', 'conv_dw_b', 'conv_ln_g', 'conv_ln_b',
           'conv_w_out', 'conv_b_out', 'xa_norm', 'mem_norm', 'xa_wq', 'xa_wkv', 'xa_wo', 'ffn_norm',
           'ffn_w_gu', 'ffn_w_down', 'final_norm']
SHARD_AXIS = {'mix_norm_e': None, 'w_in_e': 2, 'fox_f_bias': None, 'gmlp_ln_g': None, 'gmlp_ln_b': None,
              'gmlp_w_s': None, 'gmlp_b_s': None, 'w_out_e': 1, 'mix_norm_o': 1, 'conv_w_in': 2, 'conv_b_in': 1,
              'conv_dw_w': 2, 'conv_dw_b': 1, 'conv_ln_g': 1, 'conv_ln_b': 1, 'conv_w_out': 1, 'conv_b_out': 1,
              'xa_norm': None, 'mem_norm': None, 'xa_wq': 1, 'xa_wkv': 2, 'xa_wo': 1, 'ffn_norm': None,
              'ffn_w_gu': 2, 'ffn_w_down': 1, 'final_norm': None}
BIG = ['w_in_e', 'w_out_e', 'conv_w_in', 'conv_w_out', 'xa_wq', 'xa_wkv', 'xa_wo', 'ffn_w_gu', 'ffn_w_down']


def _pcall(body, **kw):
    return pl.pallas_call(body, **kw)


def _params(sem=None, **kw):
    return pltpu.CompilerParams(dimension_semantics=sem, vmem_limit_bytes=VMEM_LIMIT, **kw)


def _dot(a, b, dims):
    dn = {'nn': (((1,), (0,)), ((), ())), 'nt': (((1,), (1,)), ((), ())), 'tn': (((0,), (0,)), ((), ()))}[dims]
    return lax.dot_general(a, b, dn, preferred_element_type=F32)


def _sigmoid(x):
    return 1.0 / (1.0 + jnp.exp(-x))


def _rms_stats(xv):
    return lax.rsqrt(jnp.mean(xv * xv, axis=-1, keepdims=True) + EPS)


def _rms_bwd(xv, gain, dh):
    r = _rms_stats(xv)
    t = dh * gain
    dx = r * t - xv * (r * r * r * jnp.mean(t * xv, axis=-1, keepdims=True))
    return dx, dh * xv * r


def _fused_mm(name, *, dims, M, N, bm, bn, groups, epi, outs, x=None, gain=None, tiles=(), rows=(),
              h_out=False, reds=(), carry=None, cols_outer=False):
    bm = min(bm, M)
    nI, nJ = M // bm, N // bn
    assert nI * bm == M and nJ * bn == N
    assert not reds or nJ == 1
    arrays, specs = [], []

    def spec(shape, index):
        return pl.BlockSpec(shape, (lambda jj, ii: index(ii, jj)) if cols_outer else index)

    def add(arr, shape, index):
        arrays.append(arr)
        specs.append(spec(shape, index))
        return len(arrays) - 1

    def first_pass(i, j):
        return (jnp.where(j == 0, i, nI - 1), 0) if cols_outer else (i, 0)

    if x is not None:
        K0 = x.shape[1]
        add(x, (bm, K0), first_pass)
        add(gain, (1, K0), lambda i, j: (0, 0))
    plan = []
    for grp in groups:
        g = []
        for p in grp:
            ai = None
            if p['A'] is not None:
                ai = add(p['A'], (bm, p['Ka']), lambda i, j, o=p.get('acoff', 0): (i, o))
            ro, co = p.get('roff', 0), p.get('coff', 0)
            if dims == 'nn':
                bi = add(p['B'], (p['Ka'], bn), lambda i, j, ro=ro, co=co: (ro, j + co))
            else:
                bi = add(p['B'], (bn, p['Ka']), lambda i, j, ro=ro, co=co: (j + ro, co))
            g.append((ai, bi))
        plan.append(g)
    tile_idx = [add(a, (bm, bn), lambda i, j, o=o: (i, j + o)) for a, o in tiles]
    row_idx = [add(a, (1, bn), lambda i, j, o=o: (0, j + o)) for a, o in rows]
    n_in = len(arrays)

    out_shape = [jax.ShapeDtypeStruct((M, N), dt) for dt in outs]
    out_specs = [spec((bm, bn), lambda i, j: (i, j)) for _ in outs]
    if h_out:
        out_shape.append(jax.ShapeDtypeStruct((M, x.shape[1]), BF))
        out_specs.append(spec((bm, x.shape[1]), first_pass))
    for shp in reds:
        out_shape.append(jax.ShapeDtypeStruct(shp, F32))
        out_specs.append(spec(shp, lambda i, j: (0, 0)))
    n_main = len(outs)
    scratch = [pltpu.VMEM((M if cols_outer else bm, x.shape[1]), BF)] if x is not None else []

    def body(*refs):
        ins, out_refs, scr = refs[:n_in], refs[n_in:n_in + len(out_shape)], refs[n_in + len(out_shape):]
        i, j = (pl.program_id(1), pl.program_id(0)) if cols_outer else (pl.program_id(0), pl.program_id(1))
        if x is not None:
            hn_rows = pl.ds(pl.multiple_of(i * bm, bm), bm) if cols_outer else slice(None)
            hn_ref = scr[0]

            @pl.when(j == 0)
            def _():
                xv = ins[0][...]
                hn = (xv * _rms_stats(xv) * ins[1][...]).astype(BF)
                hn_ref[hn_rows, :] = hn
                if h_out:
                    out_refs[n_main][...] = hn

        accs = []
        for g in plan:
            acc = None
            for ai, bi in g:
                a = hn_ref[hn_rows, :] if ai is None else ins[ai][...]
                if a.dtype != BF:
                    a = a.astype(BF)
                d = _dot(a, ins[bi][...], dims)
                acc = d if acc is None else acc + d
            accs.append(acc)
        out_vals, red_vals = epi(accs, [ins[t][...] for t in tile_idx], [ins[r][...] for r in row_idx])
        for r, v in zip(out_refs[:n_main], out_vals):
            r[...] = v.astype(r.dtype)
        if reds:
            red_refs = out_refs[n_main + (1 if h_out else 0):]

            @pl.when(i == 0)
            def _():
                for r in red_refs:
                    r[...] = jnp.zeros(r.shape, F32)

            for r, v in zip(red_refs, red_vals):
                r[...] += v

    res, arrived = _carried_call(body, name=name, grid=(nJ, nI) if cols_outer else (nI, nJ), in_specs=specs,
                                 out_specs=out_specs,
                                 out_shape=out_shape, scratch_shapes=scratch, operands=arrays, carry=carry)
    return res if carry is None else (res, arrived)


def _epi_plain(accs, tiles, rows):
    return [accs[0]], []


def _epi_resid(accs, tiles, rows):
    y = tiles[0] + accs[0]
    if rows:
        y = y + rows[0]
    return [y], []


def _epi_swiglu(accs, tiles, rows):
    g, u = accs
    return [g, u, g * _sigmoid(g) * u], []


def _epi_glu(accs, tiles, rows):
    a, g = accs[0] + rows[0], accs[1] + rows[1]
    return [a, g, a * _sigmoid(g)], []


def _epi_swiglu_bwd(accs, tiles, rows):
    da = accs[0]
    g, u = tiles[0].astype(F32), tiles[1].astype(F32)
    sg = _sigmoid(g)
    return [da * u * (sg * (1.0 + g * (1.0 - sg))), da * (g * sg)], []


def _epi_rms_bwd(accs, tiles, rows):
    dx, dgr = _rms_bwd(tiles[0], rows[0], accs[0])
    return [tiles[1] + dx], [jnp.sum(dgr, axis=0, keepdims=True)]


def _epi_rms_gain_only(accs, tiles, rows):
    _, dgr = _rms_bwd(tiles[0], rows[0], accs[0])
    return [], [jnp.sum(dgr, axis=0, keepdims=True)]


def _norm_mm(name, x, gain, W, *, N, coff=0, bn, out_dtype, bm=1024, h_out=False):
    return _fused_mm(name, dims='nn', M=x.shape[0], N=N, bm=bm, bn=bn, x=x, gain=gain,
                     groups=[[dict(A=None, Ka=x.shape[1], B=W, coff=coff)]], epi=_epi_plain, outs=[out_dtype],
                     h_out=h_out)


def _mm_resid(name, pairs, resid, bias=None, bm=1024):
    M = resid.shape[0]
    return _fused_mm(name, dims='nn', M=M, N=D, bm=bm, bn=D, groups=[pairs], epi=_epi_resid, outs=[F32],
                     tiles=[(resid, 0)], rows=[(bias, 0)] if bias is not None else [])[0]


def _mm_nt_plain(name, dy, W, bm=1024):
    return _fused_mm(name, dims='nt', M=dy.shape[0], N=W.shape[0], bm=bm, bn=W.shape[0],
                     groups=[[dict(A=dy, Ka=dy.shape[1], B=W)]], epi=_epi_plain, outs=[BF])[0]


def _mm_nt_rms_bwd(name, pairs, x, gain, dx_in, bm=256, carry=None):
    out = _fused_mm(name, dims='nt', M=x.shape[0], N=D, bm=bm, bn=D, groups=[pairs], epi=_epi_rms_bwd,
                    outs=[F32], tiles=[(x, 0), (dx_in, 0)], rows=[(gain, 0)], reds=[(1, D)], carry=carry)
    if carry is None:
        return out[0], out[1]
    return out[0][0], out[0][1], out[1]


def _mm_tn(name, A, G, bk=2048, parts=1):
    T, Ka, Kg = A.shape[0], A.shape[1], G.shape[1]
    w = Kg // parts
    bm = Ka if Ka <= 1024 else Ka // 2
    bn = w if w <= 1408 else w // 2
    bk = min(bk, T)
    per = w // bn
    nI, nJ, nK = Ka // bm, Kg // bn, T // bk

    def body(a_ref, g_ref, o_ref, acc):
        k = pl.program_id(2)

        @pl.when(k == 0)
        def _():
            acc[...] = jnp.zeros(acc.shape, F32)

        acc[...] += _dot(a_ref[...].astype(BF), g_ref[...].astype(BF), 'tn')

        @pl.when(k == nK - 1)
        def _():
            o_ref[...] = acc[...].astype(BF)

    return _pcall(body, name=name, grid=(nI, nJ, nK),
                  in_specs=[pl.BlockSpec((bk, bm), lambda i, j, k: (k, i)),
                            pl.BlockSpec((bk, bn), lambda i, j, k: (k, j))],
                  out_specs=pl.BlockSpec((None, bm, bn), lambda i, j, k: (j // per, i, j % per)),
                  out_shape=jax.ShapeDtypeStruct((parts, Ka, w), BF),
                  scratch_shapes=[pltpu.VMEM((bm, bn), F32)],
                  compiler_params=_params(("arbitrary", "arbitrary", "arbitrary")))(A, G)


def _colsum(name, a, bt=512):
    M, N = a.shape
    bt = min(bt, M)

    def body(a_ref, o_ref):
        @pl.when(pl.program_id(0) == 0)
        def _():
            o_ref[...] = jnp.zeros(o_ref.shape, F32)

        o_ref[...] += jnp.sum(a_ref[...].astype(F32), axis=0, keepdims=True)

    return _pcall(body, name=name, grid=(M // bt,), in_specs=[pl.BlockSpec((bt, N), lambda i: (i, 0))],
                  out_specs=pl.BlockSpec((1, N), lambda i: (0, 0)), out_shape=jax.ShapeDtypeStruct((1, N), F32),
                  compiler_params=_params(("arbitrary",)))(a)


def _cumsum_rows(v):
    T = v.shape[0]
    row = lax.broadcasted_iota(jnp.int32, v.shape, 0)
    s = 1
    while s < T:
        v = v + jnp.where(row >= s, pltpu.roll(v, s, 0), 0.0)
        s *= 2
    return v


def _log_sigmoid(z):
    return jnp.minimum(z, 0.0) - jnp.log(1.0 + jnp.exp(-jnp.abs(z)))


def _fox_gate_fwd(fl, fbias, B, T):
    def body(fl_ref, b_ref, o_ref):
        o_ref[...] = _cumsum_rows(_log_sigmoid(fl_ref[...] + b_ref[...]))

    return _pcall(body, name="fox_gate_fwd", grid=(B,),
                  in_specs=[pl.BlockSpec((T, 128), lambda b: (b, 0)), pl.BlockSpec((1, 128), lambda b: (0, 0))],
                  out_specs=pl.BlockSpec((T, 128), lambda b: (b, 0)),
                  out_shape=jax.ShapeDtypeStruct((B * T, 128), F32), compiler_params=_params(("arbitrary",)))(fl, fbias)


def _fox_gate_bwd(fl, fbias, dcq, dck, B, T):
    def body(fl_ref, b_ref, dcq_ref, dck_ref, dfl_ref, db_ref):
        dc = dcq_ref[...] + dck_ref[...]
        rev = jnp.sum(dc, axis=0, keepdims=True) - _cumsum_rows(dc) + dc
        dfl = rev * _sigmoid(-(fl_ref[...] + b_ref[...]))
        dfl_ref[...] = dfl

        @pl.when(pl.program_id(0) == 0)
        def _():
            db_ref[...] = jnp.zeros(db_ref.shape, F32)

        db_ref[...] += jnp.sum(dfl, axis=0, keepdims=True)

    return _pcall(body, name="fox_gate_bwd", grid=(B,),
                  in_specs=[pl.BlockSpec((T, 128), lambda b: (b, 0)), pl.BlockSpec((1, 128), lambda b: (0, 0)),
                            pl.BlockSpec((T, 128), lambda b: (b, 0)), pl.BlockSpec((T, 128), lambda b: (b, 0))],
                  out_specs=[pl.BlockSpec((T, 128), lambda b: (b, 0)), pl.BlockSpec((1, 128), lambda b: (0, 0))],
                  out_shape=[jax.ShapeDtypeStruct((B * T, 128), F32), jax.ShapeDtypeStruct((1, 128), F32)],
                  compiler_params=_params(("arbitrary",)))(fl, fbias, dcq, dck)


def _carried_call(body, *, name, grid, in_specs, out_specs, out_shape, scratch_shapes, operands, carry):
    if carry is None:
        return _pcall(body, name=name, grid=grid, in_specs=in_specs, out_specs=out_specs, out_shape=out_shape,
                      scratch_shapes=scratch_shapes, compiler_params=_params(("arbitrary",) * len(grid)))(*operands), []
    n, n_in, n_out, n_scr = len(carry['inputs']), len(in_specs), len(out_specs), len(scratch_shapes)

    def wrapped(*refs):
        ins, cin = refs[:n_in], refs[n_in:n_in + n]
        outs, cout = refs[n_in + n:n_in + n + n_out], refs[n_in + n + n_out:n_in + 2 * n + n_out]
        scr = refs[n_in + 2 * n + n_out:]
        send_sems, recv_sems = scr[n_scr:]
        ids = [pl.program_id(d) for d in range(len(grid))]
        first = functools.reduce(jnp.logical_and, [i == 0 for i in ids])
        last = functools.reduce(jnp.logical_and, [i == g - 1 for i, g in zip(ids, grid)])

        @pl.when(first)
        def _():
            for cp in carry['copies'](cin, cout, send_sems, recv_sems):
                cp.start()

        body(*ins, *outs, *scr[:n_scr])

        @pl.when(last)
        def _():
            for cp in carry['copies'](cin, cout, send_sems, recv_sems):
                cp.wait()

    aliases = {n_in + a: n_out + a for a in range(n)} if carry['in_place'] else {}
    res = _pcall(wrapped, name=name, grid=grid, in_specs=list(in_specs) + [ANY] * n,
                 out_specs=list(out_specs) + [ANY] * n, out_shape=list(out_shape) + carry['out_shape'],
                 scratch_shapes=list(scratch_shapes) + _sem_pairs(carry['nsem']), input_output_aliases=aliases,
                 compiler_params=_params(("arbitrary",) * len(grid)))(*operands, *carry['inputs'])
    return res[:n_out], res[n_out:]


def _carry_gather(bufs):
    n = len(bufs)

    def copies(in_refs, out_refs, send_sems, recv_sems):
        x, y, c = _coords()
        cps = []
        for a in range(n):
            blk = out_refs[a].at[2 * x + y, c]
            cps += [pltpu.make_async_remote_copy(src_ref=blk, dst_ref=blk, send_sem=send_sems.at[3 * a + j],
                                                 recv_sem=recv_sems.at[3 * a + j], device_id=(cx, cy, c),
                                                 device_id_type=MESH) for j, (cx, cy) in enumerate(_other_chips(x, y))]
        return cps

    return dict(inputs=bufs, out_shape=[jax.ShapeDtypeStruct(b.shape, b.dtype) for b in bufs], in_place=True,
                nsem=3 * n, copies=copies)


def _carry_exchange(qs):
    n = len(qs)

    def copies(in_refs, out_refs, send_sems, recv_sems):
        x, y, c = _coords()
        return [pltpu.make_async_remote_copy(src_ref=in_refs[a].at[2 * cx + cy], dst_ref=out_refs[a].at[j],
                                             send_sem=send_sems.at[3 * a + j], recv_sem=recv_sems.at[3 * a + j],
                                             device_id=(cx, cy, c), device_id_type=MESH)
                for a in range(n) for j, (cx, cy) in enumerate(_other_chips(x, y))]

    return dict(inputs=qs, out_shape=[jax.ShapeDtypeStruct((3,) + q.shape[1:], q.dtype) for q in qs], in_place=False,
                nsem=3 * n, copies=copies)


NEG = -1e30
QSUB = 1


def _fox_fwd(qkv, cum4, B, T, qoff, bq, bk, carry=None):
    nq, nkb = T // bq, T // bk
    N = B * T

    def body(q_ref, k_ref, v_ref, cum_ref, o_ref, lse_ref):
        hp, i = pl.program_id(1), pl.program_id(2)
        sq = bq // QSUB
        lane = lax.broadcasted_iota(jnp.int32, (sq, 128), 1)
        heads = [slice(e * HD, (e + 1) * HD) for e in range(2)]
        chains = [(e, sl, slice(r * sq, (r + 1) * sq)) for e, sl in enumerate(heads) for r in range(QSUB)]
        qs = [q_ref[rows, sl] * 0.125 for _, sl, rows in chains]

        def block(j, carry, diagonal):
            ks = pl.multiple_of(j * bk, bk)
            out = []
            for n, (e, sl, rows) in enumerate(chains):
                m, l, acc = carry[n]
                s = _dot(qs[n], k_ref[pl.ds(ks, bk), sl], 'nt') - cum_ref[0, 2 * hp + e, pl.ds(j, 1), :]
                if diagonal:
                    keep = (lax.broadcasted_iota(jnp.int32, (sq, bk), 0) + rows.start
                            >= lax.broadcasted_iota(jnp.int32, (sq, bk), 1))
                    s = jnp.where(keep, s, NEG)
                m_new = jnp.maximum(m, jnp.max(s, axis=1, keepdims=True))
                p = jnp.exp(s - m_new)
                alpha = jnp.exp(m - m_new)
                l = alpha * l + jnp.sum(p, axis=1, keepdims=True)
                acc = alpha * acc + _dot(p.astype(BF), v_ref[pl.ds(ks, bk), sl], 'nn')
                out.append((m_new, l, acc))
            return tuple(out)

        init = tuple((jnp.full((sq, 1), NEG, F32), jnp.zeros((sq, 1), F32), jnp.zeros((sq, HD), F32)) for _ in chains)
        carry = lax.fori_loop(0, i, lambda j, c: block(j, c, False), init)
        carry = block(i, carry, True)
        for r in range(QSUB):
            lse_tile = jnp.zeros((sq, 128), F32)
            for n, (e, sl, rows) in enumerate(chains):
                if rows.start == r * sq:
                    m, l, acc = carry[n]
                    o_ref[rows, sl] = (acc / l).astype(BF)
                    lse_tile = jnp.where(lane == e, m + jnp.log(l), lse_tile)
            lse_ref[r * sq:(r + 1) * sq, :] = lse_tile

    return _carried_call(body, name="fox_fwd", grid=(B, 4, nq),
                         in_specs=[pl.BlockSpec((bq, 128), lambda b, h, i: (b * nq + i, qoff + h)),
                                   pl.BlockSpec((T, 128), lambda b, h, i: (b, qoff + 4 + h)),
                                   pl.BlockSpec((T, 128), lambda b, h, i: (b, qoff + 8 + h)),
                                   pl.BlockSpec((1, NG, nkb, bk), lambda b, h, i: (b, 0, 0, 0))],
                         out_specs=[pl.BlockSpec((bq, 128), lambda b, h, i: (b * nq + i, h)),
                                    pl.BlockSpec((bq, 128), lambda b, h, i: (b * nq + i, h))],
                         out_shape=[jax.ShapeDtypeStruct((N, FOXW), BF), jax.ShapeDtypeStruct((N, FOXW), F32)],
                         scratch_shapes=[], operands=(qkv, qkv, qkv, cum4), carry=carry)


def _fox_bwd(qkv, cum4, o, lse, dcat, B, T, qoff, bq, bk, carry=None):
    nq, nkb = T // bq, T // bk
    N = B * T

    def body(q_ref, k_ref, v_ref, cum_ref, o_ref, lse_ref, do_ref, dq_ref, dk_ref, dv_ref, dcum_ref, dcq_ref,
             dq_acc, dl_ref, rs_ref):
        hp = pl.program_id(1)
        heads = [slice(e * HD, (e + 1) * HD) for e in range(2)]
        keep = lax.broadcasted_iota(jnp.int32, (bq, bk), 0) >= lax.broadcasted_iota(jnp.int32, (bq, bk), 1)
        dcq_ref[...] = jnp.zeros(dcq_ref.shape, F32)
        dq_acc[...] = jnp.zeros(dq_acc.shape, F32)
        rs_ref[...] = jnp.zeros(rs_ref.shape, F32)
        for e, sl in enumerate(heads):
            dl_ref[e] = jnp.sum(do_ref[:, sl].astype(F32) * o_ref[:, sl].astype(F32), axis=1, keepdims=True)
        for j in range(nkb):
            krows = slice(j * bk, (j + 1) * bk)

            def tile(i, carry, diagonal):
                qs = i * bq if diagonal else pl.multiple_of(i * bq, bq)
                out = []
                for e, sl in enumerate(heads):
                    dk_a, dv_a, cs = carry[e]
                    q, k = q_ref[pl.ds(qs, bq), sl], k_ref[krows, sl]
                    do = do_ref[pl.ds(qs, bq), sl]
                    s = _dot(q, k, 'nt') * 0.125 - cum_ref[0, 2 * hp + e, j:j + 1, :]
                    p = jnp.exp(s - lse_ref[pl.ds(qs, bq), e:e + 1])
                    if diagonal:
                        p = jnp.where(keep, p, 0.0)
                    dv_a = dv_a + _dot(p.astype(BF), do, 'tn')
                    ds = p * (_dot(do, v_ref[krows, sl], 'nt') - dl_ref[e, pl.ds(qs, bq), :])
                    cs = cs + jnp.sum(ds, axis=0, keepdims=True)
                    rs_ref[e, pl.ds(qs, bq), :] += jnp.sum(ds, axis=1, keepdims=True)
                    dsb = ds.astype(BF)
                    dk_a = dk_a + _dot(dsb, q, 'tn')
                    dq_acc[e, pl.ds(qs, bq), :] += _dot(dsb, k, 'nn')
                    out.append((dk_a, dv_a, cs))
                return tuple(out)

            init = tuple((jnp.zeros((bk, HD), F32), jnp.zeros((bk, HD), F32), jnp.zeros((1, bk), F32)) for _ in heads)
            carry = lax.fori_loop(j + 1, nq, lambda i, c: tile(i, c, False), tile(j, init, True))
            for e, sl in enumerate(heads):
                dk_a, dv_a, cs = carry[e]
                dk_ref[krows, sl] = (dk_a * 0.125).astype(BF)
                dv_ref[krows, sl] = dv_a.astype(BF)
                dcum_ref[0, e, j:j + 1, :] = -cs
        for e, sl in enumerate(heads):
            dq_ref[:, sl] = (dq_acc[e] * 0.125).astype(BF)
            dcq_ref[:, e:e + 1] = rs_ref[e]

    seq = lambda off: pl.BlockSpec((T, 128), lambda b, h, off=off: (b, off + h))
    return _carried_call(body, name="fox_bwd", grid=(B, 4),
                         in_specs=[seq(qoff), seq(qoff + 4), seq(qoff + 8),
                                   pl.BlockSpec((1, NG, nkb, bk), lambda b, h: (b, 0, 0, 0)),
                                   seq(0), seq(0), seq(0)],
                         out_specs=[seq(0), seq(0), seq(0),
                                    pl.BlockSpec((1, 2, nkb, bk), lambda b, h: (b, h, 0, 0)), seq(0)],
                         out_shape=[jax.ShapeDtypeStruct((N, FOXW), BF)] * 3
                         + [jax.ShapeDtypeStruct((B, NG, nkb, bk), F32), jax.ShapeDtypeStruct((N, FOXW), F32)],
                         scratch_shapes=[pltpu.VMEM((2, T, HD), F32), pltpu.VMEM((2, T, 1), F32),
                                         pltpu.VMEM((2, T, 1), F32)],
                         operands=(qkv, qkv, qkv, cum4, o, lse, dcat), carry=carry)


_GC = math.sqrt(2.0 / math.pi)
_GA = 0.044715


def _gelu(z):
    return 0.5 * z * (1.0 + jnp.tanh(_GC * (z + _GA * z * z * z)))


def _gelu_grad(z):
    t = jnp.tanh(_GC * (z + _GA * z * z * z))
    return 0.5 * (1.0 + t) + 0.5 * z * (1.0 - t * t) * (_GC * (1.0 + 3.0 * _GA * z * z))


def _gmlp_common(z, lng, lnb):
    zg = _gelu(z)
    u, vg = zg[:, :GW], zg[:, GW:]
    mu = jnp.mean(vg, axis=-1, keepdims=True)
    xc = vg - mu
    rstd = lax.rsqrt(jnp.mean(xc * xc, axis=-1, keepdims=True) + EPS)
    xhat = xc * rstd
    return u, xhat, rstd, xhat * lng + lnb


def _tril_w(ws_ref):
    tri = lax.broadcasted_iota(jnp.int32, (CH, CH), 0) >= lax.broadcasted_iota(jnp.int32, (CH, CH), 1)
    return [jnp.where(tri, ws_ref[g], 0.0).astype(BF) for g in range(NG)], tri


def _split_pair(vp):
    lane = lax.broadcasted_iota(jnp.int32, vp.shape, 1)
    zero = jnp.zeros(vp.shape, vp.dtype)
    return jnp.concatenate([jnp.where(lane < HD, vp, zero), jnp.where(lane >= HD, vp, zero)], axis=0)


def _gmlp_mix(wt, vgn_b):
    outs = []
    for p in range(NG // 2):
        wcat = jnp.concatenate([wt[2 * p], wt[2 * p + 1]], axis=1)
        outs.append(_dot(wcat, _split_pair(vgn_b[:, 128 * p:128 * (p + 1)]), 'nn'))
    return jnp.concatenate(outs, axis=1)


def _gmlp_fwd(z, lng, lnb, ws, bfull, bt):
    N = z.shape[0]

    def body(z_ref, lng_ref, lnb_ref, ws_ref, bf_ref, o_ref):
        wt, _ = _tril_w(ws_ref)
        for c in range(bt // CH):
            rows = slice(c * CH, (c + 1) * CH)
            u, _, _, vgn = _gmlp_common(z_ref[rows, :], lng_ref[...], lnb_ref[...])
            mixed = _gmlp_mix(wt, vgn.astype(BF)) + bf_ref[...]
            o_ref[rows, :] = (u * mixed).astype(BF)

    full = lambda shp: pl.BlockSpec(shp, lambda i: (0,) * len(shp))
    return _pcall(body, name="gmlp_fwd", grid=(N // bt,),
                  in_specs=[pl.BlockSpec((bt, D), lambda i: (i, 0)), full((1, GW)), full((1, GW)),
                            full((NG, CH, CH)), full((CH, GW))],
                  out_specs=pl.BlockSpec((bt, GW), lambda i: (i, 0)), out_shape=jax.ShapeDtypeStruct((N, GW), BF),
                  compiler_params=_params(("arbitrary",)))(z, lng, lnb, ws, bfull)


def _gmlp_bwd(z, dcat, lng, lnb, ws, bfull, bt):
    N = z.shape[0]

    def body(z_ref, da_ref, lng_ref, lnb_ref, ws_ref, bf_ref, dz_ref, dg_ref, db_ref, dws_ref, dbf_ref):
        @pl.when(pl.program_id(0) == 0)
        def _():
            for r in (dg_ref, db_ref, dws_ref, dbf_ref):
                r[...] = jnp.zeros(r.shape, F32)

        wt, tri = _tril_w(ws_ref)
        lane = lax.broadcasted_iota(jnp.int32, (CH, 128), 1)
        for c in range(bt // CH):
            rows = slice(c * CH, (c + 1) * CH)
            zc = z_ref[rows, :]
            u, xhat, rstd, vgn = _gmlp_common(zc, lng_ref[...], lnb_ref[...])
            vgn_b = vgn.astype(BF)
            mixed = _gmlp_mix(wt, vgn_b) + bf_ref[...]
            da = da_ref[rows, :].astype(F32)
            dmix = da * u
            du = da * mixed
            dbf_ref[...] += dmix
            dvs = []
            for p in range(NG // 2):
                cols = slice(128 * p, 128 * (p + 1))
                dmp = dmix[:, cols].astype(BF)
                dwp = _dot(_split_pair(dmp), vgn_b[:, cols], 'nt')
                dws_ref[2 * p] += jnp.where(tri, dwp[:CH], 0.0)
                dws_ref[2 * p + 1] += jnp.where(tri, dwp[CH:], 0.0)
                dvs.append(jnp.where(lane < HD, _dot(wt[2 * p], dmp, 'tn'), _dot(wt[2 * p + 1], dmp, 'tn')))
            dvgn = jnp.concatenate(dvs, axis=1)
            dg_ref[...] += jnp.sum(dvgn * xhat, axis=0, keepdims=True)
            db_ref[...] += jnp.sum(dvgn, axis=0, keepdims=True)
            dxh = dvgn * lng_ref[...]
            dvg = rstd * (dxh - jnp.mean(dxh, axis=-1, keepdims=True)
                          - xhat * jnp.mean(dxh * xhat, axis=-1, keepdims=True))
            dz_ref[rows, :] = (jnp.concatenate([du, dvg], axis=1) * _gelu_grad(zc)).astype(BF)

    full = lambda shp: pl.BlockSpec(shp, lambda i: (0,) * len(shp))
    return _pcall(body, name="gmlp_bwd", grid=(N // bt,),
                  in_specs=[pl.BlockSpec((bt, D), lambda i: (i, 0)), pl.BlockSpec((bt, GW), lambda i: (i, 1)),
                            full((1, GW)), full((1, GW)), full((NG, CH, CH)), full((CH, GW))],
                  out_specs=[pl.BlockSpec((bt, D), lambda i: (i, 0)), full((1, GW)), full((1, GW)),
                             full((NG, CH, CH)), full((CH, GW))],
                  out_shape=[jax.ShapeDtypeStruct((N, D), BF), jax.ShapeDtypeStruct((1, GW), F32),
                             jax.ShapeDtypeStruct((1, GW), F32), jax.ShapeDtypeStruct((NG, CH, CH), F32),
                             jax.ShapeDtypeStruct((CH, GW), F32)],
                  compiler_params=_params(("arbitrary",)))(z, dcat, lng, lnb, ws, bfull)


def _group_sum(name, a):
    def body(a_ref, o_ref):
        lane = lax.broadcasted_iota(jnp.int32, (CH, 128), 1)
        out = jnp.zeros((CH, 128), F32)
        for g in range(NG):
            out = jnp.where(lane == g, jnp.sum(a_ref[:, g * HD:(g + 1) * HD], axis=1, keepdims=True), out)
        o_ref[...] = out

    return _pcall(body, name=name, out_shape=jax.ShapeDtypeStruct((CH, 128), F32))(a)


def _xattn_softmax(q_h, k_h):
    s = _dot(q_h, k_h, 'nt') * (XD ** -0.5)
    p = jnp.exp(s - jnp.max(s, axis=1, keepdims=True))
    return p / jnp.sum(p, axis=1, keepdims=True)


def _xattn_fwd(name, q, kv, B, T, bq):
    nq = T // bq

    def body(q_ref, kv_ref, o_ref):
        for h in range(XH):
            cols = slice(h * XD, (h + 1) * XD)
            p = _xattn_softmax(q_ref[:, cols], kv_ref[:, cols])
            o_ref[:, cols] = _dot(p.astype(BF), kv_ref[:, D + h * XD:D + (h + 1) * XD], 'nn').astype(BF)

    return _pcall(body, name=name, grid=(B, nq),
                  in_specs=[pl.BlockSpec((bq, D), lambda b, i: (b * nq + i, 0)),
                            pl.BlockSpec((NMEM, 2 * D), lambda b, i: (b, 0))],
                  out_specs=pl.BlockSpec((bq, D), lambda b, i: (b * nq + i, 0)),
                  out_shape=jax.ShapeDtypeStruct((B * T, D), BF), compiler_params=_params(("arbitrary", "arbitrary")))(q, kv)


def _xattn_bwd(name, q, kv, do, B, T, bq):
    nq = T // bq
    sc = XD ** -0.5

    def body(q_ref, kv_ref, do_ref, dq_ref, dkv_ref):
        @pl.when(pl.program_id(1) == 0)
        def _():
            dkv_ref[...] = jnp.zeros(dkv_ref.shape, F32)

        for h in range(XH):
            cols = slice(h * XD, (h + 1) * XD)
            vcols = slice(D + h * XD, D + (h + 1) * XD)
            qh, kh, doh = q_ref[:, cols], kv_ref[:, cols], do_ref[:, cols]
            p = _xattn_softmax(qh, kh)
            dp = _dot(doh, kv_ref[:, vcols], 'nt')
            ds = p * (dp - jnp.sum(p * dp, axis=1, keepdims=True))
            dsb = ds.astype(BF)
            dq_ref[:, cols] = (_dot(dsb, kh, 'nn') * sc).astype(BF)
            dkv_ref[:, cols] += _dot(dsb, qh, 'tn') * sc
            dkv_ref[:, vcols] += _dot(p.astype(BF), doh, 'tn')

    blk = pl.BlockSpec((bq, D), lambda b, i: (b * nq + i, 0))
    return _pcall(body, name=name, grid=(B, nq),
                  in_specs=[blk, pl.BlockSpec((NMEM, 2 * D), lambda b, i: (b, 0)), blk],
                  out_specs=[blk, pl.BlockSpec((NMEM, 2 * D), lambda b, i: (b, 0))],
                  out_shape=[jax.ShapeDtypeStruct((B * T, D), BF), jax.ShapeDtypeStruct((B * NMEM, 2 * D), F32)],
                  compiler_params=_params(("arbitrary", "arbitrary")))(q, kv, do)


def _ln_stats(v):
    mu = jnp.mean(v, axis=-1, keepdims=True)
    xc = v - mu
    rstd = lax.rsqrt(jnp.mean(xc * xc, axis=-1, keepdims=True) + EPS)
    return xc * rstd, rstd


SUB = 8


LANES = 128
NSTRIP = D // LANES


def _fill_window(win, parts):
    for s in range(NSTRIP):
        for r0, val in parts:
            win[s, r0:r0 + val.shape[0], :] = val[:, s * LANES:(s + 1) * LANES]


def _fill_phases(win, sh, rows):
    for b in range(1, SUB):
        for s in range(NSTRIP):
            sh[b - 1, s] = win[s, b:b + rows, :]


def _fill_taps(w8, w_ref):
    for s in range(NSTRIP):
        for j in range(CK):
            w8[s, SUB * j:SUB * (j + 1), :] = jnp.broadcast_to(w_ref[j:j + 1, s * LANES:(s + 1) * LANES], (SUB, LANES))


def _row_groups(win, sh, s):
    cache = {}

    def get(o, t):
        a, b = divmod(o, SUB)
        key = (b, t + a)
        if key not in cache:
            rows = slice(SUB * (t + a), SUB * (t + a + 1))
            cache[key] = win[s, rows, :] if b == 0 else sh[b - 1, s, rows, :]
        return cache[key]

    return get


def _from_strips(ref):
    return jnp.concatenate([ref[s] for s in range(NSTRIP)], axis=1)


def _sum_groups(name, a):
    R, C = a.shape[0] // SUB, a.shape[1]

    def body(a_ref, o_ref):
        o_ref[...] = jnp.sum(a_ref[...].reshape(R, SUB, C), axis=1)

    return _pcall(body, name=name, out_shape=jax.ShapeDtypeStruct((R, C), F32))(a)


def _conv_fwd(y, w32, wb, lng, lnb, B, T, bt):
    nt = T // bt
    hb = bt // HALO
    prows = bt + HALO - SUB

    def body(y_ref, yp_ref, w_ref, wb_ref, lng_ref, lnb_ref, s_ref, yc_ref, win, sh, w8, out):
        i = pl.program_id(1)
        _fill_window(win, [(0, jnp.where(i > 0, yp_ref[...], 0.0)), (HALO, y_ref[...])])
        _fill_phases(win, sh, prows)

        @pl.when((pl.program_id(0) == 0) & (i == 0))
        def _():
            _fill_taps(w8, w_ref)

        def strip(s, carry):
            get = _row_groups(win, sh, s)
            for t in range(bt // SUB):
                accs = [jnp.zeros((SUB, LANES), F32), jnp.zeros((SUB, LANES), F32)]
                for j in range(CK):
                    accs[j % 2] = accs[j % 2] + w8[s, SUB * j:SUB * (j + 1), :] * get(HALO - (CK - 1) + j, t)
                out[s, SUB * t:SUB * (t + 1), :] = accs[0] + accs[1]
            return carry

        lax.fori_loop(0, NSTRIP, strip, 0)
        acc = _from_strips(out) + wb_ref[...]
        yc_ref[...] = acc
        xhat, _ = _ln_stats(acc)
        ln = xhat * lng_ref[...] + lnb_ref[...]
        s_ref[...] = (ln * _sigmoid(ln)).astype(BF)

    row = lambda n: pl.BlockSpec((n, D), lambda b, i: (0, 0))
    cur = pl.BlockSpec((bt, D), lambda b, i: (b * nt + i, 0))
    return _pcall(body, name="conv_fwd", grid=(B, nt),
                  in_specs=[cur, pl.BlockSpec((HALO, D), lambda b, i: (jnp.maximum((b * nt + i) * hb - 1, 0), 0)),
                            row(HALO), row(1), row(1), row(1)],
                  out_specs=[cur, cur],
                  out_shape=[jax.ShapeDtypeStruct((B * T, D), BF), jax.ShapeDtypeStruct((B * T, D), F32)],
                  scratch_shapes=[pltpu.VMEM((NSTRIP, bt + HALO, LANES), F32),
                                  pltpu.VMEM((SUB - 1, NSTRIP, prows, LANES), F32),
                                  pltpu.VMEM((NSTRIP, HALO * SUB, LANES), F32), pltpu.VMEM((NSTRIP, bt, LANES), F32)],
                  compiler_params=_params(("arbitrary", "arbitrary")))(y, y, w32, wb, lng, lnb)


def _conv_bwd(ds, yc, y, pa, pg, w32, lng, lnb, B, T, bt):
    nt = T // bt
    hb = bt // HALO
    nblk32 = B * T // HALO

    def ln_bwd(dsv, ycv, lng, lnb):
        xhat, rstd = _ln_stats(ycv)
        ln = xhat * lng + lnb
        sg = _sigmoid(ln)
        dln = dsv * (sg * (1.0 + ln * (1.0 - sg)))
        dxh = dln * lng
        dyc = rstd * (dxh - jnp.mean(dxh, axis=-1, keepdims=True)
                      - xhat * jnp.mean(dxh * xhat, axis=-1, keepdims=True))
        return dyc, dln, xhat

    prows = bt + HALO - SUB

    def body(ds_ref, dsn_ref, yc_ref, ycn_ref, y_ref, yp_ref, pa_ref, pg_ref, w_ref, lng_ref, lnb_ref,
             dpa_ref, dpg_ref, dw_ref, dwb_ref, dlng_ref, dlnb_ref, dba_ref, dbg_ref,
             dwin, ywin, dsh, ysh, w8, dy_out, dw_out):
        i = pl.program_id(1)

        @pl.when((pl.program_id(0) == 0) & (i == 0))
        def _():
            for r in (dw_ref, dwb_ref, dlng_ref, dlnb_ref, dba_ref, dbg_ref):
                r[...] = jnp.zeros(r.shape, F32)
            _fill_taps(w8, w_ref)

        lng, lnb = lng_ref[...], lnb_ref[...]
        dyc, dln, xhat = ln_bwd(ds_ref[...].astype(F32), yc_ref[...], lng, lnb)
        dycn, _, _ = ln_bwd(dsn_ref[...].astype(F32), ycn_ref[...], lng, lnb)
        _fill_window(dwin, [(0, dyc), (bt, jnp.where(i < nt - 1, dycn, 0.0))])
        _fill_window(ywin, [(0, jnp.where(i > 0, yp_ref[...], 0.0)), (HALO, y_ref[...])])
        dlng_ref[...] += jnp.sum(dln * xhat, axis=0, keepdims=True)
        dlnb_ref[...] += jnp.sum(dln, axis=0, keepdims=True)
        dwb_ref[...] += jnp.sum(dyc, axis=0, keepdims=True)
        _fill_phases(dwin, dsh, prows)
        _fill_phases(ywin, ysh, prows)

        def strip(s, carry):
            get_d, get_y = _row_groups(dwin, dsh, s), _row_groups(ywin, ysh, s)
            dw_acc = [jnp.zeros((SUB, LANES), F32) for _ in range(CK)]
            for t in range(bt // SUB):
                dyc_g = get_d(0, t)
                dys = [jnp.zeros((SUB, LANES), F32), jnp.zeros((SUB, LANES), F32)]
                for j in range(CK):
                    dys[j % 2] = dys[j % 2] + w8[s, SUB * j:SUB * (j + 1), :] * get_d(CK - 1 - j, t)
                    dw_acc[j] = dw_acc[j] + dyc_g * get_y(HALO - (CK - 1) + j, t)
                dy_out[s, SUB * t:SUB * (t + 1), :] = dys[0] + dys[1]
            for j in range(CK):
                dw_out[s, SUB * j:SUB * (j + 1), :] = dw_acc[j]
            return carry

        lax.fori_loop(0, NSTRIP, strip, 0)
        dw_ref[0:CK * SUB, :] += _from_strips(dw_out)
        dy = _from_strips(dy_out)
        a, g = pa_ref[...].astype(F32), pg_ref[...].astype(F32)
        sg = _sigmoid(g)
        da = dy * sg
        dg = dy * a * sg * (1.0 - sg)
        dpa_ref[...] = da.astype(BF)
        dpg_ref[...] = dg.astype(BF)
        dba_ref[...] += jnp.sum(da, axis=0, keepdims=True)
        dbg_ref[...] += jnp.sum(dg, axis=0, keepdims=True)

    cur = pl.BlockSpec((bt, D), lambda b, i: (b * nt + i, 0))
    nxt = pl.BlockSpec((HALO, D), lambda b, i: (jnp.minimum((b * nt + i + 1) * hb, nblk32 - 1), 0))
    prv = pl.BlockSpec((HALO, D), lambda b, i: (jnp.maximum((b * nt + i) * hb - 1, 0), 0))
    row = lambda n: pl.BlockSpec((n, D), lambda b, i: (0, 0))
    N = B * T
    return _pcall(body, name="conv_bwd", grid=(B, nt),
                  in_specs=[cur, nxt, cur, nxt, cur, prv, cur, cur, row(HALO), row(1), row(1)],
                  out_specs=[cur, cur, row(HALO * SUB), row(1), row(1), row(1), row(1), row(1)],
                  out_shape=[jax.ShapeDtypeStruct((N, D), BF)] * 2 + [jax.ShapeDtypeStruct((HALO * SUB, D), F32)]
                  + [jax.ShapeDtypeStruct((1, D), F32)] * 5,
                  scratch_shapes=[pltpu.VMEM((NSTRIP, bt + HALO, LANES), F32), pltpu.VMEM((NSTRIP, bt + HALO, LANES), F32),
                                  pltpu.VMEM((SUB - 1, NSTRIP, prows, LANES), F32),
                                  pltpu.VMEM((SUB - 1, NSTRIP, prows, LANES), F32),
                                  pltpu.VMEM((NSTRIP, HALO * SUB, LANES), F32), pltpu.VMEM((NSTRIP, bt, LANES), F32),
                                  pltpu.VMEM((NSTRIP, CK * SUB, LANES), F32)],
                  compiler_params=_params(("arbitrary", "arbitrary")))(ds, ds, yc, yc, y, y, pa, pg, w32, lng, lnb)


def _head(x, tgt, gain, bt=512):
    N = x.shape[0]
    bt = min(bt, N)

    def body(x_ref, t_ref, g_ref, dx_ref, loss_ref, dg_ref):
        @pl.when(pl.program_id(0) == 0)
        def _():
            loss_ref[...] = jnp.zeros(loss_ref.shape, F32)
            dg_ref[...] = jnp.zeros(dg_ref.shape, F32)

        xv = x_ref[...]
        gain = g_ref[...]
        err = xv * _rms_stats(xv) * gain - t_ref[...]
        loss_ref[...] += 0.5 * jnp.sum(jnp.mean(err * err, axis=-1, keepdims=True), axis=0, keepdims=True)
        dx, dgr = _rms_bwd(xv, gain, err * (1.0 / D))
        dx_ref[...] = dx
        dg_ref[...] += jnp.sum(dgr, axis=0, keepdims=True)

    blk = pl.BlockSpec((bt, D), lambda i: (i, 0))
    return _pcall(body, name="loss_head", grid=(N // bt,),
                  in_specs=[blk, blk, pl.BlockSpec((1, D), lambda i: (0, 0))],
                  out_specs=[blk, pl.BlockSpec((1, 128), lambda i: (0, 0)), pl.BlockSpec((1, D), lambda i: (0, 0))],
                  out_shape=[jax.ShapeDtypeStruct((N, D), F32), jax.ShapeDtypeStruct((1, 128), F32),
                             jax.ShapeDtypeStruct((1, D), F32)],
                  compiler_params=_params(("arbitrary",)))(x, tgt, gain)


def _local_step(x, mem, tgt, Wb, P, hooks=None):
    B, T, _ = x.shape
    N = B * T
    bq = bk = min(512, T)
    bt = min(512, T)
    x0 = x.reshape(N, D)
    mem2 = mem.reshape(B * NMEM, D)
    tgt2 = tgt.reshape(N, D)
    row = lambda v: v.reshape(1, -1)
    G = {}

    w_in = Wb['w_in_e'][0]
    w_inp = jnp.concatenate([w_in[:, 3 * FOXW + NG:], w_in[:, :3 * FOXW], w_in[:, 3 * FOXW:3 * FOXW + NG],
                             jnp.zeros((D, 128 - NG), BF)], axis=1)
    g_e = row(P['mix_norm_e'])
    z, h0 = _norm_mm("proj_z", x0, g_e, w_inp, N=D, coff=0, bn=D, out_dtype=F32, h_out=True)
    qkv = _norm_mm("proj_qkv", x0, g_e, w_inp, N=3 * FOXW, coff=2, bn=FOXW, out_dtype=BF)[0]
    fl = _norm_mm("proj_f", x0, g_e, w_inp, N=128, coff=20, bn=128, out_dtype=F32)[0]
    fbias = jnp.concatenate([P['fox_f_bias'].reshape(1, NG), jnp.zeros((1, 128 - NG), F32)], axis=1)
    cum = _fox_gate_fwd(fl, fbias, B, T)
    cum4 = cum[:, :NG].reshape(B, T, NG).transpose(0, 2, 1).reshape(B, NG, T // bk, bk)
    (b_out, lse), arrived = _fox_fwd(qkv, cum4, B, T, 0, bq, bk, carry=hooks['fwd_carry']() if hooks else None)
    if hooks:
        Wb = {**Wb, **hooks['fwd_done'](arrived)}
    lng, lnb = row(P['gmlp_ln_g']), row(P['gmlp_ln_b'])
    ws = P['gmlp_w_s'][0]
    bfull = jnp.repeat(P['gmlp_b_s'][0].T, HD, axis=1)
    a_out = _gmlp_fwd(z, lng, lnb, ws, bfull, bt)
    w_out = Wb['w_out_e'][0]
    x1 = _mm_resid("mix_out", [dict(A=b_out, Ka=FOXW, B=w_out, roff=0), dict(A=a_out, Ka=GW, B=w_out, roff=1)], x0)

    def xa_ffn_fwd(l, xin):
        qx, hq = _norm_mm(f"xa_q{l}", xin, row(P['xa_norm'][l]), Wb['xa_wq'][l], N=D, bn=D, out_dtype=BF, h_out=True)
        kv, hm = _norm_mm(f"xa_kv{l}", mem2, row(P['mem_norm'][l]), Wb['xa_wkv'][l], N=2 * D, bn=D, out_dtype=BF,
                          h_out=True)
        o = _xattn_fwd(f"xattn_fwd{l}", qx, kv, B, T, min(1024, T))
        xm = _mm_resid(f"xa_o{l}", [dict(A=o, Ka=D, B=Wb['xa_wo'][l])], xin)
        wgu = Wb['ffn_w_gu'][l]
        carry = hooks['fwd2_carry']() if hooks and l == 0 else None
        res = _fused_mm(f"ffn_gu{l}", dims='nn', M=N, N=FF, bm=min(512, N), bn=FF // 2, x=xm,
                        gain=row(P['ffn_norm'][l]),
                        groups=[[dict(A=None, Ka=D, B=wgu, coff=0)], [dict(A=None, Ka=D, B=wgu, coff=2)]],
                        epi=_epi_swiglu, outs=[BF, BF, BF], h_out=True, carry=carry, cols_outer=True)
        if carry is not None:
            res, arrived = res
            Wb.update(hooks['fwd2_done'](arrived))
        g, u, a, hf = res
        xo = _mm_resid(f"ffn_down{l}", [dict(A=a, Ka=FF, B=Wb['ffn_w_down'][l])], xm)
        return xo, dict(xin=xin, qx=qx, hq=hq, kv=kv, hm=hm, o=o, xm=xm, g=g, u=u, a=a, hf=hf)

    x3, S0 = xa_ffn_fwd(0, x1)
    w_cin = Wb['conv_w_in'][0]
    b_cin = row(P['conv_b_in'])
    pa, pg, y, hc = _fused_mm("conv_in", dims='nn', M=N, N=D, bm=min(512, N), bn=D, x=x3, gain=row(P['mix_norm_o']),
                              groups=[[dict(A=None, Ka=D, B=w_cin, coff=0)], [dict(A=None, Ka=D, B=w_cin, coff=1)]],
                              epi=_epi_glu, outs=[BF, BF, F32], rows=[(b_cin, 0), (b_cin, 1)], h_out=True)
    w32 = jnp.concatenate([P['conv_dw_w'][0], jnp.zeros((HALO - CK, D), F32)], axis=0)
    cbt = min(256, T)
    s, yc = _conv_fwd(y, w32, row(P['conv_dw_b']), row(P['conv_ln_g']), row(P['conv_ln_b']), B, T, cbt)
    x4 = _mm_resid("conv_out", [dict(A=s, Ka=D, B=Wb['conv_w_out'][0])], x3, bias=row(P['conv_b_out']))
    x6, S1 = xa_ffn_fwd(1, x4)
    dx, loss_t, dgf = _head(x6, tgt2, row(P['final_norm']))
    G['final_norm'] = dgf.reshape(D)

    def by_rows(dw):
        return dw.reshape(NCHIP, dw.shape[1] // NCHIP, dw.shape[2])

    def xa_ffn_bwd(l, S, dx):
        wgu, wdown = Wb['ffn_w_gu'][l], Wb['ffn_w_down'][l]
        dwdown = by_rows(_mm_tn(f"dw_down{l}", S['a'], dx))
        dg, du = _fused_mm(f"ffn_dgu{l}", dims='nt', M=N, N=FF, bm=min(512, N), bn=FF // 2,
                           groups=[[dict(A=dx, Ka=D, B=wdown)]], epi=_epi_swiglu_bwd, outs=[BF, BF],
                           tiles=[(S['g'], 0), (S['u'], 0)], cols_outer=True)
        dwgu = jnp.concatenate([_mm_tn(f"dw_g{l}", S['hf'], dg, parts=2), _mm_tn(f"dw_u{l}", S['hf'], du, parts=2)])
        dx, dgn = _mm_nt_rms_bwd(f"ffn_dx{l}", [dict(A=dg, Ka=FF, B=wgu, coff=0), dict(A=du, Ka=FF, B=wgu, coff=1)],
                                 S['xm'], row(P['ffn_norm'][l]), dx)
        dwo = by_rows(_mm_tn(f"dw_o{l}", S['o'], dx))
        do = _mm_nt_plain(f"xa_do{l}", dx, Wb['xa_wo'][l])
        dq, dkv = _xattn_bwd(f"xattn_bwd{l}", S['qx'], S['kv'], do, B, T, min(1024, T))
        dwq = by_rows(_mm_tn(f"dw_q{l}", S['hq'], dq))
        dwkv = _mm_tn(f"dw_kv{l}", S['hm'], dkv, parts=NCHIP)
        dmn = _fused_mm(f"xa_dmem{l}", dims='nt', M=B * NMEM, N=D, bm=min(256, B * NMEM), bn=D,
                        groups=[[dict(A=dkv, Ka=2 * D, B=Wb['xa_wkv'][l])]], epi=_epi_rms_gain_only, outs=[],
                        tiles=[(mem2, 0)], rows=[(row(P['mem_norm'][l]), 0)], reds=[(1, D)])[0]
        dx, dxn = _mm_nt_rms_bwd(f"xa_dx{l}", [dict(A=dq, Ka=D, B=Wb['xa_wq'][l])], S['xin'],
                                 row(P['xa_norm'][l]), dx, bm=512)
        return dx, dict(ffn_w_down=dwdown, ffn_w_gu=dwgu, ffn_norm=dgn.reshape(D), xa_wo=dwo, xa_wq=dwq,
                        xa_wkv=dwkv, mem_norm=dmn.reshape(D), xa_norm=dxn.reshape(D))

    dx, G1 = xa_ffn_bwd(1, S1, dx)
    G['conv_w_out'] = [by_rows(_mm_tn("dw_cout", s, dx))]
    G['conv_b_out'] = _colsum("db_cout", dx)
    dsv = _mm_nt_plain("conv_ds", dx, Wb['conv_w_out'][0])
    dpa, dpg, dw32, dwb, dlng, dlnb, dba, dbg = _conv_bwd(dsv, yc, y, pa, pg, w32, row(P['conv_ln_g']),
                                                          row(P['conv_ln_b']), B, T, cbt)
    G['conv_dw_w'] = _sum_groups("conv_dw_sum", dw32)[:CK][None]
    G['conv_dw_b'], G['conv_ln_g'], G['conv_ln_b'] = dwb, dlng, dlnb
    G['conv_b_in'] = jnp.concatenate([dba, dbg], axis=1)
    G['conv_w_in'] = [jnp.concatenate([_mm_tn("dw_cin_a", hc, dpa, parts=2), _mm_tn("dw_cin_g", hc, dpg, parts=2)])]
    dx, dgo = _mm_nt_rms_bwd("conv_dx", [dict(A=dpa, Ka=D, B=w_cin, coff=0), dict(A=dpg, Ka=D, B=w_cin, coff=1)],
                             x3, row(P['mix_norm_o']), dx, bm=512)
    G['mix_norm_o'] = dgo
    dx, G0 = xa_ffn_bwd(0, S0, dx)
    for k in G0:
        G[k] = [G0[k], G1[k]]
    G['w_out_e'] = [by_rows(jnp.concatenate([_mm_tn("dw_out_b", b_out, dx), _mm_tn("dw_out_a", a_out, dx)], axis=1))]
    dcat = _mm_nt_plain("mix_dcat", dx, w_out)
    (dq, dk, dv, dcum4, dcq4), arrived = _fox_bwd(qkv, cum4, b_out, lse, dcat, B, T, 0, bq, bk,
                                                  carry=hooks['bwd_carry'](G) if hooks else None)
    if hooks:
        hooks['bwd_done'](arrived)
    dz, dlg, dlb, dws, dbf = _gmlp_bwd(z, dcat, lng, lnb, ws, bfull, bt)
    G['gmlp_ln_g'], G['gmlp_ln_b'], G['gmlp_w_s'] = dlg, dlb, dws[None]
    G['gmlp_b_s'] = _group_sum("gmlp_db", dbf)[:, :NG].T[None]
    pad = jnp.zeros((N, 128 - NG), F32)
    dck = jnp.concatenate([dcum4.reshape(B, NG, T).transpose(0, 2, 1).reshape(N, NG), pad], axis=1)
    dcq = jnp.concatenate([dcq4.reshape(N, NG // 2, 128)[:, :, :2].reshape(N, NG), pad], axis=1)
    dfl, dfb = _fox_gate_bwd(fl, fbias, dcq, dck, B, T)
    G['fox_f_bias'] = dfb[:, :NG]
    dw_in = jnp.concatenate([_mm_tn("dw_in_q", h0, dq)[0], _mm_tn("dw_in_k", h0, dk)[0], _mm_tn("dw_in_v", h0, dv)[0],
                             _mm_tn("dw_in_f", h0, dfl)[0][:, :NG], _mm_tn("dw_in_z", h0, dz)[0]], axis=1)
    G['w_in_e'] = [dw_in.reshape(D, NCHIP, IN_W // NCHIP).transpose(1, 0, 2)]
    pairs = [dict(A=dz, Ka=D, B=w_inp, coff=0), dict(A=dq, Ka=FOXW, B=w_inp, coff=2),
             dict(A=dk, Ka=FOXW, B=w_inp, coff=3), dict(A=dv, Ka=FOXW, B=w_inp, coff=4),
             dict(A=dfl, Ka=128, B=w_inp, coff=20)]
    if hooks:
        dx, dge, arrived = _mm_nt_rms_bwd("mix_dx", pairs, x0, g_e, dx, bm=512, carry=hooks['last_carry'](G))
        hooks['last_done'](arrived)
    else:
        dx, dge = _mm_nt_rms_bwd("mix_dx", pairs, x0, g_e, dx, bm=512)
    G['mix_norm_e'] = dge
    return loss_t[0, 0], dx.reshape(B, T, D), G


COLS = 1024
ANY = pl.BlockSpec(memory_space=pl.ANY)


def _coords():
    return lax.axis_index("x"), lax.axis_index("y"), lax.axis_index("c")


def _other_chips(x, y):
    return [(1 - x, y), (x, 1 - y), (1 - x, 1 - y)]


def _own_slot(v, me):
    return lax.dynamic_update_slice(lax.empty((NCHIP,) + v.shape, v.dtype), v[None], (me,) + (0,) * v.ndim)


def _all_gather(name, shards, me):
    n = len(shards)
    bufs = [_own_slot(v, me) for v in shards]

    def body(*refs):
        out_refs, (send_sems, recv_sems) = refs[n:2 * n], refs[2 * n:]
        x, y, c = _coords()
        mine = 2 * x + y
        sib = (x, y, 1 - c)
        chips = _other_chips(x, y)

        def rcopy(a, k, chip_idx, half, to):
            blk = out_refs[a].at[chip_idx, half]
            return pltpu.make_async_remote_copy(src_ref=blk, dst_ref=blk, send_sem=send_sems.at[6 * a + k],
                                                recv_sem=recv_sems.at[6 * a + k], device_id=to, device_id_type=MESH)

        first = [rcopy(a, j, mine, c, (cx, cy, c)) for a in range(n) for j, (cx, cy) in enumerate(chips)]
        for cp in first:
            cp.start()
        passed = []
        for a in range(n):
            for j, (cx, cy) in enumerate(chips):
                kj = 2 * cx + cy
                rcopy(a, j, kj, c, sib).wait_recv()
                fwd = rcopy(a, 3 + j, kj, c, sib)
                fwd.start()
                passed.append(fwd)
        for a in range(n):
            for j, (cx, cy) in enumerate(chips):
                rcopy(a, 3 + j, 2 * cx + cy, 1 - c, sib).wait_recv()
        for cp in first + passed:
            cp.wait_send()

    return _pcall(body, name=name, in_specs=[ANY] * n, out_specs=[ANY] * n,
                  out_shape=[jax.ShapeDtypeStruct(b.shape, b.dtype) for b in bufs],
                  input_output_aliases={a: a for a in range(n)}, scratch_shapes=_sem_pairs(6 * n))(*bufs)


def _sem_pairs(n):
    return [pltpu.SemaphoreType.DMA((n,)), pltpu.SemaphoreType.DMA((n,))]


def _gather_forward(name, bufs):
    n = len(bufs)

    def body(*refs):
        out_refs, (send_sems, recv_sems) = refs[n:2 * n], refs[2 * n:]
        x, y, c = _coords()

        def cp(a, j, kj, half):
            blk = out_refs[a].at[kj, half]
            return pltpu.make_async_remote_copy(src_ref=blk, dst_ref=blk, send_sem=send_sems.at[3 * a + j],
                                                recv_sem=recv_sems.at[3 * a + j], device_id=(x, y, 1 - c),
                                                device_id_type=MESH)

        chips = [2 * cx + cy for cx, cy in _other_chips(x, y)]
        sends = [cp(a, j, kj, c) for a in range(n) for j, kj in enumerate(chips)]
        for s in sends:
            s.start()
        for a in range(n):
            for j, kj in enumerate(chips):
                cp(a, j, kj, 1 - c).wait_recv()
        for s in sends:
            s.wait_send()

    return _pcall(body, name=name, in_specs=[ANY] * n, out_specs=[ANY] * n,
                  out_shape=[jax.ShapeDtypeStruct(b.shape, b.dtype) for b in bufs],
                  input_output_aliases={a: a for a in range(n)}, scratch_shapes=_sem_pairs(3 * n))(*bufs)


def _sibling_halves(name, ps):
    n = len(ps)

    def body(*refs):
        p_refs, out_refs, (send_sems, recv_sems) = refs[:n], refs[n:2 * n], refs[2 * n:]
        x, y, c = _coords()
        cps = [pltpu.make_async_remote_copy(src_ref=p_refs[a].at[k, 1 - c], dst_ref=out_refs[a].at[k],
                                            send_sem=send_sems.at[4 * a + k], recv_sem=recv_sems.at[4 * a + k],
                                            device_id=(x, y, 1 - c), device_id_type=MESH)
               for a in range(n) for k in range(NCHIP)]
        for cp in cps:
            cp.start()
        for cp in cps:
            cp.wait()

    return _pcall(body, name=name, in_specs=[ANY] * n, out_specs=[ANY] * n,
                  out_shape=[jax.ShapeDtypeStruct((NCHIP,) + p.shape[2:], p.dtype) for p in ps],
                  scratch_shapes=_sem_pairs(NCHIP * n))(*ps)


def _chip_exchange(name, qs):
    n = len(qs)

    def body(*refs):
        q_refs, out_refs, (send_sems, recv_sems) = refs[:n], refs[n:2 * n], refs[2 * n:]
        x, y, c = _coords()
        cps = [pltpu.make_async_remote_copy(src_ref=q_refs[a].at[2 * cx + cy], dst_ref=out_refs[a].at[j],
                                            send_sem=send_sems.at[3 * a + j], recv_sem=recv_sems.at[3 * a + j],
                                            device_id=(cx, cy, c), device_id_type=MESH)
               for a in range(n) for j, (cx, cy) in enumerate(_other_chips(x, y))]
        for cp in cps:
            cp.start()
        for cp in cps:
            cp.wait()

    return _pcall(body, name=name, in_specs=[ANY] * n, out_specs=[ANY] * n,
                  out_shape=[jax.ShapeDtypeStruct((3,) + q.shape[1:], q.dtype) for q in qs],
                  scratch_shapes=_sem_pairs(3 * n))(*qs)


def _sibling_swap(name, hs):
    n = len(hs)

    def body(*refs):
        out_refs, (send_sems, recv_sems) = refs[n:2 * n], refs[2 * n:]
        x, y, c = _coords()
        sib = (x, y, 1 - c)
        sends = [pltpu.make_async_remote_copy(src_ref=out_refs[a].at[c], dst_ref=out_refs[a].at[c],
                                              send_sem=send_sems.at[a], recv_sem=recv_sems.at[a], device_id=sib,
                                              device_id_type=MESH) for a in range(n)]
        for cp in sends:
            cp.start()
        for a in range(n):
            theirs = out_refs[a].at[1 - c]
            pltpu.make_async_remote_copy(src_ref=theirs, dst_ref=theirs, send_sem=send_sems.at[a],
                                         recv_sem=recv_sems.at[a], device_id=sib, device_id_type=MESH).wait_recv()
        for cp in sends:
            cp.wait_send()

    return _pcall(body, name=name, in_specs=[ANY] * n, out_specs=[ANY] * n,
                  out_shape=[jax.ShapeDtypeStruct(h.shape, h.dtype) for h in hs],
                  input_output_aliases={a: a for a in range(n)}, scratch_shapes=_sem_pairs(n))(*hs)


ADD_BLOCK_BYTES = 2 * 1024 * 1024


def _row_block(R, C):
    if R * C * 4 <= ADD_BLOCK_BYTES:
        return R
    for br in (512, 256, 128, 64, 32, 16, 8):
        if R % br == 0 and br * C * 4 <= ADD_BLOCK_BYTES:
            return br
    return R


def _add_own_half(name, p, recv, c_arr, out_dtype):
    _, _, R, C = p.shape
    br = _row_block(R, C)

    def body(c_ref, p_ref, r_ref, o_ref):
        o_ref[...] = (p_ref[...].astype(F32) + r_ref[...].astype(F32)).astype(o_ref.dtype)

    spec = pltpu.PrefetchScalarGridSpec(
        num_scalar_prefetch=1, grid=(NCHIP, R // br),
        in_specs=[pl.BlockSpec((None, None, br, C), lambda k, r, c_ref: (k, c_ref[0], r, 0)),
                  pl.BlockSpec((None, br, C), lambda k, r, c_ref: (k, r, 0))],
        out_specs=pl.BlockSpec((None, br, C), lambda k, r, c_ref: (k, r, 0)))
    return _pcall(body, name=name, grid_spec=spec, out_shape=jax.ShapeDtypeStruct((NCHIP, R, C), out_dtype),
                  compiler_params=_params(("arbitrary", "arbitrary")))(c_arr, p, recv)


def _add_chips(name, q, recv, idx_arr):
    _, R, C = q.shape
    br = _row_block(R, C)

    def body(idx_ref, q_ref, r_ref, o_ref):
        o_ref[...] = ((q_ref[...].astype(F32) + r_ref[0].astype(F32)) + r_ref[1].astype(F32)) + r_ref[2].astype(F32)

    spec = pltpu.PrefetchScalarGridSpec(
        num_scalar_prefetch=1, grid=(R // br,),
        in_specs=[pl.BlockSpec((None, br, C), lambda r, idx: (idx[0], r, 0)),
                  pl.BlockSpec((3, br, C), lambda r, idx: (0, r, 0))],
        out_specs=pl.BlockSpec((None, br, C), lambda r, idx: (idx[1], r, 0)))
    return _pcall(body, name=name, grid_spec=spec, out_shape=jax.ShapeDtypeStruct((2, R, C), F32),
                  compiler_params=_params(("arbitrary",)))(idx_arr, q, recv)


EARLY = ['w_in_e']
LATE = [n for n in BIG if n not in EARLY]


def _adamw(name, w, g, m, v):
    shape = w.shape
    cols = shape[-1]
    rows = w.size // cols
    w2, g2, m2, v2 = (a.reshape(rows, cols) for a in (w, g, m, v))
    bt = next((b for b in (256, 128) if rows % b == 0), rows)

    def body(w_ref, g_ref, m_ref, v_ref, d_ref, nm_ref, nv_ref):
        gv = g_ref[...]
        nm = ADAM_B1 * m_ref[...] + (1.0 - ADAM_B1) * gv
        nv = ADAM_B2 * v_ref[...] + (1.0 - ADAM_B2) * (gv * gv)
        m_hat = nm / (1.0 - ADAM_B1 ** ADAM_STEP)
        v_hat = nv / (1.0 - ADAM_B2 ** ADAM_STEP)
        d_ref[...] = -ADAM_LR * (m_hat / (jnp.sqrt(v_hat) + ADAM_EPS) + ADAM_WD * w_ref[...])
        nm_ref[...] = nm
        nv_ref[...] = nv

    blk = pl.BlockSpec((bt, cols), lambda i: (i, 0))
    outs = _pcall(body, name=name, grid=(rows // bt,), in_specs=[blk] * 4, out_specs=[blk] * 3,
                  out_shape=[jax.ShapeDtypeStruct((rows, cols), F32)] * 3, compiler_params=_params(("arbitrary",)))(
        w2, g2, m2, v2)
    return [o.reshape(shape) for o in outs]


SMALL_SHARDED = ['mix_norm_o', 'conv_b_in', 'conv_dw_w', 'conv_dw_b', 'conv_ln_g', 'conv_ln_b', 'conv_b_out']
REPLICATED = [n for n in WEIGHTS if SHARD_AXIS[n] is None]
NCHIP = 4


def _halves(flat, tile_rows):
    unit = 2 * tile_rows * COLS
    total = -(-flat.size // unit) * unit
    return jnp.pad(flat, (0, total - flat.size)).reshape(2, total // (2 * COLS), COLS)


def _flat(arrays):
    return jnp.concatenate([a.reshape(-1) for a in arrays])


def _chip_block(a, axis, k):
    n = a.shape[axis] // NCHIP
    return lax.slice_in_dim(a, k * n, (k + 1) * n, axis=axis)


def _full_shape(n, shard_shape):
    s = list(shard_shape[n])
    s[SHARD_AXIS[n]] *= NCHIP
    return tuple(s)


def _unpack(flat, names, shapes):
    out, off = {}, 0
    for n in names:
        size = math.prod(shapes[n])
        out[n] = flat[off:off + size].reshape(shapes[n])
        off += size
    return out


def kernel(x, mem, mix_norm_e, w_in_e, fox_f_bias, gmlp_ln_g, gmlp_ln_b, gmlp_w_s, gmlp_b_s, w_out_e, mix_norm_o, conv_w_in, conv_b_in, conv_dw_w, conv_dw_b, conv_ln_g, conv_ln_b, conv_w_out, conv_b_out, xa_norm, mem_norm, xa_wq, xa_wkv, xa_wo, ffn_norm, ffn_w_gu, ffn_w_down, final_norm, loss_target, m_mix_norm_e, m_w_in_e, m_fox_f_bias, m_gmlp_ln_g, m_gmlp_ln_b, m_gmlp_w_s, m_gmlp_b_s, m_w_out_e, m_mix_norm_o, m_conv_w_in, m_conv_b_in, m_conv_dw_w, m_conv_dw_b, m_conv_ln_g, m_conv_ln_b, m_conv_w_out, m_conv_b_out, m_xa_norm, m_mem_norm, m_xa_wq, m_xa_wkv, m_xa_wo, m_ffn_norm, m_ffn_w_gu, m_ffn_w_down, m_final_norm, v_mix_norm_e, v_w_in_e, v_fox_f_bias, v_gmlp_ln_g, v_gmlp_ln_b, v_gmlp_w_s, v_gmlp_b_s, v_w_out_e, v_mix_norm_o, v_conv_w_in, v_conv_b_in, v_conv_dw_w, v_conv_dw_b, v_conv_ln_g, v_conv_ln_b, v_conv_w_out, v_conv_b_out, v_xa_norm, v_mem_norm, v_xa_wq, v_xa_wkv, v_xa_wo, v_ffn_norm, v_ffn_w_gu, v_ffn_w_down, v_final_norm):
    env = locals()
    w = {n: env[n] for n in WEIGHTS}
    m = {n: env["m_" + n] for n in WEIGHTS}
    v = {n: env["v_" + n] for n in WEIGHTS}
    shard_shape = {n: w[n].shape for n in WEIGHTS}
    xi, yi, ci = _coords()
    me = 2 * xi + yi
    c_arr = jnp.reshape(ci, (1,)).astype(jnp.int32)
    idx_arr = jnp.stack([me, ci]).astype(jnp.int32)

    def two_halves(a):
        return a.reshape(2, a.shape[0] // 2, a.shape[1])

    def shard(n, l):
        return two_halves(w[n][l].astype(BF))

    def matrix(n, gathered):
        rows, cols = w[n].shape[1:]
        g = gathered.reshape(NCHIP, rows, cols)
        return g.reshape(NCHIP * rows, cols) if SHARD_AXIS[n] == 1 else g.transpose(1, 0, 2).reshape(rows, NCHIP * cols)

    got = _all_gather("gather_first", [shard(n, 0) for n in EARLY] + [_halves(_flat([w[n] for n in SMALL_SHARDED]), 8)], me)
    Wb = {n: [matrix(n, g)] for n, g in zip(EARLY, got)}
    vec = got[-1].reshape(NCHIP, -1)
    parts = [_unpack(vec[k], SMALL_SHARDED, shard_shape) for k in range(NCHIP)]
    P = {n: jnp.concatenate([parts[k][n] for k in range(NCHIP)], axis=SHARD_AXIS[n]) for n in SMALL_SHARDED}
    P.update({n: w[n] for n in REPLICATED})
    first = [(n, 0) for n in LATE]
    second = [(n, 1) for n in LATE if w[n].shape[0] > 1]
    bufs = {key: _own_slot(shard(*key), me) for key in first + second}
    state = {}

    def fwd_done(arrived):
        state['layer0'] = {n: matrix(n, g) for (n, _), g in zip(first, _gather_forward("gather_forward0", arrived))}
        return {n: [g] for n, g in state['layer0'].items()}

    def fwd2_done(arrived):
        return {n: [state['layer0'][n], matrix(n, g)]
                for (n, _), g in zip(second, _gather_forward("gather_forward1", arrived))}

    def by_halves(G, names):
        return [g.reshape(NCHIP, 2, g.shape[1] // 2, g.shape[2]) for n in names for g in G[n]]

    def pair_sums(tag, ps):
        got = _sibling_halves(f"rs_sibling_halves_{tag}", ps)
        return [_add_own_half(f"rs_add_pair_{tag}{a}", p, g, c_arr, p.dtype) for a, (p, g) in enumerate(zip(ps, got))]

    def bwd_carry(G):
        state['qs'] = pair_sums("late", by_halves(G, LATE))
        return _carry_exchange(state['qs'])

    def last_carry(G):
        state['qs_in'] = pair_sums("in", by_halves(G, ['w_in_e']))
        return _carry_exchange(state['qs_in'])

    hooks = dict(fwd_carry=lambda: _carry_gather([bufs[key] for key in first]), fwd_done=fwd_done,
                 fwd2_carry=lambda: _carry_gather([bufs[key] for key in second]), fwd2_done=fwd2_done,
                 bwd_carry=bwd_carry, bwd_done=lambda arrived: state.update(got=arrived),
                 last_carry=last_carry, last_done=lambda arrived: state.update(got_in=arrived))
    loss_part, grad_x, G = _local_step(x, mem, loss_target, Wb, P, hooks)
    loss = lax.psum(loss_part, ("x", "y", "c"))

    def layers(n):
        return G[n] if isinstance(G[n], list) else ([G[n]] if G[n].ndim == 1 else [G[n][l] for l in range(G[n].shape[0])])

    rep = _flat([a for n in REPLICATED for a in layers(n)])
    quarter = -(-rep.size // (NCHIP * 2 * 8 * COLS)) * (2 * 8 * COLS)
    rep = jnp.pad(rep, (0, NCHIP * quarter - rep.size)).reshape(NCHIP, quarter)
    segs = [[_chip_block(a, SHARD_AXIS[n] - 1, k).reshape(-1) for n in SMALL_SHARDED for a in layers(n)] + [rep[k]]
            for k in range(NCHIP)]
    size = sum(piece.size for piece in segs[0])
    total = -(-size // (2 * 8 * COLS)) * (2 * 8 * COLS)
    p_small = jnp.concatenate([piece for seg in segs for piece in seg + [jnp.zeros((total - size,), F32)]])
    p_small = p_small.reshape(NCHIP, 2, total // (2 * COLS), COLS)
    qs_small = pair_sums("vectors", [p_small])
    pending = [("late", state['qs'], state['got']), ("in", state['qs_in'], state['got_in']),
               ("vectors", qs_small, _chip_exchange("rs_chip_exchange_vectors", qs_small))]
    hs = [_add_chips(f"rs_add_chips_{tag}{a}", q, g, idx_arr)
          for tag, qs, got in pending for a, (q, g) in enumerate(zip(qs, got))]
    red = _sibling_swap("rs_sibling_swap", hs)
    mine, at = {}, 0
    for n in LATE + EARLY:
        nl = shard_shape[n][0]
        mine[n] = jnp.stack([r.reshape(shard_shape[n][1:]) for r in red[at:at + nl]])
        at += nl
    red_small = red[-1].reshape(-1)
    mine.update(_unpack(red_small, SMALL_SHARDED, shard_shape))
    off = sum(math.prod(shard_shape[n]) for n in SMALL_SHARDED)
    rep_all = _all_gather("gather_replicated_grads",
                          [red_small[off:off + quarter].reshape(2, quarter // (2 * COLS), COLS)], me)[0]
    mine.update(_unpack(rep_all.reshape(-1), REPLICATED, shard_shape))

    grads, deltas, new_m, new_v = [], [], [], []
    for n in WEIGHTS:
        d, nm, nv = _adamw("adamw_" + n, w[n], mine[n], m[n], v[n])
        grads.append(mine[n])
        deltas.append(d)
        new_m.append(nm)
        new_v.append(nv)
    return (loss, grad_x, *grads, *deltas, *new_m, *new_v)
```
